```python
import math
import jax, jax.numpy as jnp
from jax import lax
import numpy as np

D_MODEL = 1024
BATCH = 2
SEQ = 8192
DEPTH = 1

HEAD_DIM = 64
MIX_WIDTH = D_MODEL
DIFF_VDIM = 2 * HEAD_DIM
DIFF_HEADS = (MIX_WIDTH // 2) // DIFF_VDIM
SB_HEADS = (MIX_WIDTH // 2) // HEAD_DIM
DIFF_QK = DIFF_HEADS * 2 * HEAD_DIM
DIFF_V = DIFF_HEADS * DIFF_VDIM
SB_W = SB_HEADS * HEAD_DIM
IN_COLS = 2 * DIFF_QK + DIFF_V + 3 * SB_W
Q_BLOCK = 128
N_GROUPS = 4
EXPERTS_PER_GROUP = 8
TOP_K_IN_GROUP = 2
D_EXPERT = 256
N_MOD = 6
EPS = 1e-6

kernel_name = "hymba_diff_stickbreak_hmoe_block"


def rms_norm(x, g):
    xf = x.astype(jnp.float32)
    y = xf * lax.rsqrt(jnp.mean(xf * xf, axis=-1, keepdims=True) + EPS)
    return (y * g.astype(jnp.float32)).astype(x.dtype)


def modulate(h, shift, scale):
    return h * (1.0 + scale[:, None, :]) + shift[:, None, :]


def alibi_slopes(n_heads):
    return 2.0 ** (-8.0 * jnp.arange(1, n_heads + 1, dtype=jnp.float32) / n_heads)


def diff_attention(q, k, v, lam):
    B, H, _, S, d = q.shape
    nb = S // Q_BLOCK
    slopes = alibi_slopes(H)
    spos = jnp.arange(S)
    qb = q.reshape(B, H, 2, nb, Q_BLOCK, d).transpose(3, 0, 1, 2, 4, 5)

    def one_block(args):
        q_blk, i = args
        tpos = i * Q_BLOCK + jnp.arange(Q_BLOCK)
        dist = tpos[:, None] - spos[None, :]
        s = jnp.einsum('bhmqd,bhmkd->bhmqk', q_blk, k).astype(jnp.float32)
        bias = -slopes[:, None, None, None] * dist.astype(jnp.float32)
        s = jnp.where(dist >= 0, s + bias, -jnp.inf)
        p = jax.nn.softmax(s, axis=-1)
        a = p[:, :, 0] - lam * p[:, :, 1]
        return jnp.einsum('bhqk,bhkd->bhqd', a.astype(v.dtype), v)

    out = lax.map(one_block, (qb, jnp.arange(nb)))
    return out.transpose(1, 0, 3, 2, 4).reshape(B, S, H, v.shape[-1])


def stick_breaking_attention(q, k, v):
    B, H, S, d = q.shape
    nb = S // Q_BLOCK
    scale = 1.0 / math.sqrt(d)
    spos = jnp.arange(S)
    qb = q.reshape(B, H, nb, Q_BLOCK, d).transpose(2, 0, 1, 3, 4)

    def one_block(args):
        q_blk, i = args
        tpos = i * Q_BLOCK + jnp.arange(Q_BLOCK)
        mask = spos[None, :] < tpos[:, None]
        z = jnp.einsum('bhqd,bhkd->bhqk', q_blk, k).astype(jnp.float32) * scale
        log_beta = jax.nn.log_sigmoid(z)
        log_rem = jnp.where(mask, jax.nn.log_sigmoid(-z), 0.0)
        suffix = lax.cumsum(log_rem, axis=3, reverse=True) - log_rem
        a = jnp.where(mask, jnp.exp(log_beta + suffix), 0.0)
        return jnp.einsum('bhqk,bhkd->bhqd', a.astype(v.dtype), v)

    out = lax.map(one_block, (qb, jnp.arange(nb)))
    return out.transpose(1, 0, 3, 2, 4).reshape(B, S, H, d)


def hier_moe(h, w_group, b_group, w_erouter, b_expert, w1, w3, w2):
    B, S, D = h.shape
    ht = h.reshape(B * S, D)
    g_logits = (ht @ w_group + b_group).astype(jnp.float32)
    p_group = jax.nn.softmax(g_logits, axis=-1)
    _, g_sel = lax.top_k(g_logits, 1)
    gate_group = jnp.take_along_axis(p_group, g_sel, axis=-1)
    g_sel = g_sel[:, 0]
    e_all = (jnp.einsum('td,gde->tge', ht, w_erouter) + b_expert).astype(jnp.float32)
    e_logits = jnp.take_along_axis(e_all, g_sel[:, None, None], axis=1)[:, 0]
    top_vals, top_idx = lax.top_k(e_logits, TOP_K_IN_GROUP)
    top_w = jax.nn.softmax(top_vals, axis=-1)
    within = jnp.sum(jax.nn.one_hot(top_idx, EXPERTS_PER_GROUP, dtype=jnp.float32)
                     * top_w[..., None], axis=1)
    combine = (gate_group[:, :, None]
               * jax.nn.one_hot(g_sel, N_GROUPS, dtype=jnp.float32)[:, :, None]
               * within[:, None, :]).astype(ht.dtype)
    out = jnp.zeros_like(ht)
    for g in range(N_GROUPS):
        hg = jax.nn.silu(jnp.einsum('td,edf->tef', ht, w1[g])) * jnp.einsum('td,edf->tef', ht, w3[g])
        out = out + jnp.einsum('tef,efd->td', hg * combine[:, g, :, None], w2[g])
    return out.reshape(B, S, D)


def setup_inputs(seed: int = 0) -> dict:
    key = jax.random.key(seed)
    ks = jax.random.split(key, 24)
    f32 = jnp.float32
    L, D, d = DEPTH, D_MODEL, HEAD_DIM
    G, E, F = N_GROUPS, EXPERTS_PER_GROUP, D_EXPERT
    nrm = lambda k, shape, s: jax.random.normal(k, shape, f32) * s
    return {
        "x": nrm(ks[0], (BATCH, SEQ, D), 1.0),
        "c": nrm(ks[1], (BATCH, D), 1.0),
        "w_ada": nrm(ks[2], (L, D, N_MOD * D), 0.5 * D ** -0.5),
        "b_ada": nrm(ks[3], (L, N_MOD * D), 0.02),
        "g_attn": 1.0 + nrm(ks[4], (L, D), 0.05),
        "w_in": nrm(ks[5], (L, D, IN_COLS), D ** -0.5),
        "q_norm_g": 1.0 + nrm(ks[6], (L, d), 0.05),
        "k_norm_g": 1.0 + nrm(ks[7], (L, d), 0.05),
        "lambda_q1": nrm(ks[8], (L, d), 0.1),
        "lambda_k1": nrm(ks[9], (L, d), 0.1),
        "lambda_q2": nrm(ks[10], (L, d), 0.1),
        "lambda_k2": nrm(ks[11], (L, d), 0.1),
        "diff_out_g": 1.0 + nrm(ks[12], (L, DIFF_VDIM), 0.05),
        "sb_out_g": 1.0 + nrm(ks[13], (L, d), 0.05),
        "w_out": nrm(ks[14], (L, MIX_WIDTH, D), MIX_WIDTH ** -0.5),
        "g_ffn": 1.0 + nrm(ks[15], (L, D), 0.05),
        "w_group": nrm(ks[16], (L, D, G), D ** -0.5),
        "b_group": nrm(ks[17], (L, G), 0.01),
        "w_erouter": nrm(ks[18], (L, G, D, E), D ** -0.5),
        "b_expert": nrm(ks[19], (L, G, E), 0.01),
        "w1": nrm(ks[20], (L, G, E, D, F), D ** -0.5),
        "w3": nrm(ks[21], (L, G, E, D, F), D ** -0.5),
        "w2": nrm(ks[22], (L, G, E, F, D), F ** -0.5),
    }


def reference(x, c, w_ada, b_ada, g_attn, w_in, q_norm_g, k_norm_g,
              lambda_q1, lambda_k1, lambda_q2, lambda_k2, diff_out_g, sb_out_g,
              w_out, g_ffn, w_group, b_group, w_erouter, b_expert, w1, w3, w2):
    B, S, D = x.shape
    d = HEAD_DIM
    qk_scale = 1.0 / math.sqrt(d)
    for layer in range(DEPTH):
        lambda_init = 0.8 - 0.6 * math.exp(-0.3 * layer)
        mod = jax.nn.silu(c) @ w_ada[layer] + b_ada[layer]
        sh_a, sc_a, gt_a, sh_f, sc_f, gt_f = jnp.split(mod, N_MOD, axis=-1)

        h = modulate(rms_norm(x, g_attn[layer]), sh_a, sc_a)
        proj = h @ w_in[layer]
        cuts = np.cumsum([DIFF_QK, DIFF_QK, DIFF_V, SB_W, SB_W])
        dq, dk, dv, sq, sk, sv = jnp.split(proj, cuts.tolist(), axis=-1)

        dq = dq.reshape(B, S, DIFF_HEADS, 2, d).transpose(0, 2, 3, 1, 4)
        dk = dk.reshape(B, S, DIFF_HEADS, 2, d).transpose(0, 2, 3, 1, 4)
        dq = rms_norm(dq, q_norm_g[layer]) * qk_scale
        dk = rms_norm(dk, k_norm_g[layer])
        dv = dv.reshape(B, S, DIFF_HEADS, DIFF_VDIM).transpose(0, 2, 1, 3)
        lam = (jnp.exp(jnp.sum(lambda_q1[layer] * lambda_k1[layer]).astype(jnp.float32))
               - jnp.exp(jnp.sum(lambda_q2[layer] * lambda_k2[layer]).astype(jnp.float32))
               + lambda_init)
        a_out = diff_attention(dq, dk, dv, lam)
        a_out = rms_norm(a_out, diff_out_g[layer]) * (1.0 - lambda_init)
        a_out = a_out.reshape(B, S, DIFF_V)

        sq = sq.reshape(B, S, SB_HEADS, d).transpose(0, 2, 1, 3)
        sk = sk.reshape(B, S, SB_HEADS, d).transpose(0, 2, 1, 3)
        sv = sv.reshape(B, S, SB_HEADS, d).transpose(0, 2, 1, 3)
        b_out = stick_breaking_attention(sq, sk, sv)
        b_out = rms_norm(b_out, sb_out_g[layer]).reshape(B, S, SB_W)

        y = jnp.concatenate([a_out, b_out], axis=-1) @ w_out[layer]
        x = x + gt_a[:, None, :] * y

        h = modulate(rms_norm(x, g_ffn[layer]), sh_f, sc_f)
        y = hier_moe(h, w_group[layer], b_group[layer], w_erouter[layer], b_expert[layer],
                     w1[layer], w3[layer], w2[layer])
        x = x + gt_f[:, None, :] * y
    return x
```

```python
import functools
import math

import jax
import jax.numpy as jnp
from jax import lax
from jax.experimental import pallas as pl
from jax.experimental.pallas import tpu as pltpu

HEAD_DIM = 64
DIFF_HEADS = 4
SB_HEADS = 8
DIFF_VDIM = 2 * HEAD_DIM
N_GROUPS = 4
EXPERTS_PER_GROUP = 8
N_EXPERTS = N_GROUPS * EXPERTS_PER_GROUP
D_EXPERT = 256
N_MOD = 6
EPS = 1e-6
LANES = 128
MXU_COLS = 256
NEG = -1e30
VMEM_LIMIT = 56 * 1024 * 1024

F32 = jnp.float32
BF16 = jnp.bfloat16


def _nt_dot(a, b):
    return lax.dot_general(a, b, (((1,), (1,)), ((), ())), preferred_element_type=F32)


def _mod_kernel(c_ref, w_ref, b_ref, o_ref):
    c = c_ref[...]
    sc = c * jax.nn.sigmoid(c)
    o_ref[...] = jnp.dot(sc, w_ref[...], precision=lax.Precision.HIGHEST,
                         preferred_element_type=F32) + b_ref[...]


def _mod_call(c_pad, w_ada, b_ada):
    rows, d = c_pad.shape
    n = w_ada.shape[1]
    tn = 1536
    return pl.pallas_call(
        _mod_kernel,
        grid=(n // tn,),
        in_specs=[pl.BlockSpec((rows, d), lambda j: (0, 0)),
                  pl.BlockSpec((d, tn), lambda j: (0, j)),
                  pl.BlockSpec((1, tn), lambda j: (0, j))],
        out_specs=pl.BlockSpec((rows, tn), lambda j: (0, j)),
        out_shape=jax.ShapeDtypeStruct((rows, n), F32),
        compiler_params=pltpu.CompilerParams(dimension_semantics=("parallel",),
                                             vmem_limit_bytes=VMEM_LIMIT),
        name="adaln_mod",
    )(c_pad, w_ada, b_ada)


_PROJ_BLOCKS = 36


def _proj_kernel(x_ref, mod_ref, g_ref, w_ref, qg_ref, kg_ref,
                 dq_ref, dk_ref, dv_ref, sq_ref, sk_ref, sv_ref, *, tm, tk, slopes):
    x = x_ref[0]
    mod = mod_ref[0]
    shift, scale = mod[0:1, :], mod[1:2, :]
    ms = jnp.mean(x * x, axis=-1, keepdims=True)
    h = (x * lax.rsqrt(ms + EPS) * g_ref[...]) * (1.0 + scale) + shift
    hb = h.astype(BF16)

    lane = lax.broadcasted_iota(jnp.int32, (1, LANES), 1)
    lo_lane = (lane == HEAD_DIM).astype(F32)
    hi_lane = (lane == HEAD_DIM + 1).astype(F32)
    row = lax.broadcasted_iota(jnp.int32, (tm, 1), 0) + pl.program_id(1) * tm
    koff = row & (tk - 1)
    koff_lo = (koff & 255).astype(F32)
    koff_hi = (koff - (koff & 255)).astype(F32)
    koff_lanes = lo_lane * koff_lo + hi_lane * koff_hi

    for c in range(_PROJ_BLOCKS * LANES // MXU_COLS):
        pc = jnp.dot(hb, w_ref[:, c * MXU_COLS:(c + 1) * MXU_COLS], preferred_element_type=F32)
        for half in range(MXU_COLS // LANES):
            blk = c * (MXU_COLS // LANES) + half
            piece = pc[:, half * LANES:(half + 1) * LANES]
            if blk < 16:
                ss = jnp.sum(piece * piece, axis=-1, keepdims=True)
                normed = piece * lax.rsqrt(ss * (1.0 / HEAD_DIM) + EPS)
                if blk < 8:
                    dq_ref[0, blk] = (normed * qg_ref[...] + (lo_lane + hi_lane)).astype(BF16)
                else:
                    slope = slopes[(blk - 8) // 2]
                    dk_ref[0, blk - 8] = (normed * kg_ref[...] + koff_lanes * slope).astype(BF16)
            elif blk < 20:
                dv_ref[0, blk - 16] = piece.astype(BF16)
            elif blk < 28:
                sq_ref[0, blk - 20] = piece.astype(BF16)
            elif blk < 32:
                sk_ref[0, blk - 28] = piece.astype(BF16)
            else:
                sv_ref[0, blk - 32] = piece.astype(BF16)


def _proj_call(x, mod, g_attn, w_perm, qg_pad, kg_pad, *, tm, tk, slopes):
    b, s, d = x.shape
    n = w_perm.shape[1]
    nt = s // tm

    def hm(nh):
        return (jax.ShapeDtypeStruct((b, nh, s, LANES), BF16),
                pl.BlockSpec((1, nh, tm, LANES), lambda bi, ti: (bi, 0, ti, 0)))

    outs = [hm(8), hm(8), hm(4), hm(8), hm(4), hm(4)]
    return pl.pallas_call(
        functools.partial(_proj_kernel, tm=tm, tk=tk, slopes=slopes),
        grid=(b, nt),
        in_specs=[pl.BlockSpec((1, tm, d), lambda bi, ti: (bi, ti, 0)),
                  pl.BlockSpec((1, N_MOD, d), lambda bi, ti: (bi, 0, 0)),
                  pl.BlockSpec((1, d), lambda bi, ti: (0, 0)),
                  pl.BlockSpec((d, n), lambda bi, ti: (0, 0)),
                  pl.BlockSpec((1, LANES), lambda bi, ti: (0, 0)),
                  pl.BlockSpec((1, LANES), lambda bi, ti: (0, 0))],
        out_specs=[o[1] for o in outs],
        out_shape=[o[0] for o in outs],
        compiler_params=pltpu.CompilerParams(dimension_semantics=("parallel", "parallel"),
                                             vmem_limit_bytes=VMEM_LIMIT),
        name="in_proj",
    )(x, mod, g_attn, w_perm, qg_pad, kg_pad)


def _diff_kernel(lam_ref, g_ref, q_ref, k_ref, v_ref, o_ref, acc0_ref, acc1_ref, *, t, slopes, lambda_init):
    hd = pl.program_id(1)
    i = pl.program_id(2)
    hd_col = jnp.full((t, 1), hd, jnp.int32)
    slope = jnp.full((t, 1), slopes[-1], F32)
    for n in range(len(slopes) - 2, -1, -1):
        slope = jnp.where(hd_col == n, slopes[n], slope)

    q = (q_ref[0, 0], q_ref[0, 1])
    accs = (acc0_ref, acc1_ref)
    acc0_ref[...] = jnp.zeros_like(acc0_ref)
    acc1_ref[...] = jnp.zeros_like(acc1_ref)
    row = lax.broadcasted_iota(jnp.int32, (t, t), 0)
    col = lax.broadcasted_iota(jnp.int32, (t, t), 1)
    causal = col <= row

    def block(j, carry, masked):
        ks = pl.multiple_of(j * t, t)
        v = v_ref[0, 0, pl.ds(ks, t), :]
        off = slope * jnp.full((t, 1), j * t, jnp.int32).astype(F32)
        new = []
        for mp in range(2):
            m, l = carry[2 * mp], carry[2 * mp + 1]
            s = _nt_dot(q[mp], k_ref[0, mp, pl.ds(ks, t), :])
            if masked:
                s = jnp.where(causal, s, NEG)
            m_new = jnp.maximum(m, jnp.max(s, axis=-1, keepdims=True) + off)
            alpha = jnp.exp(m - m_new)
            p = jnp.exp(s - (m_new - off))
            l_new = alpha * l + jnp.sum(p, axis=-1, keepdims=True)
            accs[mp][...] = alpha * accs[mp][...] + jnp.dot(p.astype(BF16), v, preferred_element_type=F32)
            new += [m_new, l_new]
        return tuple(new)

    init = (jnp.full((t, 1), NEG, F32), jnp.zeros((t, 1), F32)) * 2
    carry = lax.fori_loop(0, i, lambda j, c: block(j, c, False), init)
    _, l0, _, l1 = block(i, carry, True)

    lp = lam_ref[...]
    lam = (jnp.exp(jnp.sum(lp[0:1] * lp[1:2], axis=-1, keepdims=True))
           - jnp.exp(jnp.sum(lp[2:3] * lp[3:4], axis=-1, keepdims=True)) + lambda_init)
    o = acc0_ref[...] / l0 - lam * (acc1_ref[...] / l1)
    ms = jnp.mean(o * o, axis=-1, keepdims=True)
    o_ref[0] = ((o * lax.rsqrt(ms + EPS) * g_ref[...]) * (1.0 - lambda_init)).astype(BF16)


def _diff_call(lam_p, g_out, dq, dk, dv, *, t, slopes, lambda_init):
    b, _, s, _ = dq.shape
    return pl.pallas_call(
        functools.partial(_diff_kernel, t=t, slopes=slopes, lambda_init=lambda_init),
        grid=(b, DIFF_HEADS, s // t),
        in_specs=[pl.BlockSpec((4, HEAD_DIM), lambda bi, hi, qi: (0, 0)),
                  pl.BlockSpec((1, LANES), lambda bi, hi, qi: (0, 0)),
                  pl.BlockSpec((1, 2, t, LANES), lambda bi, hi, qi: (bi, hi, qi, 0)),
                  pl.BlockSpec((1, 2, s, LANES), lambda bi, hi, qi: (bi, hi, 0, 0)),
                  pl.BlockSpec((1, 1, s, LANES), lambda bi, hi, qi: (bi, hi, 0, 0))],
        out_specs=pl.BlockSpec((1, t, LANES), lambda bi, hi, qi: (bi, qi, hi)),
        out_shape=jax.ShapeDtypeStruct((b, s, DIFF_HEADS * LANES), BF16),
        scratch_shapes=[pltpu.VMEM((t, LANES), F32), pltpu.VMEM((t, LANES), F32)],
        compiler_params=pltpu.CompilerParams(dimension_semantics=("parallel", "parallel", "parallel"),
                                             vmem_limit_bytes=VMEM_LIMIT),
        name="diff_attn",
    )(lam_p, g_out, dq, dk, dv)


def _sb_kernel(g_ref, q_ref, k_ref, v_ref, o_ref, acc0_ref, acc1_ref, *, t):
    i = pl.program_id(2)
    q = (q_ref[0, 0], q_ref[0, 1])
    accs = (acc0_ref, acc1_ref)
    acc0_ref[...] = jnp.zeros_like(acc0_ref)
    acc1_ref[...] = jnp.zeros_like(acc1_ref)
    row = lax.broadcasted_iota(jnp.int32, (t, t), 0)
    col = lax.broadcasted_iota(jnp.int32, (t, t), 1)
    past = col < row
    later = (row > col).astype(BF16)

    def block(j, carry, masked):
        ks = pl.multiple_of(j * t, t)
        k = k_ref[0, 0, pl.ds(ks, t), :]
        v = v_ref[0, 0, pl.ds(ks, t), :]
        new = []
        for hh in range(2):
            z = _nt_dot(q[hh], k)
            sp = jnp.maximum(z, 0.0) + jnp.log(1.0 + jnp.exp(-jnp.abs(z)))
            log_rem = -sp
            log_beta = z - sp
            if masked:
                log_rem = jnp.where(past, log_rem, 0.0)
            lr = log_rem.astype(BF16)
            suffix = jnp.dot(lr, later, preferred_element_type=F32)
            a = jnp.exp(log_beta + suffix + carry[hh])
            if masked:
                a = jnp.where(past, a, 0.0)
            accs[hh][...] += jnp.dot(a.astype(BF16), v, preferred_element_type=F32)
            new.append(carry[hh] + suffix[:, 0:1] + log_rem[:, 0:1])
        return tuple(new)

    carry = block(i, (jnp.zeros((t, 1), F32),) * 2, True)
    lax.fori_loop(0, i, lambda jj, c: block(i - 1 - jj, c, False), carry)

    lane = lax.broadcasted_iota(jnp.int32, (1, LANES), 1)
    first = lane < HEAD_DIM
    o = jnp.where(first, acc0_ref[...], acc1_ref[...])
    sq = o * o
    s_first = jnp.sum(jnp.where(first, sq, 0.0), axis=-1, keepdims=True)
    s_all = jnp.sum(sq, axis=-1, keepdims=True)
    ms = jnp.where(first, s_first, s_all - s_first) * (1.0 / HEAD_DIM)
    o_ref[0] = (o * lax.rsqrt(ms + EPS) * g_ref[...]).astype(BF16)


def _sb_call(g_pair, sq, sk, sv, *, t):
    b, _, s, _ = sq.shape
    pairs = SB_HEADS // 2
    return pl.pallas_call(
        functools.partial(_sb_kernel, t=t),
        grid=(b, pairs, s // t),
        in_specs=[pl.BlockSpec((1, LANES), lambda bi, pi, qi: (0, 0)),
                  pl.BlockSpec((1, 2, t, LANES), lambda bi, pi, qi: (bi, pi, qi, 0)),
                  pl.BlockSpec((1, 1, s, LANES), lambda bi, pi, qi: (bi, pi, 0, 0)),
                  pl.BlockSpec((1, 1, s, LANES), lambda bi, pi, qi: (bi, pi, 0, 0))],
        out_specs=pl.BlockSpec((1, t, LANES), lambda bi, pi, qi: (bi, qi, pi)),
        out_shape=jax.ShapeDtypeStruct((b, s, pairs * LANES), BF16),
        scratch_shapes=[pltpu.VMEM((t, LANES), F32), pltpu.VMEM((t, LANES), F32)],
        compiler_params=pltpu.CompilerParams(dimension_semantics=("parallel", "parallel", "parallel"),
                                             vmem_limit_bytes=VMEM_LIMIT),
        name="sb_attn",
    )(g_pair, sq, sk, sv)


def _outproj_kernel(a_ref, b_ref, w_ref, x_ref, mod_ref, g_ref, wr_ref, br_ref,
                    x1_ref, h2_ref, comb_ref):
    half = a_ref.shape[-1]
    y = (jnp.dot(a_ref[0], w_ref[0:half, :], preferred_element_type=F32)
         + jnp.dot(b_ref[0], w_ref[half:, :], preferred_element_type=F32))
    mod = mod_ref[0]
    gate_a, shift, scale = mod[2:3, :], mod[3:4, :], mod[4:5, :]
    x1 = x_ref[0] + gate_a * y
    x1_ref[0] = x1
    ms = jnp.mean(x1 * x1, axis=-1, keepdims=True)
    h2 = (x1 * lax.rsqrt(ms + EPS) * g_ref[...]) * (1.0 + scale) + shift
    h2_ref[0] = h2.astype(BF16)

    logits = jnp.dot(h2, wr_ref[...], precision=lax.Precision.HIGHEST,
                     preferred_element_type=F32) + br_ref[...]
    lane = lax.broadcasted_iota(jnp.int32, logits.shape, 1).astype(F32)
    big = jnp.float32(LANES)

    def top(vals):
        mx = jnp.max(vals, axis=-1, keepdims=True)
        idx = jnp.min(jnp.where(vals == mx, lane, big), axis=-1, keepdims=True)
        return mx, idx

    is_group = (lane >= N_EXPERTS) & (lane < N_EXPERTS + N_GROUPS)
    g_logits = jnp.where(is_group, logits, NEG)
    g_max, g_idx = top(g_logits)
    gate_group = 1.0 / jnp.sum(jnp.where(is_group, jnp.exp(logits - g_max), 0.0), axis=-1, keepdims=True)
    g_sel = g_idx - N_EXPERTS
    in_group = (lane >= g_sel * EXPERTS_PER_GROUP) & (lane < (g_sel + 1.0) * EXPERTS_PER_GROUP)
    e_logits = jnp.where(in_group, logits, NEG)
    v1, i1 = top(e_logits)
    v2, i2 = top(jnp.where(lane == i1, NEG, e_logits))
    e2 = jnp.exp(v2 - v1)
    w_first = 1.0 / (1.0 + e2)
    w_second = e2 / (1.0 + e2)
    comb_ref[0] = gate_group * (jnp.where(lane == i1, w_first, 0.0) + jnp.where(lane == i2, w_second, 0.0))


def _outproj_call(a_out, b_out, w_out, x, mod, g_ffn, w_router, b_router, *, tm):
    b, s, d = x.shape
    half = a_out.shape[-1]
    idx3 = lambda bi, ti: (bi, ti, 0)
    const2 = lambda bi, ti: (0, 0)
    return pl.pallas_call(
        _outproj_kernel,
        grid=(b, s // tm),
        in_specs=[pl.BlockSpec((1, tm, half), idx3),
                  pl.BlockSpec((1, tm, half), idx3),
                  pl.BlockSpec((2 * half, d), const2),
                  pl.BlockSpec((1, tm, d), idx3),
                  pl.BlockSpec((1, N_MOD, d), lambda bi, ti: (bi, 0, 0)),
                  pl.BlockSpec((1, d), const2),
                  pl.BlockSpec((d, LANES), const2),
                  pl.BlockSpec((1, LANES), const2)],
        out_specs=[pl.BlockSpec((1, tm, d), idx3),
                   pl.BlockSpec((1, tm, d), idx3),
                   pl.BlockSpec((1, tm, LANES), idx3)],
        out_shape=[jax.ShapeDtypeStruct((b, s, d), F32),
                   jax.ShapeDtypeStruct((b, s, d), BF16),
                   jax.ShapeDtypeStruct((b, s, LANES), F32)],
        compiler_params=pltpu.CompilerParams(dimension_semantics=("parallel", "parallel"),
                                             vmem_limit_bytes=VMEM_LIMIT),
        name="out_proj_router",
    )(a_out, b_out, w_out, x, mod, g_ffn, w_router, b_router)


def _moe_kernel(h_ref, comb_ref, w1_ref, w3_ref, w2_ref, x1_ref, mod_ref, o_ref, acc_ref):
    g = pl.program_id(2)

    @pl.when(g == 0)
    def _():
        acc_ref[...] = jnp.zeros_like(acc_ref)

    h = h_ref[0]
    comb = comb_ref[0]
    lane = lax.broadcasted_iota(jnp.int32, comb.shape, 1)
    for e in range(EXPERTS_PER_GROUP):
        weight = jnp.sum(jnp.where(lane == g * EXPERTS_PER_GROUP + e, comb, 0.0), axis=-1, keepdims=True)
        h1 = jnp.dot(h, w1_ref[0, e], preferred_element_type=F32)
        h3 = jnp.dot(h, w3_ref[0, e], preferred_element_type=F32)
        hg = (h1 * jax.nn.sigmoid(h1)) * h3 * weight
        acc_ref[...] += jnp.dot(hg.astype(BF16), w2_ref[0, e], preferred_element_type=F32)

    @pl.when(g == N_GROUPS - 1)
    def _():
        gate_f = mod_ref[0][5:6, :]
        o_ref[0] = x1_ref[0] + gate_f * acc_ref[...]


def _moe_call(h2, comb, w1, w3, w2, x1, mod, *, tm):
    b, s, d = x1.shape
    f = w1.shape[-1]
    tok3 = lambda bi, ti, gi: (bi, ti, 0)
    return pl.pallas_call(
        _moe_kernel,
        grid=(b, s // tm, N_GROUPS),
        in_specs=[pl.BlockSpec((1, tm, d), tok3),
                  pl.BlockSpec((1, tm, LANES), tok3),
                  pl.BlockSpec((1, EXPERTS_PER_GROUP, d, f), lambda bi, ti, gi: (gi, 0, 0, 0)),
                  pl.BlockSpec((1, EXPERTS_PER_GROUP, d, f), lambda bi, ti, gi: (gi, 0, 0, 0)),
                  pl.BlockSpec((1, EXPERTS_PER_GROUP, f, d), lambda bi, ti, gi: (gi, 0, 0, 0)),
                  pl.BlockSpec((1, tm, d), tok3),
                  pl.BlockSpec((1, N_MOD, d), lambda bi, ti, gi: (bi, 0, 0))],
        out_specs=pl.BlockSpec((1, tm, d), tok3),
        out_shape=jax.ShapeDtypeStruct((b, s, d), F32),
        scratch_shapes=[pltpu.VMEM((tm, d), F32)],
        compiler_params=pltpu.CompilerParams(dimension_semantics=("parallel", "parallel", "arbitrary"),
                                             vmem_limit_bytes=VMEM_LIMIT),
        name="moe_experts",
    )(h2, comb, w1, w3, w2, x1, mod)


def _permute_w_in(w_in):
    d = w_in.shape[0]
    dqk = DIFF_HEADS * 2 * HEAD_DIM
    dvw = DIFF_HEADS * DIFF_VDIM
    sbw = SB_HEADS * HEAD_DIM
    cuts = [0, dqk, 2 * dqk, 2 * dqk + dvw, 2 * dqk + dvw + sbw, 2 * dqk + dvw + 2 * sbw]
    zeros = jnp.zeros((d, HEAD_DIM), w_in.dtype)
    blocks = []
    for base in (cuts[0], cuts[1]):
        for n in range(2 * DIFF_HEADS):
            blocks += [w_in[:, base + n * HEAD_DIM: base + (n + 1) * HEAD_DIM], zeros]
    blocks.append(w_in[:, cuts[2]:cuts[3]])
    sb_scale = 1.0 / math.sqrt(HEAD_DIM)
    for n in range(SB_HEADS):
        wq = w_in[:, cuts[3] + n * HEAD_DIM: cuts[3] + (n + 1) * HEAD_DIM] * sb_scale
        blocks += [wq, zeros] if n % 2 == 0 else [zeros, wq]
    blocks.append(w_in[:, cuts[4]:cuts[5]])
    blocks.append(w_in[:, cuts[5]:])
    return jnp.concatenate(blocks, axis=1).astype(BF16)


def _pad_lanes(v, width=LANES):
    v = v.reshape(1, -1)
    return jnp.pad(v, ((0, 0), (0, width - v.shape[1])))


def kernel(x, c, w_ada, b_ada, g_attn, w_in, q_norm_g, k_norm_g, lambda_q1, lambda_k1, lambda_q2, lambda_k2,
           diff_out_g, sb_out_g, w_out, g_ffn, w_group, b_group, w_erouter, b_expert, w1, w3, w2):
    b, s, d = x.shape
    depth = w_ada.shape[0]
    tile = min(512, s)
    sb_tile = min(MXU_COLS, s)
    slopes = tuple(2.0 ** (-8.0 * (n + 1) / DIFF_HEADS) for n in range(DIFF_HEADS))
    qk_scale = 1.0 / math.sqrt(HEAD_DIM)
    c_pad = jnp.pad(c, ((0, 8 - b), (0, 0)))

    for layer in range(depth):
        lambda_init = 0.8 - 0.6 * math.exp(-0.3 * layer)
        mod = _mod_call(c_pad, w_ada[layer], b_ada[layer].reshape(1, -1))[:b].reshape(b, N_MOD, d)

        dq, dk, dv, sq, sk, sv = _proj_call(
            x, mod, g_attn[layer].reshape(1, d), _permute_w_in(w_in[layer]),
            _pad_lanes(q_norm_g[layer] * qk_scale), _pad_lanes(k_norm_g[layer]),
            tm=tile, tk=tile, slopes=slopes)

        lam_p = jnp.stack([lambda_q1[layer], lambda_k1[layer], lambda_q2[layer], lambda_k2[layer]])
        a_out = _diff_call(lam_p, diff_out_g[layer].reshape(1, LANES), dq, dk, dv,
                           t=tile, slopes=slopes, lambda_init=lambda_init)
        b_out = _sb_call(jnp.tile(sb_out_g[layer], 2).reshape(1, LANES), sq, sk, sv, t=sb_tile)

        w_router = jnp.concatenate(
            [jnp.transpose(w_erouter[layer], (1, 0, 2)).reshape(d, N_EXPERTS), w_group[layer]], axis=1)
        w_router = jnp.pad(w_router, ((0, 0), (0, LANES - w_router.shape[1])))
        b_router = _pad_lanes(jnp.concatenate([b_expert[layer].reshape(-1), b_group[layer]]))
        x1, h2, comb = _outproj_call(a_out, b_out, w_out[layer].astype(BF16), x, mod,
                                     g_ffn[layer].reshape(1, d), w_router, b_router, tm=tile)

        x = _moe_call(h2, comb, w1[layer].astype(BF16), w3[layer].astype(BF16), w2[layer].astype(BF16),
                      x1, mod, tm=tile)
    return x
```

```python
import functools
import math

import jax
import jax.numpy as jnp
from jax import lax
from jax.experimental import pallas as pl
from jax.experimental.pallas import tpu as pltpu

HEAD_DIM = 64
DIFF_HEADS = 4
SB_HEADS = 8
DIFF_VDIM = 2 * HEAD_DIM
N_GROUPS = 4
EXPERTS_PER_GROUP = 8
N_EXPERTS = N_GROUPS * EXPERTS_PER_GROUP
D_EXPERT = 256
N_MOD = 6
EPS = 1e-6
LANES = 128
MXU_COLS = 256
NEG = -1e30
SB_DEAD_LOG = -104.0
VMEM_LIMIT = 56 * 1024 * 1024

F32 = jnp.float32
BF16 = jnp.bfloat16


def _nt_dot(a, b):
    return lax.dot_general(a, b, (((1,), (1,)), ((), ())), preferred_element_type=F32)


def _mod_kernel(c_ref, w_ref, b_ref, o_ref):
    c = c_ref[...]
    sc = c * jax.nn.sigmoid(c)
    o_ref[...] = jnp.dot(sc, w_ref[...], precision=lax.Precision.HIGHEST,
                         preferred_element_type=F32) + b_ref[...]


def _mod_call(c_pad, w_ada, b_ada):
    rows, d = c_pad.shape
    n = w_ada.shape[1]
    tn = 1536
    return pl.pallas_call(
        _mod_kernel,
        grid=(n // tn,),
        in_specs=[pl.BlockSpec((rows, d), lambda j: (0, 0)),
                  pl.BlockSpec((d, tn), lambda j: (0, j)),
                  pl.BlockSpec((1, tn), lambda j: (0, j))],
        out_specs=pl.BlockSpec((rows, tn), lambda j: (0, j)),
        out_shape=jax.ShapeDtypeStruct((rows, n), F32),
        compiler_params=pltpu.CompilerParams(dimension_semantics=("parallel",),
                                             vmem_limit_bytes=VMEM_LIMIT),
        name="adaln_mod",
    )(c_pad, w_ada, b_ada)


_PROJ_BLOCKS = 36


def _proj_kernel(x_ref, mod_ref, g_ref, w_ref, qg_ref, kg_ref,
                 dq_ref, dk_ref, dv_ref, sq_ref, sk_ref, sv_ref, *, tm, tk, slopes):
    x = x_ref[0]
    mod = mod_ref[0]
    shift, scale = mod[0:1, :], mod[1:2, :]
    ms = jnp.mean(x * x, axis=-1, keepdims=True)
    h = (x * lax.rsqrt(ms + EPS) * g_ref[...]) * (1.0 + scale) + shift
    hb = h.astype(BF16)

    lane = lax.broadcasted_iota(jnp.int32, (1, LANES), 1)
    lo_lane = (lane == HEAD_DIM).astype(F32)
    hi_lane = (lane == HEAD_DIM + 1).astype(F32)
    row = lax.broadcasted_iota(jnp.int32, (tm, 1), 0) + pl.program_id(1) * tm
    koff = row & (tk - 1)
    koff_lo = (koff & 255).astype(F32)
    koff_hi = (koff - (koff & 255)).astype(F32)
    koff_lanes = lo_lane * koff_lo + hi_lane * koff_hi

    for c in range(_PROJ_BLOCKS * LANES // MXU_COLS):
        pc = jnp.dot(hb, w_ref[:, c * MXU_COLS:(c + 1) * MXU_COLS], preferred_element_type=F32)
        for half in range(MXU_COLS // LANES):
            blk = c * (MXU_COLS // LANES) + half
            piece = pc[:, half * LANES:(half + 1) * LANES]
            if blk < 16:
                ss = jnp.sum(piece * piece, axis=-1, keepdims=True)
                normed = piece * lax.rsqrt(ss * (1.0 / HEAD_DIM) + EPS)
                if blk < 8:
                    dq_ref[0, blk] = (normed * qg_ref[...] + (lo_lane + hi_lane)).astype(BF16)
                else:
                    slope = slopes[(blk - 8) // 2]
                    dk_ref[0, blk - 8] = (normed * kg_ref[...] + koff_lanes * slope).astype(BF16)
            elif blk < 20:
                dv_ref[0, blk - 16] = piece.astype(BF16)
            elif blk < 28:
                sq_ref[0, blk - 20] = piece.astype(BF16)
            elif blk < 32:
                sk_ref[0, blk - 28] = piece.astype(BF16)
            else:
                sv_ref[0, blk - 32] = piece.astype(BF16)


def _proj_call(x, mod, g_attn, w_perm, qg_pad, kg_pad, *, tm, tk, slopes):
    b, s, d = x.shape
    n = w_perm.shape[1]
    nt = s // tm

    def hm(nh):
        return (jax.ShapeDtypeStruct((b, nh, s, LANES), BF16),
                pl.BlockSpec((1, nh, tm, LANES), lambda bi, ti: (bi, 0, ti, 0)))

    outs = [hm(8), hm(8), hm(4), hm(8), hm(4), hm(4)]
    return pl.pallas_call(
        functools.partial(_proj_kernel, tm=tm, tk=tk, slopes=slopes),
        grid=(b, nt),
        in_specs=[pl.BlockSpec((1, tm, d), lambda bi, ti: (bi, ti, 0)),
                  pl.BlockSpec((1, N_MOD, d), lambda bi, ti: (bi, 0, 0)),
                  pl.BlockSpec((1, d), lambda bi, ti: (0, 0)),
                  pl.BlockSpec((d, n), lambda bi, ti: (0, 0)),
                  pl.BlockSpec((1, LANES), lambda bi, ti: (0, 0)),
                  pl.BlockSpec((1, LANES), lambda bi, ti: (0, 0))],
        out_specs=[o[1] for o in outs],
        out_shape=[o[0] for o in outs],
        compiler_params=pltpu.CompilerParams(dimension_semantics=("parallel", "parallel"),
                                             vmem_limit_bytes=VMEM_LIMIT),
        name="in_proj",
    )(x, mod, g_attn, w_perm, qg_pad, kg_pad)


def _diff_kernel(lam_ref, g_ref, q_ref, k_ref, v_ref, o_ref, acc0_ref, acc1_ref, *, t, slopes, lambda_init):
    hd = pl.program_id(1)
    i = pl.program_id(2)
    hd_col = jnp.full((t, 1), hd, jnp.int32)
    slope = jnp.full((t, 1), slopes[-1], F32)
    for n in range(len(slopes) - 2, -1, -1):
        slope = jnp.where(hd_col == n, slopes[n], slope)

    q = (q_ref[0, 0], q_ref[0, 1])
    accs = (acc0_ref, acc1_ref)
    acc0_ref[...] = jnp.zeros_like(acc0_ref)
    acc1_ref[...] = jnp.zeros_like(acc1_ref)
    row = lax.broadcasted_iota(jnp.int32, (t, t), 0)
    col = lax.broadcasted_iota(jnp.int32, (t, t), 1)
    causal = col <= row

    def block(j, carry, masked):
        ks = pl.multiple_of(j * t, t)
        v = v_ref[0, 0, pl.ds(ks, t), :]
        off = slope * jnp.full((t, 1), j * t, jnp.int32).astype(F32)
        new = []
        for mp in range(2):
            m, l = carry[2 * mp], carry[2 * mp + 1]
            s = _nt_dot(q[mp], k_ref[0, mp, pl.ds(ks, t), :])
            if masked:
                s = jnp.where(causal, s, NEG)
            m_new = jnp.maximum(m, jnp.max(s, axis=-1, keepdims=True) + off)
            alpha = jnp.exp(m - m_new)
            p = jnp.exp(s - (m_new - off))
            l_new = alpha * l + jnp.sum(p, axis=-1, keepdims=True)
            accs[mp][...] = alpha * accs[mp][...] + jnp.dot(p.astype(BF16), v, preferred_element_type=F32)
            new += [m_new, l_new]
        return tuple(new)

    init = (jnp.full((t, 1), NEG, F32), jnp.zeros((t, 1), F32)) * 2
    carry = lax.fori_loop(0, i, lambda j, c: block(j, c, False), init)
    _, l0, _, l1 = block(i, carry, True)

    lp = lam_ref[...]
    lam = (jnp.exp(jnp.sum(lp[0:1] * lp[1:2], axis=-1, keepdims=True))
           - jnp.exp(jnp.sum(lp[2:3] * lp[3:4], axis=-1, keepdims=True)) + lambda_init)
    o = acc0_ref[...] / l0 - lam * (acc1_ref[...] / l1)
    ms = jnp.mean(o * o, axis=-1, keepdims=True)
    o_ref[0] = ((o * lax.rsqrt(ms + EPS) * g_ref[...]) * (1.0 - lambda_init)).astype(BF16)


def _diff_call(lam_p, g_out, dq, dk, dv, *, t, slopes, lambda_init):
    b, _, s, _ = dq.shape
    return pl.pallas_call(
        functools.partial(_diff_kernel, t=t, slopes=slopes, lambda_init=lambda_init),
        grid=(b, DIFF_HEADS, s // t),
        in_specs=[pl.BlockSpec((4, HEAD_DIM), lambda bi, hi, qi: (0, 0)),
                  pl.BlockSpec((1, LANES), lambda bi, hi, qi: (0, 0)),
                  pl.BlockSpec((1, 2, t, LANES), lambda bi, hi, qi: (bi, hi, qi, 0)),
                  pl.BlockSpec((1, 2, s, LANES), lambda bi, hi, qi: (bi, hi, 0, 0)),
                  pl.BlockSpec((1, 1, s, LANES), lambda bi, hi, qi: (bi, hi, 0, 0))],
        out_specs=pl.BlockSpec((1, t, LANES), lambda bi, hi, qi: (bi, qi, hi)),
        out_shape=jax.ShapeDtypeStruct((b, s, DIFF_HEADS * LANES), BF16),
        scratch_shapes=[pltpu.VMEM((t, LANES), F32), pltpu.VMEM((t, LANES), F32)],
        compiler_params=pltpu.CompilerParams(dimension_semantics=("parallel", "parallel", "parallel"),
                                             vmem_limit_bytes=VMEM_LIMIT),
        name="diff_attn",
    )(lam_p, g_out, dq, dk, dv)


def _sb_kernel(g_ref, q_ref, k_ref, v_ref, o_ref, acc0_ref, acc1_ref, *, t):
    i = pl.program_id(2)
    q = (q_ref[0, 0], q_ref[0, 1])
    accs = (acc0_ref, acc1_ref)
    acc0_ref[...] = jnp.zeros_like(acc0_ref)
    acc1_ref[...] = jnp.zeros_like(acc1_ref)
    row = lax.broadcasted_iota(jnp.int32, (t, t), 0)
    col = lax.broadcasted_iota(jnp.int32, (t, t), 1)
    past = col < row
    later = (row > col).astype(BF16)

    def block(j, carry, masked):
        ks = pl.multiple_of(j * t, t)
        k = k_ref[0, 0, pl.ds(ks, t), :]
        v = v_ref[0, 0, pl.ds(ks, t), :]
        new = []
        for hh in range(2):
            z = _nt_dot(q[hh], k)
            sp = jnp.maximum(z, 0.0) + jnp.log(1.0 + jnp.exp(-jnp.abs(z)))
            log_rem = -sp
            log_beta = z - sp
            if masked:
                log_rem = jnp.where(past, log_rem, 0.0)
            lr = log_rem.astype(BF16)
            suffix = jnp.dot(lr, later, preferred_element_type=F32)
            a = jnp.exp(log_beta + suffix + carry[hh])
            if masked:
                a = jnp.where(past, a, 0.0)
            accs[hh][...] += jnp.dot(a.astype(BF16), v, preferred_element_type=F32)
            new.append(carry[hh] + suffix[:, 0:1] + log_rem[:, 0:1])
        return tuple(new)

    carry = block(i, (jnp.zeros((t, 1), F32),) * 2, True)

    def live_cond(state):
        jj, live = state[0], state[1]
        return jnp.logical_and(jj < i, live > 0)

    def live_body(state):
        jj, _, c0, c1 = state
        live = (jnp.maximum(jnp.max(c0), jnp.max(c1)) > SB_DEAD_LOG).astype(jnp.int32)
        c0, c1 = block(i - 1 - jj, (c0, c1), False)
        return jj + 1, live, c0, c1

    lax.while_loop(live_cond, live_body, (jnp.int32(0), jnp.int32(1)) + carry)

    lane = lax.broadcasted_iota(jnp.int32, (1, LANES), 1)
    first = lane < HEAD_DIM
    o = jnp.where(first, acc0_ref[...], acc1_ref[...])
    sq = o * o
    s_first = jnp.sum(jnp.where(first, sq, 0.0), axis=-1, keepdims=True)
    s_all = jnp.sum(sq, axis=-1, keepdims=True)
    ms = jnp.where(first, s_first, s_all - s_first) * (1.0 / HEAD_DIM)
    o_ref[0] = (o * lax.rsqrt(ms + EPS) * g_ref[...]).astype(BF16)


def _sb_call(g_pair, sq, sk, sv, *, t):
    b, _, s, _ = sq.shape
    pairs = SB_HEADS // 2
    return pl.pallas_call(
        functools.partial(_sb_kernel, t=t),
        grid=(b, pairs, s // t),
        in_specs=[pl.BlockSpec((1, LANES), lambda bi, pi, qi: (0, 0)),
                  pl.BlockSpec((1, 2, t, LANES), lambda bi, pi, qi: (bi, pi, qi, 0)),
                  pl.BlockSpec((1, 1, s, LANES), lambda bi, pi, qi: (bi, pi, 0, 0)),
                  pl.BlockSpec((1, 1, s, LANES), lambda bi, pi, qi: (bi, pi, 0, 0))],
        out_specs=pl.BlockSpec((1, t, LANES), lambda bi, pi, qi: (bi, qi, pi)),
        out_shape=jax.ShapeDtypeStruct((b, s, pairs * LANES), BF16),
        scratch_shapes=[pltpu.VMEM((t, LANES), F32), pltpu.VMEM((t, LANES), F32)],
        compiler_params=pltpu.CompilerParams(dimension_semantics=("parallel", "parallel", "parallel"),
                                             vmem_limit_bytes=VMEM_LIMIT),
        name="sb_attn",
    )(g_pair, sq, sk, sv)


def _outproj_kernel(a_ref, b_ref, w_ref, x_ref, mod_ref, g_ref, wr_ref, br_ref,
                    x1_ref, h2_ref, comb_ref):
    half = a_ref.shape[-1]
    y = (jnp.dot(a_ref[0], w_ref[0:half, :], preferred_element_type=F32)
         + jnp.dot(b_ref[0], w_ref[half:, :], preferred_element_type=F32))
    mod = mod_ref[0]
    gate_a, shift, scale = mod[2:3, :], mod[3:4, :], mod[4:5, :]
    x1 = x_ref[0] + gate_a * y
    x1_ref[0] = x1
    ms = jnp.mean(x1 * x1, axis=-1, keepdims=True)
    h2 = (x1 * lax.rsqrt(ms + EPS) * g_ref[...]) * (1.0 + scale) + shift
    h2_ref[0] = h2.astype(BF16)

    logits = jnp.dot(h2, wr_ref[...], precision=lax.Precision.HIGHEST,
                     preferred_element_type=F32) + br_ref[...]
    lane = lax.broadcasted_iota(jnp.int32, logits.shape, 1).astype(F32)
    big = jnp.float32(LANES)

    def top(vals):
        mx = jnp.max(vals, axis=-1, keepdims=True)
        idx = jnp.min(jnp.where(vals == mx, lane, big), axis=-1, keepdims=True)
        return mx, idx

    is_group = (lane >= N_EXPERTS) & (lane < N_EXPERTS + N_GROUPS)
    g_logits = jnp.where(is_group, logits, NEG)
    g_max, g_idx = top(g_logits)
    gate_group = 1.0 / jnp.sum(jnp.where(is_group, jnp.exp(logits - g_max), 0.0), axis=-1, keepdims=True)
    g_sel = g_idx - N_EXPERTS
    in_group = (lane >= g_sel * EXPERTS_PER_GROUP) & (lane < (g_sel + 1.0) * EXPERTS_PER_GROUP)
    e_logits = jnp.where(in_group, logits, NEG)
    v1, i1 = top(e_logits)
    v2, i2 = top(jnp.where(lane == i1, NEG, e_logits))
    e2 = jnp.exp(v2 - v1)
    w_first = 1.0 / (1.0 + e2)
    w_second = e2 / (1.0 + e2)
    comb_ref[0] = gate_group * (jnp.where(lane == i1, w_first, 0.0) + jnp.where(lane == i2, w_second, 0.0))


def _outproj_call(a_out, b_out, w_out, x, mod, g_ffn, w_router, b_router, *, tm):
    b, s, d = x.shape
    half = a_out.shape[-1]
    idx3 = lambda bi, ti: (bi, ti, 0)
    const2 = lambda bi, ti: (0, 0)
    return pl.pallas_call(
        _outproj_kernel,
        grid=(b, s // tm),
        in_specs=[pl.BlockSpec((1, tm, half), idx3),
                  pl.BlockSpec((1, tm, half), idx3),
                  pl.BlockSpec((2 * half, d), const2),
                  pl.BlockSpec((1, tm, d), idx3),
                  pl.BlockSpec((1, N_MOD, d), lambda bi, ti: (bi, 0, 0)),
                  pl.BlockSpec((1, d), const2),
                  pl.BlockSpec((d, LANES), const2),
                  pl.BlockSpec((1, LANES), const2)],
        out_specs=[pl.BlockSpec((1, tm, d), idx3),
                   pl.BlockSpec((1, tm, d), idx3),
                   pl.BlockSpec((1, tm, LANES), idx3)],
        out_shape=[jax.ShapeDtypeStruct((b, s, d), F32),
                   jax.ShapeDtypeStruct((b, s, d), BF16),
                   jax.ShapeDtypeStruct((b, s, LANES), F32)],
        compiler_params=pltpu.CompilerParams(dimension_semantics=("parallel", "parallel"),
                                             vmem_limit_bytes=VMEM_LIMIT),
        name="out_proj_router",
    )(a_out, b_out, w_out, x, mod, g_ffn, w_router, b_router)


def _moe_kernel(h_ref, comb_ref, w1_ref, w3_ref, w2_ref, x1_ref, mod_ref, o_ref, acc_ref):
    g = pl.program_id(2)

    @pl.when(g == 0)
    def _():
        acc_ref[...] = jnp.zeros_like(acc_ref)

    h = h_ref[0]
    comb = comb_ref[0]
    lane = lax.broadcasted_iota(jnp.int32, comb.shape, 1)
    for e in range(EXPERTS_PER_GROUP):
        weight = jnp.sum(jnp.where(lane == g * EXPERTS_PER_GROUP + e, comb, 0.0), axis=-1, keepdims=True)
        h1 = jnp.dot(h, w1_ref[0, e], preferred_element_type=F32)
        h3 = jnp.dot(h, w3_ref[0, e], preferred_element_type=F32)
        hg = (h1 * jax.nn.sigmoid(h1)) * h3 * weight
        acc_ref[...] += jnp.dot(hg.astype(BF16), w2_ref[0, e], preferred_element_type=F32)

    @pl.when(g == N_GROUPS - 1)
    def _():
        gate_f = mod_ref[0][5:6, :]
        o_ref[0] = x1_ref[0] + gate_f * acc_ref[...]


def _moe_call(h2, comb, w1, w3, w2, x1, mod, *, tm):
    b, s, d = x1.shape
    f = w1.shape[-1]
    tok3 = lambda bi, ti, gi: (bi, ti, 0)
    return pl.pallas_call(
        _moe_kernel,
        grid=(b, s // tm, N_GROUPS),
        in_specs=[pl.BlockSpec((1, tm, d), tok3),
                  pl.BlockSpec((1, tm, LANES), tok3),
                  pl.BlockSpec((1, EXPERTS_PER_GROUP, d, f), lambda bi, ti, gi: (gi, 0, 0, 0)),
                  pl.BlockSpec((1, EXPERTS_PER_GROUP, d, f), lambda bi, ti, gi: (gi, 0, 0, 0)),
                  pl.BlockSpec((1, EXPERTS_PER_GROUP, f, d), lambda bi, ti, gi: (gi, 0, 0, 0)),
                  pl.BlockSpec((1, tm, d), tok3),
                  pl.BlockSpec((1, N_MOD, d), lambda bi, ti, gi: (bi, 0, 0))],
        out_specs=pl.BlockSpec((1, tm, d), tok3),
        out_shape=jax.ShapeDtypeStruct((b, s, d), F32),
        scratch_shapes=[pltpu.VMEM((tm, d), F32)],
        compiler_params=pltpu.CompilerParams(dimension_semantics=("parallel", "parallel", "arbitrary"),
                                             vmem_limit_bytes=VMEM_LIMIT),
        name="moe_experts",
    )(h2, comb, w1, w3, w2, x1, mod)


def _permute_w_in(w_in):
    d = w_in.shape[0]
    dqk = DIFF_HEADS * 2 * HEAD_DIM
    dvw = DIFF_HEADS * DIFF_VDIM
    sbw = SB_HEADS * HEAD_DIM
    cuts = [0, dqk, 2 * dqk, 2 * dqk + dvw, 2 * dqk + dvw + sbw, 2 * dqk + dvw + 2 * sbw]
    zeros = jnp.zeros((d, HEAD_DIM), w_in.dtype)
    blocks = []
    for base in (cuts[0], cuts[1]):
        for n in range(2 * DIFF_HEADS):
            blocks += [w_in[:, base + n * HEAD_DIM: base + (n + 1) * HEAD_DIM], zeros]
    blocks.append(w_in[:, cuts[2]:cuts[3]])
    sb_scale = 1.0 / math.sqrt(HEAD_DIM)
    for n in range(SB_HEADS):
        wq = w_in[:, cuts[3] + n * HEAD_DIM: cuts[3] + (n + 1) * HEAD_DIM] * sb_scale
        blocks += [wq, zeros] if n % 2 == 0 else [zeros, wq]
    blocks.append(w_in[:, cuts[4]:cuts[5]])
    blocks.append(w_in[:, cuts[5]:])
    return jnp.concatenate(blocks, axis=1).astype(BF16)


def _pad_lanes(v, width=LANES):
    v = v.reshape(1, -1)
    return jnp.pad(v, ((0, 0), (0, width - v.shape[1])))


def kernel(x, c, w_ada, b_ada, g_attn, w_in, q_norm_g, k_norm_g, lambda_q1, lambda_k1, lambda_q2, lambda_k2,
           diff_out_g, sb_out_g, w_out, g_ffn, w_group, b_group, w_erouter, b_expert, w1, w3, w2):
    b, s, d = x.shape
    depth = w_ada.shape[0]
    tile = min(512, s)
    sb_tile = min(MXU_COLS, s)
    slopes = tuple(2.0 ** (-8.0 * (n + 1) / DIFF_HEADS) for n in range(DIFF_HEADS))
    qk_scale = 1.0 / math.sqrt(HEAD_DIM)
    c_pad = jnp.pad(c, ((0, 8 - b), (0, 0)))

    for layer in range(depth):
        lambda_init = 0.8 - 0.6 * math.exp(-0.3 * layer)
        mod = _mod_call(c_pad, w_ada[layer], b_ada[layer].reshape(1, -1))[:b].reshape(b, N_MOD, d)

        dq, dk, dv, sq, sk, sv = _proj_call(
            x, mod, g_attn[layer].reshape(1, d), _permute_w_in(w_in[layer]),
            _pad_lanes(q_norm_g[layer] * qk_scale), _pad_lanes(k_norm_g[layer]),
            tm=tile, tk=tile, slopes=slopes)

        lam_p = jnp.stack([lambda_q1[layer], lambda_k1[layer], lambda_q2[layer], lambda_k2[layer]])
        a_out = _diff_call(lam_p, diff_out_g[layer].reshape(1, LANES), dq, dk, dv,
                           t=tile, slopes=slopes, lambda_init=lambda_init)
        b_out = _sb_call(jnp.tile(sb_out_g[layer], 2).reshape(1, LANES), sq, sk, sv, t=sb_tile)

        w_router = jnp.concatenate(
            [jnp.transpose(w_erouter[layer], (1, 0, 2)).reshape(d, N_EXPERTS), w_group[layer]], axis=1)
        w_router = jnp.pad(w_router, ((0, 0), (0, LANES - w_router.shape[1])))
        b_router = _pad_lanes(jnp.concatenate([b_expert[layer].reshape(-1), b_group[layer]]))
        x1, h2, comb = _outproj_call(a_out, b_out, w_out[layer].astype(BF16), x, mod,
                                     g_ffn[layer].reshape(1, d), w_router, b_router, tm=tile)

        x = _moe_call(h2, comb, w1[layer].astype(BF16), w3[layer].astype(BF16), w2[layer].astype(BF16),
                      x1, mod, tm=tile)
    return x
```

```python
import functools
import math

import jax
import jax.numpy as jnp
import ml_dtypes
import numpy as np
from jax import lax
from jax.experimental import pallas as pl
from jax.experimental.pallas import tpu as pltpu

HEAD_DIM = 64
DIFF_HEADS = 4
SB_HEADS = 8
DIFF_VDIM = 2 * HEAD_DIM
N_GROUPS = 4
EXPERTS_PER_GROUP = 8
N_EXPERTS = N_GROUPS * EXPERTS_PER_GROUP
D_EXPERT = 256
N_MOD = 6
EPS = 1e-6
LANES = 128
MXU_COLS = 256
NEG = -1e30
SB_PAIRS_PER_STEP = 4
F32_DEAD_LOG = -104.0
VMEM_LIMIT = 56 * 1024 * 1024

F32 = jnp.float32
BF16 = jnp.bfloat16

LOG2E = math.log2(math.e)
LOG2E_PARTS = 3
DIFF_FIXED_REF_MAX = 32.0


def _bf16_parts(value, n):
    parts, rest = [], value
    for _ in range(n):
        part = float(np.float32(rest).astype(ml_dtypes.bfloat16))
        parts.append(part)
        rest -= part
    return parts


def _nt_dot(a, b):
    return lax.dot_general(a, b, (((1,), (1,)), ((), ())), preferred_element_type=F32)


def _mod_kernel(c_ref, w_ref, b_ref, o_ref):
    c = c_ref[...]
    sc = c * jax.nn.sigmoid(c)
    o_ref[...] = jnp.dot(sc, w_ref[...], precision=lax.Precision.HIGHEST,
                         preferred_element_type=F32) + b_ref[...]


def _mod_call(c_pad, w_ada, b_ada):
    rows, d = c_pad.shape
    n = w_ada.shape[1]
    tn = 1536
    return pl.pallas_call(
        _mod_kernel,
        grid=(n // tn,),
        in_specs=[pl.BlockSpec((rows, d), lambda j: (0, 0)),
                  pl.BlockSpec((d, tn), lambda j: (0, j)),
                  pl.BlockSpec((1, tn), lambda j: (0, j))],
        out_specs=pl.BlockSpec((rows, tn), lambda j: (0, j)),
        out_shape=jax.ShapeDtypeStruct((rows, n), F32),
        compiler_params=pltpu.CompilerParams(dimension_semantics=("parallel",),
                                             vmem_limit_bytes=VMEM_LIMIT),
        name="adaln_mod",
    )(c_pad, w_ada, b_ada)


_PROJ_BLOCKS = 36


def _proj_kernel(x_ref, mod_ref, g_ref, w_ref, qg_ref, kg_ref, qaug_ref,
                 dq_ref, dk_ref, dv_ref, sq_ref, sk_ref, sv_ref, *, tm, tk, slopes):
    x = x_ref[0]
    mod = mod_ref[0]
    shift, scale = mod[0:1, :], mod[1:2, :]
    ms = jnp.mean(x * x, axis=-1, keepdims=True)
    h = (x * lax.rsqrt(ms + EPS) * g_ref[...]) * (1.0 + scale) + shift
    hb = h.astype(BF16)

    lane = lax.broadcasted_iota(jnp.int32, (1, LANES), 1)
    aug = (lane >= HEAD_DIM) & (lane < HEAD_DIM + 2 * LOG2E_PARTS)
    lo_lane = (aug & ((lane & 1) == 0)).astype(F32)
    hi_lane = (aug & ((lane & 1) == 1)).astype(F32)
    row = lax.broadcasted_iota(jnp.int32, (tm, 1), 0) + pl.program_id(1) * tm
    koff = row & (tk - 1)
    koff_lo = (koff & 255).astype(F32)
    koff_hi = (koff - (koff & 255)).astype(F32)
    koff_lanes = lo_lane * koff_lo + hi_lane * koff_hi

    for c in range(_PROJ_BLOCKS * LANES // MXU_COLS):
        pc = jnp.dot(hb, w_ref[:, c * MXU_COLS:(c + 1) * MXU_COLS], preferred_element_type=F32)
        for half in range(MXU_COLS // LANES):
            blk = c * (MXU_COLS // LANES) + half
            piece = pc[:, half * LANES:(half + 1) * LANES]
            if blk < 16:
                ss = jnp.sum(piece * piece, axis=-1, keepdims=True)
                normed = piece * lax.rsqrt(ss * (1.0 / HEAD_DIM) + EPS)
                if blk < 8:
                    dq_ref[0, blk] = (normed * qg_ref[...] + qaug_ref[...]).astype(BF16)
                else:
                    slope = slopes[(blk - 8) // 2]
                    dk_ref[0, blk - 8] = (normed * kg_ref[...] + koff_lanes * slope).astype(BF16)
            elif blk < 20:
                dv_ref[0, blk - 16] = piece.astype(BF16)
            elif blk < 28:
                sq_ref[0, blk - 20] = piece.astype(BF16)
            elif blk < 32:
                sk_ref[0, blk - 28] = piece.astype(BF16)
            else:
                sv_ref[0, blk - 32] = piece.astype(BF16)


def _proj_call(x, mod, g_attn, w_perm, qg_pad, kg_pad, qaug_row, *, tm, tk, slopes):
    b, s, d = x.shape
    n = w_perm.shape[1]
    nt = s // tm

    def hm(nh):
        return (jax.ShapeDtypeStruct((b, nh, s, LANES), BF16),
                pl.BlockSpec((1, nh, tm, LANES), lambda bi, ti: (bi, 0, ti, 0)))

    outs = [hm(8), hm(8), hm(4), hm(8), hm(4), hm(4)]
    return pl.pallas_call(
        functools.partial(_proj_kernel, tm=tm, tk=tk, slopes=slopes),
        grid=(b, nt),
        in_specs=[pl.BlockSpec((1, tm, d), lambda bi, ti: (bi, ti, 0)),
                  pl.BlockSpec((1, N_MOD, d), lambda bi, ti: (bi, 0, 0)),
                  pl.BlockSpec((1, d), lambda bi, ti: (0, 0)),
                  pl.BlockSpec((d, n), lambda bi, ti: (0, 0)),
                  pl.BlockSpec((1, LANES), lambda bi, ti: (0, 0)),
                  pl.BlockSpec((1, LANES), lambda bi, ti: (0, 0)),
                  pl.BlockSpec((1, LANES), lambda bi, ti: (0, 0))],
        out_specs=[o[1] for o in outs],
        out_shape=[o[0] for o in outs],
        compiler_params=pltpu.CompilerParams(dimension_semantics=("parallel", "parallel"),
                                             vmem_limit_bytes=VMEM_LIMIT),
        name="in_proj",
    )(x, mod, g_attn, w_perm, qg_pad, kg_pad, qaug_row)


def _diff_kernel(nback_ref, lam_ref, g_ref, bound_ref, q_ref, k_ref, v_ref, o_ref, acc0_ref, acc1_ref,
                 *, t, slopes, lambda_init, fixed_ref):
    hd = pl.program_id(1)
    i = pl.program_id(2)
    hd_col = jnp.full((t, 1), hd, jnp.int32)
    slope = jnp.full((t, 1), slopes[-1] * LOG2E, F32)
    for n in range(len(slopes) - 2, -1, -1):
        slope = jnp.where(hd_col == n, slopes[n] * LOG2E, slope)

    q = (q_ref[0, 0], q_ref[0, 1])
    accs = (acc0_ref, acc1_ref)
    acc0_ref[...] = jnp.zeros_like(acc0_ref)
    acc1_ref[...] = jnp.zeros_like(acc1_ref)
    row = lax.broadcasted_iota(jnp.int32, (t, t), 0)
    col = lax.broadcasted_iota(jnp.int32, (t, t), 1)
    causal = col <= row
    ones_col = (lax.broadcasted_iota(jnp.int32, (t, LANES), 1) == 0).astype(BF16)
    row_pos = lax.broadcasted_iota(jnp.int32, (t, 1), 0) + i * t

    def block(j, carry, masked):
        ks = pl.multiple_of(j * t, t)
        v_aug = jnp.concatenate([v_ref[0, 0, pl.ds(ks, t), :], ones_col], axis=1)
        new = []
        for mp in range(2):
            s = _nt_dot(q[mp], k_ref[0, mp, pl.ds(ks, t), :])
            if masked:
                s = jnp.where(causal, s, NEG)
            if fixed_ref:
                ref = slope * (row_pos - j * t).astype(F32) + bound_ref[:, 0:1]
                p = jnp.exp2(s - ref)
                accs[mp][...] += jnp.dot(p.astype(BF16), v_aug, preferred_element_type=F32)
            else:
                off = slope * jnp.full((t, 1), j * t, jnp.int32).astype(F32)
                m = carry[mp]
                m_new = jnp.maximum(m, jnp.max(s, axis=-1, keepdims=True) + off)
                p = jnp.exp2(s - (m_new - off))
                accs[mp][...] = (jnp.exp2(m - m_new) * accs[mp][...]
                                 + jnp.dot(p.astype(BF16), v_aug, preferred_element_type=F32))
                new.append(m_new)
        return tuple(new)

    first = jnp.maximum(i + 1 - nback_ref[hd], 0)
    n_full = i - first

    def pair(pp, carry):
        j = first + 2 * pp
        return block(j + 1, block(j, carry, False), False)

    init = () if fixed_ref else (jnp.full((t, 1), NEG, F32),) * 2
    carry = lax.fori_loop(0, n_full // 2, pair, init)
    carry = lax.fori_loop(0, n_full % 2, lambda _, c: block(i - 1, c, False), carry)
    block(i, carry, True)

    lp = lam_ref[...]
    lam = (jnp.exp(jnp.sum(lp[0:1] * lp[1:2], axis=-1, keepdims=True))
           - jnp.exp(jnp.sum(lp[2:3] * lp[3:4], axis=-1, keepdims=True)) + lambda_init)
    o = (acc0_ref[:, 0:LANES] / acc0_ref[:, LANES:LANES + 1]
         - lam * (acc1_ref[:, 0:LANES] / acc1_ref[:, LANES:LANES + 1]))
    ms = jnp.mean(o * o, axis=-1, keepdims=True)
    o_ref[0] = ((o * lax.rsqrt(ms + EPS) * g_ref[...]) * (1.0 - lambda_init)).astype(BF16)


def _diff_call(n_back, lam_p, g_out, bound_row, dq, dk, dv, *, t, slopes, lambda_init, fixed_ref):
    b, _, s, _ = dq.shape
    const2 = lambda bi, hi, qi: (0, 0)
    return pl.pallas_call(
        functools.partial(_diff_kernel, t=t, slopes=slopes, lambda_init=lambda_init, fixed_ref=fixed_ref),
        grid=(b, DIFF_HEADS, s // t),
        in_specs=[pl.BlockSpec(memory_space=pltpu.SMEM),
                  pl.BlockSpec((4, HEAD_DIM), const2),
                  pl.BlockSpec((1, LANES), const2),
                  pl.BlockSpec((1, LANES), const2),
                  pl.BlockSpec((1, 2, t, LANES), lambda bi, hi, qi: (bi, hi, qi, 0)),
                  pl.BlockSpec((1, 2, s, LANES), lambda bi, hi, qi: (bi, hi, 0, 0)),
                  pl.BlockSpec((1, 1, s, LANES), lambda bi, hi, qi: (bi, hi, 0, 0))],
        out_specs=pl.BlockSpec((1, t, LANES), lambda bi, hi, qi: (bi, qi, hi)),
        out_shape=jax.ShapeDtypeStruct((b, s, DIFF_HEADS * LANES), BF16),
        scratch_shapes=[pltpu.VMEM((t, 2 * LANES), F32), pltpu.VMEM((t, 2 * LANES), F32)],
        compiler_params=pltpu.CompilerParams(dimension_semantics=("parallel", "parallel", "parallel"),
                                             vmem_limit_bytes=VMEM_LIMIT),
        name="diff_attn_fixed_ref" if fixed_ref else "diff_attn_running_max",
    )(n_back, lam_p, g_out, bound_row, dq, dk, dv)


def _sb_kernel(g_ref, q_ref, k_ref, v_ref, o_ref, acc_ref, *, t, pairs):
    i = pl.program_id(2)
    heads = 2 * pairs
    acc_ref[...] = jnp.zeros_like(acc_ref)
    row = lax.broadcasted_iota(jnp.int32, (t, t), 0)
    col = lax.broadcasted_iota(jnp.int32, (t, t), 1)
    past = col < row
    later = (row > col).astype(BF16)

    def block(j, carry, masked):
        ks = pl.multiple_of(j * t, t)
        new = []
        for hh in range(heads):
            k = k_ref[0, hh // 2, pl.ds(ks, t), :]
            v = v_ref[0, hh // 2, pl.ds(ks, t), :]
            z = _nt_dot(q_ref[0, hh], k)
            neg_abs = -jnp.abs(z)
            log_rem = 0.5 * (neg_abs - z) - jnp.log2(1.0 + jnp.exp2(neg_abs))
            log_beta = log_rem + z
            if masked:
                log_rem = jnp.where(past, log_rem, 0.0)
            new.append(carry[hh] + jnp.sum(log_rem, axis=-1, keepdims=True))
            suffix = jnp.dot(log_rem.astype(BF16), later, preferred_element_type=F32)
            a = jnp.exp2(log_beta + suffix + carry[hh])
            if masked:
                a = jnp.where(past, a, 0.0)
            acc_ref[hh] += jnp.dot(a.astype(BF16), v, preferred_element_type=F32)
        return tuple(new)

    def any_live(carry):
        top = functools.reduce(jnp.maximum, carry)
        return (jnp.max(top) > LOG2E * F32_DEAD_LOG).astype(jnp.int32)

    def live_cond(state):
        jj, live = state[0], state[1]
        return jnp.logical_and(jj < i, live > 0)

    def live_body(state):
        carry = block(i - 1 - state[0], state[2:], False)
        return (state[0] + 1, any_live(carry)) + carry

    carry = block(i, (jnp.zeros((t, 1), F32),) * heads, True)
    lax.while_loop(live_cond, live_body, (jnp.int32(0), any_live(carry)) + carry)

    lane = lax.broadcasted_iota(jnp.int32, (1, LANES), 1)
    first = lane < HEAD_DIM
    for pr in range(pairs):
        o = jnp.where(first, acc_ref[2 * pr], acc_ref[2 * pr + 1])
        sq = o * o
        s_first = jnp.sum(jnp.where(first, sq, 0.0), axis=-1, keepdims=True)
        s_all = jnp.sum(sq, axis=-1, keepdims=True)
        ms = jnp.where(first, s_first, s_all - s_first) * (1.0 / HEAD_DIM)
        o_ref[0, :, pr * LANES:(pr + 1) * LANES] = (o * lax.rsqrt(ms + EPS) * g_ref[...]).astype(BF16)


def _sb_call(g_pair, sq, sk, sv, *, t, pairs):
    b, _, s, _ = sq.shape
    groups = SB_HEADS // 2 // pairs
    return pl.pallas_call(
        functools.partial(_sb_kernel, t=t, pairs=pairs),
        grid=(b, groups, s // t),
        in_specs=[pl.BlockSpec((1, LANES), lambda bi, gi, qi: (0, 0)),
                  pl.BlockSpec((1, 2 * pairs, t, LANES), lambda bi, gi, qi: (bi, gi, qi, 0)),
                  pl.BlockSpec((1, pairs, s, LANES), lambda bi, gi, qi: (bi, gi, 0, 0)),
                  pl.BlockSpec((1, pairs, s, LANES), lambda bi, gi, qi: (bi, gi, 0, 0))],
        out_specs=pl.BlockSpec((1, t, pairs * LANES), lambda bi, gi, qi: (bi, qi, gi)),
        out_shape=jax.ShapeDtypeStruct((b, s, (SB_HEADS // 2) * LANES), BF16),
        scratch_shapes=[pltpu.VMEM((2 * pairs, t, LANES), F32)],
        compiler_params=pltpu.CompilerParams(dimension_semantics=("parallel", "parallel", "parallel"),
                                             vmem_limit_bytes=VMEM_LIMIT),
        name="sb_attn",
    )(g_pair, sq, sk, sv)


def _outproj_kernel(a_ref, b_ref, w_ref, x_ref, mod_ref, g_ref, wr_ref, br_ref,
                    x1_ref, h2_ref, comb_ref):
    half = a_ref.shape[-1]
    y = (jnp.dot(a_ref[0], w_ref[0:half, :], preferred_element_type=F32)
         + jnp.dot(b_ref[0], w_ref[half:, :], preferred_element_type=F32))
    mod = mod_ref[0]
    gate_a, shift, scale = mod[2:3, :], mod[3:4, :], mod[4:5, :]
    x1 = x_ref[0] + gate_a * y
    x1_ref[0] = x1
    ms = jnp.mean(x1 * x1, axis=-1, keepdims=True)
    h2 = (x1 * lax.rsqrt(ms + EPS) * g_ref[...]) * (1.0 + scale) + shift
    h2_ref[0] = h2.astype(BF16)

    logits = jnp.dot(h2, wr_ref[...], precision=lax.Precision.HIGHEST,
                     preferred_element_type=F32) + br_ref[...]
    lane = lax.broadcasted_iota(jnp.int32, logits.shape, 1).astype(F32)
    big = jnp.float32(LANES)

    def top(vals):
        mx = jnp.max(vals, axis=-1, keepdims=True)
        idx = jnp.min(jnp.where(vals == mx, lane, big), axis=-1, keepdims=True)
        return mx, idx

    is_group = (lane >= N_EXPERTS) & (lane < N_EXPERTS + N_GROUPS)
    g_logits = jnp.where(is_group, logits, NEG)
    g_max, g_idx = top(g_logits)
    gate_group = 1.0 / jnp.sum(jnp.where(is_group, jnp.exp(logits - g_max), 0.0), axis=-1, keepdims=True)
    g_sel = g_idx - N_EXPERTS
    in_group = (lane >= g_sel * EXPERTS_PER_GROUP) & (lane < (g_sel + 1.0) * EXPERTS_PER_GROUP)
    e_logits = jnp.where(in_group, logits, NEG)
    v1, i1 = top(e_logits)
    v2, i2 = top(jnp.where(lane == i1, NEG, e_logits))
    e2 = jnp.exp(v2 - v1)
    w_first = 1.0 / (1.0 + e2)
    w_second = e2 / (1.0 + e2)
    comb_ref[0] = gate_group * (jnp.where(lane == i1, w_first, 0.0) + jnp.where(lane == i2, w_second, 0.0))


def _outproj_call(a_out, b_out, w_out, x, mod, g_ffn, w_router, b_router, *, tm):
    b, s, d = x.shape
    half = a_out.shape[-1]
    idx3 = lambda bi, ti: (bi, ti, 0)
    const2 = lambda bi, ti: (0, 0)
    return pl.pallas_call(
        _outproj_kernel,
        grid=(b, s // tm),
        in_specs=[pl.BlockSpec((1, tm, half), idx3),
                  pl.BlockSpec((1, tm, half), idx3),
                  pl.BlockSpec((2 * half, d), const2),
                  pl.BlockSpec((1, tm, d), idx3),
                  pl.BlockSpec((1, N_MOD, d), lambda bi, ti: (bi, 0, 0)),
                  pl.BlockSpec((1, d), const2),
                  pl.BlockSpec((d, LANES), const2),
                  pl.BlockSpec((1, LANES), const2)],
        out_specs=[pl.BlockSpec((1, tm, d), idx3),
                   pl.BlockSpec((1, tm, d), idx3),
                   pl.BlockSpec((1, tm, LANES), idx3)],
        out_shape=[jax.ShapeDtypeStruct((b, s, d), F32),
                   jax.ShapeDtypeStruct((b, s, d), BF16),
                   jax.ShapeDtypeStruct((b, s, LANES), F32)],
        compiler_params=pltpu.CompilerParams(dimension_semantics=("parallel", "parallel"),
                                             vmem_limit_bytes=VMEM_LIMIT),
        name="out_proj_router",
    )(a_out, b_out, w_out, x, mod, g_ffn, w_router, b_router)


def _moe_kernel(h_ref, comb_ref, w1_ref, w3_ref, w2_ref, x1_ref, mod_ref, o_ref, acc_ref):
    g = pl.program_id(2)

    @pl.when(g == 0)
    def _():
        acc_ref[...] = jnp.zeros_like(acc_ref)

    h = h_ref[0]
    comb = comb_ref[0]
    lane = lax.broadcasted_iota(jnp.int32, comb.shape, 1)
    for e in range(EXPERTS_PER_GROUP):
        weight = jnp.sum(jnp.where(lane == g * EXPERTS_PER_GROUP + e, comb, 0.0), axis=-1, keepdims=True)
        h1 = jnp.dot(h, w1_ref[0, e], preferred_element_type=F32)
        h3 = jnp.dot(h, w3_ref[0, e], preferred_element_type=F32)
        hg = (h1 * jax.nn.sigmoid(h1)) * h3 * weight
        acc_ref[...] += jnp.dot(hg.astype(BF16), w2_ref[0, e], preferred_element_type=F32)

    @pl.when(g == N_GROUPS - 1)
    def _():
        gate_f = mod_ref[0][5:6, :]
        o_ref[0] = x1_ref[0] + gate_f * acc_ref[...]


def _moe_call(h2, comb, w1, w3, w2, x1, mod, *, tm):
    b, s, d = x1.shape
    f = w1.shape[-1]
    tok3 = lambda bi, ti, gi: (bi, ti, 0)
    return pl.pallas_call(
        _moe_kernel,
        grid=(b, s // tm, N_GROUPS),
        in_specs=[pl.BlockSpec((1, tm, d), tok3),
                  pl.BlockSpec((1, tm, LANES), tok3),
                  pl.BlockSpec((1, EXPERTS_PER_GROUP, d, f), lambda bi, ti, gi: (gi, 0, 0, 0)),
                  pl.BlockSpec((1, EXPERTS_PER_GROUP, d, f), lambda bi, ti, gi: (gi, 0, 0, 0)),
                  pl.BlockSpec((1, EXPERTS_PER_GROUP, f, d), lambda bi, ti, gi: (gi, 0, 0, 0)),
                  pl.BlockSpec((1, tm, d), tok3),
                  pl.BlockSpec((1, N_MOD, d), lambda bi, ti, gi: (bi, 0, 0))],
        out_specs=pl.BlockSpec((1, tm, d), tok3),
        out_shape=jax.ShapeDtypeStruct((b, s, d), F32),
        scratch_shapes=[pltpu.VMEM((tm, d), F32)],
        compiler_params=pltpu.CompilerParams(dimension_semantics=("parallel", "parallel", "arbitrary"),
                                             vmem_limit_bytes=VMEM_LIMIT),
        name="moe_experts",
    )(h2, comb, w1, w3, w2, x1, mod)


def _permute_w_in(w_in):
    d = w_in.shape[0]
    dqk = DIFF_HEADS * 2 * HEAD_DIM
    dvw = DIFF_HEADS * DIFF_VDIM
    sbw = SB_HEADS * HEAD_DIM
    cuts = [0, dqk, 2 * dqk, 2 * dqk + dvw, 2 * dqk + dvw + sbw, 2 * dqk + dvw + 2 * sbw]
    zeros = jnp.zeros((d, HEAD_DIM), w_in.dtype)
    blocks = []
    for base in (cuts[0], cuts[1]):
        for n in range(2 * DIFF_HEADS):
            blocks += [w_in[:, base + n * HEAD_DIM: base + (n + 1) * HEAD_DIM], zeros]
    blocks.append(w_in[:, cuts[2]:cuts[3]])
    sb_scale = LOG2E / math.sqrt(HEAD_DIM)
    for n in range(SB_HEADS):
        wq = w_in[:, cuts[3] + n * HEAD_DIM: cuts[3] + (n + 1) * HEAD_DIM] * sb_scale
        blocks += [wq, zeros] if n % 2 == 0 else [zeros, wq]
    blocks.append(w_in[:, cuts[4]:cuts[5]])
    blocks.append(w_in[:, cuts[5]:])
    return jnp.concatenate(blocks, axis=1).astype(BF16)


def _pad_lanes(v, width=LANES):
    v = v.reshape(1, -1)
    return jnp.pad(v, ((0, 0), (0, width - v.shape[1])))


def kernel(x, c, w_ada, b_ada, g_attn, w_in, q_norm_g, k_norm_g, lambda_q1, lambda_k1, lambda_q2, lambda_k2,
           diff_out_g, sb_out_g, w_out, g_ffn, w_group, b_group, w_erouter, b_expert, w1, w3, w2):
    b, s, d = x.shape
    depth = w_ada.shape[0]
    tile = min(512, s)
    sb_tile = min(MXU_COLS, s)
    slopes = tuple(2.0 ** (-8.0 * (n + 1) / DIFF_HEADS) for n in range(DIFF_HEADS))
    qk_scale = 1.0 / math.sqrt(HEAD_DIM)
    c_pad = jnp.pad(c, ((0, 8 - b), (0, 0)))
    qaug = np.zeros((1, LANES), np.float32)
    qaug[0, HEAD_DIM:HEAD_DIM + 2 * LOG2E_PARTS] = np.repeat(_bf16_parts(LOG2E, LOG2E_PARTS), 2)
    qaug_row = jnp.asarray(qaug)

    for layer in range(depth):
        lambda_init = 0.8 - 0.6 * math.exp(-0.3 * layer)
        mod = _mod_call(c_pad, w_ada[layer], b_ada[layer].reshape(1, -1))[:b].reshape(b, N_MOD, d)

        dq, dk, dv, sq, sk, sv = _proj_call(
            x, mod, g_attn[layer].reshape(1, d), _permute_w_in(w_in[layer]),
            _pad_lanes(q_norm_g[layer] * (qk_scale * LOG2E)), _pad_lanes(k_norm_g[layer]), qaug_row,
            tm=tile, tk=tile, slopes=slopes)

        lam_p = jnp.stack([lambda_q1[layer], lambda_k1[layer], lambda_q2[layer], lambda_k2[layer]])
        score_bound = (1.02 * HEAD_DIM * qk_scale) * jnp.max(jnp.abs(q_norm_g[layer])) * jnp.max(jnp.abs(k_norm_g[layer]))
        dead_dist = (2.0 * score_bound - F32_DEAD_LOG) / jnp.asarray(slopes, F32)
        n_back = jnp.clip(jnp.floor(dead_dist / tile) + 2.0, 1.0, s // tile).astype(jnp.int32)
        diff_args = (n_back, lam_p, diff_out_g[layer].reshape(1, LANES),
                     jnp.full((1, LANES), score_bound * LOG2E, F32), dq, dk, dv)
        diff_kw = dict(t=tile, slopes=slopes, lambda_init=lambda_init)
        a_out = lax.cond(score_bound <= DIFF_FIXED_REF_MAX,
                         lambda args: _diff_call(*args, fixed_ref=True, **diff_kw),
                         lambda args: _diff_call(*args, fixed_ref=False, **diff_kw),
                         diff_args)
        b_out = _sb_call(jnp.tile(sb_out_g[layer], 2).reshape(1, LANES), sq, sk, sv, t=sb_tile,
                         pairs=SB_PAIRS_PER_STEP)

        w_router = jnp.concatenate(
            [jnp.transpose(w_erouter[layer], (1, 0, 2)).reshape(d, N_EXPERTS), w_group[layer]], axis=1)
        w_router = jnp.pad(w_router, ((0, 0), (0, LANES - w_router.shape[1])))
        b_router = _pad_lanes(jnp.concatenate([b_expert[layer].reshape(-1), b_group[layer]]))
        x1, h2, comb = _outproj_call(a_out, b_out, w_out[layer].astype(BF16), x, mod,
                                     g_ffn[layer].reshape(1, d), w_router, b_router, tm=tile)

        x = _moe_call(h2, comb, w1[layer].astype(BF16), w3[layer].astype(BF16), w2[layer].astype(BF16),
                      x1, mod, tm=tile)
    return x
```

```python
import functools
import math

import jax
import jax.numpy as jnp
import ml_dtypes
import numpy as np
from jax import lax
from jax.experimental import pallas as pl
from jax.experimental.pallas import tpu as pltpu

HEAD_DIM = 64
DIFF_HEADS = 4
SB_HEADS = 8
DIFF_VDIM = 2 * HEAD_DIM
N_GROUPS = 4
EXPERTS_PER_GROUP = 8
N_EXPERTS = N_GROUPS * EXPERTS_PER_GROUP
D_EXPERT = 256
N_MOD = 6
EPS = 1e-6
LANES = 128
MXU_COLS = 256
NEG = -1e30
SB_PAIRS_PER_STEP = 4
F32_DEAD_LOG = -104.0
VMEM_LIMIT = 56 * 1024 * 1024

F32 = jnp.float32
BF16 = jnp.bfloat16

LOG2E = math.log2(math.e)
LOG2E_PARTS = 3
DIFF_FIXED_REF_MAX = 32.0


def _bf16_parts(value, n):
    parts, rest = [], value
    for _ in range(n):
        part = float(np.float32(rest).astype(ml_dtypes.bfloat16))
        parts.append(part)
        rest -= part
    return parts


def _nt_dot(a, b):
    return lax.dot_general(a, b, (((1,), (1,)), ((), ())), preferred_element_type=F32)


def _mod_kernel(c_ref, w_ref, b_ref, o_ref):
    c = c_ref[...]
    sc = c * jax.nn.sigmoid(c)
    o_ref[...] = jnp.dot(sc, w_ref[...], precision=lax.Precision.HIGHEST,
                         preferred_element_type=F32) + b_ref[...]


def _mod_call(c_pad, w_ada, b_ada):
    rows, d = c_pad.shape
    n = w_ada.shape[1]
    tn = 1536
    return pl.pallas_call(
        _mod_kernel,
        grid=(n // tn,),
        in_specs=[pl.BlockSpec((rows, d), lambda j: (0, 0)),
                  pl.BlockSpec((d, tn), lambda j: (0, j)),
                  pl.BlockSpec((1, tn), lambda j: (0, j))],
        out_specs=pl.BlockSpec((rows, tn), lambda j: (0, j)),
        out_shape=jax.ShapeDtypeStruct((rows, n), F32),
        compiler_params=pltpu.CompilerParams(dimension_semantics=("parallel",),
                                             vmem_limit_bytes=VMEM_LIMIT),
        name="adaln_mod",
    )(c_pad, w_ada, b_ada)


_PROJ_BLOCKS = 36


def _proj_kernel(x_ref, mod_ref, g_ref, w_ref, qg_ref, kg_ref, qaug_ref,
                 dq_ref, dk_ref, dv_ref, sq_ref, sk_ref, sv_ref, *, tm, tk, slopes):
    x = x_ref[0]
    mod = mod_ref[0]
    shift, scale = mod[0:1, :], mod[1:2, :]
    ms = jnp.mean(x * x, axis=-1, keepdims=True)
    h = (x * lax.rsqrt(ms + EPS) * g_ref[...]) * (1.0 + scale) + shift
    hb = h.astype(BF16)

    lane = lax.broadcasted_iota(jnp.int32, (1, LANES), 1)
    aug = (lane >= HEAD_DIM) & (lane < HEAD_DIM + 2 * LOG2E_PARTS)
    lo_lane = (aug & ((lane & 1) == 0)).astype(F32)
    hi_lane = (aug & ((lane & 1) == 1)).astype(F32)
    row = lax.broadcasted_iota(jnp.int32, (tm, 1), 0) + pl.program_id(1) * tm
    koff = row & (tk - 1)
    koff_lo = (koff & 255).astype(F32)
    koff_hi = (koff - (koff & 255)).astype(F32)
    koff_lanes = lo_lane * koff_lo + hi_lane * koff_hi

    for c in range(_PROJ_BLOCKS * LANES // MXU_COLS):
        pc = jnp.dot(hb, w_ref[:, c * MXU_COLS:(c + 1) * MXU_COLS], preferred_element_type=F32)
        for half in range(MXU_COLS // LANES):
            blk = c * (MXU_COLS // LANES) + half
            piece = pc[:, half * LANES:(half + 1) * LANES]
            if blk < 16:
                ss = jnp.sum(piece * piece, axis=-1, keepdims=True)
                normed = piece * lax.rsqrt(ss * (1.0 / HEAD_DIM) + EPS)
                if blk < 8:
                    dq_ref[0, blk] = (normed * qg_ref[...] + qaug_ref[...]).astype(BF16)
                else:
                    slope = slopes[(blk - 8) // 2]
                    dk_ref[0, blk - 8] = (normed * kg_ref[...] + koff_lanes * slope).astype(BF16)
            elif blk < 20:
                dv_ref[0, blk - 16] = piece.astype(BF16)
            elif blk < 28:
                sq_ref[0, blk - 20] = piece.astype(BF16)
            elif blk < 32:
                sk_ref[0, blk - 28] = piece.astype(BF16)
            else:
                sv_ref[0, blk - 32] = piece.astype(BF16)


def _proj_call(x, mod, g_attn, w_perm, qg_pad, kg_pad, qaug_row, *, tm, tk, slopes):
    b, s, d = x.shape
    n = w_perm.shape[1]
    nt = s // tm

    def hm(nh):
        return (jax.ShapeDtypeStruct((b, nh, s, LANES), BF16),
                pl.BlockSpec((1, nh, tm, LANES), lambda bi, ti: (bi, 0, ti, 0)))

    outs = [hm(8), hm(8), hm(4), hm(8), hm(4), hm(4)]
    return pl.pallas_call(
        functools.partial(_proj_kernel, tm=tm, tk=tk, slopes=slopes),
        grid=(b, nt),
        in_specs=[pl.BlockSpec((1, tm, d), lambda bi, ti: (bi, ti, 0)),
                  pl.BlockSpec((1, N_MOD, d), lambda bi, ti: (bi, 0, 0)),
                  pl.BlockSpec((1, d), lambda bi, ti: (0, 0)),
                  pl.BlockSpec((d, n), lambda bi, ti: (0, 0)),
                  pl.BlockSpec((1, LANES), lambda bi, ti: (0, 0)),
                  pl.BlockSpec((1, LANES), lambda bi, ti: (0, 0)),
                  pl.BlockSpec((1, LANES), lambda bi, ti: (0, 0))],
        out_specs=[o[1] for o in outs],
        out_shape=[o[0] for o in outs],
        compiler_params=pltpu.CompilerParams(dimension_semantics=("parallel", "parallel"),
                                             vmem_limit_bytes=VMEM_LIMIT),
        name="in_proj",
    )(x, mod, g_attn, w_perm, qg_pad, kg_pad, qaug_row)


def _diff_kernel(nback_ref, lam_ref, g_ref, bound_ref, q_ref, k_ref, v_ref, o_ref, acc0_ref, acc1_ref,
                 *, t, slopes, lambda_init, fixed_ref):
    hd = pl.program_id(1)
    i = pl.program_id(2)
    hd_col = jnp.full((t, 1), hd, jnp.int32)
    slope = jnp.full((t, 1), slopes[-1] * LOG2E, F32)
    for n in range(len(slopes) - 2, -1, -1):
        slope = jnp.where(hd_col == n, slopes[n] * LOG2E, slope)

    q = (q_ref[0, 0], q_ref[0, 1])
    accs = (acc0_ref, acc1_ref)
    acc0_ref[...] = jnp.zeros_like(acc0_ref)
    acc1_ref[...] = jnp.zeros_like(acc1_ref)
    row = lax.broadcasted_iota(jnp.int32, (t, t), 0)
    col = lax.broadcasted_iota(jnp.int32, (t, t), 1)
    causal = col <= row
    ones_col = (lax.broadcasted_iota(jnp.int32, (t, LANES), 1) == 0).astype(BF16)
    row_pos = lax.broadcasted_iota(jnp.int32, (t, 1), 0) + i * t

    def block(j, carry, masked):
        ks = pl.multiple_of(j * t, t)
        v_aug = jnp.concatenate([v_ref[0, 0, pl.ds(ks, t), :], ones_col], axis=1)
        new = []
        for mp in range(2):
            s = _nt_dot(q[mp], k_ref[0, mp, pl.ds(ks, t), :])
            if masked:
                s = jnp.where(causal, s, NEG)
            if fixed_ref:
                ref = slope * (row_pos - j * t).astype(F32) + bound_ref[:, 0:1]
                p = jnp.exp2(s - ref)
                accs[mp][...] += jnp.dot(p.astype(BF16), v_aug, preferred_element_type=F32)
            else:
                off = slope * jnp.full((t, 1), j * t, jnp.int32).astype(F32)
                m = carry[mp]
                m_new = jnp.maximum(m, jnp.max(s, axis=-1, keepdims=True) + off)
                p = jnp.exp2(s - (m_new - off))
                accs[mp][...] = (jnp.exp2(m - m_new) * accs[mp][...]
                                 + jnp.dot(p.astype(BF16), v_aug, preferred_element_type=F32))
                new.append(m_new)
        return tuple(new)

    first = jnp.maximum(i + 1 - nback_ref[hd], 0)
    n_full = i - first

    def pair(pp, carry):
        j = first + 2 * pp
        return block(j + 1, block(j, carry, False), False)

    init = () if fixed_ref else (jnp.full((t, 1), NEG, F32),) * 2
    carry = lax.fori_loop(0, n_full // 2, pair, init)
    carry = lax.fori_loop(0, n_full % 2, lambda _, c: block(i - 1, c, False), carry)
    block(i, carry, True)

    lp = lam_ref[...]
    lam = (jnp.exp(jnp.sum(lp[0:1] * lp[1:2], axis=-1, keepdims=True))
           - jnp.exp(jnp.sum(lp[2:3] * lp[3:4], axis=-1, keepdims=True)) + lambda_init)
    o = (acc0_ref[:, 0:LANES] / acc0_ref[:, LANES:LANES + 1]
         - lam * (acc1_ref[:, 0:LANES] / acc1_ref[:, LANES:LANES + 1]))
    ms = jnp.mean(o * o, axis=-1, keepdims=True)
    o_ref[0] = ((o * lax.rsqrt(ms + EPS) * g_ref[...]) * (1.0 - lambda_init)).astype(BF16)


def _diff_call(n_back, lam_p, g_out, bound_row, dq, dk, dv, *, t, slopes, lambda_init, fixed_ref):
    b, _, s, _ = dq.shape
    const2 = lambda bi, hi, qi: (0, 0)
    return pl.pallas_call(
        functools.partial(_diff_kernel, t=t, slopes=slopes, lambda_init=lambda_init, fixed_ref=fixed_ref),
        grid=(b, DIFF_HEADS, s // t),
        in_specs=[pl.BlockSpec(memory_space=pltpu.SMEM),
                  pl.BlockSpec((4, HEAD_DIM), const2),
                  pl.BlockSpec((1, LANES), const2),
                  pl.BlockSpec((1, LANES), const2),
                  pl.BlockSpec((1, 2, t, LANES), lambda bi, hi, qi: (bi, hi, qi, 0)),
                  pl.BlockSpec((1, 2, s, LANES), lambda bi, hi, qi: (bi, hi, 0, 0)),
                  pl.BlockSpec((1, 1, s, LANES), lambda bi, hi, qi: (bi, hi, 0, 0))],
        out_specs=pl.BlockSpec((1, t, LANES), lambda bi, hi, qi: (bi, qi, hi)),
        out_shape=jax.ShapeDtypeStruct((b, s, DIFF_HEADS * LANES), BF16),
        scratch_shapes=[pltpu.VMEM((t, 2 * LANES), F32), pltpu.VMEM((t, 2 * LANES), F32)],
        compiler_params=pltpu.CompilerParams(dimension_semantics=("parallel", "parallel", "parallel"),
                                             vmem_limit_bytes=VMEM_LIMIT),
        name="diff_attn_fixed_ref" if fixed_ref else "diff_attn_running_max",
    )(n_back, lam_p, g_out, bound_row, dq, dk, dv)


def _sb_kernel(g_ref, q_ref, k_ref, v_ref, o_ref, acc_ref, *, t, pairs):
    i = pl.program_id(2)
    heads = 2 * pairs
    acc_ref[...] = jnp.zeros_like(acc_ref)
    row = lax.broadcasted_iota(jnp.int32, (t, t), 0)
    col = lax.broadcasted_iota(jnp.int32, (t, t), 1)
    past = col < row
    later = (row > col).astype(BF16)

    def block(j, carry, masked):
        ks = pl.multiple_of(j * t, t)
        new = []
        for hh in range(heads):
            k = k_ref[0, hh // 2, pl.ds(ks, t), :]
            v = v_ref[0, hh // 2, pl.ds(ks, t), :]
            z = _nt_dot(q_ref[0, hh], k)
            neg_abs = -jnp.abs(z)
            log_rem = 0.5 * (neg_abs - z) - jnp.log2(1.0 + jnp.exp2(neg_abs))
            log_beta = log_rem + z
            if masked:
                log_rem = jnp.where(past, log_rem, 0.0)
            new.append(carry[hh] + jnp.sum(log_rem, axis=-1, keepdims=True))
            suffix = jnp.dot(log_rem.astype(BF16), later, preferred_element_type=F32)
            a = jnp.exp2(log_beta + suffix + carry[hh])
            if masked:
                a = jnp.where(past, a, 0.0)
            acc_ref[hh] += jnp.dot(a.astype(BF16), v, preferred_element_type=F32)
        return tuple(new)

    def any_live(carry):
        top = functools.reduce(jnp.maximum, carry)
        return (jnp.max(top) > LOG2E * F32_DEAD_LOG).astype(jnp.int32)

    def live_cond(state):
        jj, live = state[0], state[1]
        return jnp.logical_and(jj < i, live > 0)

    def live_body(state):
        carry = block(i - 1 - state[0], state[2:], False)
        return (state[0] + 1, any_live(carry)) + carry

    carry = block(i, (jnp.zeros((t, 1), F32),) * heads, True)
    lax.while_loop(live_cond, live_body, (jnp.int32(0), any_live(carry)) + carry)

    lane = lax.broadcasted_iota(jnp.int32, (1, LANES), 1)
    first = lane < HEAD_DIM
    for pr in range(pairs):
        o = jnp.where(first, acc_ref[2 * pr], acc_ref[2 * pr + 1])
        sq = o * o
        s_first = jnp.sum(jnp.where(first, sq, 0.0), axis=-1, keepdims=True)
        s_all = jnp.sum(sq, axis=-1, keepdims=True)
        ms = jnp.where(first, s_first, s_all - s_first) * (1.0 / HEAD_DIM)
        o_ref[0, :, pr * LANES:(pr + 1) * LANES] = (o * lax.rsqrt(ms + EPS) * g_ref[...]).astype(BF16)


def _sb_call(g_pair, sq, sk, sv, *, t, pairs):
    b, _, s, _ = sq.shape
    groups = SB_HEADS // 2 // pairs
    return pl.pallas_call(
        functools.partial(_sb_kernel, t=t, pairs=pairs),
        grid=(b, groups, s // t),
        in_specs=[pl.BlockSpec((1, LANES), lambda bi, gi, qi: (0, 0)),
                  pl.BlockSpec((1, 2 * pairs, t, LANES), lambda bi, gi, qi: (bi, gi, qi, 0)),
                  pl.BlockSpec((1, pairs, s, LANES), lambda bi, gi, qi: (bi, gi, 0, 0)),
                  pl.BlockSpec((1, pairs, s, LANES), lambda bi, gi, qi: (bi, gi, 0, 0))],
        out_specs=pl.BlockSpec((1, t, pairs * LANES), lambda bi, gi, qi: (bi, qi, gi)),
        out_shape=jax.ShapeDtypeStruct((b, s, (SB_HEADS // 2) * LANES), BF16),
        scratch_shapes=[pltpu.VMEM((2 * pairs, t, LANES), F32)],
        compiler_params=pltpu.CompilerParams(dimension_semantics=("parallel", "parallel", "parallel"),
                                             vmem_limit_bytes=VMEM_LIMIT),
        name="sb_attn",
    )(g_pair, sq, sk, sv)


def _outproj_kernel(a_ref, b_ref, w_ref, x_ref, mod_ref, g_ref, wr_ref, br_ref,
                    x1_ref, h2_ref, comb_ref):
    half = a_ref.shape[-1]
    y = (jnp.dot(a_ref[0], w_ref[0:half, :], preferred_element_type=F32)
         + jnp.dot(b_ref[0], w_ref[half:, :], preferred_element_type=F32))
    mod = mod_ref[0]
    gate_a, shift, scale = mod[2:3, :], mod[3:4, :], mod[4:5, :]
    x1 = x_ref[0] + gate_a * y
    x1_ref[0] = x1
    ms = jnp.mean(x1 * x1, axis=-1, keepdims=True)
    h2 = (x1 * lax.rsqrt(ms + EPS) * g_ref[...]) * (1.0 + scale) + shift
    h2_ref[0] = h2.astype(BF16)

    logits = jnp.dot(h2, wr_ref[...], precision=lax.Precision.HIGHEST,
                     preferred_element_type=F32) + br_ref[...]
    lane = lax.broadcasted_iota(jnp.int32, logits.shape, 1).astype(F32)
    big = jnp.float32(LANES)

    def top(vals):
        mx = jnp.max(vals, axis=-1, keepdims=True)
        idx = jnp.min(jnp.where(vals == mx, lane, big), axis=-1, keepdims=True)
        return mx, idx

    is_group = (lane >= N_EXPERTS) & (lane < N_EXPERTS + N_GROUPS)
    g_logits = jnp.where(is_group, logits, NEG)
    g_max, g_idx = top(g_logits)
    gate_group = 1.0 / jnp.sum(jnp.where(is_group, jnp.exp(logits - g_max), 0.0), axis=-1, keepdims=True)
    g_sel = g_idx - N_EXPERTS
    in_group = (lane >= g_sel * EXPERTS_PER_GROUP) & (lane < (g_sel + 1.0) * EXPERTS_PER_GROUP)
    e_logits = jnp.where(in_group, logits, NEG)
    v1, i1 = top(e_logits)
    v2, i2 = top(jnp.where(lane == i1, NEG, e_logits))
    e2 = jnp.exp(v2 - v1)
    w_first = 1.0 / (1.0 + e2)
    w_second = e2 / (1.0 + e2)
    comb_ref[0] = (gate_group * (jnp.where(lane == i1, w_first, 0.0) + jnp.where(lane == i2, w_second, 0.0))
                   + jnp.where(lane == g_idx, 1.0, 0.0))


def _outproj_call(a_out, b_out, w_out, x, mod, g_ffn, w_router, b_router, *, tm):
    b, s, d = x.shape
    half = a_out.shape[-1]
    idx3 = lambda bi, ti: (bi, ti, 0)
    const2 = lambda bi, ti: (0, 0)
    return pl.pallas_call(
        _outproj_kernel,
        grid=(b, s // tm),
        in_specs=[pl.BlockSpec((1, tm, half), idx3),
                  pl.BlockSpec((1, tm, half), idx3),
                  pl.BlockSpec((2 * half, d), const2),
                  pl.BlockSpec((1, tm, d), idx3),
                  pl.BlockSpec((1, N_MOD, d), lambda bi, ti: (bi, 0, 0)),
                  pl.BlockSpec((1, d), const2),
                  pl.BlockSpec((d, LANES), const2),
                  pl.BlockSpec((1, LANES), const2)],
        out_specs=[pl.BlockSpec((1, tm, d), idx3),
                   pl.BlockSpec((1, tm, d), idx3),
                   pl.BlockSpec((1, tm, LANES), idx3)],
        out_shape=[jax.ShapeDtypeStruct((b, s, d), F32),
                   jax.ShapeDtypeStruct((b, s, d), BF16),
                   jax.ShapeDtypeStruct((b, s, LANES), F32)],
        compiler_params=pltpu.CompilerParams(dimension_semantics=("parallel", "parallel"),
                                             vmem_limit_bytes=VMEM_LIMIT),
        name="out_proj_router",
    )(a_out, b_out, w_out, x, mod, g_ffn, w_router, b_router)


def _moe_kernel(h_ref, comb_ref, w1_ref, w3_ref, w2_ref, x1_ref, mod_ref, o_ref, *, ch):
    g = pl.program_id(2)
    tm = h_ref.shape[1]

    @pl.when(g == 0)
    def _():
        o_ref[0] = x1_ref[0]

    comb = comb_ref[0]
    lane = lax.broadcasted_iota(jnp.int32, (1, LANES), 1)
    member = jnp.sum(jnp.where(lane == N_EXPERTS + g, comb, 0.0), axis=-1, keepdims=True)
    member_wide = jnp.broadcast_to(member, (tm, LANES))
    earlier = (lax.broadcasted_iota(jnp.int32, (tm, tm), 0)
               > lax.broadcasted_iota(jnp.int32, (tm, tm), 1)).astype(BF16)
    rank = jnp.dot(earlier, member_wide.astype(BF16), preferred_element_type=F32)
    slot_wide = jnp.where(member_wide > 0.0, rank, -1.0)
    slot_col = slot_wide[:, 0:1]
    slot_row = slot_wide.T[0:1, :]
    count = jnp.sum(member)
    n_chunks = jnp.int32(0)
    for c in range(-(-tm // ch)):
        n_chunks += (count > float(c * ch)).astype(jnp.int32)

    comb_hi = comb.astype(BF16)
    comb_lo = (comb - comb_hi.astype(F32)).astype(BF16)
    gate_f = mod_ref[0][5:6, :]

    def chunk(c, _):
        base = (c * ch).astype(F32)
        want_col = lax.broadcasted_iota(jnp.int32, (ch, 1), 0).astype(F32) + base
        want_row = lax.broadcasted_iota(jnp.int32, (1, ch), 1).astype(F32) + base
        pick = (slot_row == want_col).astype(BF16)
        place = (slot_col == want_row).astype(BF16)
        xc = jnp.dot(pick, h_ref[0], preferred_element_type=F32).astype(BF16)
        cw = (jnp.dot(pick, comb_hi, preferred_element_type=F32)
              + jnp.dot(pick, comb_lo, preferred_element_type=F32))
        y = jnp.zeros((ch, o_ref.shape[-1]), F32)
        for e in range(EXPERTS_PER_GROUP):
            weight = jnp.sum(jnp.where(lane == g * EXPERTS_PER_GROUP + e, cw, 0.0), axis=-1, keepdims=True)
            h1 = jnp.dot(xc, w1_ref[0, e], preferred_element_type=F32)
            h3 = jnp.dot(xc, w3_ref[0, e], preferred_element_type=F32)
            hg = (h1 * jax.nn.sigmoid(h1)) * h3 * weight
            y += jnp.dot(hg.astype(BF16), w2_ref[0, e], preferred_element_type=F32)
        o_ref[0] += gate_f * jnp.dot(place, y.astype(BF16), preferred_element_type=F32)
        return 0

    lax.fori_loop(0, n_chunks, chunk, 0)


def _moe_call(h2, comb, w1, w3, w2, x1, mod, *, tm, ch):
    b, s, d = x1.shape
    f = w1.shape[-1]
    tok3 = lambda bi, ti, gi: (bi, ti, 0)
    return pl.pallas_call(
        functools.partial(_moe_kernel, ch=ch),
        grid=(b, s // tm, N_GROUPS),
        in_specs=[pl.BlockSpec((1, tm, d), tok3),
                  pl.BlockSpec((1, tm, LANES), tok3),
                  pl.BlockSpec((1, EXPERTS_PER_GROUP, d, f), lambda bi, ti, gi: (gi, 0, 0, 0)),
                  pl.BlockSpec((1, EXPERTS_PER_GROUP, d, f), lambda bi, ti, gi: (gi, 0, 0, 0)),
                  pl.BlockSpec((1, EXPERTS_PER_GROUP, f, d), lambda bi, ti, gi: (gi, 0, 0, 0)),
                  pl.BlockSpec((1, tm, d), tok3),
                  pl.BlockSpec((1, N_MOD, d), lambda bi, ti, gi: (bi, 0, 0))],
        out_specs=pl.BlockSpec((1, tm, d), tok3),
        out_shape=jax.ShapeDtypeStruct((b, s, d), F32),
        compiler_params=pltpu.CompilerParams(dimension_semantics=("parallel", "parallel", "arbitrary"),
                                             vmem_limit_bytes=VMEM_LIMIT),
        name="moe_experts",
    )(h2, comb, w1, w3, w2, x1, mod)


def _permute_w_in(w_in):
    d = w_in.shape[0]
    dqk = DIFF_HEADS * 2 * HEAD_DIM
    dvw = DIFF_HEADS * DIFF_VDIM
    sbw = SB_HEADS * HEAD_DIM
    cuts = [0, dqk, 2 * dqk, 2 * dqk + dvw, 2 * dqk + dvw + sbw, 2 * dqk + dvw + 2 * sbw]
    zeros = jnp.zeros((d, HEAD_DIM), w_in.dtype)
    blocks = []
    for base in (cuts[0], cuts[1]):
        for n in range(2 * DIFF_HEADS):
            blocks += [w_in[:, base + n * HEAD_DIM: base + (n + 1) * HEAD_DIM], zeros]
    blocks.append(w_in[:, cuts[2]:cuts[3]])
    sb_scale = LOG2E / math.sqrt(HEAD_DIM)
    for n in range(SB_HEADS):
        wq = w_in[:, cuts[3] + n * HEAD_DIM: cuts[3] + (n + 1) * HEAD_DIM] * sb_scale
        blocks += [wq, zeros] if n % 2 == 0 else [zeros, wq]
    blocks.append(w_in[:, cuts[4]:cuts[5]])
    blocks.append(w_in[:, cuts[5]:])
    return jnp.concatenate(blocks, axis=1).astype(BF16)


def _pad_lanes(v, width=LANES):
    v = v.reshape(1, -1)
    return jnp.pad(v, ((0, 0), (0, width - v.shape[1])))


def kernel(x, c, w_ada, b_ada, g_attn, w_in, q_norm_g, k_norm_g, lambda_q1, lambda_k1, lambda_q2, lambda_k2,
           diff_out_g, sb_out_g, w_out, g_ffn, w_group, b_group, w_erouter, b_expert, w1, w3, w2):
    b, s, d = x.shape
    depth = w_ada.shape[0]
    tile = min(512, s)
    moe_tile = min(1024, s)
    moe_chunk = 16 * math.ceil((moe_tile / N_GROUPS + 2.3 * math.sqrt(moe_tile * 3 / 16)) / 16)
    sb_tile = min(MXU_COLS, s)
    slopes = tuple(2.0 ** (-8.0 * (n + 1) / DIFF_HEADS) for n in range(DIFF_HEADS))
    qk_scale = 1.0 / math.sqrt(HEAD_DIM)
    c_pad = jnp.pad(c, ((0, 8 - b), (0, 0)))
    qaug = np.zeros((1, LANES), np.float32)
    qaug[0, HEAD_DIM:HEAD_DIM + 2 * LOG2E_PARTS] = np.repeat(_bf16_parts(LOG2E, LOG2E_PARTS), 2)
    qaug_row = jnp.asarray(qaug)

    for layer in range(depth):
        lambda_init = 0.8 - 0.6 * math.exp(-0.3 * layer)
        mod = _mod_call(c_pad, w_ada[layer], b_ada[layer].reshape(1, -1))[:b].reshape(b, N_MOD, d)

        dq, dk, dv, sq, sk, sv = _proj_call(
            x, mod, g_attn[layer].reshape(1, d), _permute_w_in(w_in[layer]),
            _pad_lanes(q_norm_g[layer] * (qk_scale * LOG2E)), _pad_lanes(k_norm_g[layer]), qaug_row,
            tm=tile, tk=tile, slopes=slopes)

        lam_p = jnp.stack([lambda_q1[layer], lambda_k1[layer], lambda_q2[layer], lambda_k2[layer]])
        score_bound = (1.02 * HEAD_DIM * qk_scale) * jnp.max(jnp.abs(q_norm_g[layer])) * jnp.max(jnp.abs(k_norm_g[layer]))
        dead_dist = (2.0 * score_bound - F32_DEAD_LOG) / jnp.asarray(slopes, F32)
        n_back = jnp.clip(jnp.floor(dead_dist / tile) + 2.0, 1.0, s // tile).astype(jnp.int32)
        diff_args = (n_back, lam_p, diff_out_g[layer].reshape(1, LANES),
                     jnp.full((1, LANES), score_bound * LOG2E, F32), dq, dk, dv)
        diff_kw = dict(t=tile, slopes=slopes, lambda_init=lambda_init)
        a_out = lax.cond(score_bound <= DIFF_FIXED_REF_MAX,
                         lambda args: _diff_call(*args, fixed_ref=True, **diff_kw),
                         lambda args: _diff_call(*args, fixed_ref=False, **diff_kw),
                         diff_args)
        b_out = _sb_call(jnp.tile(sb_out_g[layer], 2).reshape(1, LANES), sq, sk, sv, t=sb_tile,
                         pairs=SB_PAIRS_PER_STEP)

        w_router = jnp.concatenate(
            [jnp.transpose(w_erouter[layer], (1, 0, 2)).reshape(d, N_EXPERTS), w_group[layer]], axis=1)
        w_router = jnp.pad(w_router, ((0, 0), (0, LANES - w_router.shape[1])))
        b_router = _pad_lanes(jnp.concatenate([b_expert[layer].reshape(-1), b_group[layer]]))
        x1, h2, comb = _outproj_call(a_out, b_out, w_out[layer].astype(BF16), x, mod,
                                     g_ffn[layer].reshape(1, d), w_router, b_router, tm=tile)

        x = _moe_call(h2, comb, w1[layer].astype(BF16), w3[layer].astype(BF16), w2[layer].astype(BF16),
                      x1, mod, tm=moe_tile, ch=moe_chunk)
    return x
```

```python
import functools
import math

import jax
import jax.numpy as jnp
import ml_dtypes
import numpy as np
from jax import lax
from jax.experimental import pallas as pl
from jax.experimental.pallas import tpu as pltpu

HEAD_DIM = 64
DIFF_HEADS = 4
SB_HEADS = 8
DIFF_VDIM = 2 * HEAD_DIM
N_GROUPS = 4
EXPERTS_PER_GROUP = 8
N_EXPERTS = N_GROUPS * EXPERTS_PER_GROUP
D_EXPERT = 256
N_MOD = 6
EPS = 1e-6
LANES = 128
MXU_COLS = 256
NEG = -1e30
DIFF_BLOCKS_PER_ITER = 4
SB_PAIRS_PER_STEP = 4
F32_DEAD_LOG = -104.0
VMEM_LIMIT = 56 * 1024 * 1024

F32 = jnp.float32
BF16 = jnp.bfloat16

LOG2E = math.log2(math.e)
LOG2E_PARTS = 3
DIFF_FIXED_REF_MAX = 32.0


def _bf16_parts(value, n):
    parts, rest = [], value
    for _ in range(n):
        part = float(np.float32(rest).astype(ml_dtypes.bfloat16))
        parts.append(part)
        rest -= part
    return parts


def _nt_dot(a, b):
    return lax.dot_general(a, b, (((1,), (1,)), ((), ())), preferred_element_type=F32)


def _mod_kernel(c_ref, w_ref, b_ref, o_ref):
    c = c_ref[...]
    sc = c * jax.nn.sigmoid(c)
    o_ref[...] = jnp.dot(sc, w_ref[...], precision=lax.Precision.HIGHEST,
                         preferred_element_type=F32) + b_ref[...]


def _mod_call(c_pad, w_ada, b_ada):
    rows, d = c_pad.shape
    n = w_ada.shape[1]
    tn = 1536
    return pl.pallas_call(
        _mod_kernel,
        grid=(n // tn,),
        in_specs=[pl.BlockSpec((rows, d), lambda j: (0, 0)),
                  pl.BlockSpec((d, tn), lambda j: (0, j)),
                  pl.BlockSpec((1, tn), lambda j: (0, j))],
        out_specs=pl.BlockSpec((rows, tn), lambda j: (0, j)),
        out_shape=jax.ShapeDtypeStruct((rows, n), F32),
        compiler_params=pltpu.CompilerParams(dimension_semantics=("parallel",),
                                             vmem_limit_bytes=VMEM_LIMIT),
        name="adaln_mod",
    )(c_pad, w_ada, b_ada)


_PROJ_BLOCKS = 36


def _proj_kernel(x_ref, mod_ref, g_ref, w_ref, qg_ref, kg_ref, qaug_ref,
                 dq_ref, dk_ref, dv_ref, sq_ref, sk_ref, sv_ref, *, tm, tk, slopes):
    x = x_ref[0]
    mod = mod_ref[0]
    shift, scale = mod[0:1, :], mod[1:2, :]
    ms = jnp.mean(x * x, axis=-1, keepdims=True)
    h = (x * lax.rsqrt(ms + EPS) * g_ref[...]) * (1.0 + scale) + shift
    hb = h.astype(BF16)

    lane = lax.broadcasted_iota(jnp.int32, (1, LANES), 1)
    aug = (lane >= HEAD_DIM) & (lane < HEAD_DIM + 2 * LOG2E_PARTS)
    lo_lane = (aug & ((lane & 1) == 0)).astype(F32)
    hi_lane = (aug & ((lane & 1) == 1)).astype(F32)
    row = lax.broadcasted_iota(jnp.int32, (tm, 1), 0) + pl.program_id(1) * tm
    koff = row & (tk - 1)
    koff_lo = (koff & 255).astype(F32)
    koff_hi = (koff - (koff & 255)).astype(F32)
    koff_lanes = lo_lane * koff_lo + hi_lane * koff_hi

    for c in range(_PROJ_BLOCKS * LANES // MXU_COLS):
        pc = jnp.dot(hb, w_ref[:, c * MXU_COLS:(c + 1) * MXU_COLS], preferred_element_type=F32)
        for half in range(MXU_COLS // LANES):
            blk = c * (MXU_COLS // LANES) + half
            piece = pc[:, half * LANES:(half + 1) * LANES]
            if blk < 16:
                ss = jnp.sum(piece * piece, axis=-1, keepdims=True)
                normed = piece * lax.rsqrt(ss * (1.0 / HEAD_DIM) + EPS)
                if blk < 8:
                    dq_ref[0, blk] = (normed * qg_ref[...] + qaug_ref[...]).astype(BF16)
                else:
                    slope = slopes[(blk - 8) // 2]
                    dk_ref[0, blk - 8] = (normed * kg_ref[...] + koff_lanes * slope).astype(BF16)
            elif blk < 20:
                dv_ref[0, blk - 16] = piece.astype(BF16)
            elif blk < 28:
                sq_ref[0, blk - 20] = piece.astype(BF16)
            elif blk < 32:
                sk_ref[0, blk - 28] = piece.astype(BF16)
            else:
                sv_ref[0, blk - 32] = piece.astype(BF16)


def _proj_call(x, mod, g_attn, w_perm, qg_pad, kg_pad, qaug_row, *, tm, tk, slopes):
    b, s, d = x.shape
    n = w_perm.shape[1]
    nt = s // tm

    def hm(nh):
        return (jax.ShapeDtypeStruct((b, nh, s, LANES), BF16),
                pl.BlockSpec((1, nh, tm, LANES), lambda bi, ti: (bi, 0, ti, 0)))

    outs = [hm(8), hm(8), hm(4), hm(8), hm(4), hm(4)]
    return pl.pallas_call(
        functools.partial(_proj_kernel, tm=tm, tk=tk, slopes=slopes),
        grid=(b, nt),
        in_specs=[pl.BlockSpec((1, tm, d), lambda bi, ti: (bi, ti, 0)),
                  pl.BlockSpec((1, N_MOD, d), lambda bi, ti: (bi, 0, 0)),
                  pl.BlockSpec((1, d), lambda bi, ti: (0, 0)),
                  pl.BlockSpec((d, n), lambda bi, ti: (0, 0)),
                  pl.BlockSpec((1, LANES), lambda bi, ti: (0, 0)),
                  pl.BlockSpec((1, LANES), lambda bi, ti: (0, 0)),
                  pl.BlockSpec((1, LANES), lambda bi, ti: (0, 0))],
        out_specs=[o[1] for o in outs],
        out_shape=[o[0] for o in outs],
        compiler_params=pltpu.CompilerParams(dimension_semantics=("parallel", "parallel"),
                                             vmem_limit_bytes=VMEM_LIMIT),
        name="in_proj",
    )(x, mod, g_attn, w_perm, qg_pad, kg_pad, qaug_row)


def _diff_kernel(nback_ref, lam_ref, g_ref, bound_ref, q_ref, k_ref, v_ref, o_ref, acc0_ref, acc1_ref,
                 *, t, slopes, lambda_init, fixed_ref):
    hd = pl.program_id(1)
    i = pl.program_id(2)
    hd_col = jnp.full((t, 1), hd, jnp.int32)
    slope = jnp.full((t, 1), slopes[-1] * LOG2E, F32)
    for n in range(len(slopes) - 2, -1, -1):
        slope = jnp.where(hd_col == n, slopes[n] * LOG2E, slope)

    q = (q_ref[0, 0], q_ref[0, 1])
    accs = (acc0_ref, acc1_ref)
    acc0_ref[...] = jnp.zeros_like(acc0_ref)
    acc1_ref[...] = jnp.zeros_like(acc1_ref)
    row = lax.broadcasted_iota(jnp.int32, (t, t), 0)
    col = lax.broadcasted_iota(jnp.int32, (t, t), 1)
    causal = col <= row
    ones_col = (lax.broadcasted_iota(jnp.int32, (t, LANES), 1) == 0).astype(BF16)
    row_pos = lax.broadcasted_iota(jnp.int32, (t, 1), 0) + i * t

    def block(j, carry, masked):
        ks = pl.multiple_of(j * t, t)
        v_aug = jnp.concatenate([v_ref[0, 0, pl.ds(ks, t), :], ones_col], axis=1)
        new = []
        for mp in range(2):
            s = _nt_dot(q[mp], k_ref[0, mp, pl.ds(ks, t), :])
            if masked:
                s = jnp.where(causal, s, NEG)
            if fixed_ref:
                ref = slope * (row_pos - j * t).astype(F32) + bound_ref[:, 0:1]
                p = jnp.exp2(s - ref)
                accs[mp][...] += jnp.dot(p.astype(BF16), v_aug, preferred_element_type=F32)
            else:
                off = slope * jnp.full((t, 1), j * t, jnp.int32).astype(F32)
                m = carry[mp]
                m_new = jnp.maximum(m, jnp.max(s, axis=-1, keepdims=True) + off)
                p = jnp.exp2(s - (m_new - off))
                accs[mp][...] = (jnp.exp2(m - m_new) * accs[mp][...]
                                 + jnp.dot(p.astype(BF16), v_aug, preferred_element_type=F32))
                new.append(m_new)
        return tuple(new)

    first = jnp.maximum(i + 1 - nback_ref[hd], 0)
    n_full = i - first
    n_bunches = n_full // DIFF_BLOCKS_PER_ITER

    def bunch(bb, carry):
        for n in range(DIFF_BLOCKS_PER_ITER):
            carry = block(first + bb * DIFF_BLOCKS_PER_ITER + n, carry, False)
        return carry

    init = () if fixed_ref else (jnp.full((t, 1), NEG, F32),) * 2
    carry = lax.fori_loop(0, n_bunches, bunch, init)
    carry = lax.fori_loop(first + n_bunches * DIFF_BLOCKS_PER_ITER, i, lambda j, c: block(j, c, False), carry)
    block(i, carry, True)

    lp = lam_ref[...]
    lam = (jnp.exp(jnp.sum(lp[0:1] * lp[1:2], axis=-1, keepdims=True))
           - jnp.exp(jnp.sum(lp[2:3] * lp[3:4], axis=-1, keepdims=True)) + lambda_init)
    o = (acc0_ref[:, 0:LANES] / acc0_ref[:, LANES:LANES + 1]
         - lam * (acc1_ref[:, 0:LANES] / acc1_ref[:, LANES:LANES + 1]))
    ms = jnp.mean(o * o, axis=-1, keepdims=True)
    o_ref[0] = ((o * lax.rsqrt(ms + EPS) * g_ref[...]) * (1.0 - lambda_init)).astype(BF16)


def _diff_call(n_back, lam_p, g_out, bound_row, dq, dk, dv, *, t, slopes, lambda_init, fixed_ref):
    b, _, s, _ = dq.shape
    const2 = lambda bi, hi, qi: (0, 0)
    return pl.pallas_call(
        functools.partial(_diff_kernel, t=t, slopes=slopes, lambda_init=lambda_init, fixed_ref=fixed_ref),
        grid=(b, DIFF_HEADS, s // t),
        in_specs=[pl.BlockSpec(memory_space=pltpu.SMEM),
                  pl.BlockSpec((4, HEAD_DIM), const2),
                  pl.BlockSpec((1, LANES), const2),
                  pl.BlockSpec((1, LANES), const2),
                  pl.BlockSpec((1, 2, t, LANES), lambda bi, hi, qi: (bi, hi, qi, 0)),
                  pl.BlockSpec((1, 2, s, LANES), lambda bi, hi, qi: (bi, hi, 0, 0)),
                  pl.BlockSpec((1, 1, s, LANES), lambda bi, hi, qi: (bi, hi, 0, 0))],
        out_specs=pl.BlockSpec((1, t, LANES), lambda bi, hi, qi: (bi, qi, hi)),
        out_shape=jax.ShapeDtypeStruct((b, s, DIFF_HEADS * LANES), BF16),
        scratch_shapes=[pltpu.VMEM((t, 2 * LANES), F32), pltpu.VMEM((t, 2 * LANES), F32)],
        compiler_params=pltpu.CompilerParams(dimension_semantics=("parallel", "parallel", "parallel"),
                                             vmem_limit_bytes=VMEM_LIMIT),
        name="diff_attn_fixed_ref" if fixed_ref else "diff_attn_running_max",
    )(n_back, lam_p, g_out, bound_row, dq, dk, dv)


def _sb_kernel(g_ref, q_ref, k_ref, v_ref, o_ref, acc_ref, *, t, pairs):
    i = pl.program_id(2)
    heads = 2 * pairs
    acc_ref[...] = jnp.zeros_like(acc_ref)
    row = lax.broadcasted_iota(jnp.int32, (t, t), 0)
    col = lax.broadcasted_iota(jnp.int32, (t, t), 1)
    past = col < row
    later = (row > col).astype(BF16)

    def block(j, carry, masked):
        ks = pl.multiple_of(j * t, t)
        new = []
        for hh in range(heads):
            k = k_ref[0, hh // 2, pl.ds(ks, t), :]
            v = v_ref[0, hh // 2, pl.ds(ks, t), :]
            z = _nt_dot(q_ref[0, hh], k)
            neg_abs = -jnp.abs(z)
            log_rem = 0.5 * (neg_abs - z) - jnp.log2(1.0 + jnp.exp2(neg_abs))
            log_beta = log_rem + z
            if masked:
                log_rem = jnp.where(past, log_rem, 0.0)
            new.append(carry[hh] + jnp.sum(log_rem, axis=-1, keepdims=True))
            suffix = jnp.dot(log_rem.astype(BF16), later, preferred_element_type=F32)
            a = jnp.exp2(log_beta + suffix + carry[hh])
            if masked:
                a = jnp.where(past, a, 0.0)
            acc_ref[hh] += jnp.dot(a.astype(BF16), v, preferred_element_type=F32)
        return tuple(new)

    def any_live(carry):
        top = functools.reduce(jnp.maximum, carry)
        return (jnp.max(top) > LOG2E * F32_DEAD_LOG).astype(jnp.int32)

    def live_cond(state):
        jj, live = state[0], state[1]
        return jnp.logical_and(jj < i, live > 0)

    def live_body(state):
        carry = block(i - 1 - state[0], state[2:], False)
        return (state[0] + 1, any_live(carry)) + carry

    carry = block(i, (jnp.zeros((t, 1), F32),) * heads, True)
    lax.while_loop(live_cond, live_body, (jnp.int32(0), any_live(carry)) + carry)

    lane = lax.broadcasted_iota(jnp.int32, (1, LANES), 1)
    first = lane < HEAD_DIM
    for pr in range(pairs):
        o = jnp.where(first, acc_ref[2 * pr], acc_ref[2 * pr + 1])
        sq = o * o
        s_first = jnp.sum(jnp.where(first, sq, 0.0), axis=-1, keepdims=True)
        s_all = jnp.sum(sq, axis=-1, keepdims=True)
        ms = jnp.where(first, s_first, s_all - s_first) * (1.0 / HEAD_DIM)
        o_ref[0, :, pr * LANES:(pr + 1) * LANES] = (o * lax.rsqrt(ms + EPS) * g_ref[...]).astype(BF16)


def _sb_call(g_pair, sq, sk, sv, *, t, pairs):
    b, _, s, _ = sq.shape
    groups = SB_HEADS // 2 // pairs
    return pl.pallas_call(
        functools.partial(_sb_kernel, t=t, pairs=pairs),
        grid=(b, groups, s // t),
        in_specs=[pl.BlockSpec((1, LANES), lambda bi, gi, qi: (0, 0)),
                  pl.BlockSpec((1, 2 * pairs, t, LANES), lambda bi, gi, qi: (bi, gi, qi, 0)),
                  pl.BlockSpec((1, pairs, s, LANES), lambda bi, gi, qi: (bi, gi, 0, 0)),
                  pl.BlockSpec((1, pairs, s, LANES), lambda bi, gi, qi: (bi, gi, 0, 0))],
        out_specs=pl.BlockSpec((1, t, pairs * LANES), lambda bi, gi, qi: (bi, qi, gi)),
        out_shape=jax.ShapeDtypeStruct((b, s, (SB_HEADS // 2) * LANES), BF16),
        scratch_shapes=[pltpu.VMEM((2 * pairs, t, LANES), F32)],
        compiler_params=pltpu.CompilerParams(dimension_semantics=("parallel", "parallel", "parallel"),
                                             vmem_limit_bytes=VMEM_LIMIT),
        name="sb_attn",
    )(g_pair, sq, sk, sv)


def _outproj_kernel(a_ref, b_ref, w_ref, x_ref, mod_ref, g_ref, wr_ref, br_ref,
                    x1_ref, h2_ref, comb_ref):
    half = a_ref.shape[-1]
    y = (jnp.dot(a_ref[0], w_ref[0:half, :], preferred_element_type=F32)
         + jnp.dot(b_ref[0], w_ref[half:, :], preferred_element_type=F32))
    mod = mod_ref[0]
    gate_a, shift, scale = mod[2:3, :], mod[3:4, :], mod[4:5, :]
    x1 = x_ref[0] + gate_a * y
    x1_ref[0] = x1
    ms = jnp.mean(x1 * x1, axis=-1, keepdims=True)
    h2 = (x1 * lax.rsqrt(ms + EPS) * g_ref[...]) * (1.0 + scale) + shift
    h2_hi = h2.astype(BF16)
    h2_ref[0] = h2_hi

    h2_lo = (h2 - h2_hi.astype(F32)).astype(BF16)
    wr_hi, wr_lo = wr_ref[0], wr_ref[1]
    logits = (jnp.dot(h2_hi, wr_hi, preferred_element_type=F32)
              + jnp.dot(h2_lo, wr_hi, preferred_element_type=F32)
              + jnp.dot(h2_hi, wr_lo, preferred_element_type=F32)) + br_ref[...]
    lane = lax.broadcasted_iota(jnp.int32, logits.shape, 1).astype(F32)
    big = jnp.float32(LANES)

    def top(vals):
        mx = jnp.max(vals, axis=-1, keepdims=True)
        idx = jnp.min(jnp.where(vals == mx, lane, big), axis=-1, keepdims=True)
        return mx, idx

    is_group = (lane >= N_EXPERTS) & (lane < N_EXPERTS + N_GROUPS)
    g_logits = jnp.where(is_group, logits, NEG)
    g_max, g_idx = top(g_logits)
    gate_group = 1.0 / jnp.sum(jnp.where(is_group, jnp.exp(logits - g_max), 0.0), axis=-1, keepdims=True)
    g_sel = g_idx - N_EXPERTS
    in_group = (lane >= g_sel * EXPERTS_PER_GROUP) & (lane < (g_sel + 1.0) * EXPERTS_PER_GROUP)
    e_logits = jnp.where(in_group, logits, NEG)
    v1, i1 = top(e_logits)
    v2, i2 = top(jnp.where(lane == i1, NEG, e_logits))
    e2 = jnp.exp(v2 - v1)
    w_first = 1.0 / (1.0 + e2)
    w_second = e2 / (1.0 + e2)
    comb_ref[0] = (gate_group * (jnp.where(lane == i1, w_first, 0.0) + jnp.where(lane == i2, w_second, 0.0))
                   + jnp.where(lane == g_idx, 1.0, 0.0))


def _outproj_call(a_out, b_out, w_out, x, mod, g_ffn, w_router, b_router, *, tm):
    b, s, d = x.shape
    half = a_out.shape[-1]
    idx3 = lambda bi, ti: (bi, ti, 0)
    const2 = lambda bi, ti: (0, 0)
    return pl.pallas_call(
        _outproj_kernel,
        grid=(b, s // tm),
        in_specs=[pl.BlockSpec((1, tm, half), idx3),
                  pl.BlockSpec((1, tm, half), idx3),
                  pl.BlockSpec((2 * half, d), const2),
                  pl.BlockSpec((1, tm, d), idx3),
                  pl.BlockSpec((1, N_MOD, d), lambda bi, ti: (bi, 0, 0)),
                  pl.BlockSpec((1, d), const2),
                  pl.BlockSpec((2, d, LANES), lambda bi, ti: (0, 0, 0)),
                  pl.BlockSpec((1, LANES), const2)],
        out_specs=[pl.BlockSpec((1, tm, d), idx3),
                   pl.BlockSpec((1, tm, d), idx3),
                   pl.BlockSpec((1, tm, LANES), idx3)],
        out_shape=[jax.ShapeDtypeStruct((b, s, d), F32),
                   jax.ShapeDtypeStruct((b, s, d), BF16),
                   jax.ShapeDtypeStruct((b, s, LANES), F32)],
        compiler_params=pltpu.CompilerParams(dimension_semantics=("parallel", "parallel"),
                                             vmem_limit_bytes=VMEM_LIMIT),
        name="out_proj_router",
    )(a_out, b_out, w_out, x, mod, g_ffn, w_router, b_router)


def _moe_kernel(h_ref, comb_ref, w1_ref, w3_ref, w2_ref, x1_ref, mod_ref, o_ref, *, unit, max_units):
    g = pl.program_id(2)
    tm = h_ref.shape[1]

    @pl.when(g == 0)
    def _():
        o_ref[0] = x1_ref[0]

    comb = comb_ref[0]
    lane = lax.broadcasted_iota(jnp.int32, (1, LANES), 1)
    member = jnp.sum(jnp.where(lane == N_EXPERTS + g, comb, 0.0), axis=-1, keepdims=True)
    member_wide = jnp.broadcast_to(member, (tm, LANES))
    earlier = (lax.broadcasted_iota(jnp.int32, (tm, tm), 0)
               > lax.broadcasted_iota(jnp.int32, (tm, tm), 1)).astype(BF16)
    rank = jnp.dot(earlier, member_wide.astype(BF16), preferred_element_type=F32)
    slot_wide = jnp.where(member_wide > 0.0, rank, -1.0)
    slot_col = slot_wide[:, 0:1]
    slot_row = slot_wide.T[0:1, :]
    count = jnp.sum(member)
    units = jnp.int32(0)
    for c in range(tm // unit):
        units += (count > float(c * unit)).astype(jnp.int32)

    comb_hi = comb.astype(BF16)
    comb_lo = (comb - comb_hi.astype(F32)).astype(BF16)
    gate_f = mod_ref[0][5:6, :]

    def chunk(first_unit, ch):
        base = (first_unit * unit).astype(F32)
        want_col = lax.broadcasted_iota(jnp.int32, (ch, 1), 0).astype(F32) + base
        want_row = lax.broadcasted_iota(jnp.int32, (1, ch), 1).astype(F32) + base
        pick = (slot_row == want_col).astype(BF16)
        place = (slot_col == want_row).astype(BF16)
        xc = jnp.dot(pick, h_ref[0], preferred_element_type=F32).astype(BF16)
        cw = (jnp.dot(pick, comb_hi, preferred_element_type=F32)
              + jnp.dot(pick, comb_lo, preferred_element_type=F32))
        y = jnp.zeros((ch, o_ref.shape[-1]), F32)
        for e in range(EXPERTS_PER_GROUP):
            weight = jnp.sum(jnp.where(lane == g * EXPERTS_PER_GROUP + e, cw, 0.0), axis=-1, keepdims=True)
            h1 = jnp.dot(xc, w1_ref[0, e], preferred_element_type=F32)
            h3 = jnp.dot(xc, w3_ref[0, e], preferred_element_type=F32)
            hg = (h1 * jax.nn.sigmoid(h1)) * h3 * weight
            y += jnp.dot(hg.astype(BF16), w2_ref[0, e], preferred_element_type=F32)
        o_ref[0] += gate_f * jnp.dot(place, y.astype(BF16), preferred_element_type=F32)

    def step(done):
        todo = jnp.minimum(units - done, max_units)
        for n in range(1, max_units + 1):
            pl.when(todo == n)(functools.partial(chunk, done, n * unit))
        return done + todo

    lax.while_loop(lambda done: done < units, step, jnp.int32(0))


def _moe_call(h2, comb, w1, w3, w2, x1, mod, *, tm, unit, max_units):
    b, s, d = x1.shape
    f = w1.shape[-1]
    tok3 = lambda bi, ti, gi: (bi, ti, 0)
    return pl.pallas_call(
        functools.partial(_moe_kernel, unit=unit, max_units=max_units),
        grid=(b, s // tm, N_GROUPS),
        in_specs=[pl.BlockSpec((1, tm, d), tok3),
                  pl.BlockSpec((1, tm, LANES), tok3),
                  pl.BlockSpec((1, EXPERTS_PER_GROUP, d, f), lambda bi, ti, gi: (gi, 0, 0, 0)),
                  pl.BlockSpec((1, EXPERTS_PER_GROUP, d, f), lambda bi, ti, gi: (gi, 0, 0, 0)),
                  pl.BlockSpec((1, EXPERTS_PER_GROUP, f, d), lambda bi, ti, gi: (gi, 0, 0, 0)),
                  pl.BlockSpec((1, tm, d), tok3),
                  pl.BlockSpec((1, N_MOD, d), lambda bi, ti, gi: (bi, 0, 0))],
        out_specs=pl.BlockSpec((1, tm, d), tok3),
        out_shape=jax.ShapeDtypeStruct((b, s, d), F32),
        compiler_params=pltpu.CompilerParams(dimension_semantics=("parallel", "parallel", "arbitrary"),
                                             vmem_limit_bytes=VMEM_LIMIT),
        name="moe_experts",
    )(h2, comb, w1, w3, w2, x1, mod)


def _permute_w_in(w_in):
    d = w_in.shape[0]
    dqk = DIFF_HEADS * 2 * HEAD_DIM
    dvw = DIFF_HEADS * DIFF_VDIM
    sbw = SB_HEADS * HEAD_DIM
    cuts = [0, dqk, 2 * dqk, 2 * dqk + dvw, 2 * dqk + dvw + sbw, 2 * dqk + dvw + 2 * sbw]
    zeros = jnp.zeros((d, HEAD_DIM), w_in.dtype)
    blocks = []
    for base in (cuts[0], cuts[1]):
        for n in range(2 * DIFF_HEADS):
            blocks += [w_in[:, base + n * HEAD_DIM: base + (n + 1) * HEAD_DIM], zeros]
    blocks.append(w_in[:, cuts[2]:cuts[3]])
    sb_scale = LOG2E / math.sqrt(HEAD_DIM)
    for n in range(SB_HEADS):
        wq = w_in[:, cuts[3] + n * HEAD_DIM: cuts[3] + (n + 1) * HEAD_DIM] * sb_scale
        blocks += [wq, zeros] if n % 2 == 0 else [zeros, wq]
    blocks.append(w_in[:, cuts[4]:cuts[5]])
    blocks.append(w_in[:, cuts[5]:])
    return jnp.concatenate(blocks, axis=1).astype(BF16)


def _pad_lanes(v, width=LANES):
    v = v.reshape(1, -1)
    return jnp.pad(v, ((0, 0), (0, width - v.shape[1])))


def kernel(x, c, w_ada, b_ada, g_attn, w_in, q_norm_g, k_norm_g, lambda_q1, lambda_k1, lambda_q2, lambda_k2,
           diff_out_g, sb_out_g, w_out, g_ffn, w_group, b_group, w_erouter, b_expert, w1, w3, w2):
    b, s, d = x.shape
    depth = w_ada.shape[0]
    tile = min(512, s)
    moe_tile = min(1024, s)
    moe_unit = min(128, moe_tile)
    moe_max_units = 4
    sb_tile = min(MXU_COLS, s)
    slopes = tuple(2.0 ** (-8.0 * (n + 1) / DIFF_HEADS) for n in range(DIFF_HEADS))
    qk_scale = 1.0 / math.sqrt(HEAD_DIM)
    c_pad = jnp.pad(c, ((0, 8 - b), (0, 0)))
    qaug = np.zeros((1, LANES), np.float32)
    qaug[0, HEAD_DIM:HEAD_DIM + 2 * LOG2E_PARTS] = np.repeat(_bf16_parts(LOG2E, LOG2E_PARTS), 2)
    qaug_row = jnp.asarray(qaug)

    for layer in range(depth):
        lambda_init = 0.8 - 0.6 * math.exp(-0.3 * layer)
        mod = _mod_call(c_pad, w_ada[layer], b_ada[layer].reshape(1, -1))[:b].reshape(b, N_MOD, d)

        dq, dk, dv, sq, sk, sv = _proj_call(
            x, mod, g_attn[layer].reshape(1, d), _permute_w_in(w_in[layer]),
            _pad_lanes(q_norm_g[layer] * (qk_scale * LOG2E)), _pad_lanes(k_norm_g[layer]), qaug_row,
            tm=tile, tk=tile, slopes=slopes)

        lam_p = jnp.stack([lambda_q1[layer], lambda_k1[layer], lambda_q2[layer], lambda_k2[layer]])
        score_bound = (1.02 * HEAD_DIM * qk_scale) * jnp.max(jnp.abs(q_norm_g[layer])) * jnp.max(jnp.abs(k_norm_g[layer]))
        dead_dist = (2.0 * score_bound - F32_DEAD_LOG) / jnp.asarray(slopes, F32)
        n_back = jnp.clip(jnp.floor(dead_dist / tile) + 2.0, 1.0, s // tile).astype(jnp.int32)
        diff_args = (n_back, lam_p, diff_out_g[layer].reshape(1, LANES),
                     jnp.full((1, LANES), score_bound * LOG2E, F32), dq, dk, dv)
        diff_kw = dict(t=tile, slopes=slopes, lambda_init=lambda_init)
        a_out = lax.cond(score_bound <= DIFF_FIXED_REF_MAX,
                         lambda args: _diff_call(*args, fixed_ref=True, **diff_kw),
                         lambda args: _diff_call(*args, fixed_ref=False, **diff_kw),
                         diff_args)
        b_out = _sb_call(jnp.tile(sb_out_g[layer], 2).reshape(1, LANES), sq, sk, sv, t=sb_tile,
                         pairs=SB_PAIRS_PER_STEP)

        w_router = jnp.concatenate(
            [jnp.transpose(w_erouter[layer], (1, 0, 2)).reshape(d, N_EXPERTS), w_group[layer]], axis=1)
        w_router = jnp.pad(w_router, ((0, 0), (0, LANES - w_router.shape[1])))
        w_router_hi = w_router.astype(BF16)
        w_router = jnp.stack([w_router_hi, (w_router - w_router_hi.astype(F32)).astype(BF16)])
        b_router = _pad_lanes(jnp.concatenate([b_expert[layer].reshape(-1), b_group[layer]]))
        x1, h2, comb = _outproj_call(a_out, b_out, w_out[layer].astype(BF16), x, mod,
                                     g_ffn[layer].reshape(1, d), w_router, b_router, tm=tile)

        x = _moe_call(h2, comb, w1[layer].astype(BF16), w3[layer].astype(BF16), w2[layer].astype(BF16),
                      x1, mod, tm=moe_tile, unit=moe_unit, max_units=moe_max_units)
    return x
```

```python
import functools
import math

import jax
import jax.numpy as jnp
import ml_dtypes
import numpy as np
from jax import lax
from jax.experimental import pallas as pl
from jax.experimental.pallas import tpu as pltpu

HEAD_DIM = 64
DIFF_HEADS = 4
SB_HEADS = 8
DIFF_VDIM = 2 * HEAD_DIM
N_GROUPS = 4
EXPERTS_PER_GROUP = 8
N_EXPERTS = N_GROUPS * EXPERTS_PER_GROUP
D_EXPERT = 256
N_MOD = 6
EPS = 1e-6
LANES = 128
MXU_COLS = 256
NEG = -1e30
DIFF_BLOCKS_PER_ITER = 4
SB_PAIRS_PER_STEP = 4
F32_DEAD_LOG = -104.0
VMEM_LIMIT = 56 * 1024 * 1024

F32 = jnp.float32
BF16 = jnp.bfloat16

LOG2E = math.log2(math.e)
LOG2E_PARTS = 3
DIFF_FIXED_REF_MAX = 32.0


def _bf16_parts(value, n):
    parts, rest = [], value
    for _ in range(n):
        part = float(np.float32(rest).astype(ml_dtypes.bfloat16))
        parts.append(part)
        rest -= part
    return parts


def _nt_dot(a, b):
    return lax.dot_general(a, b, (((1,), (1,)), ((), ())), preferred_element_type=F32)


def _mod_kernel(c_ref, w_ref, b_ref, o_ref):
    c = c_ref[...]
    sc = c * jax.nn.sigmoid(c)
    o_ref[...] = jnp.dot(sc, w_ref[...], precision=lax.Precision.HIGHEST,
                         preferred_element_type=F32) + b_ref[...]


def _mod_call(c_pad, w_ada, b_ada):
    rows, d = c_pad.shape
    n = w_ada.shape[1]
    tn = 1536
    return pl.pallas_call(
        _mod_kernel,
        grid=(n // tn,),
        in_specs=[pl.BlockSpec((rows, d), lambda j: (0, 0)),
                  pl.BlockSpec((d, tn), lambda j: (0, j)),
                  pl.BlockSpec((1, tn), lambda j: (0, j))],
        out_specs=pl.BlockSpec((rows, tn), lambda j: (0, j)),
        out_shape=jax.ShapeDtypeStruct((rows, n), F32),
        compiler_params=pltpu.CompilerParams(dimension_semantics=("parallel",),
                                             vmem_limit_bytes=VMEM_LIMIT),
        name="adaln_mod",
    )(c_pad, w_ada, b_ada)


_PROJ_BLOCKS = 24


def _proj_kernel(x_ref, mod_ref, g_ref, w_ref, qg_ref, kg_ref, qaug_ref,
                 dq_ref, dk_ref, dv_ref, sq_ref, sk_ref, sv_ref, *, tm, tk, slopes):
    x = x_ref[0]
    mod = mod_ref[0]
    shift, scale = mod[0:1, :], mod[1:2, :]
    ms = jnp.mean(x * x, axis=-1, keepdims=True)
    h = (x * lax.rsqrt(ms + EPS) * g_ref[...]) * (1.0 + scale) + shift
    hb = h.astype(BF16)

    lane = lax.broadcasted_iota(jnp.int32, (1, LANES), 1)
    aug = (lane >= HEAD_DIM) & (lane < HEAD_DIM + 2 * LOG2E_PARTS)
    lo_lane = (aug & ((lane & 1) == 0)).astype(F32)
    hi_lane = (aug & ((lane & 1) == 1)).astype(F32)
    row = lax.broadcasted_iota(jnp.int32, (tm, 1), 0) + pl.program_id(1) * tm
    koff = row & (tk - 1)
    koff_lo = (koff & 255).astype(F32)
    koff_hi = (koff - (koff & 255)).astype(F32)
    koff_lanes = lo_lane * koff_lo + hi_lane * koff_hi
    low = lane < HEAD_DIM

    for c in range(_PROJ_BLOCKS * LANES // MXU_COLS):
        pc = jnp.dot(hb, w_ref[:, c * MXU_COLS:(c + 1) * MXU_COLS], preferred_element_type=F32)
        for half in range(MXU_COLS // LANES):
            blk = c * (MXU_COLS // LANES) + half
            piece = pc[:, half * LANES:(half + 1) * LANES]
            if blk < 8:
                sq = piece * piece
                ss_low = jnp.sum(jnp.where(low, sq, 0.0), axis=-1, keepdims=True)
                ss_high = jnp.sum(sq, axis=-1, keepdims=True) - ss_low
                inv = jnp.where(low, lax.rsqrt(ss_low * (1.0 / HEAD_DIM) + EPS),
                                lax.rsqrt(ss_high * (1.0 / HEAD_DIM) + EPS))
                head = blk % DIFF_HEADS
                if blk < DIFF_HEADS:
                    scaled, extra, out = piece * inv * qg_ref[...], qaug_ref[...], dq_ref
                else:
                    scaled, extra, out = piece * inv * kg_ref[...], koff_lanes * slopes[head], dk_ref
                out[0, 2 * head] = (jnp.where(low, scaled, 0.0) + extra).astype(BF16)
                out[0, 2 * head + 1] = (jnp.where(low, pltpu.roll(scaled, HEAD_DIM, 1), 0.0) + extra).astype(BF16)
            elif blk < 12:
                dv_ref[0, blk - 8] = piece.astype(BF16)
            elif blk < 16:
                sq_ref[0, 2 * (blk - 12)] = jnp.where(low, piece, 0.0).astype(BF16)
                sq_ref[0, 2 * (blk - 12) + 1] = jnp.where(low, 0.0, piece).astype(BF16)
            elif blk < 20:
                sk_ref[0, blk - 16] = piece.astype(BF16)
            else:
                sv_ref[0, blk - 20] = piece.astype(BF16)


def _proj_call(x, mod, g_attn, w_perm, qg_pad, kg_pad, qaug_row, *, tm, tk, slopes):
    b, s, d = x.shape
    n = w_perm.shape[1]
    nt = s // tm

    def hm(nh):
        return (jax.ShapeDtypeStruct((b, nh, s, LANES), BF16),
                pl.BlockSpec((1, nh, tm, LANES), lambda bi, ti: (bi, 0, ti, 0)))

    outs = [hm(8), hm(8), hm(4), hm(8), hm(4), hm(4)]
    return pl.pallas_call(
        functools.partial(_proj_kernel, tm=tm, tk=tk, slopes=slopes),
        grid=(b, nt),
        in_specs=[pl.BlockSpec((1, tm, d), lambda bi, ti: (bi, ti, 0)),
                  pl.BlockSpec((1, N_MOD, d), lambda bi, ti: (bi, 0, 0)),
                  pl.BlockSpec((1, d), lambda bi, ti: (0, 0)),
                  pl.BlockSpec((d, n), lambda bi, ti: (0, 0)),
                  pl.BlockSpec((1, LANES), lambda bi, ti: (0, 0)),
                  pl.BlockSpec((1, LANES), lambda bi, ti: (0, 0)),
                  pl.BlockSpec((1, LANES), lambda bi, ti: (0, 0))],
        out_specs=[o[1] for o in outs],
        out_shape=[o[0] for o in outs],
        compiler_params=pltpu.CompilerParams(dimension_semantics=("parallel", "parallel"),
                                             vmem_limit_bytes=VMEM_LIMIT),
        name="in_proj",
    )(x, mod, g_attn, w_perm, qg_pad, kg_pad, qaug_row)


def _diff_kernel(nback_ref, lam_ref, g_ref, bound_ref, q_ref, k_ref, v_ref, o_ref, acc0_ref, acc1_ref,
                 *, t, slopes, lambda_init, fixed_ref):
    hd = pl.program_id(1)
    i = pl.program_id(2)
    hd_col = jnp.full((t, 1), hd, jnp.int32)
    slope = jnp.full((t, 1), slopes[-1] * LOG2E, F32)
    for n in range(len(slopes) - 2, -1, -1):
        slope = jnp.where(hd_col == n, slopes[n] * LOG2E, slope)

    q = (q_ref[0, 0], q_ref[0, 1])
    accs = (acc0_ref, acc1_ref)
    acc0_ref[...] = jnp.zeros_like(acc0_ref)
    acc1_ref[...] = jnp.zeros_like(acc1_ref)
    row = lax.broadcasted_iota(jnp.int32, (t, t), 0)
    col = lax.broadcasted_iota(jnp.int32, (t, t), 1)
    causal = col <= row
    ones_col = (lax.broadcasted_iota(jnp.int32, (t, LANES), 1) == 0).astype(BF16)
    row_pos = lax.broadcasted_iota(jnp.int32, (t, 1), 0) + i * t

    def block(j, carry, masked):
        ks = pl.multiple_of(j * t, t)
        v_aug = jnp.concatenate([v_ref[0, 0, pl.ds(ks, t), :], ones_col], axis=1)
        new = []
        for mp in range(2):
            s = _nt_dot(q[mp], k_ref[0, mp, pl.ds(ks, t), :])
            if masked:
                s = jnp.where(causal, s, NEG)
            if fixed_ref:
                ref = slope * (row_pos - j * t).astype(F32) + bound_ref[:, 0:1]
                p = jnp.exp2(s - ref)
                accs[mp][...] += jnp.dot(p.astype(BF16), v_aug, preferred_element_type=F32)
            else:
                off = slope * jnp.full((t, 1), j * t, jnp.int32).astype(F32)
                m = carry[mp]
                m_new = jnp.maximum(m, jnp.max(s, axis=-1, keepdims=True) + off)
                p = jnp.exp2(s - (m_new - off))
                accs[mp][...] = (jnp.exp2(m - m_new) * accs[mp][...]
                                 + jnp.dot(p.astype(BF16), v_aug, preferred_element_type=F32))
                new.append(m_new)
        return tuple(new)

    first = jnp.maximum(i + 1 - nback_ref[hd], 0)
    n_full = i - first
    n_bunches = n_full // DIFF_BLOCKS_PER_ITER

    def bunch(bb, carry):
        for n in range(DIFF_BLOCKS_PER_ITER):
            carry = block(first + bb * DIFF_BLOCKS_PER_ITER + n, carry, False)
        return carry

    init = () if fixed_ref else (jnp.full((t, 1), NEG, F32),) * 2
    carry = lax.fori_loop(0, n_bunches, bunch, init)
    carry = lax.fori_loop(first + n_bunches * DIFF_BLOCKS_PER_ITER, i, lambda j, c: block(j, c, False), carry)
    block(i, carry, True)

    lp = lam_ref[...]
    lam = (jnp.exp(jnp.sum(lp[0:1] * lp[1:2], axis=-1, keepdims=True))
           - jnp.exp(jnp.sum(lp[2:3] * lp[3:4], axis=-1, keepdims=True)) + lambda_init)
    o = (acc0_ref[:, 0:LANES] / acc0_ref[:, LANES:LANES + 1]
         - lam * (acc1_ref[:, 0:LANES] / acc1_ref[:, LANES:LANES + 1]))
    ms = jnp.mean(o * o, axis=-1, keepdims=True)
    o_ref[0] = ((o * lax.rsqrt(ms + EPS) * g_ref[...]) * (1.0 - lambda_init)).astype(BF16)


def _diff_call(n_back, lam_p, g_out, bound_row, dq, dk, dv, *, t, slopes, lambda_init, fixed_ref):
    b, _, s, _ = dq.shape
    const2 = lambda bi, hi, qi: (0, 0)
    return pl.pallas_call(
        functools.partial(_diff_kernel, t=t, slopes=slopes, lambda_init=lambda_init, fixed_ref=fixed_ref),
        grid=(b, DIFF_HEADS, s // t),
        in_specs=[pl.BlockSpec(memory_space=pltpu.SMEM),
                  pl.BlockSpec((4, HEAD_DIM), const2),
                  pl.BlockSpec((1, LANES), const2),
                  pl.BlockSpec((1, LANES), const2),
                  pl.BlockSpec((1, 2, t, LANES), lambda bi, hi, qi: (bi, hi, qi, 0)),
                  pl.BlockSpec((1, 2, s, LANES), lambda bi, hi, qi: (bi, hi, 0, 0)),
                  pl.BlockSpec((1, 1, s, LANES), lambda bi, hi, qi: (bi, hi, 0, 0))],
        out_specs=pl.BlockSpec((1, t, LANES), lambda bi, hi, qi: (bi, qi, hi)),
        out_shape=jax.ShapeDtypeStruct((b, s, DIFF_HEADS * LANES), BF16),
        scratch_shapes=[pltpu.VMEM((t, 2 * LANES), F32), pltpu.VMEM((t, 2 * LANES), F32)],
        compiler_params=pltpu.CompilerParams(dimension_semantics=("parallel", "parallel", "parallel"),
                                             vmem_limit_bytes=VMEM_LIMIT),
        name="diff_attn_fixed_ref" if fixed_ref else "diff_attn_running_max",
    )(n_back, lam_p, g_out, bound_row, dq, dk, dv)


def _sb_kernel(g_ref, q_ref, k_ref, v_ref, o_ref, acc_ref, *, t, pairs):
    i = pl.program_id(2)
    heads = 2 * pairs
    acc_ref[...] = jnp.zeros_like(acc_ref)
    row = lax.broadcasted_iota(jnp.int32, (t, t), 0)
    col = lax.broadcasted_iota(jnp.int32, (t, t), 1)
    past = col < row
    later = (row > col).astype(BF16)

    def block(j, carry, masked):
        ks = pl.multiple_of(j * t, t)
        new = []
        for hh in range(heads):
            k = k_ref[0, hh // 2, pl.ds(ks, t), :]
            v = v_ref[0, hh // 2, pl.ds(ks, t), :]
            z = _nt_dot(q_ref[0, hh], k)
            neg_abs = -jnp.abs(z)
            log_rem = 0.5 * (neg_abs - z) - jnp.log2(1.0 + jnp.exp2(neg_abs))
            log_beta = log_rem + z
            if masked:
                log_rem = jnp.where(past, log_rem, 0.0)
            new.append(carry[hh] + jnp.sum(log_rem, axis=-1, keepdims=True))
            suffix = jnp.dot(log_rem.astype(BF16), later, preferred_element_type=F32)
            a = jnp.exp2(log_beta + suffix + carry[hh])
            if masked:
                a = jnp.where(past, a, 0.0)
            acc_ref[hh] += jnp.dot(a.astype(BF16), v, preferred_element_type=F32)
        return tuple(new)

    def any_live(carry):
        top = functools.reduce(jnp.maximum, carry)
        return (jnp.max(top) > LOG2E * F32_DEAD_LOG).astype(jnp.int32)

    def live_cond(state):
        jj, live = state[0], state[1]
        return jnp.logical_and(jj < i, live > 0)

    def live_body(state):
        carry = block(i - 1 - state[0], state[2:], False)
        return (state[0] + 1, any_live(carry)) + carry

    carry = block(i, (jnp.zeros((t, 1), F32),) * heads, True)
    lax.while_loop(live_cond, live_body, (jnp.int32(0), any_live(carry)) + carry)

    lane = lax.broadcasted_iota(jnp.int32, (1, LANES), 1)
    first = lane < HEAD_DIM
    for pr in range(pairs):
        o = jnp.where(first, acc_ref[2 * pr], acc_ref[2 * pr + 1])
        sq = o * o
        s_first = jnp.sum(jnp.where(first, sq, 0.0), axis=-1, keepdims=True)
        s_all = jnp.sum(sq, axis=-1, keepdims=True)
        ms = jnp.where(first, s_first, s_all - s_first) * (1.0 / HEAD_DIM)
        o_ref[0, :, pr * LANES:(pr + 1) * LANES] = (o * lax.rsqrt(ms + EPS) * g_ref[...]).astype(BF16)


def _sb_call(g_pair, sq, sk, sv, *, t, pairs):
    b, _, s, _ = sq.shape
    groups = SB_HEADS // 2 // pairs
    return pl.pallas_call(
        functools.partial(_sb_kernel, t=t, pairs=pairs),
        grid=(b, groups, s // t),
        in_specs=[pl.BlockSpec((1, LANES), lambda bi, gi, qi: (0, 0)),
                  pl.BlockSpec((1, 2 * pairs, t, LANES), lambda bi, gi, qi: (bi, gi, qi, 0)),
                  pl.BlockSpec((1, pairs, s, LANES), lambda bi, gi, qi: (bi, gi, 0, 0)),
                  pl.BlockSpec((1, pairs, s, LANES), lambda bi, gi, qi: (bi, gi, 0, 0))],
        out_specs=pl.BlockSpec((1, t, pairs * LANES), lambda bi, gi, qi: (bi, qi, gi)),
        out_shape=jax.ShapeDtypeStruct((b, s, (SB_HEADS // 2) * LANES), BF16),
        scratch_shapes=[pltpu.VMEM((2 * pairs, t, LANES), F32)],
        compiler_params=pltpu.CompilerParams(dimension_semantics=("parallel", "parallel", "parallel"),
                                             vmem_limit_bytes=VMEM_LIMIT),
        name="sb_attn",
    )(g_pair, sq, sk, sv)


def _outproj_kernel(a_ref, b_ref, w_ref, x_ref, mod_ref, g_ref, wr_ref, br_ref,
                    x1_ref, h2_ref, comb_ref):
    half = a_ref.shape[-1]
    y = (jnp.dot(a_ref[0], w_ref[0:half, :], preferred_element_type=F32)
         + jnp.dot(b_ref[0], w_ref[half:, :], preferred_element_type=F32))
    mod = mod_ref[0]
    gate_a, shift, scale = mod[2:3, :], mod[3:4, :], mod[4:5, :]
    x1 = x_ref[0] + gate_a * y
    x1_ref[0] = x1
    ms = jnp.mean(x1 * x1, axis=-1, keepdims=True)
    h2 = (x1 * lax.rsqrt(ms + EPS) * g_ref[...]) * (1.0 + scale) + shift
    h2_hi = h2.astype(BF16)
    h2_ref[0] = h2_hi

    h2_lo = (h2 - h2_hi.astype(F32)).astype(BF16)
    wr_hi, wr_lo = wr_ref[0], wr_ref[1]
    logits = (jnp.dot(h2_hi, wr_hi, preferred_element_type=F32)
              + jnp.dot(h2_lo, wr_hi, preferred_element_type=F32)
              + jnp.dot(h2_hi, wr_lo, preferred_element_type=F32)) + br_ref[...]
    lane = lax.broadcasted_iota(jnp.int32, logits.shape, 1).astype(F32)
    big = jnp.float32(LANES)

    def top(vals):
        mx = jnp.max(vals, axis=-1, keepdims=True)
        idx = jnp.min(jnp.where(vals == mx, lane, big), axis=-1, keepdims=True)
        return mx, idx

    is_group = (lane >= N_EXPERTS) & (lane < N_EXPERTS + N_GROUPS)
    g_logits = jnp.where(is_group, logits, NEG)
    g_max, g_idx = top(g_logits)
    gate_group = 1.0 / jnp.sum(jnp.where(is_group, jnp.exp(logits - g_max), 0.0), axis=-1, keepdims=True)
    g_sel = g_idx - N_EXPERTS
    in_group = (lane >= g_sel * EXPERTS_PER_GROUP) & (lane < (g_sel + 1.0) * EXPERTS_PER_GROUP)
    e_logits = jnp.where(in_group, logits, NEG)
    v1, i1 = top(e_logits)
    v2, i2 = top(jnp.where(lane == i1, NEG, e_logits))
    e2 = jnp.exp(v2 - v1)
    w_first = 1.0 / (1.0 + e2)
    w_second = e2 / (1.0 + e2)
    comb_ref[0] = (gate_group * (jnp.where(lane == i1, w_first, 0.0) + jnp.where(lane == i2, w_second, 0.0))
                   + jnp.where(lane == g_idx, 1.0, 0.0))


def _outproj_call(a_out, b_out, w_out, x, mod, g_ffn, w_router, b_router, *, tm):
    b, s, d = x.shape
    half = a_out.shape[-1]
    idx3 = lambda bi, ti: (bi, ti, 0)
    const2 = lambda bi, ti: (0, 0)
    return pl.pallas_call(
        _outproj_kernel,
        grid=(b, s // tm),
        in_specs=[pl.BlockSpec((1, tm, half), idx3),
                  pl.BlockSpec((1, tm, half), idx3),
                  pl.BlockSpec((2 * half, d), const2),
                  pl.BlockSpec((1, tm, d), idx3),
                  pl.BlockSpec((1, N_MOD, d), lambda bi, ti: (bi, 0, 0)),
                  pl.BlockSpec((1, d), const2),
                  pl.BlockSpec((2, d, LANES), lambda bi, ti: (0, 0, 0)),
                  pl.BlockSpec((1, LANES), const2)],
        out_specs=[pl.BlockSpec((1, tm, d), idx3),
                   pl.BlockSpec((1, tm, d), idx3),
                   pl.BlockSpec((1, tm, LANES), idx3)],
        out_shape=[jax.ShapeDtypeStruct((b, s, d), F32),
                   jax.ShapeDtypeStruct((b, s, d), BF16),
                   jax.ShapeDtypeStruct((b, s, LANES), F32)],
        compiler_params=pltpu.CompilerParams(dimension_semantics=("parallel", "parallel"),
                                             vmem_limit_bytes=VMEM_LIMIT),
        name="out_proj_router",
    )(a_out, b_out, w_out, x, mod, g_ffn, w_router, b_router)


def _moe_kernel(h_ref, comb_ref, earlier_ref, w1_ref, w3_ref, w2_ref, x1_ref, mod_ref, o_ref, *, unit, max_units):
    g = pl.program_id(2)
    tm = h_ref.shape[1]

    @pl.when(g == 0)
    def _():
        o_ref[0] = x1_ref[0]

    comb = comb_ref[0]
    lane = lax.broadcasted_iota(jnp.int32, (1, LANES), 1)
    member = jnp.sum(jnp.where(lane == N_EXPERTS + g, comb, 0.0), axis=-1, keepdims=True)
    member_wide = jnp.broadcast_to(member, (tm, LANES))
    rank = jnp.dot(earlier_ref[...], member_wide.astype(BF16), preferred_element_type=F32)
    slot_wide = jnp.where(member_wide > 0.0, rank, -1.0)
    slot_col = slot_wide[:, 0:1]
    slot_row = slot_wide.T[0:1, :]
    count = jnp.sum(member)
    units = jnp.int32(0)
    for c in range(tm // unit):
        units += (count > float(c * unit)).astype(jnp.int32)

    comb_hi = comb.astype(BF16)
    comb_lo = (comb - comb_hi.astype(F32)).astype(BF16)
    gate_f = mod_ref[0][5:6, :]

    def chunk(first_unit, ch):
        base = (first_unit * unit).astype(F32)
        want_col = lax.broadcasted_iota(jnp.int32, (ch, 1), 0).astype(F32) + base
        want_row = lax.broadcasted_iota(jnp.int32, (1, ch), 1).astype(F32) + base
        pick = (slot_row == want_col).astype(BF16)
        place = (slot_col == want_row).astype(BF16)
        xc = jnp.dot(pick, h_ref[0], preferred_element_type=F32).astype(BF16)
        cw = (jnp.dot(pick, comb_hi, preferred_element_type=F32)
              + jnp.dot(pick, comb_lo, preferred_element_type=F32))
        y = jnp.zeros((ch, o_ref.shape[-1]), F32)
        for e in range(EXPERTS_PER_GROUP):
            weight = jnp.sum(jnp.where(lane == g * EXPERTS_PER_GROUP + e, cw, 0.0), axis=-1, keepdims=True)
            h1 = jnp.dot(xc, w1_ref[0, e], preferred_element_type=F32)
            h3 = jnp.dot(xc, w3_ref[0, e], preferred_element_type=F32)
            hg = (h1 * jax.nn.sigmoid(h1)) * h3 * weight
            y += jnp.dot(hg.astype(BF16), w2_ref[0, e], preferred_element_type=F32)
        o_ref[0] += gate_f * jnp.dot(place, y.astype(BF16), preferred_element_type=F32)

    def step(done):
        todo = jnp.minimum(units - done, max_units)
        for n in range(1, max_units + 1):
            pl.when(todo == n)(functools.partial(chunk, done, n * unit))
        return done + todo

    lax.while_loop(lambda done: done < units, step, jnp.int32(0))


def _moe_call(h2, comb, w1, w3, w2, x1, mod, *, tm, unit, max_units):
    b, s, d = x1.shape
    f = w1.shape[-1]
    tok3 = lambda bi, ti, gi: (bi, ti, 0)
    return pl.pallas_call(
        functools.partial(_moe_kernel, unit=unit, max_units=max_units),
        grid=(b, s // tm, N_GROUPS),
        in_specs=[pl.BlockSpec((1, tm, d), tok3),
                  pl.BlockSpec((1, tm, LANES), tok3),
                  pl.BlockSpec((tm, tm), lambda bi, ti, gi: (0, 0)),
                  pl.BlockSpec((1, EXPERTS_PER_GROUP, d, f), lambda bi, ti, gi: (gi, 0, 0, 0)),
                  pl.BlockSpec((1, EXPERTS_PER_GROUP, d, f), lambda bi, ti, gi: (gi, 0, 0, 0)),
                  pl.BlockSpec((1, EXPERTS_PER_GROUP, f, d), lambda bi, ti, gi: (gi, 0, 0, 0)),
                  pl.BlockSpec((1, tm, d), tok3),
                  pl.BlockSpec((1, N_MOD, d), lambda bi, ti, gi: (bi, 0, 0))],
        out_specs=pl.BlockSpec((1, tm, d), tok3),
        out_shape=jax.ShapeDtypeStruct((b, s, d), F32),
        compiler_params=pltpu.CompilerParams(dimension_semantics=("parallel", "parallel", "arbitrary"),
                                             vmem_limit_bytes=VMEM_LIMIT),
        name="moe_experts",
    )(h2, comb, jnp.tri(tm, k=-1, dtype=BF16), w1, w3, w2, x1, mod)


def _prepare_w_in(w_in):
    sq_start = 2 * DIFF_HEADS * 2 * HEAD_DIM + DIFF_HEADS * DIFF_VDIM
    col = np.ones((1, w_in.shape[1]), np.float32)
    col[0, sq_start:sq_start + SB_HEADS * HEAD_DIM] = LOG2E / math.sqrt(HEAD_DIM)
    return (w_in * col).astype(BF16)


def _pad_lanes(v, width=LANES):
    v = v.reshape(1, -1)
    return jnp.pad(v, ((0, 0), (0, width - v.shape[1])))


def kernel(x, c, w_ada, b_ada, g_attn, w_in, q_norm_g, k_norm_g, lambda_q1, lambda_k1, lambda_q2, lambda_k2,
           diff_out_g, sb_out_g, w_out, g_ffn, w_group, b_group, w_erouter, b_expert, w1, w3, w2):
    b, s, d = x.shape
    depth = w_ada.shape[0]
    tile = min(512, s)
    moe_tile = min(1024, s)
    moe_unit = min(128, moe_tile)
    moe_max_units = 4
    sb_tile = min(MXU_COLS, s)
    slopes = tuple(2.0 ** (-8.0 * (n + 1) / DIFF_HEADS) for n in range(DIFF_HEADS))
    qk_scale = 1.0 / math.sqrt(HEAD_DIM)
    c_pad = jnp.pad(c, ((0, 8 - b), (0, 0)))
    qaug = np.zeros((1, LANES), np.float32)
    qaug[0, HEAD_DIM:HEAD_DIM + 2 * LOG2E_PARTS] = np.repeat(_bf16_parts(LOG2E, LOG2E_PARTS), 2)
    qaug_row = jnp.asarray(qaug)

    for layer in range(depth):
        lambda_init = 0.8 - 0.6 * math.exp(-0.3 * layer)
        mod = _mod_call(c_pad, w_ada[layer], b_ada[layer].reshape(1, -1))[:b].reshape(b, N_MOD, d)

        dq, dk, dv, sq, sk, sv = _proj_call(
            x, mod, g_attn[layer].reshape(1, d), _prepare_w_in(w_in[layer]),
            jnp.tile(q_norm_g[layer] * (qk_scale * LOG2E), 2).reshape(1, LANES),
            jnp.tile(k_norm_g[layer], 2).reshape(1, LANES), qaug_row,
            tm=tile, tk=tile, slopes=slopes)

        lam_p = jnp.stack([lambda_q1[layer], lambda_k1[layer], lambda_q2[layer], lambda_k2[layer]])
        score_bound = (1.02 * HEAD_DIM * qk_scale) * jnp.max(jnp.abs(q_norm_g[layer])) * jnp.max(jnp.abs(k_norm_g[layer]))
        dead_dist = (2.0 * score_bound - F32_DEAD_LOG) / jnp.asarray(slopes, F32)
        n_back = jnp.clip(jnp.floor(dead_dist / tile) + 2.0, 1.0, s // tile).astype(jnp.int32)
        diff_args = (n_back, lam_p, diff_out_g[layer].reshape(1, LANES),
                     jnp.full((1, LANES), score_bound * LOG2E, F32), dq, dk, dv)
        diff_kw = dict(t=tile, slopes=slopes, lambda_init=lambda_init)
        a_out = lax.cond(score_bound <= DIFF_FIXED_REF_MAX,
                         lambda args: _diff_call(*args, fixed_ref=True, **diff_kw),
                         lambda args: _diff_call(*args, fixed_ref=False, **diff_kw),
                         diff_args)
        b_out = _sb_call(jnp.tile(sb_out_g[layer], 2).reshape(1, LANES), sq, sk, sv, t=sb_tile,
                         pairs=SB_PAIRS_PER_STEP)

        w_router = jnp.concatenate(
            [jnp.transpose(w_erouter[layer], (1, 0, 2)).reshape(d, N_EXPERTS), w_group[layer]], axis=1)
        w_router = jnp.pad(w_router, ((0, 0), (0, LANES - w_router.shape[1])))
        w_router_hi = w_router.astype(BF16)
        w_router = jnp.stack([w_router_hi, (w_router - w_router_hi.astype(F32)).astype(BF16)])
        b_router = _pad_lanes(jnp.concatenate([b_expert[layer].reshape(-1), b_group[layer]]))
        x1, h2, comb = _outproj_call(a_out, b_out, w_out[layer].astype(BF16), x, mod,
                                     g_ffn[layer].reshape(1, d), w_router, b_router, tm=tile)

        x = _moe_call(h2, comb, w1[layer].astype(BF16), w3[layer].astype(BF16), w2[layer].astype(BF16),
                      x1, mod, tm=moe_tile, unit=moe_unit, max_units=moe_max_units)
    return x
```

```python
import functools
import math

import jax
import jax.numpy as jnp
import ml_dtypes
import numpy as np
from jax import lax
from jax.experimental import pallas as pl
from jax.experimental.pallas import tpu as pltpu

HEAD_DIM = 64
DIFF_HEADS = 4
SB_HEADS = 8
DIFF_VDIM = 2 * HEAD_DIM
N_GROUPS = 4
EXPERTS_PER_GROUP = 8
N_EXPERTS = N_GROUPS * EXPERTS_PER_GROUP
D_EXPERT = 256
N_MOD = 6
EPS = 1e-6
LANES = 128
MXU_COLS = 256
NEG = -1e30
DIFF_BLOCKS_PER_ITER = 4
SB_PAIRS_PER_STEP = 4
F32_DEAD_LOG = -104.0
VMEM_LIMIT = 56 * 1024 * 1024

F32 = jnp.float32
BF16 = jnp.bfloat16

LOG2E = math.log2(math.e)
LOG2E_PARTS = 3
DIFF_FIXED_REF_MAX = 32.0


def _bf16_parts(value, n):
    parts, rest = [], value
    for _ in range(n):
        part = float(np.float32(rest).astype(ml_dtypes.bfloat16))
        parts.append(part)
        rest -= part
    return parts


def _nt_dot(a, b):
    return lax.dot_general(a, b, (((1,), (1,)), ((), ())), preferred_element_type=F32)


def _mod_kernel(c_ref, w_ref, b_ref, o_ref):
    c = c_ref[...]
    sc = c * jax.nn.sigmoid(c)
    w = w_ref[...]
    sc_hi, w_hi = sc.astype(BF16), w.astype(BF16)
    sc_lo, w_lo = (sc - sc_hi.astype(F32)).astype(BF16), (w - w_hi.astype(F32)).astype(BF16)
    o_ref[...] = (jnp.dot(sc_hi, w_hi, preferred_element_type=F32)
                  + jnp.dot(sc_lo, w_hi, preferred_element_type=F32)
                  + jnp.dot(sc_hi, w_lo, preferred_element_type=F32)) + b_ref[...]


def _mod_call(c_pad, w_ada, b_ada):
    rows, d = c_pad.shape
    n = w_ada.shape[1]
    tn = 1536
    return pl.pallas_call(
        _mod_kernel,
        grid=(n // tn,),
        in_specs=[pl.BlockSpec((rows, d), lambda j: (0, 0)),
                  pl.BlockSpec((d, tn), lambda j: (0, j)),
                  pl.BlockSpec((1, tn), lambda j: (0, j))],
        out_specs=pl.BlockSpec((rows, tn), lambda j: (0, j)),
        out_shape=jax.ShapeDtypeStruct((rows, n), F32),
        compiler_params=pltpu.CompilerParams(dimension_semantics=("parallel",),
                                             vmem_limit_bytes=VMEM_LIMIT),
        name="adaln_mod",
    )(c_pad, w_ada, b_ada)


_PROJ_BLOCKS = 24


def _proj_kernel(x_ref, mod_ref, g_ref, w_ref, qg_ref, kg_ref, qaug_ref,
                 dq_ref, dk_ref, dv_ref, sq_ref, sk_ref, sv_ref, *, tm, tk, slopes):
    x = x_ref[0]
    mod = mod_ref[0]
    shift, scale = mod[0:1, :], mod[1:2, :]
    ms = jnp.mean(x * x, axis=-1, keepdims=True)
    h = (x * lax.rsqrt(ms + EPS) * g_ref[...]) * (1.0 + scale) + shift
    hb = h.astype(BF16)

    lane = lax.broadcasted_iota(jnp.int32, (1, LANES), 1)
    aug = (lane >= HEAD_DIM) & (lane < HEAD_DIM + 2 * LOG2E_PARTS)
    lo_lane = (aug & ((lane & 1) == 0)).astype(F32)
    hi_lane = (aug & ((lane & 1) == 1)).astype(F32)
    row = lax.broadcasted_iota(jnp.int32, (tm, 1), 0) + pl.program_id(1) * tm
    koff = row & (tk - 1)
    koff_lo = (koff & 255).astype(F32)
    koff_hi = (koff - (koff & 255)).astype(F32)
    koff_lanes = lo_lane * koff_lo + hi_lane * koff_hi
    low = lane < HEAD_DIM

    for c in range(_PROJ_BLOCKS * LANES // MXU_COLS):
        pc = jnp.dot(hb, w_ref[:, c * MXU_COLS:(c + 1) * MXU_COLS], preferred_element_type=F32)
        for half in range(MXU_COLS // LANES):
            blk = c * (MXU_COLS // LANES) + half
            piece = pc[:, half * LANES:(half + 1) * LANES]
            if blk < 8:
                sq = piece * piece
                ss_low = jnp.sum(jnp.where(low, sq, 0.0), axis=-1, keepdims=True)
                ss_high = jnp.sum(sq, axis=-1, keepdims=True) - ss_low
                inv = jnp.where(low, lax.rsqrt(ss_low * (1.0 / HEAD_DIM) + EPS),
                                lax.rsqrt(ss_high * (1.0 / HEAD_DIM) + EPS))
                head = blk % DIFF_HEADS
                if blk < DIFF_HEADS:
                    scaled, extra, out = piece * inv * qg_ref[...], qaug_ref[...], dq_ref
                else:
                    scaled, extra, out = piece * inv * kg_ref[...], koff_lanes * slopes[head], dk_ref
                out[0, 2 * head] = (jnp.where(low, scaled, 0.0) + extra).astype(BF16)
                out[0, 2 * head + 1] = (jnp.where(low, pltpu.roll(scaled, HEAD_DIM, 1), 0.0) + extra).astype(BF16)
            elif blk < 12:
                dv_ref[0, blk - 8] = piece.astype(BF16)
            elif blk < 16:
                sq_ref[0, 2 * (blk - 12)] = jnp.where(low, piece, 0.0).astype(BF16)
                sq_ref[0, 2 * (blk - 12) + 1] = jnp.where(low, 0.0, piece).astype(BF16)
            elif blk < 20:
                sk_ref[0, blk - 16] = piece.astype(BF16)
            else:
                sv_ref[0, blk - 20] = piece.astype(BF16)


def _proj_call(x, mod, g_attn, w_perm, qg_pad, kg_pad, qaug_row, *, tm, tk, slopes):
    b, s, d = x.shape
    n = w_perm.shape[1]
    nt = s // tm

    def hm(nh):
        return (jax.ShapeDtypeStruct((b, nh, s, LANES), BF16),
                pl.BlockSpec((1, nh, tm, LANES), lambda bi, ti: (bi, 0, ti, 0)))

    outs = [hm(8), hm(8), hm(4), hm(8), hm(4), hm(4)]
    return pl.pallas_call(
        functools.partial(_proj_kernel, tm=tm, tk=tk, slopes=slopes),
        grid=(b, nt),
        in_specs=[pl.BlockSpec((1, tm, d), lambda bi, ti: (bi, ti, 0)),
                  pl.BlockSpec((1, N_MOD, d), lambda bi, ti: (bi, 0, 0)),
                  pl.BlockSpec((1, d), lambda bi, ti: (0, 0)),
                  pl.BlockSpec((d, n), lambda bi, ti: (0, 0)),
                  pl.BlockSpec((1, LANES), lambda bi, ti: (0, 0)),
                  pl.BlockSpec((1, LANES), lambda bi, ti: (0, 0)),
                  pl.BlockSpec((1, LANES), lambda bi, ti: (0, 0))],
        out_specs=[o[1] for o in outs],
        out_shape=[o[0] for o in outs],
        compiler_params=pltpu.CompilerParams(dimension_semantics=("parallel", "parallel"),
                                             vmem_limit_bytes=VMEM_LIMIT),
        name="in_proj",
    )(x, mod, g_attn, w_perm, qg_pad, kg_pad, qaug_row)


def _diff_kernel(nback_ref, lam_ref, g_ref, bound_ref, q_ref, k_ref, v_ref, o_ref, acc0_ref, acc1_ref,
                 *, t, slopes, lambda_init, fixed_ref):
    hd = pl.program_id(1)
    i = pl.program_id(2)
    hd_col = jnp.full((t, 1), hd, jnp.int32)
    slope = jnp.full((t, 1), slopes[-1] * LOG2E, F32)
    for n in range(len(slopes) - 2, -1, -1):
        slope = jnp.where(hd_col == n, slopes[n] * LOG2E, slope)

    q = (q_ref[0, 0], q_ref[0, 1])
    accs = (acc0_ref, acc1_ref)
    acc0_ref[...] = jnp.zeros_like(acc0_ref)
    acc1_ref[...] = jnp.zeros_like(acc1_ref)
    row = lax.broadcasted_iota(jnp.int32, (t, t), 0)
    col = lax.broadcasted_iota(jnp.int32, (t, t), 1)
    causal = col <= row
    ones_col = (lax.broadcasted_iota(jnp.int32, (t, LANES), 1) == 0).astype(BF16)
    row_pos = lax.broadcasted_iota(jnp.int32, (t, 1), 0) + i * t

    def block(j, carry, masked):
        ks = pl.multiple_of(j * t, t)
        v_aug = jnp.concatenate([v_ref[0, 0, pl.ds(ks, t), :], ones_col], axis=1)
        new = []
        for mp in range(2):
            s = _nt_dot(q[mp], k_ref[0, mp, pl.ds(ks, t), :])
            if masked:
                s = jnp.where(causal, s, NEG)
            if fixed_ref:
                ref = slope * (row_pos - j * t).astype(F32) + bound_ref[:, 0:1]
                p = jnp.exp2(s - ref)
                accs[mp][...] += jnp.dot(p.astype(BF16), v_aug, preferred_element_type=F32)
            else:
                off = slope * jnp.full((t, 1), j * t, jnp.int32).astype(F32)
                m = carry[mp]
                m_new = jnp.maximum(m, jnp.max(s, axis=-1, keepdims=True) + off)
                p = jnp.exp2(s - (m_new - off))
                accs[mp][...] = (jnp.exp2(m - m_new) * accs[mp][...]
                                 + jnp.dot(p.astype(BF16), v_aug, preferred_element_type=F32))
                new.append(m_new)
        return tuple(new)

    first = jnp.maximum(i + 1 - nback_ref[hd], 0)
    n_full = i - first
    n_bunches = n_full // DIFF_BLOCKS_PER_ITER

    def bunch(bb, carry):
        for n in range(DIFF_BLOCKS_PER_ITER):
            carry = block(first + bb * DIFF_BLOCKS_PER_ITER + n, carry, False)
        return carry

    init = () if fixed_ref else (jnp.full((t, 1), NEG, F32),) * 2
    carry = lax.fori_loop(0, n_bunches, bunch, init)
    carry = lax.fori_loop(first + n_bunches * DIFF_BLOCKS_PER_ITER, i, lambda j, c: block(j, c, False), carry)
    block(i, carry, True)

    lp = lam_ref[...]
    lam = (jnp.exp(jnp.sum(lp[0:1] * lp[1:2], axis=-1, keepdims=True))
           - jnp.exp(jnp.sum(lp[2:3] * lp[3:4], axis=-1, keepdims=True)) + lambda_init)
    o = (acc0_ref[:, 0:LANES] / acc0_ref[:, LANES:LANES + 1]
         - lam * (acc1_ref[:, 0:LANES] / acc1_ref[:, LANES:LANES + 1]))
    ms = jnp.mean(o * o, axis=-1, keepdims=True)
    o_ref[0] = ((o * lax.rsqrt(ms + EPS) * g_ref[...]) * (1.0 - lambda_init)).astype(BF16)


def _diff_call(n_back, lam_p, g_out, bound_row, dq, dk, dv, *, t, slopes, lambda_init, fixed_ref):
    b, _, s, _ = dq.shape
    const2 = lambda bi, hi, qi: (0, 0)
    return pl.pallas_call(
        functools.partial(_diff_kernel, t=t, slopes=slopes, lambda_init=lambda_init, fixed_ref=fixed_ref),
        grid=(b, DIFF_HEADS, s // t),
        in_specs=[pl.BlockSpec(memory_space=pltpu.SMEM),
                  pl.BlockSpec((4, HEAD_DIM), const2),
                  pl.BlockSpec((1, LANES), const2),
                  pl.BlockSpec((1, LANES), const2),
                  pl.BlockSpec((1, 2, t, LANES), lambda bi, hi, qi: (bi, hi, qi, 0)),
                  pl.BlockSpec((1, 2, s, LANES), lambda bi, hi, qi: (bi, hi, 0, 0)),
                  pl.BlockSpec((1, 1, s, LANES), lambda bi, hi, qi: (bi, hi, 0, 0))],
        out_specs=pl.BlockSpec((1, t, LANES), lambda bi, hi, qi: (bi, qi, hi)),
        out_shape=jax.ShapeDtypeStruct((b, s, DIFF_HEADS * LANES), BF16),
        scratch_shapes=[pltpu.VMEM((t, 2 * LANES), F32), pltpu.VMEM((t, 2 * LANES), F32)],
        compiler_params=pltpu.CompilerParams(dimension_semantics=("parallel", "parallel", "parallel"),
                                             vmem_limit_bytes=VMEM_LIMIT),
        name="diff_attn_fixed_ref" if fixed_ref else "diff_attn_running_max",
    )(n_back, lam_p, g_out, bound_row, dq, dk, dv)


def _sb_kernel(g_ref, q_ref, k_ref, v_ref, o_ref, acc_ref, *, t, tk, pairs):
    i = pl.program_id(2)
    heads = 2 * pairs
    diag_blocks = t // tk
    acc_ref[...] = jnp.zeros_like(acc_ref)
    row = lax.broadcasted_iota(jnp.int32, (t, tk), 0)
    col = lax.broadcasted_iota(jnp.int32, (t, tk), 1)
    later = (lax.broadcasted_iota(jnp.int32, (tk, tk), 0) > lax.broadcasted_iota(jnp.int32, (tk, tk), 1)).astype(BF16)

    def block(j, carry, past):
        masked = past is not None
        ks = pl.multiple_of(j * tk, tk)
        new = []
        for hh in range(heads):
            k = k_ref[0, hh // 2, pl.ds(ks, tk), :]
            v = v_ref[0, hh // 2, pl.ds(ks, tk), :]
            z = _nt_dot(q_ref[0, hh], k)
            neg_abs = -jnp.abs(z)
            log_rem = 0.5 * (neg_abs - z) - jnp.log2(1.0 + jnp.exp2(neg_abs))
            log_beta = log_rem + z
            if masked:
                log_rem = jnp.where(past, log_rem, 0.0)
            new.append(carry[hh] + jnp.sum(log_rem, axis=-1, keepdims=True))
            suffix = jnp.dot(log_rem.astype(BF16), later, preferred_element_type=F32)
            a = jnp.exp2(log_beta + suffix + carry[hh])
            if masked:
                a = jnp.where(past, a, 0.0)
            acc_ref[hh] += jnp.dot(a.astype(BF16), v, preferred_element_type=F32)
        return tuple(new)

    def any_live(carry):
        top = functools.reduce(jnp.maximum, carry)
        return (jnp.max(top) > LOG2E * F32_DEAD_LOG).astype(jnp.int32)

    older = i * diag_blocks - 1

    def live_cond(state):
        jj, live = state[0], state[1]
        return jnp.logical_and(jj < older, live > 0)

    def live_body(state):
        carry = block(older - 1 - state[0], state[2:], None)
        return (state[0] + 1, any_live(carry)) + carry

    carry = (jnp.zeros((t, 1), F32),) * heads
    for d in reversed(range(diag_blocks)):
        carry = block(i * diag_blocks + d, carry, col + d * tk < row)
    exists = jnp.where(jnp.full((t, 1), older, jnp.int32) >= 0, 0.0, NEG)
    carry = block(jnp.maximum(older, 0), tuple(c + exists for c in carry), None)
    lax.while_loop(live_cond, live_body, (jnp.int32(0), any_live(carry)) + carry)

    lane = lax.broadcasted_iota(jnp.int32, (1, LANES), 1)
    first = lane < HEAD_DIM
    for pr in range(pairs):
        o = jnp.where(first, acc_ref[2 * pr], acc_ref[2 * pr + 1])
        sq = o * o
        s_first = jnp.sum(jnp.where(first, sq, 0.0), axis=-1, keepdims=True)
        s_all = jnp.sum(sq, axis=-1, keepdims=True)
        ms = jnp.where(first, s_first, s_all - s_first) * (1.0 / HEAD_DIM)
        o_ref[0, :, pr * LANES:(pr + 1) * LANES] = (o * lax.rsqrt(ms + EPS) * g_ref[...]).astype(BF16)


def _sb_call(g_pair, sq, sk, sv, *, t, tk, pairs):
    b, _, s, _ = sq.shape
    groups = SB_HEADS // 2 // pairs
    return pl.pallas_call(
        functools.partial(_sb_kernel, t=t, tk=tk, pairs=pairs),
        grid=(b, groups, s // t),
        in_specs=[pl.BlockSpec((1, LANES), lambda bi, gi, qi: (0, 0)),
                  pl.BlockSpec((1, 2 * pairs, t, LANES), lambda bi, gi, qi: (bi, gi, qi, 0)),
                  pl.BlockSpec((1, pairs, s, LANES), lambda bi, gi, qi: (bi, gi, 0, 0)),
                  pl.BlockSpec((1, pairs, s, LANES), lambda bi, gi, qi: (bi, gi, 0, 0))],
        out_specs=pl.BlockSpec((1, t, pairs * LANES), lambda bi, gi, qi: (bi, qi, gi)),
        out_shape=jax.ShapeDtypeStruct((b, s, (SB_HEADS // 2) * LANES), BF16),
        scratch_shapes=[pltpu.VMEM((2 * pairs, t, LANES), F32)],
        compiler_params=pltpu.CompilerParams(dimension_semantics=("parallel", "parallel", "parallel"),
                                             vmem_limit_bytes=VMEM_LIMIT),
        name="sb_attn",
    )(g_pair, sq, sk, sv)


def _outproj_kernel(a_ref, b_ref, w_ref, x_ref, mod_ref, g_ref, wr_ref, br_ref,
                    x1_ref, h2_ref, comb_ref):
    half = a_ref.shape[-1]
    y = (jnp.dot(a_ref[0], w_ref[0:half, :], preferred_element_type=F32)
         + jnp.dot(b_ref[0], w_ref[half:, :], preferred_element_type=F32))
    mod = mod_ref[0]
    gate_a, shift, scale = mod[2:3, :], mod[3:4, :], mod[4:5, :]
    x1 = x_ref[0] + gate_a * y
    x1_ref[0] = x1
    ms = jnp.mean(x1 * x1, axis=-1, keepdims=True)
    h2 = (x1 * lax.rsqrt(ms + EPS) * g_ref[...]) * (1.0 + scale) + shift
    h2_hi = h2.astype(BF16)
    h2_ref[0] = h2_hi

    h2_lo = (h2 - h2_hi.astype(F32)).astype(BF16)
    wr_hi, wr_lo = wr_ref[0], wr_ref[1]
    logits = (jnp.dot(h2_hi, wr_hi, preferred_element_type=F32)
              + jnp.dot(h2_lo, wr_hi, preferred_element_type=F32)
              + jnp.dot(h2_hi, wr_lo, preferred_element_type=F32)) + br_ref[...]
    lane = lax.broadcasted_iota(jnp.int32, logits.shape, 1).astype(F32)
    big = jnp.float32(LANES)

    def top(vals):
        mx = jnp.max(vals, axis=-1, keepdims=True)
        idx = jnp.min(jnp.where(vals == mx, lane, big), axis=-1, keepdims=True)
        return mx, idx

    is_group = (lane >= N_EXPERTS) & (lane < N_EXPERTS + N_GROUPS)
    g_logits = jnp.where(is_group, logits, NEG)
    g_max, g_idx = top(g_logits)
    gate_group = 1.0 / jnp.sum(jnp.where(is_group, jnp.exp(logits - g_max), 0.0), axis=-1, keepdims=True)
    g_sel = g_idx - N_EXPERTS
    in_group = (lane >= g_sel * EXPERTS_PER_GROUP) & (lane < (g_sel + 1.0) * EXPERTS_PER_GROUP)
    e_logits = jnp.where(in_group, logits, NEG)
    v1, i1 = top(e_logits)
    v2, i2 = top(jnp.where(lane == i1, NEG, e_logits))
    e2 = jnp.exp(v2 - v1)
    w_first = 1.0 / (1.0 + e2)
    w_second = e2 / (1.0 + e2)
    comb_ref[0] = (gate_group * (jnp.where(lane == i1, w_first, 0.0) + jnp.where(lane == i2, w_second, 0.0))
                   + jnp.where(lane == g_idx, 1.0, 0.0))


def _outproj_call(a_out, b_out, w_out, x, mod, g_ffn, w_router, b_router, *, tm):
    b, s, d = x.shape
    half = a_out.shape[-1]
    idx3 = lambda bi, ti: (bi, ti, 0)
    const2 = lambda bi, ti: (0, 0)
    return pl.pallas_call(
        _outproj_kernel,
        grid=(b, s // tm),
        in_specs=[pl.BlockSpec((1, tm, half), idx3),
                  pl.BlockSpec((1, tm, half), idx3),
                  pl.BlockSpec((2 * half, d), const2),
                  pl.BlockSpec((1, tm, d), idx3),
                  pl.BlockSpec((1, N_MOD, d), lambda bi, ti: (bi, 0, 0)),
                  pl.BlockSpec((1, d), const2),
                  pl.BlockSpec((2, d, LANES), lambda bi, ti: (0, 0, 0)),
                  pl.BlockSpec((1, LANES), const2)],
        out_specs=[pl.BlockSpec((1, tm, d), idx3),
                   pl.BlockSpec((1, tm, d), idx3),
                   pl.BlockSpec((1, tm, LANES), idx3)],
        out_shape=[jax.ShapeDtypeStruct((b, s, d), F32),
                   jax.ShapeDtypeStruct((b, s, d), BF16),
                   jax.ShapeDtypeStruct((b, s, LANES), F32)],
        compiler_params=pltpu.CompilerParams(dimension_semantics=("parallel", "parallel"),
                                             vmem_limit_bytes=VMEM_LIMIT),
        name="out_proj_router",
    )(a_out, b_out, w_out, x, mod, g_ffn, w_router, b_router)


def _moe_kernel(h_ref, comb_ref, earlier_ref, w1_ref, w3_ref, w2_ref, x1_ref, mod_ref, o_ref, *, unit, max_units):
    g = pl.program_id(2)
    tm = h_ref.shape[1]

    @pl.when(g == 0)
    def _():
        o_ref[0] = x1_ref[0]

    comb = comb_ref[0]
    lane = lax.broadcasted_iota(jnp.int32, (1, LANES), 1)
    member = jnp.sum(jnp.where(lane == N_EXPERTS + g, comb, 0.0), axis=-1, keepdims=True)
    sub = earlier_ref.shape[0]
    by_block = sum(jnp.where(lane == n, member[n * sub:(n + 1) * sub], 0.0) for n in range(tm // sub))
    by_block_t = by_block.T
    before_col = jnp.dot(earlier_ref[...], by_block.astype(BF16), preferred_element_type=F32)
    before_row = _nt_dot(by_block_t.astype(BF16), earlier_ref[...])
    totals = jnp.sum(by_block, axis=0, keepdims=True)
    offset = jnp.zeros((1, 1), F32)
    slot_cols, slot_rows = [], []
    for n in range(tm // sub):
        col_n = jnp.sum(jnp.where(lane == n, before_col, 0.0), axis=-1, keepdims=True) + offset
        slot_cols.append(jnp.where(member[n * sub:(n + 1) * sub] > 0.0, col_n, -1.0))
        slot_rows.append(jnp.where(by_block_t[n:n + 1, :] > 0.0, before_row[n:n + 1, :] + offset, -1.0))
        offset = offset + jnp.sum(jnp.where(lane == n, totals, 0.0), axis=-1, keepdims=True)
    slot_col = jnp.concatenate(slot_cols, axis=0)
    slot_row = jnp.concatenate(slot_rows, axis=1)
    count = jnp.sum(member)
    units = jnp.int32(0)
    for c in range(tm // unit):
        units += (count > float(c * unit)).astype(jnp.int32)

    comb_hi = comb.astype(BF16)
    comb_lo = (comb - comb_hi.astype(F32)).astype(BF16)
    gate_f = mod_ref[0][5:6, :]

    def chunk(first_unit, ch):
        base = (first_unit * unit).astype(F32)
        want_col = lax.broadcasted_iota(jnp.int32, (ch, 1), 0).astype(F32) + base
        want_row = lax.broadcasted_iota(jnp.int32, (1, ch), 1).astype(F32) + base
        pick = (slot_row == want_col).astype(BF16)
        place = (slot_col == want_row).astype(BF16)
        xc = jnp.dot(pick, h_ref[0], preferred_element_type=F32).astype(BF16)
        cw = (jnp.dot(pick, comb_hi, preferred_element_type=F32)
              + jnp.dot(pick, comb_lo, preferred_element_type=F32))
        y = jnp.zeros((ch, o_ref.shape[-1]), F32)
        for e in range(EXPERTS_PER_GROUP):
            weight = jnp.sum(jnp.where(lane == g * EXPERTS_PER_GROUP + e, cw, 0.0), axis=-1, keepdims=True)
            h1 = jnp.dot(xc, w1_ref[0, e], preferred_element_type=F32)
            h3 = jnp.dot(xc, w3_ref[0, e], preferred_element_type=F32)
            hg = (h1 * jax.nn.sigmoid(h1)) * h3 * weight
            y += jnp.dot(hg.astype(BF16), w2_ref[0, e], preferred_element_type=F32)
        o_ref[0] += gate_f * jnp.dot(place, y.astype(BF16), preferred_element_type=F32)

    def step(done):
        todo = jnp.minimum(units - done, max_units)
        for n in range(1, max_units + 1):
            pl.when(todo == n)(functools.partial(chunk, done, n * unit))
        return done + todo

    lax.while_loop(lambda done: done < units, step, jnp.int32(0))


def _moe_call(h2, comb, w1, w3, w2, x1, mod, *, tm, unit, max_units):
    b, s, d = x1.shape
    f = w1.shape[-1]
    sub = min(MXU_COLS, tm)
    tok3 = lambda bi, ti, gi: (bi, ti, 0)
    return pl.pallas_call(
        functools.partial(_moe_kernel, unit=unit, max_units=max_units),
        grid=(b, s // tm, N_GROUPS),
        in_specs=[pl.BlockSpec((1, tm, d), tok3),
                  pl.BlockSpec((1, tm, LANES), tok3),
                  pl.BlockSpec((sub, sub), lambda bi, ti, gi: (0, 0)),
                  pl.BlockSpec((1, EXPERTS_PER_GROUP, d, f), lambda bi, ti, gi: (gi, 0, 0, 0)),
                  pl.BlockSpec((1, EXPERTS_PER_GROUP, d, f), lambda bi, ti, gi: (gi, 0, 0, 0)),
                  pl.BlockSpec((1, EXPERTS_PER_GROUP, f, d), lambda bi, ti, gi: (gi, 0, 0, 0)),
                  pl.BlockSpec((1, tm, d), tok3),
                  pl.BlockSpec((1, N_MOD, d), lambda bi, ti, gi: (bi, 0, 0))],
        out_specs=pl.BlockSpec((1, tm, d), tok3),
        out_shape=jax.ShapeDtypeStruct((b, s, d), F32),
        compiler_params=pltpu.CompilerParams(dimension_semantics=("parallel", "parallel", "arbitrary"),
                                             vmem_limit_bytes=VMEM_LIMIT),
        name="moe_experts",
    )(h2, comb, jnp.tri(sub, k=-1, dtype=BF16), w1, w3, w2, x1, mod)


def _prepare_w_in(w_in):
    sq_start = 2 * DIFF_HEADS * 2 * HEAD_DIM + DIFF_HEADS * DIFF_VDIM
    col = np.ones((1, w_in.shape[1]), np.float32)
    col[0, sq_start:sq_start + SB_HEADS * HEAD_DIM] = LOG2E / math.sqrt(HEAD_DIM)
    return (w_in * col).astype(BF16)


def _pad_lanes(v, width=LANES):
    v = v.reshape(1, -1)
    return jnp.pad(v, ((0, 0), (0, width - v.shape[1])))


def kernel(x, c, w_ada, b_ada, g_attn, w_in, q_norm_g, k_norm_g, lambda_q1, lambda_k1, lambda_q2, lambda_k2,
           diff_out_g, sb_out_g, w_out, g_ffn, w_group, b_group, w_erouter, b_expert, w1, w3, w2):
    b, s, d = x.shape
    depth = w_ada.shape[0]
    tile = min(512, s)
    moe_tile = min(1024, s)
    moe_unit = min(128, moe_tile)
    moe_max_units = 4
    sb_tile = min(256, s)
    sb_keys = sb_tile
    slopes = tuple(2.0 ** (-8.0 * (n + 1) / DIFF_HEADS) for n in range(DIFF_HEADS))
    qk_scale = 1.0 / math.sqrt(HEAD_DIM)
    c_pad = jnp.pad(c, ((0, 16 - b), (0, 0)))
    qaug = np.zeros((1, LANES), np.float32)
    qaug[0, HEAD_DIM:HEAD_DIM + 2 * LOG2E_PARTS] = np.repeat(_bf16_parts(LOG2E, LOG2E_PARTS), 2)
    qaug_row = jnp.asarray(qaug)

    for layer in range(depth):
        lambda_init = 0.8 - 0.6 * math.exp(-0.3 * layer)
        mod = _mod_call(c_pad, w_ada[layer], b_ada[layer].reshape(1, -1))[:b].reshape(b, N_MOD, d)

        dq, dk, dv, sq, sk, sv = _proj_call(
            x, mod, g_attn[layer].reshape(1, d), _prepare_w_in(w_in[layer]),
            jnp.tile(q_norm_g[layer] * (qk_scale * LOG2E), 2).reshape(1, LANES),
            jnp.tile(k_norm_g[layer], 2).reshape(1, LANES), qaug_row,
            tm=tile, tk=tile, slopes=slopes)

        lam_p = jnp.stack([lambda_q1[layer], lambda_k1[layer], lambda_q2[layer], lambda_k2[layer]])
        score_bound = (1.02 * HEAD_DIM * qk_scale) * jnp.max(jnp.abs(q_norm_g[layer])) * jnp.max(jnp.abs(k_norm_g[layer]))
        dead_dist = (2.0 * score_bound - F32_DEAD_LOG) / jnp.asarray(slopes, F32)
        n_back = jnp.clip(jnp.floor(dead_dist / tile) + 2.0, 1.0, s // tile).astype(jnp.int32)
        diff_args = (n_back, lam_p, diff_out_g[layer].reshape(1, LANES),
                     jnp.full((1, LANES), score_bound * LOG2E, F32), dq, dk, dv)
        diff_kw = dict(t=tile, slopes=slopes, lambda_init=lambda_init)
        a_out = lax.cond(score_bound <= DIFF_FIXED_REF_MAX,
                         lambda args: _diff_call(*args, fixed_ref=True, **diff_kw),
                         lambda args: _diff_call(*args, fixed_ref=False, **diff_kw),
                         diff_args)
        b_out = _sb_call(jnp.tile(sb_out_g[layer], 2).reshape(1, LANES), sq, sk, sv, t=sb_tile, tk=sb_keys,
                         pairs=SB_PAIRS_PER_STEP)

        w_router = jnp.concatenate(
            [jnp.transpose(w_erouter[layer], (1, 0, 2)).reshape(d, N_EXPERTS), w_group[layer]], axis=1)
        w_router = jnp.pad(w_router, ((0, 0), (0, LANES - w_router.shape[1])))
        w_router_hi = w_router.astype(BF16)
        w_router = jnp.stack([w_router_hi, (w_router - w_router_hi.astype(F32)).astype(BF16)])
        b_router = _pad_lanes(jnp.concatenate([b_expert[layer].reshape(-1), b_group[layer]]))
        x1, h2, comb = _outproj_call(a_out, b_out, w_out[layer].astype(BF16), x, mod,
                                     g_ffn[layer].reshape(1, d), w_router, b_router, tm=tile)

        x = _moe_call(h2, comb, w1[layer].astype(BF16), w3[layer].astype(BF16), w2[layer].astype(BF16),
                      x1, mod, tm=moe_tile, unit=moe_unit, max_units=moe_max_units)
    return x
```

```python
import functools
import math

import jax
import jax.numpy as jnp
import ml_dtypes
import numpy as np
from jax import lax
from jax.experimental import pallas as pl
from jax.experimental.pallas import tpu as pltpu

HEAD_DIM = 64
DIFF_HEADS = 4
SB_HEADS = 8
DIFF_VDIM = 2 * HEAD_DIM
N_GROUPS = 4
EXPERTS_PER_GROUP = 8
N_EXPERTS = N_GROUPS * EXPERTS_PER_GROUP
D_EXPERT = 256
N_MOD = 6
EPS = 1e-6
LANES = 128
MXU_COLS = 256
NEG = -1e30
DIFF_BLOCKS_PER_ITER = 4
SB_PAIRS_PER_STEP = 4
F32_DEAD_LOG = -104.0
VMEM_LIMIT = 56 * 1024 * 1024

F32 = jnp.float32
BF16 = jnp.bfloat16

LOG2E = math.log2(math.e)
LOG2E_PARTS = 3
DIFF_FIXED_REF_MAX = 32.0


def _bf16_parts(value, n):
    parts, rest = [], value
    for _ in range(n):
        part = float(np.float32(rest).astype(ml_dtypes.bfloat16))
        parts.append(part)
        rest -= part
    return parts


def _nt_dot(a, b):
    return lax.dot_general(a, b, (((1,), (1,)), ((), ())), preferred_element_type=F32)


def _mod_kernel(c_ref, w_ref, b_ref, o_ref):
    c = c_ref[...]
    sc = c * jax.nn.sigmoid(c)
    w = w_ref[...]
    sc_hi, w_hi = sc.astype(BF16), w.astype(BF16)
    sc_lo, w_lo = (sc - sc_hi.astype(F32)).astype(BF16), (w - w_hi.astype(F32)).astype(BF16)
    o_ref[...] = (jnp.dot(sc_hi, w_hi, preferred_element_type=F32)
                  + jnp.dot(sc_lo, w_hi, preferred_element_type=F32)
                  + jnp.dot(sc_hi, w_lo, preferred_element_type=F32)) + b_ref[...]


def _mod_call(c_pad, w_ada, b_ada):
    rows, d = c_pad.shape
    n = w_ada.shape[1]
    tn = 1536
    return pl.pallas_call(
        _mod_kernel,
        grid=(n // tn,),
        in_specs=[pl.BlockSpec((rows, d), lambda j: (0, 0)),
                  pl.BlockSpec((d, tn), lambda j: (0, j)),
                  pl.BlockSpec((1, tn), lambda j: (0, j))],
        out_specs=pl.BlockSpec((rows, tn), lambda j: (0, j)),
        out_shape=jax.ShapeDtypeStruct((rows, n), F32),
        compiler_params=pltpu.CompilerParams(dimension_semantics=("parallel",),
                                             vmem_limit_bytes=VMEM_LIMIT),
        name="adaln_mod",
    )(c_pad, w_ada, b_ada)


_PROJ_BLOCKS = 24


def _proj_kernel(x_ref, mod_ref, g_ref, w_ref, qg_ref, kg_ref, qaug_ref,
                 dq_ref, dk_ref, dv_ref, sq_ref, sk_ref, sv_ref, *, tm, tk, slopes):
    x = x_ref[0]
    mod = mod_ref[0]
    shift, scale = mod[0:1, :], mod[1:2, :]
    ms = jnp.mean(x * x, axis=-1, keepdims=True)
    h = (x * lax.rsqrt(ms + EPS) * g_ref[...]) * (1.0 + scale) + shift
    hb = h.astype(BF16)

    lane = lax.broadcasted_iota(jnp.int32, (1, LANES), 1)
    aug = (lane >= HEAD_DIM) & (lane < HEAD_DIM + 2 * LOG2E_PARTS)
    lo_lane = (aug & ((lane & 1) == 0)).astype(F32)
    hi_lane = (aug & ((lane & 1) == 1)).astype(F32)
    row = lax.broadcasted_iota(jnp.int32, (tm, 1), 0) + pl.program_id(1) * tm
    koff = row & (tk - 1)
    koff_lo = (koff & 255).astype(F32)
    koff_hi = (koff - (koff & 255)).astype(F32)
    koff_lanes = lo_lane * koff_lo + hi_lane * koff_hi
    low = lane < HEAD_DIM

    for c in range(_PROJ_BLOCKS * LANES // MXU_COLS):
        pc = jnp.dot(hb, w_ref[:, c * MXU_COLS:(c + 1) * MXU_COLS], preferred_element_type=F32)
        for half in range(MXU_COLS // LANES):
            blk = c * (MXU_COLS // LANES) + half
            piece = pc[:, half * LANES:(half + 1) * LANES]
            if blk < 8:
                sq = piece * piece
                ss_low = jnp.sum(jnp.where(low, sq, 0.0), axis=-1, keepdims=True)
                ss_high = jnp.sum(sq, axis=-1, keepdims=True) - ss_low
                inv = jnp.where(low, lax.rsqrt(ss_low * (1.0 / HEAD_DIM) + EPS),
                                lax.rsqrt(ss_high * (1.0 / HEAD_DIM) + EPS))
                head = blk % DIFF_HEADS
                if blk < DIFF_HEADS:
                    scaled, extra, out = piece * inv * qg_ref[...], qaug_ref[...], dq_ref
                else:
                    scaled, extra, out = piece * inv * kg_ref[...], koff_lanes * slopes[head], dk_ref
                out[0, 2 * head] = (jnp.where(low, scaled, 0.0) + extra).astype(BF16)
                out[0, 2 * head + 1] = (jnp.where(low, pltpu.roll(scaled, HEAD_DIM, 1), 0.0) + extra).astype(BF16)
            elif blk < 12:
                dv_ref[0, blk - 8] = piece.astype(BF16)
            elif blk < 16:
                sq_ref[0, 2 * (blk - 12)] = jnp.where(low, piece, 0.0).astype(BF16)
                sq_ref[0, 2 * (blk - 12) + 1] = jnp.where(low, 0.0, piece).astype(BF16)
            elif blk < 20:
                sk_ref[0, blk - 16] = piece.astype(BF16)
            else:
                sv_ref[0, blk - 20] = piece.astype(BF16)


def _proj_call(x, mod, g_attn, w_perm, qg_pad, kg_pad, qaug_row, *, tm, tk, slopes):
    b, s, d = x.shape
    n = w_perm.shape[1]
    nt = s // tm

    def hm(nh):
        return (jax.ShapeDtypeStruct((b, nh, s, LANES), BF16),
                pl.BlockSpec((1, nh, tm, LANES), lambda bi, ti: (bi, 0, ti, 0)))

    outs = [hm(8), hm(8), hm(4), hm(8), hm(4), hm(4)]
    return pl.pallas_call(
        functools.partial(_proj_kernel, tm=tm, tk=tk, slopes=slopes),
        grid=(b, nt),
        in_specs=[pl.BlockSpec((1, tm, d), lambda bi, ti: (bi, ti, 0)),
                  pl.BlockSpec((1, N_MOD, d), lambda bi, ti: (bi, 0, 0)),
                  pl.BlockSpec((1, d), lambda bi, ti: (0, 0)),
                  pl.BlockSpec((d, n), lambda bi, ti: (0, 0)),
                  pl.BlockSpec((1, LANES), lambda bi, ti: (0, 0)),
                  pl.BlockSpec((1, LANES), lambda bi, ti: (0, 0)),
                  pl.BlockSpec((1, LANES), lambda bi, ti: (0, 0))],
        out_specs=[o[1] for o in outs],
        out_shape=[o[0] for o in outs],
        compiler_params=pltpu.CompilerParams(dimension_semantics=("parallel", "parallel"),
                                             vmem_limit_bytes=VMEM_LIMIT),
        name="in_proj",
    )(x, mod, g_attn, w_perm, qg_pad, kg_pad, qaug_row)


def _diff_kernel(nback_ref, lam_ref, g_ref, bound_ref, q_ref, k_ref, v_ref, o_ref, acc0_ref, acc1_ref,
                 *, t, slopes, lambda_init, fixed_ref):
    hd = pl.program_id(1)
    i = pl.program_id(2)
    hd_col = jnp.full((t, 1), hd, jnp.int32)
    slope = jnp.full((t, 1), slopes[-1] * LOG2E, F32)
    for n in range(len(slopes) - 2, -1, -1):
        slope = jnp.where(hd_col == n, slopes[n] * LOG2E, slope)

    q = (q_ref[0, 0], q_ref[0, 1])
    accs = (acc0_ref, acc1_ref)
    acc0_ref[...] = jnp.zeros_like(acc0_ref)
    acc1_ref[...] = jnp.zeros_like(acc1_ref)
    row = lax.broadcasted_iota(jnp.int32, (t, t), 0)
    col = lax.broadcasted_iota(jnp.int32, (t, t), 1)
    causal = col <= row
    ones_col = (lax.broadcasted_iota(jnp.int32, (t, LANES), 1) == 0).astype(BF16)
    row_pos = lax.broadcasted_iota(jnp.int32, (t, 1), 0) + i * t

    def block(j, carry, masked):
        ks = pl.multiple_of(j * t, t)
        v_aug = jnp.concatenate([v_ref[0, 0, pl.ds(ks, t), :], ones_col], axis=1)
        new = []
        for mp in range(2):
            s = _nt_dot(q[mp], k_ref[0, mp, pl.ds(ks, t), :])
            if masked:
                s = jnp.where(causal, s, NEG)
            if fixed_ref:
                ref = slope * (row_pos - j * t).astype(F32) + bound_ref[:, 0:1]
                p = jnp.exp2(s - ref)
                accs[mp][...] += jnp.dot(p.astype(BF16), v_aug, preferred_element_type=F32)
            else:
                off = slope * jnp.full((t, 1), j * t, jnp.int32).astype(F32)
                m = carry[mp]
                m_new = jnp.maximum(m, jnp.max(s, axis=-1, keepdims=True) + off)
                p = jnp.exp2(s - (m_new - off))
                accs[mp][...] = (jnp.exp2(m - m_new) * accs[mp][...]
                                 + jnp.dot(p.astype(BF16), v_aug, preferred_element_type=F32))
                new.append(m_new)
        return tuple(new)

    first = jnp.maximum(i + 1 - nback_ref[hd], 0)
    n_full = i - first
    n_bunches = n_full // DIFF_BLOCKS_PER_ITER

    def bunch(bb, carry):
        for n in range(DIFF_BLOCKS_PER_ITER):
            carry = block(first + bb * DIFF_BLOCKS_PER_ITER + n, carry, False)
        return carry

    init = () if fixed_ref else (jnp.full((t, 1), NEG, F32),) * 2
    carry = lax.fori_loop(0, n_bunches, bunch, init)
    carry = lax.fori_loop(first + n_bunches * DIFF_BLOCKS_PER_ITER, i, lambda j, c: block(j, c, False), carry)
    block(i, carry, True)

    lp = lam_ref[...]
    lam = (jnp.exp(jnp.sum(lp[0:1] * lp[1:2], axis=-1, keepdims=True))
           - jnp.exp(jnp.sum(lp[2:3] * lp[3:4], axis=-1, keepdims=True)) + lambda_init)
    o = (acc0_ref[:, 0:LANES] / acc0_ref[:, LANES:LANES + 1]
         - lam * (acc1_ref[:, 0:LANES] / acc1_ref[:, LANES:LANES + 1]))
    ms = jnp.mean(o * o, axis=-1, keepdims=True)
    o_ref[0] = ((o * lax.rsqrt(ms + EPS) * g_ref[...]) * (1.0 - lambda_init)).astype(BF16)


def _diff_call(n_back, lam_p, g_out, bound_row, dq, dk, dv, *, t, slopes, lambda_init, fixed_ref):
    b, _, s, _ = dq.shape
    const2 = lambda bi, hi, qi: (0, 0)
    return pl.pallas_call(
        functools.partial(_diff_kernel, t=t, slopes=slopes, lambda_init=lambda_init, fixed_ref=fixed_ref),
        grid=(b, DIFF_HEADS, s // t),
        in_specs=[pl.BlockSpec(memory_space=pltpu.SMEM),
                  pl.BlockSpec((4, HEAD_DIM), const2),
                  pl.BlockSpec((1, LANES), const2),
                  pl.BlockSpec((1, LANES), const2),
                  pl.BlockSpec((1, 2, t, LANES), lambda bi, hi, qi: (bi, hi, qi, 0)),
                  pl.BlockSpec((1, 2, s, LANES), lambda bi, hi, qi: (bi, hi, 0, 0)),
                  pl.BlockSpec((1, 1, s, LANES), lambda bi, hi, qi: (bi, hi, 0, 0))],
        out_specs=pl.BlockSpec((1, t, LANES), lambda bi, hi, qi: (bi, qi, hi)),
        out_shape=jax.ShapeDtypeStruct((b, s, DIFF_HEADS * LANES), BF16),
        scratch_shapes=[pltpu.VMEM((t, 2 * LANES), F32), pltpu.VMEM((t, 2 * LANES), F32)],
        compiler_params=pltpu.CompilerParams(dimension_semantics=("parallel", "parallel", "parallel"),
                                             vmem_limit_bytes=VMEM_LIMIT),
        name="diff_attn_fixed_ref" if fixed_ref else "diff_attn_running_max",
    )(n_back, lam_p, g_out, bound_row, dq, dk, dv)


def _sb_kernel(g_ref, q_ref, k_ref, v_ref, o_ref, acc_ref, *, t, tk, pairs):
    i = pl.program_id(2)
    heads = 2 * pairs
    diag_blocks = t // tk
    acc_ref[...] = jnp.zeros_like(acc_ref)
    row = lax.broadcasted_iota(jnp.int32, (t, tk), 0)
    col = lax.broadcasted_iota(jnp.int32, (t, tk), 1)
    later = (lax.broadcasted_iota(jnp.int32, (tk, tk), 0) > lax.broadcasted_iota(jnp.int32, (tk, tk), 1)).astype(BF16)

    def block(j, carry, past):
        masked = past is not None
        ks = pl.multiple_of(j * tk, tk)
        new = []
        for hh in range(heads):
            k = k_ref[0, hh // 2, pl.ds(ks, tk), :]
            v = v_ref[0, hh // 2, pl.ds(ks, tk), :]
            z = _nt_dot(q_ref[0, hh], k)
            neg_abs = -jnp.abs(z)
            log_rem = 0.5 * (neg_abs - z) - jnp.log2(1.0 + jnp.exp2(neg_abs))
            log_beta = log_rem + z
            if masked:
                log_rem = jnp.where(past, log_rem, 0.0)
            new.append(carry[hh] + jnp.sum(log_rem, axis=-1, keepdims=True))
            suffix = jnp.dot(log_rem.astype(BF16), later, preferred_element_type=F32)
            a = jnp.exp2(log_beta + suffix + carry[hh])
            if masked:
                a = jnp.where(past, a, 0.0)
            acc_ref[hh] += jnp.dot(a.astype(BF16), v, preferred_element_type=F32)
        return tuple(new)

    def any_live(carry):
        top = functools.reduce(jnp.maximum, carry)
        return (jnp.max(top) > LOG2E * F32_DEAD_LOG).astype(jnp.int32)

    older = i * diag_blocks - 1

    def live_cond(state):
        jj, live = state[0], state[1]
        return jnp.logical_and(jj < older, live > 0)

    def live_body(state):
        carry = block(older - 1 - state[0], state[2:], None)
        return (state[0] + 1, any_live(carry)) + carry

    carry = (jnp.zeros((t, 1), F32),) * heads
    for d in reversed(range(diag_blocks)):
        carry = block(i * diag_blocks + d, carry, col + d * tk < row)
    exists = jnp.where(jnp.full((t, 1), older, jnp.int32) >= 0, 0.0, NEG)
    carry = block(jnp.maximum(older, 0), tuple(c + exists for c in carry), None)
    lax.while_loop(live_cond, live_body, (jnp.int32(0), any_live(carry)) + carry)

    lane = lax.broadcasted_iota(jnp.int32, (1, LANES), 1)
    first = lane < HEAD_DIM
    for pr in range(pairs):
        o = jnp.where(first, acc_ref[2 * pr], acc_ref[2 * pr + 1])
        sq = o * o
        s_first = jnp.sum(jnp.where(first, sq, 0.0), axis=-1, keepdims=True)
        s_all = jnp.sum(sq, axis=-1, keepdims=True)
        ms = jnp.where(first, s_first, s_all - s_first) * (1.0 / HEAD_DIM)
        o_ref[0, :, pr * LANES:(pr + 1) * LANES] = (o * lax.rsqrt(ms + EPS) * g_ref[...]).astype(BF16)


def _sb_call(g_pair, sq, sk, sv, *, t, tk, pairs):
    b, _, s, _ = sq.shape
    groups = SB_HEADS // 2 // pairs
    return pl.pallas_call(
        functools.partial(_sb_kernel, t=t, tk=tk, pairs=pairs),
        grid=(b, groups, s // t),
        in_specs=[pl.BlockSpec((1, LANES), lambda bi, gi, qi: (0, 0)),
                  pl.BlockSpec((1, 2 * pairs, t, LANES), lambda bi, gi, qi: (bi, gi, qi, 0)),
                  pl.BlockSpec((1, pairs, s, LANES), lambda bi, gi, qi: (bi, gi, 0, 0)),
                  pl.BlockSpec((1, pairs, s, LANES), lambda bi, gi, qi: (bi, gi, 0, 0))],
        out_specs=pl.BlockSpec((1, t, pairs * LANES), lambda bi, gi, qi: (bi, qi, gi)),
        out_shape=jax.ShapeDtypeStruct((b, s, (SB_HEADS // 2) * LANES), BF16),
        scratch_shapes=[pltpu.VMEM((2 * pairs, t, LANES), F32)],
        compiler_params=pltpu.CompilerParams(dimension_semantics=("parallel", "parallel", "parallel"),
                                             vmem_limit_bytes=VMEM_LIMIT),
        name="sb_attn",
    )(g_pair, sq, sk, sv)


def _attn_kernel(nback_ref, lam_ref, gd_ref, bound_ref, dq_ref, dk_ref, dv_ref, gs_ref, sq_ref, sk_ref, sv_ref,
                 a_ref, b_ref, acc0_ref, acc1_ref, sacc_ref, *, t, ts, slopes, lambda_init, fixed_ref):
    hd = pl.program_id(1)
    i = pl.program_id(2)
    sub = t // ts
    chains = [(u, hh) for u in range(sub) for hh in range(2)]

    hd_col = jnp.full((t, 1), hd, jnp.int32)
    slope = jnp.full((t, 1), slopes[-1] * LOG2E, F32)
    for n in range(len(slopes) - 2, -1, -1):
        slope = jnp.where(hd_col == n, slopes[n] * LOG2E, slope)
    dq = (dq_ref[0, 0], dq_ref[0, 1])
    accs = (acc0_ref, acc1_ref)
    acc0_ref[...] = jnp.zeros_like(acc0_ref)
    acc1_ref[...] = jnp.zeros_like(acc1_ref)
    causal = lax.broadcasted_iota(jnp.int32, (t, t), 1) <= lax.broadcasted_iota(jnp.int32, (t, t), 0)
    ones_col = (lax.broadcasted_iota(jnp.int32, (t, LANES), 1) == 0).astype(BF16)
    row_pos = lax.broadcasted_iota(jnp.int32, (t, 1), 0) + i * t

    def d_block(j, carry, masked, kill=None):
        ks = pl.multiple_of(j * t, t)
        v_aug = jnp.concatenate([dv_ref[0, 0, pl.ds(ks, t), :], ones_col], axis=1)
        new = []
        for mp in range(2):
            s = _nt_dot(dq[mp], dk_ref[0, mp, pl.ds(ks, t), :])
            if masked:
                s = jnp.where(causal, s, NEG)
            if kill is not None:
                s = s - kill
            if fixed_ref:
                ref = slope * (row_pos - j * t).astype(F32) + bound_ref[:, 0:1]
                p = jnp.exp2(s - ref)
                accs[mp][...] += jnp.dot(p.astype(BF16), v_aug, preferred_element_type=F32)
            else:
                off = slope * jnp.full((t, 1), j * t, jnp.int32).astype(F32)
                m = carry[mp]
                m_new = jnp.maximum(m, jnp.max(s, axis=-1, keepdims=True) + off)
                p = jnp.exp2(s - (m_new - off))
                accs[mp][...] = (jnp.exp2(m - m_new) * accs[mp][...]
                                 + jnp.dot(p.astype(BF16), v_aug, preferred_element_type=F32))
                new.append(m_new)
        return tuple(new)

    sacc_ref[...] = jnp.zeros_like(sacc_ref)
    srow = lax.broadcasted_iota(jnp.int32, (ts, ts), 0)
    scol = lax.broadcasted_iota(jnp.int32, (ts, ts), 1)
    past = scol < srow
    later = (srow > scol).astype(BF16)

    def s_layer(layer, carry, masked):
        new = []
        for c, (u, hh) in enumerate(chains):
            jb = sub * i + u - layer
            c_in = carry[c] + jnp.where(jnp.full((ts, 1), jb, jnp.int32) >= 0, 0.0, NEG)
            ks = pl.multiple_of(jnp.maximum(jb, 0) * ts, ts)
            k = sk_ref[0, 0, pl.ds(ks, ts), :]
            v = sv_ref[0, 0, pl.ds(ks, ts), :]
            z = _nt_dot(sq_ref[0, hh, u * ts:(u + 1) * ts, :], k)
            neg_abs = -jnp.abs(z)
            log_rem = 0.5 * (neg_abs - z) - jnp.log2(1.0 + jnp.exp2(neg_abs))
            log_beta = log_rem + z
            if masked:
                log_rem = jnp.where(past, log_rem, 0.0)
            new.append(c_in + jnp.sum(log_rem, axis=-1, keepdims=True))
            suffix = jnp.dot(log_rem.astype(BF16), later, preferred_element_type=F32)
            a = jnp.exp2(log_beta + suffix + c_in)
            if masked:
                a = jnp.where(past, a, 0.0)
            sacc_ref[c] += jnp.dot(a.astype(BF16), v, preferred_element_type=F32)
        return tuple(new)

    d_carry = d_block(i, () if fixed_ref else (jnp.full((t, 1), NEG, F32),) * 2, True)
    s_carry = s_layer(0, (jnp.zeros((ts, 1), F32),) * len(chains), True)
    no_prev = jnp.where(jnp.full((t, 1), i, jnp.int32) >= 1, 0.0, -NEG)
    d_carry = d_block(jnp.maximum(i - 1, 0), d_carry, False, kill=no_prev)
    s_carry = s_layer(1, s_carry, False)

    first = jnp.maximum(i + 1 - nback_ref[hd], 0)
    n_rest = jnp.maximum(i - 1 - first, 0)
    n_bunches = n_rest // DIFF_BLOCKS_PER_ITER

    def bunch(bb, carry):
        for n in range(DIFF_BLOCKS_PER_ITER):
            carry = d_block(first + bb * DIFF_BLOCKS_PER_ITER + n, carry, False)
        return carry

    d_carry = lax.fori_loop(0, n_bunches, bunch, d_carry)
    lax.fori_loop(first + n_bunches * DIFF_BLOCKS_PER_ITER, first + n_rest,
                  lambda j, cr: d_block(j, cr, False), d_carry)

    def any_live(carry):
        top = functools.reduce(jnp.maximum, carry)
        return (jnp.max(top) > LOG2E * F32_DEAD_LOG).astype(jnp.int32)

    def live_cond(state):
        return jnp.logical_and(state[0] <= sub * i + sub - 1, state[1] > 0)

    def live_body(state):
        carry = s_layer(state[0], state[2:], False)
        return (state[0] + 1, any_live(carry)) + carry

    lax.while_loop(live_cond, live_body, (jnp.int32(2), any_live(s_carry)) + s_carry)

    lp = lam_ref[...]
    lam = (jnp.exp(jnp.sum(lp[0:1] * lp[1:2], axis=-1, keepdims=True))
           - jnp.exp(jnp.sum(lp[2:3] * lp[3:4], axis=-1, keepdims=True)) + lambda_init)
    o = (acc0_ref[:, 0:LANES] / acc0_ref[:, LANES:LANES + 1]
         - lam * (acc1_ref[:, 0:LANES] / acc1_ref[:, LANES:LANES + 1]))
    ms = jnp.mean(o * o, axis=-1, keepdims=True)
    a_ref[0] = ((o * lax.rsqrt(ms + EPS) * gd_ref[...]) * (1.0 - lambda_init)).astype(BF16)

    first_half = lax.broadcasted_iota(jnp.int32, (1, LANES), 1) < HEAD_DIM
    for u in range(sub):
        o = jnp.where(first_half, sacc_ref[2 * u], sacc_ref[2 * u + 1])
        sq = o * o
        s_first = jnp.sum(jnp.where(first_half, sq, 0.0), axis=-1, keepdims=True)
        s_all = jnp.sum(sq, axis=-1, keepdims=True)
        ms = jnp.where(first_half, s_first, s_all - s_first) * (1.0 / HEAD_DIM)
        b_ref[0, u * ts:(u + 1) * ts, :] = (o * lax.rsqrt(ms + EPS) * gs_ref[...]).astype(BF16)


def _attn_call(n_back, lam_p, g_diff, bound_row, dq, dk, dv, g_pair, sq, sk, sv,
               *, t, ts, slopes, lambda_init, fixed_ref):
    b, _, s, _ = dq.shape
    const2 = lambda bi, hi, qi: (0, 0)
    tile4 = lambda bi, hi, qi: (bi, hi, qi, 0)
    full4 = lambda bi, hi, qi: (bi, hi, 0, 0)
    out_spec = pl.BlockSpec((1, t, LANES), lambda bi, hi, qi: (bi, qi, hi))
    out_shape = jax.ShapeDtypeStruct((b, s, DIFF_HEADS * LANES), BF16)
    return pl.pallas_call(
        functools.partial(_attn_kernel, t=t, ts=ts, slopes=slopes, lambda_init=lambda_init, fixed_ref=fixed_ref),
        grid=(b, DIFF_HEADS, s // t),
        in_specs=[pl.BlockSpec(memory_space=pltpu.SMEM),
                  pl.BlockSpec((4, HEAD_DIM), const2),
                  pl.BlockSpec((1, LANES), const2),
                  pl.BlockSpec((1, LANES), const2),
                  pl.BlockSpec((1, 2, t, LANES), tile4),
                  pl.BlockSpec((1, 2, s, LANES), full4),
                  pl.BlockSpec((1, 1, s, LANES), full4),
                  pl.BlockSpec((1, LANES), const2),
                  pl.BlockSpec((1, 2, t, LANES), tile4),
                  pl.BlockSpec((1, 1, s, LANES), full4),
                  pl.BlockSpec((1, 1, s, LANES), full4)],
        out_specs=[out_spec, out_spec],
        out_shape=[out_shape, out_shape],
        scratch_shapes=[pltpu.VMEM((t, 2 * LANES), F32), pltpu.VMEM((t, 2 * LANES), F32),
                        pltpu.VMEM((2 * (t // ts), ts, LANES), F32)],
        compiler_params=pltpu.CompilerParams(dimension_semantics=("parallel", "parallel", "parallel"),
                                             vmem_limit_bytes=VMEM_LIMIT),
        name="attn_fixed_ref" if fixed_ref else "attn_running_max",
    )(n_back, lam_p, g_diff, bound_row, dq, dk, dv, g_pair, sq, sk, sv)


def _outproj_kernel(a_ref, b_ref, w_ref, x_ref, mod_ref, g_ref, wr_ref, br_ref,
                    x1_ref, h2_ref, comb_ref):
    half = a_ref.shape[-1]
    y = (jnp.dot(a_ref[0], w_ref[0:half, :], preferred_element_type=F32)
         + jnp.dot(b_ref[0], w_ref[half:, :], preferred_element_type=F32))
    mod = mod_ref[0]
    gate_a, shift, scale = mod[2:3, :], mod[3:4, :], mod[4:5, :]
    x1 = x_ref[0] + gate_a * y
    x1_ref[0] = x1
    ms = jnp.mean(x1 * x1, axis=-1, keepdims=True)
    h2 = (x1 * lax.rsqrt(ms + EPS) * g_ref[...]) * (1.0 + scale) + shift
    h2_hi = h2.astype(BF16)
    h2_ref[0] = h2_hi

    h2_lo = (h2 - h2_hi.astype(F32)).astype(BF16)
    wr_hi, wr_lo = wr_ref[0], wr_ref[1]
    logits = (jnp.dot(h2_hi, wr_hi, preferred_element_type=F32)
              + jnp.dot(h2_lo, wr_hi, preferred_element_type=F32)
              + jnp.dot(h2_hi, wr_lo, preferred_element_type=F32)) + br_ref[...]
    lane = lax.broadcasted_iota(jnp.int32, logits.shape, 1).astype(F32)
    big = jnp.float32(LANES)

    def top(vals):
        mx = jnp.max(vals, axis=-1, keepdims=True)
        idx = jnp.min(jnp.where(vals == mx, lane, big), axis=-1, keepdims=True)
        return mx, idx

    is_group = (lane >= N_EXPERTS) & (lane < N_EXPERTS + N_GROUPS)
    g_logits = jnp.where(is_group, logits, NEG)
    g_max, g_idx = top(g_logits)
    gate_group = 1.0 / jnp.sum(jnp.where(is_group, jnp.exp(logits - g_max), 0.0), axis=-1, keepdims=True)
    g_sel = g_idx - N_EXPERTS
    in_group = (lane >= g_sel * EXPERTS_PER_GROUP) & (lane < (g_sel + 1.0) * EXPERTS_PER_GROUP)
    e_logits = jnp.where(in_group, logits, NEG)
    v1, i1 = top(e_logits)
    v2, i2 = top(jnp.where(lane == i1, NEG, e_logits))
    e2 = jnp.exp(v2 - v1)
    w_first = 1.0 / (1.0 + e2)
    w_second = e2 / (1.0 + e2)
    comb_ref[0] = (gate_group * (jnp.where(lane == i1, w_first, 0.0) + jnp.where(lane == i2, w_second, 0.0))
                   + jnp.where(lane == g_idx, 1.0, 0.0))


def _outproj_call(a_out, b_out, w_out, x, mod, g_ffn, w_router, b_router, *, tm):
    b, s, d = x.shape
    half = a_out.shape[-1]
    idx3 = lambda bi, ti: (bi, ti, 0)
    const2 = lambda bi, ti: (0, 0)
    return pl.pallas_call(
        _outproj_kernel,
        grid=(b, s // tm),
        in_specs=[pl.BlockSpec((1, tm, half), idx3),
                  pl.BlockSpec((1, tm, half), idx3),
                  pl.BlockSpec((2 * half, d), const2),
                  pl.BlockSpec((1, tm, d), idx3),
                  pl.BlockSpec((1, N_MOD, d), lambda bi, ti: (bi, 0, 0)),
                  pl.BlockSpec((1, d), const2),
                  pl.BlockSpec((2, d, LANES), lambda bi, ti: (0, 0, 0)),
                  pl.BlockSpec((1, LANES), const2)],
        out_specs=[pl.BlockSpec((1, tm, d), idx3),
                   pl.BlockSpec((1, tm, d), idx3),
                   pl.BlockSpec((1, tm, LANES), idx3)],
        out_shape=[jax.ShapeDtypeStruct((b, s, d), F32),
                   jax.ShapeDtypeStruct((b, s, d), BF16),
                   jax.ShapeDtypeStruct((b, s, LANES), F32)],
        compiler_params=pltpu.CompilerParams(dimension_semantics=("parallel", "parallel"),
                                             vmem_limit_bytes=VMEM_LIMIT),
        name="out_proj_router",
    )(a_out, b_out, w_out, x, mod, g_ffn, w_router, b_router)


def _moe_kernel(h_ref, comb_ref, earlier_ref, w1_ref, w3_ref, w2_ref, x1_ref, mod_ref, o_ref, *, unit, max_units):
    g = pl.program_id(2)
    tm = h_ref.shape[1]

    @pl.when(g == 0)
    def _():
        o_ref[0] = x1_ref[0]

    comb = comb_ref[0]
    lane = lax.broadcasted_iota(jnp.int32, (1, LANES), 1)
    member = jnp.sum(jnp.where(lane == N_EXPERTS + g, comb, 0.0), axis=-1, keepdims=True)
    sub = earlier_ref.shape[0]
    by_block = sum(jnp.where(lane == n, member[n * sub:(n + 1) * sub], 0.0) for n in range(tm // sub))
    by_block_t = by_block.T
    before_col = jnp.dot(earlier_ref[...], by_block.astype(BF16), preferred_element_type=F32)
    before_row = _nt_dot(by_block_t.astype(BF16), earlier_ref[...])
    totals = jnp.sum(by_block, axis=0, keepdims=True)
    offset = jnp.zeros((1, 1), F32)
    slot_cols, slot_rows = [], []
    for n in range(tm // sub):
        col_n = jnp.sum(jnp.where(lane == n, before_col, 0.0), axis=-1, keepdims=True) + offset
        slot_cols.append(jnp.where(member[n * sub:(n + 1) * sub] > 0.0, col_n, -1.0))
        slot_rows.append(jnp.where(by_block_t[n:n + 1, :] > 0.0, before_row[n:n + 1, :] + offset, -1.0))
        offset = offset + jnp.sum(jnp.where(lane == n, totals, 0.0), axis=-1, keepdims=True)
    slot_col = jnp.concatenate(slot_cols, axis=0)
    slot_row = jnp.concatenate(slot_rows, axis=1)
    count = jnp.sum(member)
    units = jnp.int32(0)
    for c in range(tm // unit):
        units += (count > float(c * unit)).astype(jnp.int32)

    comb_hi = comb.astype(BF16)
    comb_lo = (comb - comb_hi.astype(F32)).astype(BF16)
    gate_f = mod_ref[0][5:6, :]

    def chunk(first_unit, ch):
        base = (first_unit * unit).astype(F32)
        want_col = lax.broadcasted_iota(jnp.int32, (ch, 1), 0).astype(F32) + base
        want_row = lax.broadcasted_iota(jnp.int32, (1, ch), 1).astype(F32) + base
        pick = (slot_row == want_col).astype(BF16)
        place = (slot_col == want_row).astype(BF16)
        xc = jnp.dot(pick, h_ref[0], preferred_element_type=F32).astype(BF16)
        cw = (jnp.dot(pick, comb_hi, preferred_element_type=F32)
              + jnp.dot(pick, comb_lo, preferred_element_type=F32))
        y = jnp.zeros((ch, o_ref.shape[-1]), F32)
        for e in range(EXPERTS_PER_GROUP):
            weight = jnp.sum(jnp.where(lane == g * EXPERTS_PER_GROUP + e, cw, 0.0), axis=-1, keepdims=True)
            h1 = jnp.dot(xc, w1_ref[0, e], preferred_element_type=F32)
            h3 = jnp.dot(xc, w3_ref[0, e], preferred_element_type=F32)
            hg = (h1 * jax.nn.sigmoid(h1)) * h3 * weight
            y += jnp.dot(hg.astype(BF16), w2_ref[0, e], preferred_element_type=F32)
        o_ref[0] += gate_f * jnp.dot(place, y.astype(BF16), preferred_element_type=F32)

    def step(done):
        todo = jnp.minimum(units - done, max_units)
        for n in range(1, max_units + 1):
            pl.when(todo == n)(functools.partial(chunk, done, n * unit))
        return done + todo

    lax.while_loop(lambda done: done < units, step, jnp.int32(0))


def _moe_call(h2, comb, w1, w3, w2, x1, mod, *, tm, unit, max_units):
    b, s, d = x1.shape
    f = w1.shape[-1]
    sub = min(MXU_COLS, tm)
    tok3 = lambda bi, ti, gi: (bi, ti, 0)
    return pl.pallas_call(
        functools.partial(_moe_kernel, unit=unit, max_units=max_units),
        grid=(b, s // tm, N_GROUPS),
        in_specs=[pl.BlockSpec((1, tm, d), tok3),
                  pl.BlockSpec((1, tm, LANES), tok3),
                  pl.BlockSpec((sub, sub), lambda bi, ti, gi: (0, 0)),
                  pl.BlockSpec((1, EXPERTS_PER_GROUP, d, f), lambda bi, ti, gi: (gi, 0, 0, 0)),
                  pl.BlockSpec((1, EXPERTS_PER_GROUP, d, f), lambda bi, ti, gi: (gi, 0, 0, 0)),
                  pl.BlockSpec((1, EXPERTS_PER_GROUP, f, d), lambda bi, ti, gi: (gi, 0, 0, 0)),
                  pl.BlockSpec((1, tm, d), tok3),
                  pl.BlockSpec((1, N_MOD, d), lambda bi, ti, gi: (bi, 0, 0))],
        out_specs=pl.BlockSpec((1, tm, d), tok3),
        out_shape=jax.ShapeDtypeStruct((b, s, d), F32),
        compiler_params=pltpu.CompilerParams(dimension_semantics=("parallel", "parallel", "arbitrary"),
                                             vmem_limit_bytes=VMEM_LIMIT),
        name="moe_experts",
    )(h2, comb, jnp.tri(sub, k=-1, dtype=BF16), w1, w3, w2, x1, mod)


def _prepare_w_in(w_in):
    sq_start = 2 * DIFF_HEADS * 2 * HEAD_DIM + DIFF_HEADS * DIFF_VDIM
    col = np.ones((1, w_in.shape[1]), np.float32)
    col[0, sq_start:sq_start + SB_HEADS * HEAD_DIM] = LOG2E / math.sqrt(HEAD_DIM)
    return (w_in * col).astype(BF16)


def _pad_lanes(v, width=LANES):
    v = v.reshape(1, -1)
    return jnp.pad(v, ((0, 0), (0, width - v.shape[1])))


def kernel(x, c, w_ada, b_ada, g_attn, w_in, q_norm_g, k_norm_g, lambda_q1, lambda_k1, lambda_q2, lambda_k2,
           diff_out_g, sb_out_g, w_out, g_ffn, w_group, b_group, w_erouter, b_expert, w1, w3, w2):
    b, s, d = x.shape
    depth = w_ada.shape[0]
    tile = min(512, s)
    moe_tile = min(1024, s)
    moe_unit = min(128, moe_tile)
    moe_max_units = 4
    sb_tile = min(256, s)
    sb_keys = sb_tile
    slopes = tuple(2.0 ** (-8.0 * (n + 1) / DIFF_HEADS) for n in range(DIFF_HEADS))
    qk_scale = 1.0 / math.sqrt(HEAD_DIM)
    c_pad = jnp.pad(c, ((0, 16 - b), (0, 0)))
    qaug = np.zeros((1, LANES), np.float32)
    qaug[0, HEAD_DIM:HEAD_DIM + 2 * LOG2E_PARTS] = np.repeat(_bf16_parts(LOG2E, LOG2E_PARTS), 2)
    qaug_row = jnp.asarray(qaug)

    for layer in range(depth):
        lambda_init = 0.8 - 0.6 * math.exp(-0.3 * layer)
        mod = _mod_call(c_pad, w_ada[layer], b_ada[layer].reshape(1, -1))[:b].reshape(b, N_MOD, d)

        dq, dk, dv, sq, sk, sv = _proj_call(
            x, mod, g_attn[layer].reshape(1, d), _prepare_w_in(w_in[layer]),
            jnp.tile(q_norm_g[layer] * (qk_scale * LOG2E), 2).reshape(1, LANES),
            jnp.tile(k_norm_g[layer], 2).reshape(1, LANES), qaug_row,
            tm=tile, tk=tile, slopes=slopes)

        lam_p = jnp.stack([lambda_q1[layer], lambda_k1[layer], lambda_q2[layer], lambda_k2[layer]])
        score_bound = (1.02 * HEAD_DIM * qk_scale) * jnp.max(jnp.abs(q_norm_g[layer])) * jnp.max(jnp.abs(k_norm_g[layer]))
        dead_dist = (2.0 * score_bound - F32_DEAD_LOG) / jnp.asarray(slopes, F32)
        n_back = jnp.clip(jnp.floor(dead_dist / tile) + 2.0, 1.0, s // tile).astype(jnp.int32)
        attn_args = (n_back, lam_p, diff_out_g[layer].reshape(1, LANES),
                     jnp.full((1, LANES), score_bound * LOG2E, F32), dq, dk, dv,
                     jnp.tile(sb_out_g[layer], 2).reshape(1, LANES), sq, sk, sv)
        attn_kw = dict(t=tile, ts=sb_tile, slopes=slopes, lambda_init=lambda_init)
        a_out, b_out = lax.cond(score_bound <= DIFF_FIXED_REF_MAX,
                                lambda args: _attn_call(*args, fixed_ref=True, **attn_kw),
                                lambda args: _attn_call(*args, fixed_ref=False, **attn_kw),
                                attn_args)

        w_router = jnp.concatenate(
            [jnp.transpose(w_erouter[layer], (1, 0, 2)).reshape(d, N_EXPERTS), w_group[layer]], axis=1)
        w_router = jnp.pad(w_router, ((0, 0), (0, LANES - w_router.shape[1])))
        w_router_hi = w_router.astype(BF16)
        w_router = jnp.stack([w_router_hi, (w_router - w_router_hi.astype(F32)).astype(BF16)])
        b_router = _pad_lanes(jnp.concatenate([b_expert[layer].reshape(-1), b_group[layer]]))
        x1, h2, comb = _outproj_call(a_out, b_out, w_out[layer].astype(BF16), x, mod,
                                     g_ffn[layer].reshape(1, d), w_router, b_router, tm=tile)

        x = _moe_call(h2, comb, w1[layer].astype(BF16), w3[layer].astype(BF16), w2[layer].astype(BF16),
                      x1, mod, tm=moe_tile, unit=moe_unit, max_units=moe_max_units)
    return x
```

```python
import functools
import math

import jax
import jax.numpy as jnp
import ml_dtypes
import numpy as np
from jax import lax
from jax.experimental import pallas as pl
from jax.experimental.pallas import tpu as pltpu

HEAD_DIM = 64
DIFF_HEADS = 4
SB_HEADS = 8
DIFF_VDIM = 2 * HEAD_DIM
N_GROUPS = 4
EXPERTS_PER_GROUP = 8
N_EXPERTS = N_GROUPS * EXPERTS_PER_GROUP
D_EXPERT = 256
N_MOD = 6
EPS = 1e-6
LANES = 128
MXU_COLS = 256
NEG = -1e30
DIFF_BLOCKS_PER_ITER = 4
SB_PAIRS_PER_STEP = 4
F32_DEAD_LOG = -104.0
VMEM_LIMIT = 56 * 1024 * 1024

F32 = jnp.float32
BF16 = jnp.bfloat16

LOG2E = math.log2(math.e)
LOG2E_PARTS = 3
DIFF_FIXED_REF_MAX = 32.0


def _bf16_parts(value, n):
    parts, rest = [], value
    for _ in range(n):
        part = float(np.float32(rest).astype(ml_dtypes.bfloat16))
        parts.append(part)
        rest -= part
    return parts


def _nt_dot(a, b):
    return lax.dot_general(a, b, (((1,), (1,)), ((), ())), preferred_element_type=F32)


def _mod_kernel(c_ref, w_ref, b_ref, o_ref):
    c = c_ref[...]
    sc = c * jax.nn.sigmoid(c)
    w = w_ref[...]
    sc_hi, w_hi = sc.astype(BF16), w.astype(BF16)
    sc_lo, w_lo = (sc - sc_hi.astype(F32)).astype(BF16), (w - w_hi.astype(F32)).astype(BF16)
    o_ref[...] = (jnp.dot(sc_hi, w_hi, preferred_element_type=F32)
                  + jnp.dot(sc_lo, w_hi, preferred_element_type=F32)
                  + jnp.dot(sc_hi, w_lo, preferred_element_type=F32)) + b_ref[...]


def _mod_call(c_pad, w_ada, b_ada):
    rows, d = c_pad.shape
    n = w_ada.shape[1]
    tn = 1536
    return pl.pallas_call(
        _mod_kernel,
        grid=(n // tn,),
        in_specs=[pl.BlockSpec((rows, d), lambda j: (0, 0)),
                  pl.BlockSpec((d, tn), lambda j: (0, j)),
                  pl.BlockSpec((1, tn), lambda j: (0, j))],
        out_specs=pl.BlockSpec((rows, tn), lambda j: (0, j)),
        out_shape=jax.ShapeDtypeStruct((rows, n), F32),
        compiler_params=pltpu.CompilerParams(dimension_semantics=("parallel",),
                                             vmem_limit_bytes=VMEM_LIMIT),
        name="adaln_mod",
    )(c_pad, w_ada, b_ada)


_PROJ_BLOCKS = 24


def _proj_kernel(x_ref, mod_ref, g_ref, w_ref, qg_ref, kg_ref, qaug_ref,
                 dq_ref, dk_ref, dv_ref, sq_ref, sk_ref, sv_ref, *, tm, tk, slopes):
    x = x_ref[0]
    mod = mod_ref[0]
    shift, scale = mod[0:1, :], mod[1:2, :]
    ms = jnp.mean(x * x, axis=-1, keepdims=True)
    h = (x * lax.rsqrt(ms + EPS) * g_ref[...]) * (1.0 + scale) + shift
    hb = h.astype(BF16)

    lane = lax.broadcasted_iota(jnp.int32, (1, LANES), 1)
    aug = (lane >= HEAD_DIM) & (lane < HEAD_DIM + 2 * LOG2E_PARTS)
    lo_lane = (aug & ((lane & 1) == 0)).astype(F32)
    hi_lane = (aug & ((lane & 1) == 1)).astype(F32)
    row = lax.broadcasted_iota(jnp.int32, (tm, 1), 0) + pl.program_id(1) * tm
    koff = row & (tk - 1)
    koff_lo = (koff & 255).astype(F32)
    koff_hi = (koff - (koff & 255)).astype(F32)
    koff_lanes = lo_lane * koff_lo + hi_lane * koff_hi
    low = lane < HEAD_DIM

    for c in range(_PROJ_BLOCKS * LANES // MXU_COLS):
        pc = jnp.dot(hb, w_ref[:, c * MXU_COLS:(c + 1) * MXU_COLS], preferred_element_type=F32)
        for half in range(MXU_COLS // LANES):
            blk = c * (MXU_COLS // LANES) + half
            piece = pc[:, half * LANES:(half + 1) * LANES]
            if blk < 8:
                sq = piece * piece
                ss_low = jnp.sum(jnp.where(low, sq, 0.0), axis=-1, keepdims=True)
                ss_high = jnp.sum(sq, axis=-1, keepdims=True) - ss_low
                inv = jnp.where(low, lax.rsqrt(ss_low * (1.0 / HEAD_DIM) + EPS),
                                lax.rsqrt(ss_high * (1.0 / HEAD_DIM) + EPS))
                head = blk % DIFF_HEADS
                if blk < DIFF_HEADS:
                    scaled, extra, out = piece * inv * qg_ref[...], qaug_ref[...], dq_ref
                else:
                    scaled, extra, out = piece * inv * kg_ref[...], koff_lanes * slopes[head], dk_ref
                out[0, 2 * head] = (jnp.where(low, scaled, 0.0) + extra).astype(BF16)
                out[0, 2 * head + 1] = (jnp.where(low, pltpu.roll(scaled, HEAD_DIM, 1), 0.0) + extra).astype(BF16)
            elif blk < 12:
                dv_ref[0, blk - 8] = piece.astype(BF16)
            elif blk < 16:
                sq_ref[0, 2 * (blk - 12)] = jnp.where(low, piece, 0.0).astype(BF16)
                sq_ref[0, 2 * (blk - 12) + 1] = jnp.where(low, 0.0, piece).astype(BF16)
            elif blk < 20:
                sk_ref[0, blk - 16] = piece.astype(BF16)
            else:
                sv_ref[0, blk - 20] = piece.astype(BF16)


def _proj_call(x, mod, g_attn, w_perm, qg_pad, kg_pad, qaug_row, *, tm, tk, slopes):
    b, s, d = x.shape
    n = w_perm.shape[1]
    nt = s // tm

    def hm(nh):
        return (jax.ShapeDtypeStruct((b, nh, s, LANES), BF16),
                pl.BlockSpec((1, nh, tm, LANES), lambda bi, ti: (bi, 0, ti, 0)))

    outs = [hm(8), hm(8), hm(4), hm(8), hm(4), hm(4)]
    return pl.pallas_call(
        functools.partial(_proj_kernel, tm=tm, tk=tk, slopes=slopes),
        grid=(b, nt),
        in_specs=[pl.BlockSpec((1, tm, d), lambda bi, ti: (bi, ti, 0)),
                  pl.BlockSpec((1, N_MOD, d), lambda bi, ti: (bi, 0, 0)),
                  pl.BlockSpec((1, d), lambda bi, ti: (0, 0)),
                  pl.BlockSpec((d, n), lambda bi, ti: (0, 0)),
                  pl.BlockSpec((1, LANES), lambda bi, ti: (0, 0)),
                  pl.BlockSpec((1, LANES), lambda bi, ti: (0, 0)),
                  pl.BlockSpec((1, LANES), lambda bi, ti: (0, 0))],
        out_specs=[o[1] for o in outs],
        out_shape=[o[0] for o in outs],
        compiler_params=pltpu.CompilerParams(dimension_semantics=("parallel", "parallel"),
                                             vmem_limit_bytes=VMEM_LIMIT),
        name="in_proj",
    )(x, mod, g_attn, w_perm, qg_pad, kg_pad, qaug_row)


def _diff_kernel(nback_ref, lam_ref, g_ref, bound_ref, q_ref, k_ref, v_ref, o_ref, acc0_ref, acc1_ref,
                 *, t, slopes, lambda_init, fixed_ref):
    hd = pl.program_id(1)
    i = pl.program_id(2)
    hd_col = jnp.full((t, 1), hd, jnp.int32)
    slope = jnp.full((t, 1), slopes[-1] * LOG2E, F32)
    for n in range(len(slopes) - 2, -1, -1):
        slope = jnp.where(hd_col == n, slopes[n] * LOG2E, slope)

    q = (q_ref[0, 0], q_ref[0, 1])
    accs = (acc0_ref, acc1_ref)
    acc0_ref[...] = jnp.zeros_like(acc0_ref)
    acc1_ref[...] = jnp.zeros_like(acc1_ref)
    row = lax.broadcasted_iota(jnp.int32, (t, t), 0)
    col = lax.broadcasted_iota(jnp.int32, (t, t), 1)
    causal = col <= row
    ones_col = (lax.broadcasted_iota(jnp.int32, (t, LANES), 1) == 0).astype(BF16)
    row_pos = lax.broadcasted_iota(jnp.int32, (t, 1), 0) + i * t

    def block(j, carry, masked):
        ks = pl.multiple_of(j * t, t)
        v_aug = jnp.concatenate([v_ref[0, 0, pl.ds(ks, t), :], ones_col], axis=1)
        new = []
        for mp in range(2):
            s = _nt_dot(q[mp], k_ref[0, mp, pl.ds(ks, t), :])
            if masked:
                s = jnp.where(causal, s, NEG)
            if fixed_ref:
                ref = slope * (row_pos - j * t).astype(F32) + bound_ref[:, 0:1]
                p = jnp.exp2(s - ref)
                accs[mp][...] += jnp.dot(p.astype(BF16), v_aug, preferred_element_type=F32)
            else:
                off = slope * jnp.full((t, 1), j * t, jnp.int32).astype(F32)
                m = carry[mp]
                m_new = jnp.maximum(m, jnp.max(s, axis=-1, keepdims=True) + off)
                p = jnp.exp2(s - (m_new - off))
                accs[mp][...] = (jnp.exp2(m - m_new) * accs[mp][...]
                                 + jnp.dot(p.astype(BF16), v_aug, preferred_element_type=F32))
                new.append(m_new)
        return tuple(new)

    first = jnp.maximum(i + 1 - nback_ref[hd], 0)
    n_full = i - first
    n_bunches = n_full // DIFF_BLOCKS_PER_ITER

    def bunch(bb, carry):
        for n in range(DIFF_BLOCKS_PER_ITER):
            carry = block(first + bb * DIFF_BLOCKS_PER_ITER + n, carry, False)
        return carry

    init = () if fixed_ref else (jnp.full((t, 1), NEG, F32),) * 2
    carry = lax.fori_loop(0, n_bunches, bunch, init)
    carry = lax.fori_loop(first + n_bunches * DIFF_BLOCKS_PER_ITER, i, lambda j, c: block(j, c, False), carry)
    block(i, carry, True)

    lp = lam_ref[...]
    lam = (jnp.exp(jnp.sum(lp[0:1] * lp[1:2], axis=-1, keepdims=True))
           - jnp.exp(jnp.sum(lp[2:3] * lp[3:4], axis=-1, keepdims=True)) + lambda_init)
    o = (acc0_ref[:, 0:LANES] / acc0_ref[:, LANES:LANES + 1]
         - lam * (acc1_ref[:, 0:LANES] / acc1_ref[:, LANES:LANES + 1]))
    ms = jnp.mean(o * o, axis=-1, keepdims=True)
    o_ref[0] = ((o * lax.rsqrt(ms + EPS) * g_ref[...]) * (1.0 - lambda_init)).astype(BF16)


def _diff_call(n_back, lam_p, g_out, bound_row, dq, dk, dv, *, t, slopes, lambda_init, fixed_ref):
    b, _, s, _ = dq.shape
    const2 = lambda bi, hi, qi: (0, 0)
    return pl.pallas_call(
        functools.partial(_diff_kernel, t=t, slopes=slopes, lambda_init=lambda_init, fixed_ref=fixed_ref),
        grid=(b, DIFF_HEADS, s // t),
        in_specs=[pl.BlockSpec(memory_space=pltpu.SMEM),
                  pl.BlockSpec((4, HEAD_DIM), const2),
                  pl.BlockSpec((1, LANES), const2),
                  pl.BlockSpec((1, LANES), const2),
                  pl.BlockSpec((1, 2, t, LANES), lambda bi, hi, qi: (bi, hi, qi, 0)),
                  pl.BlockSpec((1, 2, s, LANES), lambda bi, hi, qi: (bi, hi, 0, 0)),
                  pl.BlockSpec((1, 1, s, LANES), lambda bi, hi, qi: (bi, hi, 0, 0))],
        out_specs=pl.BlockSpec((1, t, LANES), lambda bi, hi, qi: (bi, qi, hi)),
        out_shape=jax.ShapeDtypeStruct((b, s, DIFF_HEADS * LANES), BF16),
        scratch_shapes=[pltpu.VMEM((t, 2 * LANES), F32), pltpu.VMEM((t, 2 * LANES), F32)],
        compiler_params=pltpu.CompilerParams(dimension_semantics=("parallel", "parallel", "parallel"),
                                             vmem_limit_bytes=VMEM_LIMIT),
        name="diff_attn_fixed_ref" if fixed_ref else "diff_attn_running_max",
    )(n_back, lam_p, g_out, bound_row, dq, dk, dv)


def _sb_kernel(g_ref, q_ref, k_ref, v_ref, o_ref, acc_ref, *, t, tk, pairs):
    i = pl.program_id(2)
    heads = 2 * pairs
    diag_blocks = t // tk
    acc_ref[...] = jnp.zeros_like(acc_ref)
    row = lax.broadcasted_iota(jnp.int32, (t, tk), 0)
    col = lax.broadcasted_iota(jnp.int32, (t, tk), 1)
    later = (lax.broadcasted_iota(jnp.int32, (tk, tk), 0) > lax.broadcasted_iota(jnp.int32, (tk, tk), 1)).astype(BF16)

    def block(j, carry, past):
        masked = past is not None
        ks = pl.multiple_of(j * tk, tk)
        new = []
        for hh in range(heads):
            k = k_ref[0, hh // 2, pl.ds(ks, tk), :]
            v = v_ref[0, hh // 2, pl.ds(ks, tk), :]
            z = _nt_dot(q_ref[0, hh], k)
            neg_abs = -jnp.abs(z)
            log_rem = 0.5 * (neg_abs - z) - jnp.log2(1.0 + jnp.exp2(neg_abs))
            log_beta = log_rem + z
            if masked:
                log_rem = jnp.where(past, log_rem, 0.0)
            new.append(carry[hh] + jnp.sum(log_rem, axis=-1, keepdims=True))
            suffix = jnp.dot(log_rem.astype(BF16), later, preferred_element_type=F32)
            a = jnp.exp2(log_beta + suffix + carry[hh])
            if masked:
                a = jnp.where(past, a, 0.0)
            acc_ref[hh] += jnp.dot(a.astype(BF16), v, preferred_element_type=F32)
        return tuple(new)

    def any_live(carry):
        top = functools.reduce(jnp.maximum, carry)
        return (jnp.max(top) > LOG2E * F32_DEAD_LOG).astype(jnp.int32)

    older = i * diag_blocks - 1

    def live_cond(state):
        jj, live = state[0], state[1]
        return jnp.logical_and(jj < older, live > 0)

    def live_body(state):
        carry = block(older - 1 - state[0], state[2:], None)
        return (state[0] + 1, any_live(carry)) + carry

    carry = (jnp.zeros((t, 1), F32),) * heads
    for d in reversed(range(diag_blocks)):
        carry = block(i * diag_blocks + d, carry, col + d * tk < row)
    exists = jnp.where(jnp.full((t, 1), older, jnp.int32) >= 0, 0.0, NEG)
    carry = block(jnp.maximum(older, 0), tuple(c + exists for c in carry), None)
    lax.while_loop(live_cond, live_body, (jnp.int32(0), any_live(carry)) + carry)

    lane = lax.broadcasted_iota(jnp.int32, (1, LANES), 1)
    first = lane < HEAD_DIM
    for pr in range(pairs):
        o = jnp.where(first, acc_ref[2 * pr], acc_ref[2 * pr + 1])
        sq = o * o
        s_first = jnp.sum(jnp.where(first, sq, 0.0), axis=-1, keepdims=True)
        s_all = jnp.sum(sq, axis=-1, keepdims=True)
        ms = jnp.where(first, s_first, s_all - s_first) * (1.0 / HEAD_DIM)
        o_ref[0, :, pr * LANES:(pr + 1) * LANES] = (o * lax.rsqrt(ms + EPS) * g_ref[...]).astype(BF16)


def _sb_call(g_pair, sq, sk, sv, *, t, tk, pairs):
    b, _, s, _ = sq.shape
    groups = SB_HEADS // 2 // pairs
    return pl.pallas_call(
        functools.partial(_sb_kernel, t=t, tk=tk, pairs=pairs),
        grid=(b, groups, s // t),
        in_specs=[pl.BlockSpec((1, LANES), lambda bi, gi, qi: (0, 0)),
                  pl.BlockSpec((1, 2 * pairs, t, LANES), lambda bi, gi, qi: (bi, gi, qi, 0)),
                  pl.BlockSpec((1, pairs, s, LANES), lambda bi, gi, qi: (bi, gi, 0, 0)),
                  pl.BlockSpec((1, pairs, s, LANES), lambda bi, gi, qi: (bi, gi, 0, 0))],
        out_specs=pl.BlockSpec((1, t, pairs * LANES), lambda bi, gi, qi: (bi, qi, gi)),
        out_shape=jax.ShapeDtypeStruct((b, s, (SB_HEADS // 2) * LANES), BF16),
        scratch_shapes=[pltpu.VMEM((2 * pairs, t, LANES), F32)],
        compiler_params=pltpu.CompilerParams(dimension_semantics=("parallel", "parallel", "parallel"),
                                             vmem_limit_bytes=VMEM_LIMIT),
        name="sb_attn",
    )(g_pair, sq, sk, sv)


def _attn_kernel(nback_ref, lam_ref, gd_ref, bound_ref, dq_ref, dk_ref, dv_ref, gs_ref, sq_ref, sk_ref, sv_ref,
                 a_ref, b_ref, acc0_ref, acc1_ref, sacc_ref, *, t, ts, slopes, lambda_init, fixed_ref):
    hd = pl.program_id(1)
    i = pl.program_id(2)
    sub = t // ts
    chains = [(u, hh) for u in range(sub) for hh in range(2)]

    hd_col = jnp.full((t, 1), hd, jnp.int32)
    slope = jnp.full((t, 1), slopes[-1] * LOG2E, F32)
    for n in range(len(slopes) - 2, -1, -1):
        slope = jnp.where(hd_col == n, slopes[n] * LOG2E, slope)
    dq = (dq_ref[0, 0], dq_ref[0, 1])
    accs = (acc0_ref, acc1_ref)
    causal = lax.broadcasted_iota(jnp.int32, (t, t), 1) <= lax.broadcasted_iota(jnp.int32, (t, t), 0)
    ones_col = (lax.broadcasted_iota(jnp.int32, (t, LANES), 1) == 0).astype(BF16)
    row_pos = lax.broadcasted_iota(jnp.int32, (t, 1), 0) + i * t

    def d_block(j, carry, diagonal, kill=None):
        ks = pl.multiple_of(j * t, t)
        v_aug = jnp.concatenate([dv_ref[0, 0, pl.ds(ks, t), :], ones_col], axis=1)
        new = []
        for mp in range(2):
            s = _nt_dot(dq[mp], dk_ref[0, mp, pl.ds(ks, t), :])
            if diagonal:
                s = jnp.where(causal, s, NEG)
            if kill is not None:
                s = s - kill
            if fixed_ref:
                ref = slope * (row_pos - j * t).astype(F32) + bound_ref[:, 0:1]
                pv = jnp.dot(jnp.exp2(s - ref).astype(BF16), v_aug, preferred_element_type=F32)
                accs[mp][...] = pv if diagonal else accs[mp][...] + pv
            else:
                off = slope * jnp.full((t, 1), j * t, jnp.int32).astype(F32)
                m_new = jnp.max(s, axis=-1, keepdims=True) + off
                if not diagonal:
                    m_new = jnp.maximum(carry[mp], m_new)
                pv = jnp.dot(jnp.exp2(s - (m_new - off)).astype(BF16), v_aug, preferred_element_type=F32)
                accs[mp][...] = pv if diagonal else jnp.exp2(carry[mp] - m_new) * accs[mp][...] + pv
                new.append(m_new)
        return tuple(new)

    srow = lax.broadcasted_iota(jnp.int32, (ts, ts), 0)
    scol = lax.broadcasted_iota(jnp.int32, (ts, ts), 1)
    past = scol < srow
    later = (srow > scol).astype(BF16)

    def s_layer(layer, carry, masked):
        new = []
        for c, (u, hh) in enumerate(chains):
            jb = sub * i + u - layer
            c_in = carry[c] + jnp.where(jnp.full((ts, 1), jb, jnp.int32) >= 0, 0.0, NEG)
            ks = pl.multiple_of(jnp.maximum(jb, 0) * ts, ts)
            k = sk_ref[0, 0, pl.ds(ks, ts), :]
            v = sv_ref[0, 0, pl.ds(ks, ts), :]
            z = _nt_dot(sq_ref[0, hh, u * ts:(u + 1) * ts, :], k)
            neg_abs = -jnp.abs(z)
            log_rem = 0.5 * (neg_abs - z) - jnp.log2(1.0 + jnp.exp2(neg_abs))
            log_beta = log_rem + z
            if masked:
                log_rem = jnp.where(past, log_rem, 0.0)
            new.append(c_in + jnp.sum(log_rem, axis=-1, keepdims=True))
            suffix = jnp.dot(log_rem.astype(BF16), later, preferred_element_type=F32)
            a = jnp.exp2(log_beta + suffix + c_in)
            if masked:
                a = jnp.where(past, a, 0.0)
            av = jnp.dot(a.astype(BF16), v, preferred_element_type=F32)
            sacc_ref[c] = av if masked else sacc_ref[c] + av
        return tuple(new)

    d_carry = d_block(i, (), True)
    s_carry = s_layer(0, (jnp.zeros((ts, 1), F32),) * len(chains), True)
    no_prev = jnp.where(jnp.full((t, 1), i, jnp.int32) >= 1, 0.0, -NEG)
    d_carry = d_block(jnp.maximum(i - 1, 0), d_carry, False, kill=no_prev)
    s_carry = s_layer(1, s_carry, False)

    first = jnp.maximum(i + 1 - nback_ref[hd], 0)
    n_rest = jnp.maximum(i - 1 - first, 0)
    n_bunches = n_rest // DIFF_BLOCKS_PER_ITER

    def bunch(bb, carry):
        for n in range(DIFF_BLOCKS_PER_ITER):
            carry = d_block(first + bb * DIFF_BLOCKS_PER_ITER + n, carry, False)
        return carry

    d_carry = lax.fori_loop(0, n_bunches, bunch, d_carry)
    lax.fori_loop(first + n_bunches * DIFF_BLOCKS_PER_ITER, first + n_rest,
                  lambda j, cr: d_block(j, cr, False), d_carry)

    def any_live(carry):
        top = functools.reduce(jnp.maximum, carry)
        return (jnp.max(top) > LOG2E * F32_DEAD_LOG).astype(jnp.int32)

    def live_cond(state):
        return jnp.logical_and(state[0] <= sub * i + sub - 1, state[1] > 0)

    def live_body(state):
        carry = s_layer(state[0], state[2:], False)
        return (state[0] + 1, any_live(carry)) + carry

    lax.while_loop(live_cond, live_body, (jnp.int32(2), any_live(s_carry)) + s_carry)

    lp = lam_ref[...]
    lam = (jnp.exp(jnp.sum(lp[0:1] * lp[1:2], axis=-1, keepdims=True))
           - jnp.exp(jnp.sum(lp[2:3] * lp[3:4], axis=-1, keepdims=True)) + lambda_init)
    o = (acc0_ref[:, 0:LANES] / acc0_ref[:, LANES:LANES + 1]
         - lam * (acc1_ref[:, 0:LANES] / acc1_ref[:, LANES:LANES + 1]))
    ms = jnp.mean(o * o, axis=-1, keepdims=True)
    a_ref[0] = ((o * lax.rsqrt(ms + EPS) * gd_ref[...]) * (1.0 - lambda_init)).astype(BF16)

    first_half = lax.broadcasted_iota(jnp.int32, (1, LANES), 1) < HEAD_DIM
    for u in range(sub):
        o = jnp.where(first_half, sacc_ref[2 * u], sacc_ref[2 * u + 1])
        sq = o * o
        s_first = jnp.sum(jnp.where(first_half, sq, 0.0), axis=-1, keepdims=True)
        s_all = jnp.sum(sq, axis=-1, keepdims=True)
        ms = jnp.where(first_half, s_first, s_all - s_first) * (1.0 / HEAD_DIM)
        b_ref[0, u * ts:(u + 1) * ts, :] = (o * lax.rsqrt(ms + EPS) * gs_ref[...]).astype(BF16)


def _attn_call(n_back, lam_p, g_diff, bound_row, dq, dk, dv, g_pair, sq, sk, sv,
               *, t, ts, slopes, lambda_init, fixed_ref):
    b, _, s, _ = dq.shape
    const2 = lambda bi, hi, qi: (0, 0)
    tile4 = lambda bi, hi, qi: (bi, hi, qi, 0)
    full4 = lambda bi, hi, qi: (bi, hi, 0, 0)
    out_spec = pl.BlockSpec((1, t, LANES), lambda bi, hi, qi: (bi, qi, hi))
    out_shape = jax.ShapeDtypeStruct((b, s, DIFF_HEADS * LANES), BF16)
    return pl.pallas_call(
        functools.partial(_attn_kernel, t=t, ts=ts, slopes=slopes, lambda_init=lambda_init, fixed_ref=fixed_ref),
        grid=(b, DIFF_HEADS, s // t),
        in_specs=[pl.BlockSpec(memory_space=pltpu.SMEM),
                  pl.BlockSpec((4, HEAD_DIM), const2),
                  pl.BlockSpec((1, LANES), const2),
                  pl.BlockSpec((1, LANES), const2),
                  pl.BlockSpec((1, 2, t, LANES), tile4),
                  pl.BlockSpec((1, 2, s, LANES), full4),
                  pl.BlockSpec((1, 1, s, LANES), full4),
                  pl.BlockSpec((1, LANES), const2),
                  pl.BlockSpec((1, 2, t, LANES), tile4),
                  pl.BlockSpec((1, 1, s, LANES), full4),
                  pl.BlockSpec((1, 1, s, LANES), full4)],
        out_specs=[out_spec, out_spec],
        out_shape=[out_shape, out_shape],
        scratch_shapes=[pltpu.VMEM((t, 2 * LANES), F32), pltpu.VMEM((t, 2 * LANES), F32),
                        pltpu.VMEM((2 * (t // ts), ts, LANES), F32)],
        compiler_params=pltpu.CompilerParams(dimension_semantics=("parallel", "parallel", "parallel"),
                                             vmem_limit_bytes=VMEM_LIMIT),
        name="attn_fixed_ref" if fixed_ref else "attn_running_max",
    )(n_back, lam_p, g_diff, bound_row, dq, dk, dv, g_pair, sq, sk, sv)


def _outproj_kernel(a_ref, b_ref, w_ref, x_ref, mod_ref, g_ref, wr_ref, br_ref,
                    x1_ref, h2_ref, comb_ref):
    half = a_ref.shape[-1]
    y = (jnp.dot(a_ref[0], w_ref[0:half, :], preferred_element_type=F32)
         + jnp.dot(b_ref[0], w_ref[half:, :], preferred_element_type=F32))
    mod = mod_ref[0]
    gate_a, shift, scale = mod[2:3, :], mod[3:4, :], mod[4:5, :]
    x1 = x_ref[0] + gate_a * y
    x1_ref[0] = x1
    ms = jnp.mean(x1 * x1, axis=-1, keepdims=True)
    h2 = (x1 * lax.rsqrt(ms + EPS) * g_ref[...]) * (1.0 + scale) + shift
    h2_hi = h2.astype(BF16)
    h2_ref[0] = h2_hi

    h2_lo = (h2 - h2_hi.astype(F32)).astype(BF16)
    wr_hi, wr_lo = wr_ref[0], wr_ref[1]
    logits = (jnp.dot(h2_hi, wr_hi, preferred_element_type=F32)
              + jnp.dot(h2_lo, wr_hi, preferred_element_type=F32)
              + jnp.dot(h2_hi, wr_lo, preferred_element_type=F32)) + br_ref[...]
    lane = lax.broadcasted_iota(jnp.int32, logits.shape, 1).astype(F32)
    big = jnp.float32(LANES)

    def top(vals):
        mx = jnp.max(vals, axis=-1, keepdims=True)
        idx = jnp.min(jnp.where(vals == mx, lane, big), axis=-1, keepdims=True)
        return mx, idx

    is_group = (lane >= N_EXPERTS) & (lane < N_EXPERTS + N_GROUPS)
    g_logits = jnp.where(is_group, logits, NEG)
    g_max, g_idx = top(g_logits)
    gate_group = 1.0 / jnp.sum(jnp.where(is_group, jnp.exp(logits - g_max), 0.0), axis=-1, keepdims=True)
    g_sel = g_idx - N_EXPERTS
    in_group = (lane >= g_sel * EXPERTS_PER_GROUP) & (lane < (g_sel + 1.0) * EXPERTS_PER_GROUP)
    e_logits = jnp.where(in_group, logits, NEG)
    v1, i1 = top(e_logits)
    v2, i2 = top(jnp.where(lane == i1, NEG, e_logits))
    e2 = jnp.exp(v2 - v1)
    w_first = 1.0 / (1.0 + e2)
    w_second = e2 / (1.0 + e2)
    comb_ref[0] = (gate_group * (jnp.where(lane == i1, w_first, 0.0) + jnp.where(lane == i2, w_second, 0.0))
                   + jnp.where(lane == g_idx, 1.0, 0.0))


def _outproj_call(a_out, b_out, w_out, x, mod, g_ffn, w_router, b_router, *, tm):
    b, s, d = x.shape
    half = a_out.shape[-1]
    idx3 = lambda bi, ti: (bi, ti, 0)
    const2 = lambda bi, ti: (0, 0)
    return pl.pallas_call(
        _outproj_kernel,
        grid=(b, s // tm),
        in_specs=[pl.BlockSpec((1, tm, half), idx3),
                  pl.BlockSpec((1, tm, half), idx3),
                  pl.BlockSpec((2 * half, d), const2),
                  pl.BlockSpec((1, tm, d), idx3),
                  pl.BlockSpec((1, N_MOD, d), lambda bi, ti: (bi, 0, 0)),
                  pl.BlockSpec((1, d), const2),
                  pl.BlockSpec((2, d, LANES), lambda bi, ti: (0, 0, 0)),
                  pl.BlockSpec((1, LANES), const2)],
        out_specs=[pl.BlockSpec((1, tm, d), idx3),
                   pl.BlockSpec((1, tm, d), idx3),
                   pl.BlockSpec((1, tm, LANES), idx3)],
        out_shape=[jax.ShapeDtypeStruct((b, s, d), F32),
                   jax.ShapeDtypeStruct((b, s, d), BF16),
                   jax.ShapeDtypeStruct((b, s, LANES), F32)],
        compiler_params=pltpu.CompilerParams(dimension_semantics=("parallel", "parallel"),
                                             vmem_limit_bytes=VMEM_LIMIT),
        name="out_proj_router",
    )(a_out, b_out, w_out, x, mod, g_ffn, w_router, b_router)


def _moe_kernel(h_ref, comb_ref, earlier_ref, w1_ref, w3_ref, w2_ref, x1_ref, mod_ref, o_ref, *, unit, max_units):
    g = pl.program_id(2)
    tm = h_ref.shape[1]

    @pl.when(g == 0)
    def _():
        o_ref[0] = x1_ref[0]

    comb = comb_ref[0]
    lane = lax.broadcasted_iota(jnp.int32, (1, LANES), 1)
    member = jnp.sum(jnp.where(lane == N_EXPERTS + g, comb, 0.0), axis=-1, keepdims=True)
    sub = earlier_ref.shape[0]
    by_block = sum(jnp.where(lane == n, member[n * sub:(n + 1) * sub], 0.0) for n in range(tm // sub))
    by_block_t = by_block.T
    before_col = jnp.dot(earlier_ref[...], by_block.astype(BF16), preferred_element_type=F32)
    before_row = _nt_dot(by_block_t.astype(BF16), earlier_ref[...])
    totals = jnp.sum(by_block, axis=0, keepdims=True)
    offset = jnp.zeros((1, 1), F32)
    slot_cols, slot_rows = [], []
    for n in range(tm // sub):
        col_n = jnp.sum(jnp.where(lane == n, before_col, 0.0), axis=-1, keepdims=True) + offset
        slot_cols.append(jnp.where(member[n * sub:(n + 1) * sub] > 0.0, col_n, -1.0))
        slot_rows.append(jnp.where(by_block_t[n:n + 1, :] > 0.0, before_row[n:n + 1, :] + offset, -1.0))
        offset = offset + jnp.sum(jnp.where(lane == n, totals, 0.0), axis=-1, keepdims=True)
    slot_col = jnp.concatenate(slot_cols, axis=0)
    slot_row = jnp.concatenate(slot_rows, axis=1)
    count = jnp.sum(member)
    units = jnp.int32(0)
    for c in range(tm // unit):
        units += (count > float(c * unit)).astype(jnp.int32)

    comb_hi = comb.astype(BF16)
    comb_lo = (comb - comb_hi.astype(F32)).astype(BF16)
    gate_f = mod_ref[0][5:6, :]

    def chunk(first_unit, ch):
        base = (first_unit * unit).astype(F32)
        want_col = lax.broadcasted_iota(jnp.int32, (ch, 1), 0).astype(F32) + base
        want_row = lax.broadcasted_iota(jnp.int32, (1, ch), 1).astype(F32) + base
        pick = (slot_row == want_col).astype(BF16)
        place = (slot_col == want_row).astype(BF16)
        xc = jnp.dot(pick, h_ref[0], preferred_element_type=F32).astype(BF16)
        cw = (jnp.dot(pick, comb_hi, preferred_element_type=F32)
              + jnp.dot(pick, comb_lo, preferred_element_type=F32))
        y = jnp.zeros((ch, o_ref.shape[-1]), F32)
        for e in range(EXPERTS_PER_GROUP):
            weight = jnp.sum(jnp.where(lane == g * EXPERTS_PER_GROUP + e, cw, 0.0), axis=-1, keepdims=True)
            h1 = jnp.dot(xc, w1_ref[0, e], preferred_element_type=F32)
            h3 = jnp.dot(xc, w3_ref[0, e], preferred_element_type=F32)
            hg = (h1 * jax.nn.sigmoid(h1)) * h3 * weight
            y += jnp.dot(hg.astype(BF16), w2_ref[0, e], preferred_element_type=F32)
        o_ref[0] += gate_f * jnp.dot(place, y.astype(BF16), preferred_element_type=F32)

    def step(done):
        todo = jnp.minimum(units - done, max_units)
        for n in range(1, max_units + 1):
            pl.when(todo == n)(functools.partial(chunk, done, n * unit))
        return done + todo

    lax.while_loop(lambda done: done < units, step, jnp.int32(0))


def _moe_call(h2, comb, w1, w3, w2, x1, mod, *, tm, unit, max_units):
    b, s, d = x1.shape
    f = w1.shape[-1]
    sub = min(MXU_COLS, tm)
    tok3 = lambda bi, ti, gi: (bi, ti, 0)
    return pl.pallas_call(
        functools.partial(_moe_kernel, unit=unit, max_units=max_units),
        grid=(b, s // tm, N_GROUPS),
        in_specs=[pl.BlockSpec((1, tm, d), tok3),
                  pl.BlockSpec((1, tm, LANES), tok3),
                  pl.BlockSpec((sub, sub), lambda bi, ti, gi: (0, 0)),
                  pl.BlockSpec((1, EXPERTS_PER_GROUP, d, f), lambda bi, ti, gi: (gi, 0, 0, 0)),
                  pl.BlockSpec((1, EXPERTS_PER_GROUP, d, f), lambda bi, ti, gi: (gi, 0, 0, 0)),
                  pl.BlockSpec((1, EXPERTS_PER_GROUP, f, d), lambda bi, ti, gi: (gi, 0, 0, 0)),
                  pl.BlockSpec((1, tm, d), tok3),
                  pl.BlockSpec((1, N_MOD, d), lambda bi, ti, gi: (bi, 0, 0))],
        out_specs=pl.BlockSpec((1, tm, d), tok3),
        out_shape=jax.ShapeDtypeStruct((b, s, d), F32),
        compiler_params=pltpu.CompilerParams(dimension_semantics=("parallel", "parallel", "arbitrary"),
                                             vmem_limit_bytes=VMEM_LIMIT),
        name="moe_experts",
    )(h2, comb, jnp.tri(sub, k=-1, dtype=BF16), w1, w3, w2, x1, mod)


def _prepare_w_in(w_in):
    sq_start = 2 * DIFF_HEADS * 2 * HEAD_DIM + DIFF_HEADS * DIFF_VDIM
    col = np.ones((1, w_in.shape[1]), np.float32)
    col[0, sq_start:sq_start + SB_HEADS * HEAD_DIM] = LOG2E / math.sqrt(HEAD_DIM)
    return (w_in * col).astype(BF16)


def _pad_lanes(v, width=LANES):
    v = v.reshape(1, -1)
    return jnp.pad(v, ((0, 0), (0, width - v.shape[1])))


def kernel(x, c, w_ada, b_ada, g_attn, w_in, q_norm_g, k_norm_g, lambda_q1, lambda_k1, lambda_q2, lambda_k2,
           diff_out_g, sb_out_g, w_out, g_ffn, w_group, b_group, w_erouter, b_expert, w1, w3, w2):
    b, s, d = x.shape
    depth = w_ada.shape[0]
    tile = min(512, s)
    moe_tile = min(1024, s)
    moe_unit = min(64, moe_tile)
    moe_max_units = 8
    sb_tile = min(256, s)
    sb_keys = sb_tile
    slopes = tuple(2.0 ** (-8.0 * (n + 1) / DIFF_HEADS) for n in range(DIFF_HEADS))
    qk_scale = 1.0 / math.sqrt(HEAD_DIM)
    c_pad = jnp.pad(c, ((0, 16 - b), (0, 0)))
    qaug = np.zeros((1, LANES), np.float32)
    qaug[0, HEAD_DIM:HEAD_DIM + 2 * LOG2E_PARTS] = np.repeat(_bf16_parts(LOG2E, LOG2E_PARTS), 2)
    qaug_row = jnp.asarray(qaug)

    for layer in range(depth):
        lambda_init = 0.8 - 0.6 * math.exp(-0.3 * layer)
        mod = _mod_call(c_pad, w_ada[layer], b_ada[layer].reshape(1, -1))[:b].reshape(b, N_MOD, d)

        dq, dk, dv, sq, sk, sv = _proj_call(
            x, mod, g_attn[layer].reshape(1, d), _prepare_w_in(w_in[layer]),
            jnp.tile(q_norm_g[layer] * (qk_scale * LOG2E), 2).reshape(1, LANES),
            jnp.tile(k_norm_g[layer], 2).reshape(1, LANES), qaug_row,
            tm=tile, tk=tile, slopes=slopes)

        lam_p = jnp.stack([lambda_q1[layer], lambda_k1[layer], lambda_q2[layer], lambda_k2[layer]])
        score_bound = (1.02 * HEAD_DIM * qk_scale) * jnp.max(jnp.abs(q_norm_g[layer])) * jnp.max(jnp.abs(k_norm_g[layer]))
        dead_dist = (2.0 * score_bound - F32_DEAD_LOG) / jnp.asarray(slopes, F32)
        n_back = jnp.clip(jnp.floor(dead_dist / tile) + 2.0, 1.0, s // tile).astype(jnp.int32)
        attn_args = (n_back, lam_p, diff_out_g[layer].reshape(1, LANES),
                     jnp.full((1, LANES), score_bound * LOG2E, F32), dq, dk, dv,
                     jnp.tile(sb_out_g[layer], 2).reshape(1, LANES), sq, sk, sv)
        attn_kw = dict(t=tile, ts=sb_tile, slopes=slopes, lambda_init=lambda_init)
        a_out, b_out = lax.cond(score_bound <= DIFF_FIXED_REF_MAX,
                                lambda args: _attn_call(*args, fixed_ref=True, **attn_kw),
                                lambda args: _attn_call(*args, fixed_ref=False, **attn_kw),
                                attn_args)

        w_router = jnp.concatenate(
            [jnp.transpose(w_erouter[layer], (1, 0, 2)).reshape(d, N_EXPERTS), w_group[layer]], axis=1)
        w_router = jnp.pad(w_router, ((0, 0), (0, LANES - w_router.shape[1])))
        w_router_hi = w_router.astype(BF16)
        w_router = jnp.stack([w_router_hi, (w_router - w_router_hi.astype(F32)).astype(BF16)])
        b_router = _pad_lanes(jnp.concatenate([b_expert[layer].reshape(-1), b_group[layer]]))
        x1, h2, comb = _outproj_call(a_out, b_out, w_out[layer].astype(BF16), x, mod,
                                     g_ffn[layer].reshape(1, d), w_router, b_router, tm=tile)

        x = _moe_call(h2, comb, w1[layer].astype(BF16), w3[layer].astype(BF16), w2[layer].astype(BF16),
                      x1, mod, tm=moe_tile, unit=moe_unit, max_units=moe_max_units)
    return x
```

```python
import functools
import math

import jax
import jax.numpy as jnp
import ml_dtypes
import numpy as np
from jax import lax
from jax.experimental import pallas as pl
from jax.experimental.pallas import tpu as pltpu

HEAD_DIM = 64
DIFF_HEADS = 4
SB_HEADS = 8
DIFF_VDIM = 2 * HEAD_DIM
N_GROUPS = 4
EXPERTS_PER_GROUP = 8
N_EXPERTS = N_GROUPS * EXPERTS_PER_GROUP
D_EXPERT = 256
N_MOD = 6
EPS = 1e-6
LANES = 128
MXU_COLS = 256
NEG = -1e30
DIFF_BLOCKS_PER_ITER = 4
SB_PAIRS_PER_STEP = 4
F32_DEAD_LOG = -104.0
VMEM_LIMIT = 56 * 1024 * 1024

F32 = jnp.float32
BF16 = jnp.bfloat16

LOG2E = math.log2(math.e)
LOG2E_PARTS = 3
DIFF_FIXED_REF_MAX = 32.0


def _bf16_parts(value, n):
    parts, rest = [], value
    for _ in range(n):
        part = float(np.float32(rest).astype(ml_dtypes.bfloat16))
        parts.append(part)
        rest -= part
    return parts


def _nt_dot(a, b):
    return lax.dot_general(a, b, (((1,), (1,)), ((), ())), preferred_element_type=F32)


def _mod_kernel(c_ref, w_ref, b_ref, o_ref):
    c = c_ref[...]
    sc = c * jax.nn.sigmoid(c)
    w = w_ref[...]
    sc_hi, w_hi = sc.astype(BF16), w.astype(BF16)
    sc_lo, w_lo = (sc - sc_hi.astype(F32)).astype(BF16), (w - w_hi.astype(F32)).astype(BF16)
    o_ref[...] = (jnp.dot(sc_hi, w_hi, preferred_element_type=F32)
                  + jnp.dot(sc_lo, w_hi, preferred_element_type=F32)
                  + jnp.dot(sc_hi, w_lo, preferred_element_type=F32)) + b_ref[...]


def _mod_call(c_pad, w_ada, b_ada):
    rows, d = c_pad.shape
    n = w_ada.shape[1]
    tn = 1536
    return pl.pallas_call(
        _mod_kernel,
        grid=(n // tn,),
        in_specs=[pl.BlockSpec((rows, d), lambda j: (0, 0)),
                  pl.BlockSpec((d, tn), lambda j: (0, j)),
                  pl.BlockSpec((1, tn), lambda j: (0, j))],
        out_specs=pl.BlockSpec((rows, tn), lambda j: (0, j)),
        out_shape=jax.ShapeDtypeStruct((rows, n), F32),
        compiler_params=pltpu.CompilerParams(dimension_semantics=("parallel",),
                                             vmem_limit_bytes=VMEM_LIMIT),
        name="adaln_mod",
    )(c_pad, w_ada, b_ada)


_PROJ_BLOCKS = 24


def _proj_kernel(x_ref, mod_ref, g_ref, w_ref, qg_ref, kg_ref, qaug_ref,
                 dq_ref, dk_ref, dv_ref, sq_ref, sk_ref, sv_ref, *, tm, tk, slopes):
    x = x_ref[0]
    mod = mod_ref[0]
    shift, scale = mod[0:1, :], mod[1:2, :]
    ms = jnp.mean(x * x, axis=-1, keepdims=True)
    h = (x * lax.rsqrt(ms + EPS) * g_ref[...]) * (1.0 + scale) + shift
    hb = h.astype(BF16)

    lane = lax.broadcasted_iota(jnp.int32, (1, LANES), 1)
    aug = (lane >= HEAD_DIM) & (lane < HEAD_DIM + 2 * LOG2E_PARTS)
    lo_lane = (aug & ((lane & 1) == 0)).astype(F32)
    hi_lane = (aug & ((lane & 1) == 1)).astype(F32)
    row = lax.broadcasted_iota(jnp.int32, (tm, 1), 0) + pl.program_id(1) * tm
    koff = row & (tk - 1)
    koff_lo = (koff & 255).astype(F32)
    koff_hi = (koff - (koff & 255)).astype(F32)
    koff_lanes = lo_lane * koff_lo + hi_lane * koff_hi
    low = lane < HEAD_DIM

    for c in range(_PROJ_BLOCKS * LANES // MXU_COLS):
        pc = jnp.dot(hb, w_ref[:, c * MXU_COLS:(c + 1) * MXU_COLS], preferred_element_type=F32)
        for half in range(MXU_COLS // LANES):
            blk = c * (MXU_COLS // LANES) + half
            piece = pc[:, half * LANES:(half + 1) * LANES]
            if blk < 8:
                sq = piece * piece
                ss_low = jnp.sum(jnp.where(low, sq, 0.0), axis=-1, keepdims=True)
                ss_high = jnp.sum(sq, axis=-1, keepdims=True) - ss_low
                inv = jnp.where(low, lax.rsqrt(ss_low * (1.0 / HEAD_DIM) + EPS),
                                lax.rsqrt(ss_high * (1.0 / HEAD_DIM) + EPS))
                head = blk % DIFF_HEADS
                if blk < DIFF_HEADS:
                    scaled, extra, out = piece * inv * qg_ref[...], qaug_ref[...], dq_ref
                else:
                    scaled, extra, out = piece * inv * kg_ref[...], koff_lanes * slopes[head], dk_ref
                out[0, 2 * head] = (jnp.where(low, scaled, 0.0) + extra).astype(BF16)
                out[0, 2 * head + 1] = (jnp.where(low, pltpu.roll(scaled, HEAD_DIM, 1), 0.0) + extra).astype(BF16)
            elif blk < 12:
                dv_ref[0, blk - 8] = piece.astype(BF16)
            elif blk < 16:
                sq_ref[0, 2 * (blk - 12)] = jnp.where(low, piece, 0.0).astype(BF16)
                sq_ref[0, 2 * (blk - 12) + 1] = jnp.where(low, 0.0, piece).astype(BF16)
            elif blk < 20:
                sk_ref[0, blk - 16] = piece.astype(BF16)
            else:
                sv_ref[0, blk - 20] = piece.astype(BF16)


def _proj_call(x, mod, g_attn, w_perm, qg_pad, kg_pad, qaug_row, *, tm, tk, slopes):
    b, s, d = x.shape
    n = w_perm.shape[1]
    nt = s // tm

    def hm(nh):
        return (jax.ShapeDtypeStruct((b, nh, s, LANES), BF16),
                pl.BlockSpec((1, nh, tm, LANES), lambda bi, ti: (bi, 0, ti, 0)))

    outs = [hm(8), hm(8), hm(4), hm(8), hm(4), hm(4)]
    return pl.pallas_call(
        functools.partial(_proj_kernel, tm=tm, tk=tk, slopes=slopes),
        grid=(b, nt),
        in_specs=[pl.BlockSpec((1, tm, d), lambda bi, ti: (bi, ti, 0)),
                  pl.BlockSpec((1, N_MOD, d), lambda bi, ti: (bi, 0, 0)),
                  pl.BlockSpec((1, d), lambda bi, ti: (0, 0)),
                  pl.BlockSpec((d, n), lambda bi, ti: (0, 0)),
                  pl.BlockSpec((1, LANES), lambda bi, ti: (0, 0)),
                  pl.BlockSpec((1, LANES), lambda bi, ti: (0, 0)),
                  pl.BlockSpec((1, LANES), lambda bi, ti: (0, 0))],
        out_specs=[o[1] for o in outs],
        out_shape=[o[0] for o in outs],
        compiler_params=pltpu.CompilerParams(dimension_semantics=("parallel", "parallel"),
                                             vmem_limit_bytes=VMEM_LIMIT),
        name="in_proj",
    )(x, mod, g_attn, w_perm, qg_pad, kg_pad, qaug_row)


def _diff_kernel(nback_ref, lam_ref, g_ref, bound_ref, q_ref, k_ref, v_ref, o_ref, acc0_ref, acc1_ref,
                 *, t, slopes, lambda_init, fixed_ref):
    hd = pl.program_id(1)
    i = pl.program_id(2)
    hd_col = jnp.full((t, 1), hd, jnp.int32)
    slope = jnp.full((t, 1), slopes[-1] * LOG2E, F32)
    for n in range(len(slopes) - 2, -1, -1):
        slope = jnp.where(hd_col == n, slopes[n] * LOG2E, slope)

    q = (q_ref[0, 0], q_ref[0, 1])
    accs = (acc0_ref, acc1_ref)
    acc0_ref[...] = jnp.zeros_like(acc0_ref)
    acc1_ref[...] = jnp.zeros_like(acc1_ref)
    row = lax.broadcasted_iota(jnp.int32, (t, t), 0)
    col = lax.broadcasted_iota(jnp.int32, (t, t), 1)
    causal = col <= row
    ones_col = (lax.broadcasted_iota(jnp.int32, (t, LANES), 1) == 0).astype(BF16)
    row_pos = lax.broadcasted_iota(jnp.int32, (t, 1), 0) + i * t

    def block(j, carry, masked):
        ks = pl.multiple_of(j * t, t)
        v_aug = jnp.concatenate([v_ref[0, 0, pl.ds(ks, t), :], ones_col], axis=1)
        new = []
        for mp in range(2):
            s = _nt_dot(q[mp], k_ref[0, mp, pl.ds(ks, t), :])
            if masked:
                s = jnp.where(causal, s, NEG)
            if fixed_ref:
                ref = slope * (row_pos - j * t).astype(F32) + bound_ref[:, 0:1]
                p = jnp.exp2(s - ref)
                accs[mp][...] += jnp.dot(p.astype(BF16), v_aug, preferred_element_type=F32)
            else:
                off = slope * jnp.full((t, 1), j * t, jnp.int32).astype(F32)
                m = carry[mp]
                m_new = jnp.maximum(m, jnp.max(s, axis=-1, keepdims=True) + off)
                p = jnp.exp2(s - (m_new - off))
                accs[mp][...] = (jnp.exp2(m - m_new) * accs[mp][...]
                                 + jnp.dot(p.astype(BF16), v_aug, preferred_element_type=F32))
                new.append(m_new)
        return tuple(new)

    first = jnp.maximum(i + 1 - nback_ref[hd], 0)
    n_full = i - first
    n_bunches = n_full // DIFF_BLOCKS_PER_ITER

    def bunch(bb, carry):
        for n in range(DIFF_BLOCKS_PER_ITER):
            carry = block(first + bb * DIFF_BLOCKS_PER_ITER + n, carry, False)
        return carry

    init = () if fixed_ref else (jnp.full((t, 1), NEG, F32),) * 2
    carry = lax.fori_loop(0, n_bunches, bunch, init)
    carry = lax.fori_loop(first + n_bunches * DIFF_BLOCKS_PER_ITER, i, lambda j, c: block(j, c, False), carry)
    block(i, carry, True)

    lp = lam_ref[...]
    lam = (jnp.exp(jnp.sum(lp[0:1] * lp[1:2], axis=-1, keepdims=True))
           - jnp.exp(jnp.sum(lp[2:3] * lp[3:4], axis=-1, keepdims=True)) + lambda_init)
    o = (acc0_ref[:, 0:LANES] / acc0_ref[:, LANES:LANES + 1]
         - lam * (acc1_ref[:, 0:LANES] / acc1_ref[:, LANES:LANES + 1]))
    ms = jnp.mean(o * o, axis=-1, keepdims=True)
    o_ref[0] = ((o * lax.rsqrt(ms + EPS) * g_ref[...]) * (1.0 - lambda_init)).astype(BF16)


def _diff_call(n_back, lam_p, g_out, bound_row, dq, dk, dv, *, t, slopes, lambda_init, fixed_ref):
    b, _, s, _ = dq.shape
    const2 = lambda bi, hi, qi: (0, 0)
    return pl.pallas_call(
        functools.partial(_diff_kernel, t=t, slopes=slopes, lambda_init=lambda_init, fixed_ref=fixed_ref),
        grid=(b, DIFF_HEADS, s // t),
        in_specs=[pl.BlockSpec(memory_space=pltpu.SMEM),
                  pl.BlockSpec((4, HEAD_DIM), const2),
                  pl.BlockSpec((1, LANES), const2),
                  pl.BlockSpec((1, LANES), const2),
                  pl.BlockSpec((1, 2, t, LANES), lambda bi, hi, qi: (bi, hi, qi, 0)),
                  pl.BlockSpec((1, 2, s, LANES), lambda bi, hi, qi: (bi, hi, 0, 0)),
                  pl.BlockSpec((1, 1, s, LANES), lambda bi, hi, qi: (bi, hi, 0, 0))],
        out_specs=pl.BlockSpec((1, t, LANES), lambda bi, hi, qi: (bi, qi, hi)),
        out_shape=jax.ShapeDtypeStruct((b, s, DIFF_HEADS * LANES), BF16),
        scratch_shapes=[pltpu.VMEM((t, 2 * LANES), F32), pltpu.VMEM((t, 2 * LANES), F32)],
        compiler_params=pltpu.CompilerParams(dimension_semantics=("parallel", "parallel", "parallel"),
                                             vmem_limit_bytes=VMEM_LIMIT),
        name="diff_attn_fixed_ref" if fixed_ref else "diff_attn_running_max",
    )(n_back, lam_p, g_out, bound_row, dq, dk, dv)


def _sb_kernel(g_ref, q_ref, k_ref, v_ref, o_ref, acc_ref, *, t, tk, pairs):
    i = pl.program_id(2)
    heads = 2 * pairs
    diag_blocks = t // tk
    acc_ref[...] = jnp.zeros_like(acc_ref)
    row = lax.broadcasted_iota(jnp.int32, (t, tk), 0)
    col = lax.broadcasted_iota(jnp.int32, (t, tk), 1)
    later = (lax.broadcasted_iota(jnp.int32, (tk, tk), 0) > lax.broadcasted_iota(jnp.int32, (tk, tk), 1)).astype(BF16)

    def block(j, carry, past):
        masked = past is not None
        ks = pl.multiple_of(j * tk, tk)
        new = []
        for hh in range(heads):
            k = k_ref[0, hh // 2, pl.ds(ks, tk), :]
            v = v_ref[0, hh // 2, pl.ds(ks, tk), :]
            z = _nt_dot(q_ref[0, hh], k)
            neg_abs = -jnp.abs(z)
            log_rem = 0.5 * (neg_abs - z) - jnp.log2(1.0 + jnp.exp2(neg_abs))
            log_beta = log_rem + z
            if masked:
                log_rem = jnp.where(past, log_rem, 0.0)
            new.append(carry[hh] + jnp.sum(log_rem, axis=-1, keepdims=True))
            suffix = jnp.dot(log_rem.astype(BF16), later, preferred_element_type=F32)
            a = jnp.exp2(log_beta + suffix + carry[hh])
            if masked:
                a = jnp.where(past, a, 0.0)
            acc_ref[hh] += jnp.dot(a.astype(BF16), v, preferred_element_type=F32)
        return tuple(new)

    def any_live(carry):
        top = functools.reduce(jnp.maximum, carry)
        return (jnp.max(top) > LOG2E * F32_DEAD_LOG).astype(jnp.int32)

    older = i * diag_blocks - 1

    def live_cond(state):
        jj, live = state[0], state[1]
        return jnp.logical_and(jj < older, live > 0)

    def live_body(state):
        carry = block(older - 1 - state[0], state[2:], None)
        return (state[0] + 1, any_live(carry)) + carry

    carry = (jnp.zeros((t, 1), F32),) * heads
    for d in reversed(range(diag_blocks)):
        carry = block(i * diag_blocks + d, carry, col + d * tk < row)
    exists = jnp.where(jnp.full((t, 1), older, jnp.int32) >= 0, 0.0, NEG)
    carry = block(jnp.maximum(older, 0), tuple(c + exists for c in carry), None)
    lax.while_loop(live_cond, live_body, (jnp.int32(0), any_live(carry)) + carry)

    lane = lax.broadcasted_iota(jnp.int32, (1, LANES), 1)
    first = lane < HEAD_DIM
    for pr in range(pairs):
        o = jnp.where(first, acc_ref[2 * pr], acc_ref[2 * pr + 1])
        sq = o * o
        s_first = jnp.sum(jnp.where(first, sq, 0.0), axis=-1, keepdims=True)
        s_all = jnp.sum(sq, axis=-1, keepdims=True)
        ms = jnp.where(first, s_first, s_all - s_first) * (1.0 / HEAD_DIM)
        o_ref[0, :, pr * LANES:(pr + 1) * LANES] = (o * lax.rsqrt(ms + EPS) * g_ref[...]).astype(BF16)


def _sb_call(g_pair, sq, sk, sv, *, t, tk, pairs):
    b, _, s, _ = sq.shape
    groups = SB_HEADS // 2 // pairs
    return pl.pallas_call(
        functools.partial(_sb_kernel, t=t, tk=tk, pairs=pairs),
        grid=(b, groups, s // t),
        in_specs=[pl.BlockSpec((1, LANES), lambda bi, gi, qi: (0, 0)),
                  pl.BlockSpec((1, 2 * pairs, t, LANES), lambda bi, gi, qi: (bi, gi, qi, 0)),
                  pl.BlockSpec((1, pairs, s, LANES), lambda bi, gi, qi: (bi, gi, 0, 0)),
                  pl.BlockSpec((1, pairs, s, LANES), lambda bi, gi, qi: (bi, gi, 0, 0))],
        out_specs=pl.BlockSpec((1, t, pairs * LANES), lambda bi, gi, qi: (bi, qi, gi)),
        out_shape=jax.ShapeDtypeStruct((b, s, (SB_HEADS // 2) * LANES), BF16),
        scratch_shapes=[pltpu.VMEM((2 * pairs, t, LANES), F32)],
        compiler_params=pltpu.CompilerParams(dimension_semantics=("parallel", "parallel", "parallel"),
                                             vmem_limit_bytes=VMEM_LIMIT),
        name="sb_attn",
    )(g_pair, sq, sk, sv)


def _attn_kernel(nback_ref, lam_ref, gd_ref, bound_ref, dq_ref, dk_ref, dv_ref, gs_ref, sq_ref, sk_ref, sv_ref,
                 w1_ref, w3_ref, w2_ref, a_ref, b_ref, w1b_ref, w3b_ref, w2b_ref, acc0_ref, acc1_ref, sacc_ref,
                 *, t, ts, slopes, lambda_init, fixed_ref):
    hd = pl.program_id(1)
    i = pl.program_id(2)
    sub = t // ts
    chains = [(u, hh) for u in range(sub) for hh in range(2)]

    hd_col = jnp.full((t, 1), hd, jnp.int32)
    slope = jnp.full((t, 1), slopes[-1] * LOG2E, F32)
    for n in range(len(slopes) - 2, -1, -1):
        slope = jnp.where(hd_col == n, slopes[n] * LOG2E, slope)
    dq = (dq_ref[0, 0], dq_ref[0, 1])
    accs = (acc0_ref, acc1_ref)
    causal = lax.broadcasted_iota(jnp.int32, (t, t), 1) <= lax.broadcasted_iota(jnp.int32, (t, t), 0)
    ones_col = (lax.broadcasted_iota(jnp.int32, (t, LANES), 1) == 0).astype(BF16)
    row_pos = lax.broadcasted_iota(jnp.int32, (t, 1), 0) + i * t

    def d_block(j, carry, diagonal, kill=None):
        ks = pl.multiple_of(j * t, t)
        v_aug = jnp.concatenate([dv_ref[0, 0, pl.ds(ks, t), :], ones_col], axis=1)
        new = []
        for mp in range(2):
            s = _nt_dot(dq[mp], dk_ref[0, mp, pl.ds(ks, t), :])
            if diagonal:
                s = jnp.where(causal, s, NEG)
            if kill is not None:
                s = s - kill
            if fixed_ref:
                ref = slope * (row_pos - j * t).astype(F32) + bound_ref[:, 0:1]
                pv = jnp.dot(jnp.exp2(s - ref).astype(BF16), v_aug, preferred_element_type=F32)
                accs[mp][...] = pv if diagonal else accs[mp][...] + pv
            else:
                off = slope * jnp.full((t, 1), j * t, jnp.int32).astype(F32)
                m_new = jnp.max(s, axis=-1, keepdims=True) + off
                if not diagonal:
                    m_new = jnp.maximum(carry[mp], m_new)
                pv = jnp.dot(jnp.exp2(s - (m_new - off)).astype(BF16), v_aug, preferred_element_type=F32)
                accs[mp][...] = pv if diagonal else jnp.exp2(carry[mp] - m_new) * accs[mp][...] + pv
                new.append(m_new)
        return tuple(new)

    srow = lax.broadcasted_iota(jnp.int32, (ts, ts), 0)
    scol = lax.broadcasted_iota(jnp.int32, (ts, ts), 1)
    past = scol < srow
    later = (srow > scol).astype(BF16)

    def s_layer(layer, carry, masked):
        new = []
        for c, (u, hh) in enumerate(chains):
            jb = sub * i + u - layer
            c_in = carry[c] + jnp.where(jnp.full((ts, 1), jb, jnp.int32) >= 0, 0.0, NEG)
            ks = pl.multiple_of(jnp.maximum(jb, 0) * ts, ts)
            k = sk_ref[0, 0, pl.ds(ks, ts), :]
            v = sv_ref[0, 0, pl.ds(ks, ts), :]
            z = _nt_dot(sq_ref[0, hh, u * ts:(u + 1) * ts, :], k)
            neg_abs = -jnp.abs(z)
            log_rem = 0.5 * (neg_abs - z) - jnp.log2(1.0 + jnp.exp2(neg_abs))
            log_beta = log_rem + z
            if masked:
                log_rem = jnp.where(past, log_rem, 0.0)
            new.append(c_in + jnp.sum(log_rem, axis=-1, keepdims=True))
            suffix = jnp.dot(log_rem.astype(BF16), later, preferred_element_type=F32)
            a = jnp.exp2(log_beta + suffix + c_in)
            if masked:
                a = jnp.where(past, a, 0.0)
            av = jnp.dot(a.astype(BF16), v, preferred_element_type=F32)
            sacc_ref[c] = av if masked else sacc_ref[c] + av
        return tuple(new)

    d_carry = d_block(i, (), True)
    s_carry = s_layer(0, (jnp.zeros((ts, 1), F32),) * len(chains), True)
    no_prev = jnp.where(jnp.full((t, 1), i, jnp.int32) >= 1, 0.0, -NEG)
    d_carry = d_block(jnp.maximum(i - 1, 0), d_carry, False, kill=no_prev)
    s_carry = s_layer(1, s_carry, False)

    first = jnp.maximum(i + 1 - nback_ref[hd], 0)
    n_rest = jnp.maximum(i - 1 - first, 0)
    n_bunches = n_rest // DIFF_BLOCKS_PER_ITER

    def bunch(bb, carry):
        for n in range(DIFF_BLOCKS_PER_ITER):
            carry = d_block(first + bb * DIFF_BLOCKS_PER_ITER + n, carry, False)
        return carry

    d_carry = lax.fori_loop(0, n_bunches, bunch, d_carry)
    lax.fori_loop(first + n_bunches * DIFF_BLOCKS_PER_ITER, first + n_rest,
                  lambda j, cr: d_block(j, cr, False), d_carry)

    def any_live(carry):
        top = functools.reduce(jnp.maximum, carry)
        return (jnp.max(top) > LOG2E * F32_DEAD_LOG).astype(jnp.int32)

    def live_cond(state):
        return jnp.logical_and(state[0] <= sub * i + sub - 1, state[1] > 0)

    def live_body(state):
        carry = s_layer(state[0], state[2:], False)
        return (state[0] + 1, any_live(carry)) + carry

    lax.while_loop(live_cond, live_body, (jnp.int32(2), any_live(s_carry)) + s_carry)

    lp = lam_ref[...]
    lam = (jnp.exp(jnp.sum(lp[0:1] * lp[1:2], axis=-1, keepdims=True))
           - jnp.exp(jnp.sum(lp[2:3] * lp[3:4], axis=-1, keepdims=True)) + lambda_init)
    o = (acc0_ref[:, 0:LANES] / acc0_ref[:, LANES:LANES + 1]
         - lam * (acc1_ref[:, 0:LANES] / acc1_ref[:, LANES:LANES + 1]))
    ms = jnp.mean(o * o, axis=-1, keepdims=True)
    a_ref[0] = ((o * lax.rsqrt(ms + EPS) * gd_ref[...]) * (1.0 - lambda_init)).astype(BF16)

    first_half = lax.broadcasted_iota(jnp.int32, (1, LANES), 1) < HEAD_DIM
    for u in range(sub):
        o = jnp.where(first_half, sacc_ref[2 * u], sacc_ref[2 * u + 1])
        sq = o * o
        s_first = jnp.sum(jnp.where(first_half, sq, 0.0), axis=-1, keepdims=True)
        s_all = jnp.sum(sq, axis=-1, keepdims=True)
        ms = jnp.where(first_half, s_first, s_all - s_first) * (1.0 / HEAD_DIM)
        b_ref[0, u * ts:(u + 1) * ts, :] = (o * lax.rsqrt(ms + EPS) * gs_ref[...]).astype(BF16)

    w1b_ref[...] = w1_ref[...].astype(BF16)
    w3b_ref[...] = w3_ref[...].astype(BF16)
    w2b_ref[...] = w2_ref[...].astype(BF16)


def _attn_call(n_back, lam_p, g_diff, bound_row, dq, dk, dv, g_pair, sq, sk, sv, w1, w3, w2,
               *, t, ts, slopes, lambda_init, fixed_ref):
    b, _, s, _ = dq.shape
    nq = s // t
    steps = b * DIFF_HEADS * nq
    const2 = lambda bi, hi, qi: (0, 0)
    tile4 = lambda bi, hi, qi: (bi, hi, qi, 0)
    full4 = lambda bi, hi, qi: (bi, hi, 0, 0)
    step2 = lambda bi, hi, qi: ((bi * DIFF_HEADS + hi) * nq + qi, 0)
    out_spec = pl.BlockSpec((1, t, LANES), lambda bi, hi, qi: (bi, qi, hi))
    out_shape = jax.ShapeDtypeStruct((b, s, DIFF_HEADS * LANES), BF16)
    w_specs = [pl.BlockSpec((w.shape[0] // steps, w.shape[1]), step2) for w in (w1, w3, w2)]
    w_shapes = [jax.ShapeDtypeStruct(w.shape, BF16) for w in (w1, w3, w2)]
    return pl.pallas_call(
        functools.partial(_attn_kernel, t=t, ts=ts, slopes=slopes, lambda_init=lambda_init, fixed_ref=fixed_ref),
        grid=(b, DIFF_HEADS, s // t),
        in_specs=[pl.BlockSpec(memory_space=pltpu.SMEM),
                  pl.BlockSpec((4, HEAD_DIM), const2),
                  pl.BlockSpec((1, LANES), const2),
                  pl.BlockSpec((1, LANES), const2),
                  pl.BlockSpec((1, 2, t, LANES), tile4),
                  pl.BlockSpec((1, 2, s, LANES), full4),
                  pl.BlockSpec((1, 1, s, LANES), full4),
                  pl.BlockSpec((1, LANES), const2),
                  pl.BlockSpec((1, 2, t, LANES), tile4),
                  pl.BlockSpec((1, 1, s, LANES), full4),
                  pl.BlockSpec((1, 1, s, LANES), full4)] + w_specs,
        out_specs=[out_spec, out_spec] + w_specs,
        out_shape=[out_shape, out_shape] + w_shapes,
        scratch_shapes=[pltpu.VMEM((t, 2 * LANES), F32), pltpu.VMEM((t, 2 * LANES), F32),
                        pltpu.VMEM((2 * (t // ts), ts, LANES), F32)],
        compiler_params=pltpu.CompilerParams(dimension_semantics=("parallel", "parallel", "parallel"),
                                             vmem_limit_bytes=VMEM_LIMIT),
        name="attn_fixed_ref" if fixed_ref else "attn_running_max",
    )(n_back, lam_p, g_diff, bound_row, dq, dk, dv, g_pair, sq, sk, sv, w1, w3, w2)


def _outproj_kernel(a_ref, b_ref, w_ref, x_ref, mod_ref, g_ref, wr_ref, br_ref,
                    x1_ref, h2_ref, comb_ref):
    half = a_ref.shape[-1]
    y = (jnp.dot(a_ref[0], w_ref[0:half, :], preferred_element_type=F32)
         + jnp.dot(b_ref[0], w_ref[half:, :], preferred_element_type=F32))
    mod = mod_ref[0]
    gate_a, shift, scale = mod[2:3, :], mod[3:4, :], mod[4:5, :]
    x1 = x_ref[0] + gate_a * y
    x1_ref[0] = x1
    ms = jnp.mean(x1 * x1, axis=-1, keepdims=True)
    h2 = (x1 * lax.rsqrt(ms + EPS) * g_ref[...]) * (1.0 + scale) + shift
    h2_hi = h2.astype(BF16)
    h2_ref[0] = h2_hi

    h2_lo = (h2 - h2_hi.astype(F32)).astype(BF16)
    wr_hi, wr_lo = wr_ref[0], wr_ref[1]
    logits = (jnp.dot(h2_hi, wr_hi, preferred_element_type=F32)
              + jnp.dot(h2_lo, wr_hi, preferred_element_type=F32)
              + jnp.dot(h2_hi, wr_lo, preferred_element_type=F32)) + br_ref[...]
    lane = lax.broadcasted_iota(jnp.int32, logits.shape, 1).astype(F32)
    big = jnp.float32(LANES)

    def top(vals):
        mx = jnp.max(vals, axis=-1, keepdims=True)
        idx = jnp.min(jnp.where(vals == mx, lane, big), axis=-1, keepdims=True)
        return mx, idx

    is_group = (lane >= N_EXPERTS) & (lane < N_EXPERTS + N_GROUPS)
    g_logits = jnp.where(is_group, logits, NEG)
    g_max, g_idx = top(g_logits)
    gate_group = 1.0 / jnp.sum(jnp.where(is_group, jnp.exp(logits - g_max), 0.0), axis=-1, keepdims=True)
    g_sel = g_idx - N_EXPERTS
    in_group = (lane >= g_sel * EXPERTS_PER_GROUP) & (lane < (g_sel + 1.0) * EXPERTS_PER_GROUP)
    e_logits = jnp.where(in_group, logits, NEG)
    v1, i1 = top(e_logits)
    v2, i2 = top(jnp.where(lane == i1, NEG, e_logits))
    e2 = jnp.exp(v2 - v1)
    w_first = 1.0 / (1.0 + e2)
    w_second = e2 / (1.0 + e2)
    comb_ref[0] = (gate_group * (jnp.where(lane == i1, w_first, 0.0) + jnp.where(lane == i2, w_second, 0.0))
                   + jnp.where(lane == g_idx, 1.0, 0.0))


def _outproj_call(a_out, b_out, w_out, x, mod, g_ffn, w_router, b_router, *, tm):
    b, s, d = x.shape
    half = a_out.shape[-1]
    idx3 = lambda bi, ti: (bi, ti, 0)
    const2 = lambda bi, ti: (0, 0)
    return pl.pallas_call(
        _outproj_kernel,
        grid=(b, s // tm),
        in_specs=[pl.BlockSpec((1, tm, half), idx3),
                  pl.BlockSpec((1, tm, half), idx3),
                  pl.BlockSpec((2 * half, d), const2),
                  pl.BlockSpec((1, tm, d), idx3),
                  pl.BlockSpec((1, N_MOD, d), lambda bi, ti: (bi, 0, 0)),
                  pl.BlockSpec((1, d), const2),
                  pl.BlockSpec((2, d, LANES), lambda bi, ti: (0, 0, 0)),
                  pl.BlockSpec((1, LANES), const2)],
        out_specs=[pl.BlockSpec((1, tm, d), idx3),
                   pl.BlockSpec((1, tm, d), idx3),
                   pl.BlockSpec((1, tm, LANES), idx3)],
        out_shape=[jax.ShapeDtypeStruct((b, s, d), F32),
                   jax.ShapeDtypeStruct((b, s, d), BF16),
                   jax.ShapeDtypeStruct((b, s, LANES), F32)],
        compiler_params=pltpu.CompilerParams(dimension_semantics=("parallel", "parallel"),
                                             vmem_limit_bytes=VMEM_LIMIT),
        name="out_proj_router",
    )(a_out, b_out, w_out, x, mod, g_ffn, w_router, b_router)


def _moe_kernel(h_ref, comb_ref, earlier_ref, w1_ref, w3_ref, w2_ref, x1_ref, mod_ref, o_ref, *, unit, max_units):
    g = pl.program_id(2)
    tm = h_ref.shape[1]

    @pl.when(g == 0)
    def _():
        o_ref[0] = x1_ref[0]

    comb = comb_ref[0]
    lane = lax.broadcasted_iota(jnp.int32, (1, LANES), 1)
    member = jnp.sum(jnp.where(lane == N_EXPERTS + g, comb, 0.0), axis=-1, keepdims=True)
    sub = earlier_ref.shape[0]
    by_block = sum(jnp.where(lane == n, member[n * sub:(n + 1) * sub], 0.0) for n in range(tm // sub))
    by_block_t = by_block.T
    before_col = jnp.dot(earlier_ref[...], by_block.astype(BF16), preferred_element_type=F32)
    before_row = _nt_dot(by_block_t.astype(BF16), earlier_ref[...])
    totals = jnp.sum(by_block, axis=0, keepdims=True)
    offset = jnp.zeros((1, 1), F32)
    slot_cols, slot_rows = [], []
    for n in range(tm // sub):
        col_n = jnp.sum(jnp.where(lane == n, before_col, 0.0), axis=-1, keepdims=True) + offset
        slot_cols.append(jnp.where(member[n * sub:(n + 1) * sub] > 0.0, col_n, -1.0))
        slot_rows.append(jnp.where(by_block_t[n:n + 1, :] > 0.0, before_row[n:n + 1, :] + offset, -1.0))
        offset = offset + jnp.sum(jnp.where(lane == n, totals, 0.0), axis=-1, keepdims=True)
    slot_col = jnp.concatenate(slot_cols, axis=0)
    slot_row = jnp.concatenate(slot_rows, axis=1)
    count = jnp.sum(member)
    units = jnp.int32(0)
    for c in range(tm // unit):
        units += (count > float(c * unit)).astype(jnp.int32)

    comb_hi = comb.astype(BF16)
    comb_lo = (comb - comb_hi.astype(F32)).astype(BF16)
    gate_f = mod_ref[0][5:6, :]

    def chunk(first_unit, ch):
        base = (first_unit * unit).astype(F32)
        want_col = lax.broadcasted_iota(jnp.int32, (ch, 1), 0).astype(F32) + base
        want_row = lax.broadcasted_iota(jnp.int32, (1, ch), 1).astype(F32) + base
        pick = (slot_row == want_col).astype(BF16)
        place = (slot_col == want_row).astype(BF16)
        xc = jnp.dot(pick, h_ref[0], preferred_element_type=F32).astype(BF16)
        cw = (jnp.dot(pick, comb_hi, preferred_element_type=F32)
              + jnp.dot(pick, comb_lo, preferred_element_type=F32))
        y = jnp.zeros((ch, o_ref.shape[-1]), F32)
        for e in range(EXPERTS_PER_GROUP):
            weight = jnp.sum(jnp.where(lane == g * EXPERTS_PER_GROUP + e, cw, 0.0), axis=-1, keepdims=True)
            h1 = jnp.dot(xc, w1_ref[0, e], preferred_element_type=F32)
            h3 = jnp.dot(xc, w3_ref[0, e], preferred_element_type=F32)
            hg = (h1 * jax.nn.sigmoid(h1)) * h3 * weight
            y += jnp.dot(hg.astype(BF16), w2_ref[0, e], preferred_element_type=F32)
        o_ref[0] += gate_f * jnp.dot(place, y.astype(BF16), preferred_element_type=F32)

    def step(done):
        todo = jnp.minimum(units - done, max_units)
        for n in range(1, max_units + 1):
            pl.when(todo == n)(functools.partial(chunk, done, n * unit))
        return done + todo

    lax.while_loop(lambda done: done < units, step, jnp.int32(0))


def _moe_call(h2, comb, w1, w3, w2, x1, mod, *, tm, unit, max_units):
    b, s, d = x1.shape
    f = w1.shape[-1]
    sub = min(MXU_COLS, tm)
    tok3 = lambda bi, ti, gi: (bi, ti, 0)
    return pl.pallas_call(
        functools.partial(_moe_kernel, unit=unit, max_units=max_units),
        grid=(b, s // tm, N_GROUPS),
        in_specs=[pl.BlockSpec((1, tm, d), tok3),
                  pl.BlockSpec((1, tm, LANES), tok3),
                  pl.BlockSpec((sub, sub), lambda bi, ti, gi: (0, 0)),
                  pl.BlockSpec((1, EXPERTS_PER_GROUP, d, f), lambda bi, ti, gi: (gi, 0, 0, 0)),
                  pl.BlockSpec((1, EXPERTS_PER_GROUP, d, f), lambda bi, ti, gi: (gi, 0, 0, 0)),
                  pl.BlockSpec((1, EXPERTS_PER_GROUP, f, d), lambda bi, ti, gi: (gi, 0, 0, 0)),
                  pl.BlockSpec((1, tm, d), tok3),
                  pl.BlockSpec((1, N_MOD, d), lambda bi, ti, gi: (bi, 0, 0))],
        out_specs=pl.BlockSpec((1, tm, d), tok3),
        out_shape=jax.ShapeDtypeStruct((b, s, d), F32),
        compiler_params=pltpu.CompilerParams(dimension_semantics=("parallel", "parallel", "arbitrary"),
                                             vmem_limit_bytes=VMEM_LIMIT),
        name="moe_experts",
    )(h2, comb, jnp.tri(sub, k=-1, dtype=BF16), w1, w3, w2, x1, mod)


def _prepare_w_in(w_in):
    sq_start = 2 * DIFF_HEADS * 2 * HEAD_DIM + DIFF_HEADS * DIFF_VDIM
    col = np.ones((1, w_in.shape[1]), np.float32)
    col[0, sq_start:sq_start + SB_HEADS * HEAD_DIM] = LOG2E / math.sqrt(HEAD_DIM)
    return (w_in * col).astype(BF16)


def _pad_lanes(v, width=LANES):
    v = v.reshape(1, -1)
    return jnp.pad(v, ((0, 0), (0, width - v.shape[1])))


def kernel(x, c, w_ada, b_ada, g_attn, w_in, q_norm_g, k_norm_g, lambda_q1, lambda_k1, lambda_q2, lambda_k2,
           diff_out_g, sb_out_g, w_out, g_ffn, w_group, b_group, w_erouter, b_expert, w1, w3, w2):
    b, s, d = x.shape
    depth = w_ada.shape[0]
    tile = min(512, s)
    moe_tile = min(1024, s)
    moe_unit = min(128, moe_tile)
    moe_max_units = 4
    sb_tile = min(256, s)
    sb_keys = sb_tile
    slopes = tuple(2.0 ** (-8.0 * (n + 1) / DIFF_HEADS) for n in range(DIFF_HEADS))
    qk_scale = 1.0 / math.sqrt(HEAD_DIM)
    c_pad = jnp.pad(c, ((0, 16 - b), (0, 0)))
    qaug = np.zeros((1, LANES), np.float32)
    qaug[0, HEAD_DIM:HEAD_DIM + 2 * LOG2E_PARTS] = np.repeat(_bf16_parts(LOG2E, LOG2E_PARTS), 2)
    qaug_row = jnp.asarray(qaug)

    for layer in range(depth):
        lambda_init = 0.8 - 0.6 * math.exp(-0.3 * layer)
        mod = _mod_call(c_pad, w_ada[layer], b_ada[layer].reshape(1, -1))[:b].reshape(b, N_MOD, d)

        dq, dk, dv, sq, sk, sv = _proj_call(
            x, mod, g_attn[layer].reshape(1, d), _prepare_w_in(w_in[layer]),
            jnp.tile(q_norm_g[layer] * (qk_scale * LOG2E), 2).reshape(1, LANES),
            jnp.tile(k_norm_g[layer], 2).reshape(1, LANES), qaug_row,
            tm=tile, tk=tile, slopes=slopes)

        lam_p = jnp.stack([lambda_q1[layer], lambda_k1[layer], lambda_q2[layer], lambda_k2[layer]])
        score_bound = (1.02 * HEAD_DIM * qk_scale) * jnp.max(jnp.abs(q_norm_g[layer])) * jnp.max(jnp.abs(k_norm_g[layer]))
        dead_dist = (2.0 * score_bound - F32_DEAD_LOG) / jnp.asarray(slopes, F32)
        n_back = jnp.clip(jnp.floor(dead_dist / tile) + 2.0, 1.0, s // tile).astype(jnp.int32)
        attn_args = (n_back, lam_p, diff_out_g[layer].reshape(1, LANES),
                     jnp.full((1, LANES), score_bound * LOG2E, F32), dq, dk, dv,
                     jnp.tile(sb_out_g[layer], 2).reshape(1, LANES), sq, sk, sv,
                     w1[layer].reshape(-1, w1.shape[-1]), w3[layer].reshape(-1, w3.shape[-1]),
                     w2[layer].reshape(-1, w2.shape[-1]))
        attn_kw = dict(t=tile, ts=sb_tile, slopes=slopes, lambda_init=lambda_init)
        a_out, b_out, w1_bf, w3_bf, w2_bf = lax.cond(score_bound <= DIFF_FIXED_REF_MAX,
                                                     lambda args: _attn_call(*args, fixed_ref=True, **attn_kw),
                                                     lambda args: _attn_call(*args, fixed_ref=False, **attn_kw),
                                                     attn_args)

        w_router = jnp.concatenate(
            [jnp.transpose(w_erouter[layer], (1, 0, 2)).reshape(d, N_EXPERTS), w_group[layer]], axis=1)
        w_router = jnp.pad(w_router, ((0, 0), (0, LANES - w_router.shape[1])))
        w_router_hi = w_router.astype(BF16)
        w_router = jnp.stack([w_router_hi, (w_router - w_router_hi.astype(F32)).astype(BF16)])
        b_router = _pad_lanes(jnp.concatenate([b_expert[layer].reshape(-1), b_group[layer]]))
        x1, h2, comb = _outproj_call(a_out, b_out, w_out[layer].astype(BF16), x, mod,
                                     g_ffn[layer].reshape(1, d), w_router, b_router, tm=tile)

        x = _moe_call(h2, comb, w1_bf.reshape(w1.shape[1:]), w3_bf.reshape(w3.shape[1:]), w2_bf.reshape(w2.shape[1:]),
                      x1, mod, tm=moe_tile, unit=moe_unit, max_units=moe_max_units)
    return x
```

```python
import functools
import math

import jax
import jax.numpy as jnp
import ml_dtypes
import numpy as np
from jax import lax
from jax.experimental import pallas as pl
from jax.experimental.pallas import tpu as pltpu

HEAD_DIM = 64
DIFF_HEADS = 4
SB_HEADS = 8
DIFF_VDIM = 2 * HEAD_DIM
N_GROUPS = 4
EXPERTS_PER_GROUP = 8
N_EXPERTS = N_GROUPS * EXPERTS_PER_GROUP
N_MOD = 6
EPS = 1e-6
LANES = 128
MXU_COLS = 256
NEG = -1e30
DIFF_BLOCKS_PER_ITER = 4
F32_DEAD_LOG = -104.0
VMEM_LIMIT = 56 * 1024 * 1024

F32 = jnp.float32
BF16 = jnp.bfloat16

LOG2E = math.log2(math.e)
LOG2E_PARTS = 3
DIFF_FIXED_REF_MAX = 32.0


def _bf16_parts(value, n):
    parts, rest = [], value
    for _ in range(n):
        part = float(np.float32(rest).astype(ml_dtypes.bfloat16))
        parts.append(part)
        rest -= part
    return parts


def _nt_dot(a, b):
    return lax.dot_general(a, b, (((1,), (1,)), ((), ())), preferred_element_type=F32)


def _mod_kernel(c_ref, w_ref, b_ref, o_ref):
    c = c_ref[...]
    sc = c * jax.nn.sigmoid(c)
    w = w_ref[...]
    sc_hi, w_hi = sc.astype(BF16), w.astype(BF16)
    sc_lo, w_lo = (sc - sc_hi.astype(F32)).astype(BF16), (w - w_hi.astype(F32)).astype(BF16)
    o_ref[...] = (jnp.dot(sc_hi, w_hi, preferred_element_type=F32)
                  + jnp.dot(sc_lo, w_hi, preferred_element_type=F32)
                  + jnp.dot(sc_hi, w_lo, preferred_element_type=F32)) + b_ref[...]


def _mod_call(c_pad, w_ada, b_ada):
    rows, d = c_pad.shape
    n = w_ada.shape[1]
    tn = 1536
    return pl.pallas_call(
        _mod_kernel,
        grid=(n // tn,),
        in_specs=[pl.BlockSpec((rows, d), lambda j: (0, 0)),
                  pl.BlockSpec((d, tn), lambda j: (0, j)),
                  pl.BlockSpec((1, tn), lambda j: (0, j))],
        out_specs=pl.BlockSpec((rows, tn), lambda j: (0, j)),
        out_shape=jax.ShapeDtypeStruct((rows, n), F32),
        compiler_params=pltpu.CompilerParams(dimension_semantics=("parallel",),
                                             vmem_limit_bytes=VMEM_LIMIT),
        name="adaln_mod",
    )(c_pad, w_ada, b_ada)


_PROJ_BLOCKS = 24


def _proj_kernel(x_ref, mod_ref, g_ref, w32_ref, wcol_ref, qg_ref, kg_ref, qaug_ref,
                 dq_ref, dk_ref, dv_ref, sq_ref, sk_ref, sv_ref, w_ref, *, tm, tk, slopes):
    @pl.when((pl.program_id(0) == 0) & (pl.program_id(1) == 0))
    def _():
        w_ref[...] = (w32_ref[...] * wcol_ref[...]).astype(BF16)

    x = x_ref[0]
    mod = mod_ref[0]
    shift, scale = mod[0:1, :], mod[1:2, :]
    ms = jnp.mean(x * x, axis=-1, keepdims=True)
    h = (x * lax.rsqrt(ms + EPS) * g_ref[...]) * (1.0 + scale) + shift
    hb = h.astype(BF16)

    lane = lax.broadcasted_iota(jnp.int32, (1, LANES), 1)
    aug = (lane >= HEAD_DIM) & (lane < HEAD_DIM + 2 * LOG2E_PARTS)
    lo_lane = (aug & ((lane & 1) == 0)).astype(F32)
    hi_lane = (aug & ((lane & 1) == 1)).astype(F32)
    row = lax.broadcasted_iota(jnp.int32, (tm, 1), 0) + pl.program_id(1) * tm
    koff = row & (tk - 1)
    koff_lo = (koff & 255).astype(F32)
    koff_hi = (koff - (koff & 255)).astype(F32)
    koff_lanes = lo_lane * koff_lo + hi_lane * koff_hi
    low = lane < HEAD_DIM

    for c in range(_PROJ_BLOCKS * LANES // MXU_COLS):
        pc = jnp.dot(hb, w_ref[:, c * MXU_COLS:(c + 1) * MXU_COLS], preferred_element_type=F32)
        for half in range(MXU_COLS // LANES):
            blk = c * (MXU_COLS // LANES) + half
            piece = pc[:, half * LANES:(half + 1) * LANES]
            if blk < 8:
                sq = piece * piece
                ss_low = jnp.sum(jnp.where(low, sq, 0.0), axis=-1, keepdims=True)
                ss_high = jnp.sum(sq, axis=-1, keepdims=True) - ss_low
                inv = jnp.where(low, lax.rsqrt(ss_low * (1.0 / HEAD_DIM) + EPS),
                                lax.rsqrt(ss_high * (1.0 / HEAD_DIM) + EPS))
                head = blk % DIFF_HEADS
                if blk < DIFF_HEADS:
                    scaled, extra, out = piece * inv * qg_ref[...], qaug_ref[...], dq_ref
                else:
                    scaled, extra, out = piece * inv * kg_ref[...], koff_lanes * slopes[head], dk_ref
                out[0, 2 * head] = (jnp.where(low, scaled, 0.0) + extra).astype(BF16)
                out[0, 2 * head + 1] = (jnp.where(low, pltpu.roll(scaled, HEAD_DIM, 1), 0.0) + extra).astype(BF16)
            elif blk < 12:
                dv_ref[0, blk - 8] = piece.astype(BF16)
            elif blk < 16:
                sq_ref[0, 2 * (blk - 12)] = jnp.where(low, piece, 0.0).astype(BF16)
                sq_ref[0, 2 * (blk - 12) + 1] = jnp.where(low, 0.0, piece).astype(BF16)
            elif blk < 20:
                sk_ref[0, blk - 16] = piece.astype(BF16)
            else:
                sv_ref[0, blk - 20] = piece.astype(BF16)


def _proj_call(x, mod, g_attn, w_in, w_col, qg_pad, kg_pad, qaug_row, *, tm, tk, slopes):
    b, s, d = x.shape
    n = w_in.shape[1]
    nt = s // tm

    def hm(nh):
        return (jax.ShapeDtypeStruct((b, nh, s, LANES), BF16),
                pl.BlockSpec((1, nh, tm, LANES), lambda bi, ti: (bi, 0, ti, 0)))

    outs = [hm(8), hm(8), hm(4), hm(8), hm(4), hm(4)]
    return pl.pallas_call(
        functools.partial(_proj_kernel, tm=tm, tk=tk, slopes=slopes),
        grid=(b, nt),
        in_specs=[pl.BlockSpec((1, tm, d), lambda bi, ti: (bi, ti, 0)),
                  pl.BlockSpec((1, N_MOD, d), lambda bi, ti: (bi, 0, 0)),
                  pl.BlockSpec((1, d), lambda bi, ti: (0, 0)),
                  pl.BlockSpec((d, n), lambda bi, ti: (0, 0)),
                  pl.BlockSpec((1, n), lambda bi, ti: (0, 0)),
                  pl.BlockSpec((1, LANES), lambda bi, ti: (0, 0)),
                  pl.BlockSpec((1, LANES), lambda bi, ti: (0, 0)),
                  pl.BlockSpec((1, LANES), lambda bi, ti: (0, 0))],
        out_specs=[o[1] for o in outs],
        out_shape=[o[0] for o in outs],
        scratch_shapes=[pltpu.VMEM((d, n), BF16)],
        compiler_params=pltpu.CompilerParams(dimension_semantics=("arbitrary", "arbitrary"),
                                             vmem_limit_bytes=VMEM_LIMIT),
        name="in_proj",
    )(x, mod, g_attn, w_in, w_col, qg_pad, kg_pad, qaug_row)


def _attn_kernel(nback_ref, lam_ref, gd_ref, bound_ref, dq_ref, dk_ref, dv_ref, gs_ref, sq_ref, sk_ref, sv_ref,
                 w1_ref, w3_ref, w2_ref, a_ref, b_ref, w1b_ref, w3b_ref, w2b_ref, acc0_ref, acc1_ref, sacc_ref,
                 *, t, ts, slopes, lambda_init, fixed_ref):
    hd = pl.program_id(1)
    i = pl.program_id(2)
    sub = t // ts
    chains = [(u, hh) for u in range(sub) for hh in range(2)]

    hd_col = jnp.full((t, 1), hd, jnp.int32)
    slope = jnp.full((t, 1), slopes[-1] * LOG2E, F32)
    for n in range(len(slopes) - 2, -1, -1):
        slope = jnp.where(hd_col == n, slopes[n] * LOG2E, slope)
    dq = (dq_ref[0, 0], dq_ref[0, 1])
    accs = (acc0_ref, acc1_ref)
    causal = lax.broadcasted_iota(jnp.int32, (t, t), 1) <= lax.broadcasted_iota(jnp.int32, (t, t), 0)
    ones_col = (lax.broadcasted_iota(jnp.int32, (t, LANES), 1) == 0).astype(BF16)
    row_pos = lax.broadcasted_iota(jnp.int32, (t, 1), 0) + i * t

    def d_block(j, carry, diagonal, kill=None):
        ks = pl.multiple_of(j * t, t)
        v_aug = jnp.concatenate([dv_ref[0, 0, pl.ds(ks, t), :], ones_col], axis=1)
        new = [None, None]
        for mp in range(2):
            s = _nt_dot(dq[mp], dk_ref[0, mp, pl.ds(ks, t), :])
            if diagonal:
                s = jnp.where(causal, s, NEG)
            if kill is not None:
                s = s - kill
            if fixed_ref:
                ref = slope * (row_pos - j * t).astype(F32) + bound_ref[:, 0:1]
                pv = jnp.dot(jnp.exp2(s - ref).astype(BF16), v_aug, preferred_element_type=F32)
                accs[mp][...] = pv if diagonal else accs[mp][...] + pv
            else:
                off = slope * jnp.full((t, 1), j * t, jnp.int32).astype(F32)
                m_new = jnp.max(s, axis=-1, keepdims=True) + off
                if not diagonal:
                    m_new = jnp.maximum(carry[mp], m_new)
                pv = jnp.dot(jnp.exp2(s - (m_new - off)).astype(BF16), v_aug, preferred_element_type=F32)
                accs[mp][...] = pv if diagonal else jnp.exp2(carry[mp] - m_new) * accs[mp][...] + pv
                new[mp] = m_new
        return () if fixed_ref else tuple(new)

    srow = lax.broadcasted_iota(jnp.int32, (ts, ts), 0)
    scol = lax.broadcasted_iota(jnp.int32, (ts, ts), 1)
    past = scol < srow
    later = (srow > scol).astype(BF16)

    def s_layer(layer, carry, masked):
        new = list(carry)
        for c, (u, hh) in enumerate(chains):
            jb = sub * i + u - layer
            c_in = carry[c] + jnp.where(jnp.full((ts, 1), jb, jnp.int32) >= 0, 0.0, NEG)
            ks = pl.multiple_of(jnp.maximum(jb, 0) * ts, ts)
            k = sk_ref[0, 0, pl.ds(ks, ts), :]
            v = sv_ref[0, 0, pl.ds(ks, ts), :]
            z = _nt_dot(sq_ref[0, hh, u * ts:(u + 1) * ts, :], k)
            neg_abs = -jnp.abs(z)
            log_rem = 0.5 * (neg_abs - z) - jnp.log2(1.0 + jnp.exp2(neg_abs))
            log_beta = log_rem + z
            if masked:
                log_rem = jnp.where(past, log_rem, 0.0)
            new[c] = c_in + jnp.sum(log_rem, axis=-1, keepdims=True)
            suffix = jnp.dot(log_rem.astype(BF16), later, preferred_element_type=F32)
            a = jnp.exp2(log_beta + suffix + c_in)
            if masked:
                a = jnp.where(past, a, 0.0)
            av = jnp.dot(a.astype(BF16), v, preferred_element_type=F32)
            sacc_ref[c] = av if masked else sacc_ref[c] + av
        return tuple(new)

    no_prev = jnp.where(jnp.full((t, 1), i, jnp.int32) >= 1, 0.0, -NEG)
    prev = jnp.maximum(i - 1, 0)
    d_carry, s_carry = (), (jnp.zeros((ts, 1), F32),) * len(chains)
    d_carry = d_block(i, d_carry, True)
    s_carry = s_layer(0, s_carry, True)
    d_carry = d_block(prev, d_carry, False, kill=no_prev)
    s_carry = s_layer(1, s_carry, False)

    first = jnp.maximum(i + 1 - nback_ref[hd], 0)
    n_rest = jnp.maximum(i - 1 - first, 0)
    n_bunches = n_rest // DIFF_BLOCKS_PER_ITER

    def bunch(bb, carry):
        for n in range(DIFF_BLOCKS_PER_ITER):
            carry = d_block(first + bb * DIFF_BLOCKS_PER_ITER + n, carry, False)
        return carry

    d_carry = lax.fori_loop(0, n_bunches, bunch, d_carry)
    lax.fori_loop(first + n_bunches * DIFF_BLOCKS_PER_ITER, first + n_rest,
                  lambda j, cr: d_block(j, cr, False), d_carry)

    def any_live(carry):
        top = functools.reduce(jnp.maximum, carry)
        return (jnp.max(top) > LOG2E * F32_DEAD_LOG).astype(jnp.int32)

    def live_cond(state):
        return jnp.logical_and(state[0] <= sub * i + sub - 1, state[1] > 0)

    def live_body(state):
        carry = s_layer(state[0], state[2:], False)
        return (state[0] + 1, any_live(carry)) + carry

    lax.while_loop(live_cond, live_body, (jnp.int32(2), any_live(s_carry)) + s_carry)

    lp = lam_ref[...]
    lam = (jnp.exp(jnp.sum(lp[0:1] * lp[1:2], axis=-1, keepdims=True))
           - jnp.exp(jnp.sum(lp[2:3] * lp[3:4], axis=-1, keepdims=True)) + lambda_init)
    o = (acc0_ref[:, 0:LANES] / acc0_ref[:, LANES:LANES + 1]
         - lam * (acc1_ref[:, 0:LANES] / acc1_ref[:, LANES:LANES + 1]))
    ms = jnp.mean(o * o, axis=-1, keepdims=True)
    a_ref[0] = ((o * lax.rsqrt(ms + EPS) * gd_ref[...]) * (1.0 - lambda_init)).astype(BF16)

    first_half = lax.broadcasted_iota(jnp.int32, (1, LANES), 1) < HEAD_DIM
    for u in range(sub):
        o = jnp.where(first_half, sacc_ref[2 * u], sacc_ref[2 * u + 1])
        sq = o * o
        s_first = jnp.sum(jnp.where(first_half, sq, 0.0), axis=-1, keepdims=True)
        s_all = jnp.sum(sq, axis=-1, keepdims=True)
        ms = jnp.where(first_half, s_first, s_all - s_first) * (1.0 / HEAD_DIM)
        b_ref[0, u * ts:(u + 1) * ts, :] = (o * lax.rsqrt(ms + EPS) * gs_ref[...]).astype(BF16)

    w1b_ref[...] = w1_ref[...].astype(BF16)
    w3b_ref[...] = w3_ref[...].astype(BF16)
    w2b_ref[...] = w2_ref[...].astype(BF16)


def _attn_call(n_back, lam_p, g_diff, bound_row, dq, dk, dv, g_pair, sq, sk, sv, w1, w3, w2,
               *, t, ts, slopes, lambda_init, fixed_ref):
    b, _, s, _ = dq.shape
    nq = s // t
    steps = b * DIFF_HEADS * nq
    const2 = lambda bi, hi, qi: (0, 0)
    tile4 = lambda bi, hi, qi: (bi, hi, qi, 0)
    full4 = lambda bi, hi, qi: (bi, hi, 0, 0)
    step2 = lambda bi, hi, qi: ((bi * DIFF_HEADS + hi) * nq + qi, 0)
    out_spec = pl.BlockSpec((1, t, LANES), lambda bi, hi, qi: (bi, qi, hi))
    out_shape = jax.ShapeDtypeStruct((b, s, DIFF_HEADS * LANES), BF16)
    w_specs = [pl.BlockSpec((w.shape[0] // steps, w.shape[1]), step2) for w in (w1, w3, w2)]
    w_shapes = [jax.ShapeDtypeStruct(w.shape, BF16) for w in (w1, w3, w2)]
    return pl.pallas_call(
        functools.partial(_attn_kernel, t=t, ts=ts, slopes=slopes, lambda_init=lambda_init, fixed_ref=fixed_ref),
        grid=(b, DIFF_HEADS, s // t),
        in_specs=[pl.BlockSpec(memory_space=pltpu.SMEM),
                  pl.BlockSpec((4, HEAD_DIM), const2),
                  pl.BlockSpec((1, LANES), const2),
                  pl.BlockSpec((1, LANES), const2),
                  pl.BlockSpec((1, 2, t, LANES), tile4),
                  pl.BlockSpec((1, 2, s, LANES), full4),
                  pl.BlockSpec((1, 1, s, LANES), full4),
                  pl.BlockSpec((1, LANES), const2),
                  pl.BlockSpec((1, 2, t, LANES), tile4),
                  pl.BlockSpec((1, 1, s, LANES), full4),
                  pl.BlockSpec((1, 1, s, LANES), full4)] + w_specs,
        out_specs=[out_spec, out_spec] + w_specs,
        out_shape=[out_shape, out_shape] + w_shapes,
        scratch_shapes=[pltpu.VMEM((t, 2 * LANES), F32), pltpu.VMEM((t, 2 * LANES), F32),
                        pltpu.VMEM((2 * (t // ts), ts, LANES), F32)],
        compiler_params=pltpu.CompilerParams(dimension_semantics=("parallel", "parallel", "parallel"),
                                             vmem_limit_bytes=VMEM_LIMIT),
        name="attn_fixed_ref" if fixed_ref else "attn_running_max",
    )(n_back, lam_p, g_diff, bound_row, dq, dk, dv, g_pair, sq, sk, sv, w1, w3, w2)


def _outproj_kernel(a_ref, b_ref, w_ref, x_ref, mod_ref, g_ref, wr_ref, br_ref,
                    x1_ref, h2_ref, comb_ref):
    half = a_ref.shape[-1]
    y = (jnp.dot(a_ref[0], w_ref[0:half, :], preferred_element_type=F32)
         + jnp.dot(b_ref[0], w_ref[half:, :], preferred_element_type=F32))
    mod = mod_ref[0]
    gate_a, shift, scale = mod[2:3, :], mod[3:4, :], mod[4:5, :]
    x1 = x_ref[0] + gate_a * y
    x1_ref[0] = x1
    ms = jnp.mean(x1 * x1, axis=-1, keepdims=True)
    h2 = (x1 * lax.rsqrt(ms + EPS) * g_ref[...]) * (1.0 + scale) + shift
    h2_hi = h2.astype(BF16)
    h2_ref[0] = h2_hi

    h2_lo = (h2 - h2_hi.astype(F32)).astype(BF16)
    both = jnp.dot(h2_hi, wr_ref[...], preferred_element_type=F32)
    logits = (both[:, 0:LANES] + both[:, LANES:2 * LANES]
              + jnp.dot(h2_lo, wr_ref[:, 0:LANES], preferred_element_type=F32)) + br_ref[...]
    lane = lax.broadcasted_iota(jnp.int32, logits.shape, 1).astype(F32)
    big = jnp.float32(LANES)

    def top(vals):
        mx = jnp.max(vals, axis=-1, keepdims=True)
        idx = jnp.min(jnp.where(vals == mx, lane, big), axis=-1, keepdims=True)
        return mx, idx

    is_group = (lane >= N_EXPERTS) & (lane < N_EXPERTS + N_GROUPS)
    g_logits = jnp.where(is_group, logits, NEG)
    g_max, g_idx = top(g_logits)
    gate_group = 1.0 / jnp.sum(jnp.where(is_group, jnp.exp(logits - g_max), 0.0), axis=-1, keepdims=True)
    g_sel = g_idx - N_EXPERTS
    in_group = (lane >= g_sel * EXPERTS_PER_GROUP) & (lane < (g_sel + 1.0) * EXPERTS_PER_GROUP)
    e_logits = jnp.where(in_group, logits, NEG)
    v1, i1 = top(e_logits)
    v2, i2 = top(jnp.where(lane == i1, NEG, e_logits))
    e2 = jnp.exp(v2 - v1)
    w_first = 1.0 / (1.0 + e2)
    w_second = e2 / (1.0 + e2)
    comb_ref[0] = (gate_group * (jnp.where(lane == i1, w_first, 0.0) + jnp.where(lane == i2, w_second, 0.0))
                   + jnp.where(lane == g_idx, 1.0, 0.0))


def _outproj_call(a_out, b_out, w_out, x, mod, g_ffn, w_router, b_router, *, tm):
    b, s, d = x.shape
    half = a_out.shape[-1]
    idx3 = lambda bi, ti: (bi, ti, 0)
    const2 = lambda bi, ti: (0, 0)
    return pl.pallas_call(
        _outproj_kernel,
        grid=(b, s // tm),
        in_specs=[pl.BlockSpec((1, tm, half), idx3),
                  pl.BlockSpec((1, tm, half), idx3),
                  pl.BlockSpec((2 * half, d), const2),
                  pl.BlockSpec((1, tm, d), idx3),
                  pl.BlockSpec((1, N_MOD, d), lambda bi, ti: (bi, 0, 0)),
                  pl.BlockSpec((1, d), const2),
                  pl.BlockSpec((d, 2 * LANES), const2),
                  pl.BlockSpec((1, LANES), const2)],
        out_specs=[pl.BlockSpec((1, tm, d), idx3),
                   pl.BlockSpec((1, tm, d), idx3),
                   pl.BlockSpec((1, tm, LANES), idx3)],
        out_shape=[jax.ShapeDtypeStruct((b, s, d), F32),
                   jax.ShapeDtypeStruct((b, s, d), BF16),
                   jax.ShapeDtypeStruct((b, s, LANES), F32)],
        compiler_params=pltpu.CompilerParams(dimension_semantics=("parallel", "parallel"),
                                             vmem_limit_bytes=VMEM_LIMIT),
        name="out_proj_router",
    )(a_out, b_out, w_out, x, mod, g_ffn, w_router, b_router)


def _moe_kernel(h_ref, comb_ref, earlier_ref, w1_ref, w3_ref, w2_ref, x1_ref, mod_ref, o_ref, *, unit, max_units):
    g = pl.program_id(2)
    tm = h_ref.shape[1]

    @pl.when(g == 0)
    def _():
        o_ref[0] = x1_ref[0]

    comb = comb_ref[0]
    lane = lax.broadcasted_iota(jnp.int32, (1, LANES), 1)
    member = jnp.sum(jnp.where(lane == N_EXPERTS + g, comb, 0.0), axis=-1, keepdims=True)
    sub = earlier_ref.shape[0]
    by_block = sum(jnp.where(lane == n, member[n * sub:(n + 1) * sub], 0.0) for n in range(tm // sub))
    by_block_t = by_block.T
    before_col = jnp.dot(earlier_ref[...], by_block.astype(BF16), preferred_element_type=F32)
    before_row = _nt_dot(by_block_t.astype(BF16), earlier_ref[...])
    totals = jnp.sum(by_block, axis=0, keepdims=True)
    offset = jnp.zeros((1, 1), F32)
    slot_cols, slot_rows = [], []
    for n in range(tm // sub):
        col_n = jnp.sum(jnp.where(lane == n, before_col, 0.0), axis=-1, keepdims=True) + offset
        slot_cols.append(jnp.where(member[n * sub:(n + 1) * sub] > 0.0, col_n, -1.0))
        slot_rows.append(jnp.where(by_block_t[n:n + 1, :] > 0.0, before_row[n:n + 1, :] + offset, -1.0))
        offset = offset + jnp.sum(jnp.where(lane == n, totals, 0.0), axis=-1, keepdims=True)
    slot_col = jnp.concatenate(slot_cols, axis=0)
    slot_row = jnp.concatenate(slot_rows, axis=1)
    count = jnp.sum(member)
    units = jnp.int32(0)
    for c in range(tm // unit):
        units += (count > float(c * unit)).astype(jnp.int32)

    comb_hi = comb.astype(BF16)
    comb_lo = (comb - comb_hi.astype(F32)).astype(BF16)
    gate_f = mod_ref[0][5:6, :]

    def chunk(first_unit, ch):
        base = (first_unit * unit).astype(F32)
        want_col = lax.broadcasted_iota(jnp.int32, (ch, 1), 0).astype(F32) + base
        want_row = lax.broadcasted_iota(jnp.int32, (1, ch), 1).astype(F32) + base
        pick = (slot_row == want_col).astype(BF16)
        place = (slot_col == want_row).astype(BF16)
        xc = jnp.dot(pick, h_ref[0], preferred_element_type=F32).astype(BF16)
        cw = (jnp.dot(pick, comb_hi, preferred_element_type=F32)
              + jnp.dot(pick, comb_lo, preferred_element_type=F32))
        y = jnp.zeros((ch, o_ref.shape[-1]), F32)
        for e in range(EXPERTS_PER_GROUP):
            weight = jnp.sum(jnp.where(lane == g * EXPERTS_PER_GROUP + e, cw, 0.0), axis=-1, keepdims=True)
            h1 = jnp.dot(xc, w1_ref[0, e], preferred_element_type=F32)
            h3 = jnp.dot(xc, w3_ref[0, e], preferred_element_type=F32)
            hg = (h1 * jax.nn.sigmoid(h1)) * h3 * weight
            y += jnp.dot(hg.astype(BF16), w2_ref[0, e], preferred_element_type=F32)
        o_ref[0] += gate_f * jnp.dot(place, y.astype(BF16), preferred_element_type=F32)

    def step(done):
        todo = jnp.minimum(units - done, max_units)
        for n in range(1, max_units + 1):
            pl.when(todo == n)(functools.partial(chunk, done, n * unit))
        return done + todo

    lax.while_loop(lambda done: done < units, step, jnp.int32(0))


def _moe_call(h2, comb, w1, w3, w2, x1, mod, *, tm, unit, max_units):
    b, s, d = x1.shape
    f = w1.shape[-1]
    sub = min(MXU_COLS, tm)
    tok3 = lambda bi, ti, gi: (bi, ti, 0)
    return pl.pallas_call(
        functools.partial(_moe_kernel, unit=unit, max_units=max_units),
        grid=(b, s // tm, N_GROUPS),
        in_specs=[pl.BlockSpec((1, tm, d), tok3),
                  pl.BlockSpec((1, tm, LANES), tok3),
                  pl.BlockSpec((sub, sub), lambda bi, ti, gi: (0, 0)),
                  pl.BlockSpec((1, EXPERTS_PER_GROUP, d, f), lambda bi, ti, gi: (gi, 0, 0, 0)),
                  pl.BlockSpec((1, EXPERTS_PER_GROUP, d, f), lambda bi, ti, gi: (gi, 0, 0, 0)),
                  pl.BlockSpec((1, EXPERTS_PER_GROUP, f, d), lambda bi, ti, gi: (gi, 0, 0, 0)),
                  pl.BlockSpec((1, tm, d), tok3),
                  pl.BlockSpec((1, N_MOD, d), lambda bi, ti, gi: (bi, 0, 0))],
        out_specs=pl.BlockSpec((1, tm, d), tok3),
        out_shape=jax.ShapeDtypeStruct((b, s, d), F32),
        compiler_params=pltpu.CompilerParams(dimension_semantics=("parallel", "parallel", "arbitrary"),
                                             vmem_limit_bytes=VMEM_LIMIT),
        name="moe_experts",
    )(h2, comb, jnp.tri(sub, k=-1, dtype=BF16), w1, w3, w2, x1, mod)


def _w_in_column_scale(n_cols):
    sq_start = 2 * DIFF_HEADS * 2 * HEAD_DIM + DIFF_HEADS * DIFF_VDIM
    col = np.ones((1, n_cols), np.float32)
    col[0, sq_start:sq_start + SB_HEADS * HEAD_DIM] = LOG2E / math.sqrt(HEAD_DIM)
    return jnp.asarray(col)


def _pad_lanes(v, width=LANES):
    v = v.reshape(1, -1)
    return jnp.pad(v, ((0, 0), (0, width - v.shape[1])))


def kernel(x, c, w_ada, b_ada, g_attn, w_in, q_norm_g, k_norm_g, lambda_q1, lambda_k1, lambda_q2, lambda_k2,
           diff_out_g, sb_out_g, w_out, g_ffn, w_group, b_group, w_erouter, b_expert, w1, w3, w2):
    b, s, d = x.shape
    depth = w_ada.shape[0]
    tile = min(512, s)
    proj_tile = tile
    moe_tile = min(1024, s)
    moe_unit = min(128, moe_tile)
    moe_max_units = 4
    sb_tile = min(256, s)
    slopes = tuple(2.0 ** (-8.0 * (n + 1) / DIFF_HEADS) for n in range(DIFF_HEADS))
    qk_scale = 1.0 / math.sqrt(HEAD_DIM)
    c_pad = jnp.pad(c, ((0, 16 - b), (0, 0)))
    qaug = np.zeros((1, LANES), np.float32)
    qaug[0, HEAD_DIM:HEAD_DIM + 2 * LOG2E_PARTS] = np.repeat(_bf16_parts(LOG2E, LOG2E_PARTS), 2)
    qaug_row = jnp.asarray(qaug)

    for layer in range(depth):
        lambda_init = 0.8 - 0.6 * math.exp(-0.3 * layer)
        mod = _mod_call(c_pad, w_ada[layer], b_ada[layer].reshape(1, -1))[:b].reshape(b, N_MOD, d)

        dq, dk, dv, sq, sk, sv = _proj_call(
            x, mod, g_attn[layer].reshape(1, d), w_in[layer], _w_in_column_scale(w_in.shape[-1]),
            jnp.tile(q_norm_g[layer] * (qk_scale * LOG2E), 2).reshape(1, LANES),
            jnp.tile(k_norm_g[layer], 2).reshape(1, LANES), qaug_row,
            tm=proj_tile, tk=tile, slopes=slopes)

        lam_p = jnp.stack([lambda_q1[layer], lambda_k1[layer], lambda_q2[layer], lambda_k2[layer]])
        score_bound = (1.02 * HEAD_DIM * qk_scale) * jnp.max(jnp.abs(q_norm_g[layer])) * jnp.max(jnp.abs(k_norm_g[layer]))
        dead_dist = (2.0 * score_bound - F32_DEAD_LOG) / jnp.asarray(slopes, F32)
        n_back = jnp.clip(jnp.floor(dead_dist / tile) + 2.0, 1.0, s // tile).astype(jnp.int32)
        attn_args = (n_back, lam_p, diff_out_g[layer].reshape(1, LANES),
                     jnp.full((1, LANES), score_bound * LOG2E, F32), dq, dk, dv,
                     jnp.tile(sb_out_g[layer], 2).reshape(1, LANES), sq, sk, sv,
                     w1[layer].reshape(-1, w1.shape[-1]), w3[layer].reshape(-1, w3.shape[-1]),
                     w2[layer].reshape(-1, w2.shape[-1]))
        attn_kw = dict(t=tile, ts=sb_tile, slopes=slopes, lambda_init=lambda_init)
        a_out, b_out, w1_bf, w3_bf, w2_bf = lax.cond(score_bound <= DIFF_FIXED_REF_MAX,
                                                     lambda args: _attn_call(*args, fixed_ref=True, **attn_kw),
                                                     lambda args: _attn_call(*args, fixed_ref=False, **attn_kw),
                                                     attn_args)

        w_router = jnp.concatenate(
            [jnp.transpose(w_erouter[layer], (1, 0, 2)).reshape(d, N_EXPERTS), w_group[layer]], axis=1)
        w_router = jnp.pad(w_router, ((0, 0), (0, LANES - w_router.shape[1])))
        w_router_hi = w_router.astype(BF16)
        w_router = jnp.concatenate([w_router_hi, (w_router - w_router_hi.astype(F32)).astype(BF16)], axis=1)
        b_router = _pad_lanes(jnp.concatenate([b_expert[layer].reshape(-1), b_group[layer]]))
        x1, h2, comb = _outproj_call(a_out, b_out, w_out[layer].astype(BF16), x, mod,
                                     g_ffn[layer].reshape(1, d), w_router, b_router, tm=proj_tile)

        x = _moe_call(h2, comb, w1_bf.reshape(w1.shape[1:]), w3_bf.reshape(w3.shape[1:]), w2_bf.reshape(w2.shape[1:]),
                      x1, mod, tm=moe_tile, unit=moe_unit, max_units=moe_max_units)
    return x
```

```python
import functools
import math

import jax
import jax.numpy as jnp
import ml_dtypes
import numpy as np
from jax import lax
from jax.experimental import pallas as pl
from jax.experimental.pallas import tpu as pltpu

HEAD_DIM = 64
DIFF_HEADS = 4
SB_HEADS = 8
DIFF_VDIM = 2 * HEAD_DIM
N_GROUPS = 4
EXPERTS_PER_GROUP = 8
N_EXPERTS = N_GROUPS * EXPERTS_PER_GROUP
N_MOD = 6
EPS = 1e-6
LANES = 128
MXU_COLS = 256
NEG = -1e30
DIFF_BLOCKS_PER_ITER = 4
F32_DEAD_LOG = -104.0
VMEM_LIMIT = 56 * 1024 * 1024

F32 = jnp.float32
BF16 = jnp.bfloat16

LOG2E = math.log2(math.e)
LOG2E_PARTS = 3
DIFF_FIXED_REF_MAX = 32.0


def _bf16_parts(value, n):
    parts, rest = [], value
    for _ in range(n):
        part = float(np.float32(rest).astype(ml_dtypes.bfloat16))
        parts.append(part)
        rest -= part
    return parts


def _nt_dot(a, b):
    return lax.dot_general(a, b, (((1,), (1,)), ((), ())), preferred_element_type=F32)


def _mod_kernel(c_ref, w_ref, b_ref, o_ref):
    c = c_ref[...]
    sc = c * jax.nn.sigmoid(c)
    w = w_ref[...]
    sc_hi, w_hi = sc.astype(BF16), w.astype(BF16)
    sc_lo, w_lo = (sc - sc_hi.astype(F32)).astype(BF16), (w - w_hi.astype(F32)).astype(BF16)
    o_ref[...] = (jnp.dot(sc_hi, w_hi, preferred_element_type=F32)
                  + jnp.dot(sc_lo, w_hi, preferred_element_type=F32)
                  + jnp.dot(sc_hi, w_lo, preferred_element_type=F32)) + b_ref[...]


def _mod_call(c_pad, w_ada, b_ada):
    rows, d = c_pad.shape
    n = w_ada.shape[1]
    tn = 1536
    return pl.pallas_call(
        _mod_kernel,
        grid=(n // tn,),
        in_specs=[pl.BlockSpec((rows, d), lambda j: (0, 0)),
                  pl.BlockSpec((d, tn), lambda j: (0, j)),
                  pl.BlockSpec((1, tn), lambda j: (0, j))],
        out_specs=pl.BlockSpec((rows, tn), lambda j: (0, j)),
        out_shape=jax.ShapeDtypeStruct((rows, n), F32),
        compiler_params=pltpu.CompilerParams(dimension_semantics=("parallel",),
                                             vmem_limit_bytes=VMEM_LIMIT),
        name="adaln_mod",
    )(c_pad, w_ada, b_ada)


_PROJ_BLOCKS = 24


def _proj_kernel(x_ref, mod_ref, g_ref, w32_ref, wcol_ref, qg_ref, kg_ref, qaug_ref,
                 dq_ref, dk_ref, dv_ref, sq_ref, sk_ref, sv_ref, w_ref, *, tm, tk, slopes):
    @pl.when((pl.program_id(0) == 0) & (pl.program_id(1) == 0))
    def _():
        w_ref[...] = (w32_ref[...] * wcol_ref[...]).astype(BF16)

    x = x_ref[0]
    mod = mod_ref[0]
    shift, scale = mod[0:1, :], mod[1:2, :]
    ms = jnp.mean(x * x, axis=-1, keepdims=True)
    h = (x * lax.rsqrt(ms + EPS) * g_ref[...]) * (1.0 + scale) + shift
    hb = h.astype(BF16)

    lane = lax.broadcasted_iota(jnp.int32, (1, LANES), 1)
    aug = (lane >= HEAD_DIM) & (lane < HEAD_DIM + 2 * LOG2E_PARTS)
    lo_lane = (aug & ((lane & 1) == 0)).astype(F32)
    hi_lane = (aug & ((lane & 1) == 1)).astype(F32)
    row = lax.broadcasted_iota(jnp.int32, (tm, 1), 0) + pl.program_id(1) * tm
    koff = row & (tk - 1)
    koff_lo = (koff & 255).astype(F32)
    koff_hi = (koff - (koff & 255)).astype(F32)
    koff_lanes = lo_lane * koff_lo + hi_lane * koff_hi
    low = lane < HEAD_DIM

    for c in range(_PROJ_BLOCKS * LANES // MXU_COLS):
        pc = jnp.dot(hb, w_ref[:, c * MXU_COLS:(c + 1) * MXU_COLS], preferred_element_type=F32)
        for half in range(MXU_COLS // LANES):
            blk = c * (MXU_COLS // LANES) + half
            piece = pc[:, half * LANES:(half + 1) * LANES]
            if blk < 8:
                sq = piece * piece
                ss_low = jnp.sum(jnp.where(low, sq, 0.0), axis=-1, keepdims=True)
                ss_high = jnp.sum(sq, axis=-1, keepdims=True) - ss_low
                inv = jnp.where(low, lax.rsqrt(ss_low * (1.0 / HEAD_DIM) + EPS),
                                lax.rsqrt(ss_high * (1.0 / HEAD_DIM) + EPS))
                head = blk % DIFF_HEADS
                if blk < DIFF_HEADS:
                    scaled, extra, out = piece * inv * qg_ref[...], qaug_ref[...], dq_ref
                else:
                    scaled, extra, out = piece * inv * kg_ref[...], koff_lanes * slopes[head], dk_ref
                out[0, 2 * head] = (jnp.where(low, scaled, 0.0) + extra).astype(BF16)
                out[0, 2 * head + 1] = (jnp.where(low, pltpu.roll(scaled, HEAD_DIM, 1), 0.0) + extra).astype(BF16)
            elif blk < 12:
                dv_ref[0, blk - 8] = piece.astype(BF16)
            elif blk < 16:
                sq_ref[0, 2 * (blk - 12)] = jnp.where(low, piece, 0.0).astype(BF16)
                sq_ref[0, 2 * (blk - 12) + 1] = jnp.where(low, 0.0, piece).astype(BF16)
            elif blk < 20:
                sk_ref[0, blk - 16] = piece.astype(BF16)
            else:
                sv_ref[0, blk - 20] = piece.astype(BF16)


def _proj_call(x, mod, g_attn, w_in, w_col, qg_pad, kg_pad, qaug_row, *, tm, tk, slopes):
    b, s, d = x.shape
    n = w_in.shape[1]
    nt = s // tm

    def hm(nh):
        return (jax.ShapeDtypeStruct((b, nh, s, LANES), BF16),
                pl.BlockSpec((1, nh, tm, LANES), lambda bi, ti: (bi, 0, ti, 0)))

    outs = [hm(8), hm(8), hm(4), hm(8), hm(4), hm(4)]
    return pl.pallas_call(
        functools.partial(_proj_kernel, tm=tm, tk=tk, slopes=slopes),
        grid=(b, nt),
        in_specs=[pl.BlockSpec((1, tm, d), lambda bi, ti: (bi, ti, 0)),
                  pl.BlockSpec((1, N_MOD, d), lambda bi, ti: (bi, 0, 0)),
                  pl.BlockSpec((1, d), lambda bi, ti: (0, 0)),
                  pl.BlockSpec((d, n), lambda bi, ti: (0, 0)),
                  pl.BlockSpec((1, n), lambda bi, ti: (0, 0)),
                  pl.BlockSpec((1, LANES), lambda bi, ti: (0, 0)),
                  pl.BlockSpec((1, LANES), lambda bi, ti: (0, 0)),
                  pl.BlockSpec((1, LANES), lambda bi, ti: (0, 0))],
        out_specs=[o[1] for o in outs],
        out_shape=[o[0] for o in outs],
        scratch_shapes=[pltpu.VMEM((d, n), BF16)],
        compiler_params=pltpu.CompilerParams(dimension_semantics=("arbitrary", "arbitrary"),
                                             vmem_limit_bytes=VMEM_LIMIT),
        name="in_proj",
    )(x, mod, g_attn, w_in, w_col, qg_pad, kg_pad, qaug_row)


def _attn_kernel(nback_ref, lam_ref, gd_ref, bound_ref, dq_ref, dk_ref, dv_ref, gs_ref, sq_ref, sk_ref, sv_ref,
                 w1_ref, w3_ref, w2_ref, a_ref, b_ref, w1b_ref, w3b_ref, w2b_ref, acc0_ref, acc1_ref, sacc_ref,
                 *, t, ts, slopes, lambda_init, fixed_ref):
    hd = pl.program_id(1)
    i = pl.program_id(2)
    sub = t // ts
    chains = [(u, hh) for u in range(sub) for hh in range(2)]

    hd_col = jnp.full((t, 1), hd, jnp.int32)
    slope = jnp.full((t, 1), slopes[-1] * LOG2E, F32)
    for n in range(len(slopes) - 2, -1, -1):
        slope = jnp.where(hd_col == n, slopes[n] * LOG2E, slope)
    dq = (dq_ref[0, 0], dq_ref[0, 1])
    accs = (acc0_ref, acc1_ref)
    causal = lax.broadcasted_iota(jnp.int32, (t, t), 1) <= lax.broadcasted_iota(jnp.int32, (t, t), 0)
    ones_col = (lax.broadcasted_iota(jnp.int32, (t, LANES), 1) == 0).astype(BF16)
    row_pos = lax.broadcasted_iota(jnp.int32, (t, 1), 0) + i * t

    def d_block(j, carry, diagonal, kill=None):
        ks = pl.multiple_of(j * t, t)
        v_aug = jnp.concatenate([dv_ref[0, 0, pl.ds(ks, t), :], ones_col], axis=1)
        new = [None, None]
        for mp in range(2):
            s = _nt_dot(dq[mp], dk_ref[0, mp, pl.ds(ks, t), :])
            if diagonal:
                s = jnp.where(causal, s, NEG)
            if kill is not None:
                s = s - kill
            if fixed_ref:
                ref = slope * (row_pos - j * t).astype(F32) + bound_ref[:, 0:1]
                pv = jnp.dot(jnp.exp2(s - ref).astype(BF16), v_aug, preferred_element_type=F32)
                accs[mp][...] = pv if diagonal else accs[mp][...] + pv
            else:
                off = slope * jnp.full((t, 1), j * t, jnp.int32).astype(F32)
                m_new = jnp.max(s, axis=-1, keepdims=True) + off
                if not diagonal:
                    m_new = jnp.maximum(carry[mp], m_new)
                pv = jnp.dot(jnp.exp2(s - (m_new - off)).astype(BF16), v_aug, preferred_element_type=F32)
                accs[mp][...] = pv if diagonal else jnp.exp2(carry[mp] - m_new) * accs[mp][...] + pv
                new[mp] = m_new
        return () if fixed_ref else tuple(new)

    srow = lax.broadcasted_iota(jnp.int32, (ts, ts), 0)
    scol = lax.broadcasted_iota(jnp.int32, (ts, ts), 1)
    past = scol < srow
    later = (srow > scol).astype(BF16)

    def s_layer(layer, carry, masked):
        new = list(carry)
        for c, (u, hh) in enumerate(chains):
            jb = sub * i + u - layer
            c_in = carry[c] + jnp.where(jnp.full((ts, 1), jb, jnp.int32) >= 0, 0.0, NEG)
            ks = pl.multiple_of(jnp.maximum(jb, 0) * ts, ts)
            k = sk_ref[0, 0, pl.ds(ks, ts), :]
            v = sv_ref[0, 0, pl.ds(ks, ts), :]
            z = _nt_dot(sq_ref[0, hh, u * ts:(u + 1) * ts, :], k)
            neg_abs = -jnp.abs(z)
            log_rem = 0.5 * (neg_abs - z) - jnp.log2(1.0 + jnp.exp2(neg_abs))
            log_beta = log_rem + z
            if masked:
                log_rem = jnp.where(past, log_rem, 0.0)
            new[c] = c_in + jnp.sum(log_rem, axis=-1, keepdims=True)
            suffix = jnp.dot(log_rem.astype(BF16), later, preferred_element_type=F32)
            a = jnp.exp2(log_beta + suffix + c_in)
            if masked:
                a = jnp.where(past, a, 0.0)
            av = jnp.dot(a.astype(BF16), v, preferred_element_type=F32)
            sacc_ref[c] = av if masked else sacc_ref[c] + av
        return tuple(new)

    no_prev = jnp.where(jnp.full((t, 1), i, jnp.int32) >= 1, 0.0, -NEG)
    prev = jnp.maximum(i - 1, 0)
    d_carry, s_carry = (), (jnp.zeros((ts, 1), F32),) * len(chains)
    d_carry = d_block(i, d_carry, True)
    s_carry = s_layer(0, s_carry, True)
    d_carry = d_block(prev, d_carry, False, kill=no_prev)
    s_carry = s_layer(1, s_carry, False)

    first = jnp.maximum(i + 1 - nback_ref[hd], 0)
    n_rest = jnp.maximum(i - 1 - first, 0)

    def bunch(size, start, bb, carry):
        for n in range(size):
            carry = d_block(start + bb * size + n, carry, False)
        return carry

    start, left, size = first, n_rest, DIFF_BLOCKS_PER_ITER
    while size >= 1:
        trips = left // size
        d_carry = lax.fori_loop(0, trips, functools.partial(bunch, size, start), d_carry)
        start, left, size = start + trips * size, left - trips * size, size // 2

    def any_live(carry):
        top = functools.reduce(jnp.maximum, carry)
        return (jnp.max(top) > LOG2E * F32_DEAD_LOG).astype(jnp.int32)

    def live_cond(state):
        return jnp.logical_and(state[0] <= sub * i + sub - 1, state[1] > 0)

    def live_body(state):
        carry = s_layer(state[0], state[2:], False)
        return (state[0] + 1, any_live(carry)) + carry

    lax.while_loop(live_cond, live_body, (jnp.int32(2), any_live(s_carry)) + s_carry)

    lp = lam_ref[...]
    lam = (jnp.exp(jnp.sum(lp[0:1] * lp[1:2], axis=-1, keepdims=True))
           - jnp.exp(jnp.sum(lp[2:3] * lp[3:4], axis=-1, keepdims=True)) + lambda_init)
    o = (acc0_ref[:, 0:LANES] / acc0_ref[:, LANES:LANES + 1]
         - lam * (acc1_ref[:, 0:LANES] / acc1_ref[:, LANES:LANES + 1]))
    ms = jnp.mean(o * o, axis=-1, keepdims=True)
    a_ref[0] = ((o * lax.rsqrt(ms + EPS) * gd_ref[...]) * (1.0 - lambda_init)).astype(BF16)

    first_half = lax.broadcasted_iota(jnp.int32, (1, LANES), 1) < HEAD_DIM
    for u in range(sub):
        o = jnp.where(first_half, sacc_ref[2 * u], sacc_ref[2 * u + 1])
        sq = o * o
        s_first = jnp.sum(jnp.where(first_half, sq, 0.0), axis=-1, keepdims=True)
        s_all = jnp.sum(sq, axis=-1, keepdims=True)
        ms = jnp.where(first_half, s_first, s_all - s_first) * (1.0 / HEAD_DIM)
        b_ref[0, u * ts:(u + 1) * ts, :] = (o * lax.rsqrt(ms + EPS) * gs_ref[...]).astype(BF16)

    w1b_ref[...] = w1_ref[...].astype(BF16)
    w3b_ref[...] = w3_ref[...].astype(BF16)
    w2b_ref[...] = w2_ref[...].astype(BF16)


def _attn_call(n_back, lam_p, g_diff, bound_row, dq, dk, dv, g_pair, sq, sk, sv, w1, w3, w2,
               *, t, ts, slopes, lambda_init, fixed_ref):
    b, _, s, _ = dq.shape
    nq = s // t
    steps = b * DIFF_HEADS * nq
    const2 = lambda bi, hi, qi: (0, 0)
    tile4 = lambda bi, hi, qi: (bi, hi, qi, 0)
    full4 = lambda bi, hi, qi: (bi, hi, 0, 0)
    step2 = lambda bi, hi, qi: ((bi * DIFF_HEADS + hi) * nq + qi, 0)
    out_spec = pl.BlockSpec((1, t, LANES), lambda bi, hi, qi: (bi, qi, hi))
    out_shape = jax.ShapeDtypeStruct((b, s, DIFF_HEADS * LANES), BF16)
    w_specs = [pl.BlockSpec((w.shape[0] // steps, w.shape[1]), step2) for w in (w1, w3, w2)]
    w_shapes = [jax.ShapeDtypeStruct(w.shape, BF16) for w in (w1, w3, w2)]
    return pl.pallas_call(
        functools.partial(_attn_kernel, t=t, ts=ts, slopes=slopes, lambda_init=lambda_init, fixed_ref=fixed_ref),
        grid=(b, DIFF_HEADS, s // t),
        in_specs=[pl.BlockSpec(memory_space=pltpu.SMEM),
                  pl.BlockSpec((4, HEAD_DIM), const2),
                  pl.BlockSpec((1, LANES), const2),
                  pl.BlockSpec((1, LANES), const2),
                  pl.BlockSpec((1, 2, t, LANES), tile4),
                  pl.BlockSpec((1, 2, s, LANES), full4),
                  pl.BlockSpec((1, 1, s, LANES), full4),
                  pl.BlockSpec((1, LANES), const2),
                  pl.BlockSpec((1, 2, t, LANES), tile4),
                  pl.BlockSpec((1, 1, s, LANES), full4),
                  pl.BlockSpec((1, 1, s, LANES), full4)] + w_specs,
        out_specs=[out_spec, out_spec] + w_specs,
        out_shape=[out_shape, out_shape] + w_shapes,
        scratch_shapes=[pltpu.VMEM((t, 2 * LANES), F32), pltpu.VMEM((t, 2 * LANES), F32),
                        pltpu.VMEM((2 * (t // ts), ts, LANES), F32)],
        compiler_params=pltpu.CompilerParams(dimension_semantics=("parallel", "parallel", "parallel"),
                                             vmem_limit_bytes=VMEM_LIMIT),
        name="attn_fixed_ref" if fixed_ref else "attn_running_max",
    )(n_back, lam_p, g_diff, bound_row, dq, dk, dv, g_pair, sq, sk, sv, w1, w3, w2)


def _outproj_kernel(a_ref, b_ref, w_ref, x_ref, mod_ref, g_ref, wr_ref, br_ref,
                    x1_ref, h2_ref, comb_ref):
    half = a_ref.shape[-1]
    y = (jnp.dot(a_ref[0], w_ref[0:half, :], preferred_element_type=F32)
         + jnp.dot(b_ref[0], w_ref[half:, :], preferred_element_type=F32))
    mod = mod_ref[0]
    gate_a, shift, scale = mod[2:3, :], mod[3:4, :], mod[4:5, :]
    x1 = x_ref[0] + gate_a * y
    x1_ref[0] = x1
    ms = jnp.mean(x1 * x1, axis=-1, keepdims=True)
    h2 = (x1 * lax.rsqrt(ms + EPS) * g_ref[...]) * (1.0 + scale) + shift
    h2_hi = h2.astype(BF16)
    h2_ref[0] = h2_hi

    h2_lo = (h2 - h2_hi.astype(F32)).astype(BF16)
    both = jnp.dot(h2_hi, wr_ref[...], preferred_element_type=F32)
    logits = (both[:, 0:LANES] + both[:, LANES:2 * LANES]
              + jnp.dot(h2_lo, wr_ref[:, 0:LANES], preferred_element_type=F32)) + br_ref[...]
    lane = lax.broadcasted_iota(jnp.int32, logits.shape, 1).astype(F32)
    big = jnp.float32(LANES)

    def top(vals):
        mx = jnp.max(vals, axis=-1, keepdims=True)
        idx = jnp.min(jnp.where(vals == mx, lane, big), axis=-1, keepdims=True)
        return mx, idx

    is_group = (lane >= N_EXPERTS) & (lane < N_EXPERTS + N_GROUPS)
    g_logits = jnp.where(is_group, logits, NEG)
    g_max, g_idx = top(g_logits)
    gate_group = 1.0 / jnp.sum(jnp.where(is_group, jnp.exp(logits - g_max), 0.0), axis=-1, keepdims=True)
    g_sel = g_idx - N_EXPERTS
    in_group = (lane >= g_sel * EXPERTS_PER_GROUP) & (lane < (g_sel + 1.0) * EXPERTS_PER_GROUP)
    e_logits = jnp.where(in_group, logits, NEG)
    v1, i1 = top(e_logits)
    v2, i2 = top(jnp.where(lane == i1, NEG, e_logits))
    e2 = jnp.exp(v2 - v1)
    w_first = 1.0 / (1.0 + e2)
    w_second = e2 / (1.0 + e2)
    comb_ref[0] = (gate_group * (jnp.where(lane == i1, w_first, 0.0) + jnp.where(lane == i2, w_second, 0.0))
                   + jnp.where(lane == g_idx, 1.0, 0.0))


def _outproj_call(a_out, b_out, w_out, x, mod, g_ffn, w_router, b_router, *, tm):
    b, s, d = x.shape
    half = a_out.shape[-1]
    idx3 = lambda bi, ti: (bi, ti, 0)
    const2 = lambda bi, ti: (0, 0)
    return pl.pallas_call(
        _outproj_kernel,
        grid=(b, s // tm),
        in_specs=[pl.BlockSpec((1, tm, half), idx3),
                  pl.BlockSpec((1, tm, half), idx3),
                  pl.BlockSpec((2 * half, d), const2),
                  pl.BlockSpec((1, tm, d), idx3),
                  pl.BlockSpec((1, N_MOD, d), lambda bi, ti: (bi, 0, 0)),
                  pl.BlockSpec((1, d), const2),
                  pl.BlockSpec((d, 2 * LANES), const2),
                  pl.BlockSpec((1, LANES), const2)],
        out_specs=[pl.BlockSpec((1, tm, d), idx3),
                   pl.BlockSpec((1, tm, d), idx3),
                   pl.BlockSpec((1, tm, LANES), idx3)],
        out_shape=[jax.ShapeDtypeStruct((b, s, d), F32),
                   jax.ShapeDtypeStruct((b, s, d), BF16),
                   jax.ShapeDtypeStruct((b, s, LANES), F32)],
        compiler_params=pltpu.CompilerParams(dimension_semantics=("parallel", "parallel"),
                                             vmem_limit_bytes=VMEM_LIMIT),
        name="out_proj_router",
    )(a_out, b_out, w_out, x, mod, g_ffn, w_router, b_router)


def _moe_kernel(h_ref, comb_ref, earlier_ref, w1_ref, w3_ref, w2_ref, x1_ref, mod_ref, o_ref, *, unit, max_units):
    g = pl.program_id(2)
    tm = h_ref.shape[1]

    @pl.when(g == 0)
    def _():
        o_ref[0] = x1_ref[0]

    comb = comb_ref[0]
    lane = lax.broadcasted_iota(jnp.int32, (1, LANES), 1)
    member = jnp.sum(jnp.where(lane == N_EXPERTS + g, comb, 0.0), axis=-1, keepdims=True)
    sub = earlier_ref.shape[0]
    by_block = sum(jnp.where(lane == n, member[n * sub:(n + 1) * sub], 0.0) for n in range(tm // sub))
    by_block_t = by_block.T
    before_col = jnp.dot(earlier_ref[...], by_block.astype(BF16), preferred_element_type=F32)
    before_row = _nt_dot(by_block_t.astype(BF16), earlier_ref[...])
    totals = jnp.sum(by_block, axis=0, keepdims=True)
    offset = jnp.zeros((1, 1), F32)
    slot_cols, slot_rows = [], []
    for n in range(tm // sub):
        col_n = jnp.sum(jnp.where(lane == n, before_col, 0.0), axis=-1, keepdims=True) + offset
        slot_cols.append(jnp.where(member[n * sub:(n + 1) * sub] > 0.0, col_n, -1.0))
        slot_rows.append(jnp.where(by_block_t[n:n + 1, :] > 0.0, before_row[n:n + 1, :] + offset, -1.0))
        offset = offset + jnp.sum(jnp.where(lane == n, totals, 0.0), axis=-1, keepdims=True)
    slot_col = jnp.concatenate(slot_cols, axis=0)
    slot_row = jnp.concatenate(slot_rows, axis=1)
    count = jnp.sum(member)
    units = jnp.int32(0)
    for c in range(tm // unit):
        units += (count > float(c * unit)).astype(jnp.int32)

    comb_hi = comb.astype(BF16)
    comb_lo = (comb - comb_hi.astype(F32)).astype(BF16)
    gate_f = mod_ref[0][5:6, :]

    def chunk(first_unit, ch):
        base = (first_unit * unit).astype(F32)
        want_col = lax.broadcasted_iota(jnp.int32, (ch, 1), 0).astype(F32) + base
        want_row = lax.broadcasted_iota(jnp.int32, (1, ch), 1).astype(F32) + base
        pick = (slot_row == want_col).astype(BF16)
        place = (slot_col == want_row).astype(BF16)
        xc = jnp.dot(pick, h_ref[0], preferred_element_type=F32).astype(BF16)
        cw = (jnp.dot(pick, comb_hi, preferred_element_type=F32)
              + jnp.dot(pick, comb_lo, preferred_element_type=F32))
        y = jnp.zeros((ch, o_ref.shape[-1]), F32)
        for e in range(EXPERTS_PER_GROUP):
            weight = jnp.sum(jnp.where(lane == g * EXPERTS_PER_GROUP + e, cw, 0.0), axis=-1, keepdims=True)
            h1 = jnp.dot(xc, w1_ref[0, e], preferred_element_type=F32)
            h3 = jnp.dot(xc, w3_ref[0, e], preferred_element_type=F32)
            hg = (h1 * jax.nn.sigmoid(h1)) * h3 * weight
            y += jnp.dot(hg.astype(BF16), w2_ref[0, e], preferred_element_type=F32)
        o_ref[0] += gate_f * jnp.dot(place, y.astype(BF16), preferred_element_type=F32)

    def step(done):
        todo = jnp.minimum(units - done, max_units)
        for n in range(1, max_units + 1):
            pl.when(todo == n)(functools.partial(chunk, done, n * unit))
        return done + todo

    lax.while_loop(lambda done: done < units, step, jnp.int32(0))


def _moe_call(h2, comb, w1, w3, w2, x1, mod, *, tm, unit, max_units):
    b, s, d = x1.shape
    f = w1.shape[-1]
    sub = min(MXU_COLS, tm)
    tok3 = lambda bi, ti, gi: (bi, ti, 0)
    return pl.pallas_call(
        functools.partial(_moe_kernel, unit=unit, max_units=max_units),
        grid=(b, s // tm, N_GROUPS),
        in_specs=[pl.BlockSpec((1, tm, d), tok3),
                  pl.BlockSpec((1, tm, LANES), tok3),
                  pl.BlockSpec((sub, sub), lambda bi, ti, gi: (0, 0)),
                  pl.BlockSpec((1, EXPERTS_PER_GROUP, d, f), lambda bi, ti, gi: (gi, 0, 0, 0)),
                  pl.BlockSpec((1, EXPERTS_PER_GROUP, d, f), lambda bi, ti, gi: (gi, 0, 0, 0)),
                  pl.BlockSpec((1, EXPERTS_PER_GROUP, f, d), lambda bi, ti, gi: (gi, 0, 0, 0)),
                  pl.BlockSpec((1, tm, d), tok3),
                  pl.BlockSpec((1, N_MOD, d), lambda bi, ti, gi: (bi, 0, 0))],
        out_specs=pl.BlockSpec((1, tm, d), tok3),
        out_shape=jax.ShapeDtypeStruct((b, s, d), F32),
        compiler_params=pltpu.CompilerParams(dimension_semantics=("parallel", "parallel", "arbitrary"),
                                             vmem_limit_bytes=VMEM_LIMIT),
        name="moe_experts",
    )(h2, comb, jnp.tri(sub, k=-1, dtype=BF16), w1, w3, w2, x1, mod)


def _w_in_column_scale(n_cols):
    sq_start = 2 * DIFF_HEADS * 2 * HEAD_DIM + DIFF_HEADS * DIFF_VDIM
    col = np.ones((1, n_cols), np.float32)
    col[0, sq_start:sq_start + SB_HEADS * HEAD_DIM] = LOG2E / math.sqrt(HEAD_DIM)
    return jnp.asarray(col)


def _pad_lanes(v, width=LANES):
    v = v.reshape(1, -1)
    return jnp.pad(v, ((0, 0), (0, width - v.shape[1])))


def kernel(x, c, w_ada, b_ada, g_attn, w_in, q_norm_g, k_norm_g, lambda_q1, lambda_k1, lambda_q2, lambda_k2,
           diff_out_g, sb_out_g, w_out, g_ffn, w_group, b_group, w_erouter, b_expert, w1, w3, w2):
    b, s, d = x.shape
    depth = w_ada.shape[0]
    tile = min(512, s)
    proj_tile = tile
    moe_tile = min(1024, s)
    moe_unit = min(128, moe_tile)
    moe_max_units = 4
    sb_tile = min(256, s)
    slopes = tuple(2.0 ** (-8.0 * (n + 1) / DIFF_HEADS) for n in range(DIFF_HEADS))
    qk_scale = 1.0 / math.sqrt(HEAD_DIM)
    c_pad = jnp.pad(c, ((0, 16 - b), (0, 0)))
    qaug = np.zeros((1, LANES), np.float32)
    qaug[0, HEAD_DIM:HEAD_DIM + 2 * LOG2E_PARTS] = np.repeat(_bf16_parts(LOG2E, LOG2E_PARTS), 2)
    qaug_row = jnp.asarray(qaug)

    for layer in range(depth):
        lambda_init = 0.8 - 0.6 * math.exp(-0.3 * layer)
        mod = _mod_call(c_pad, w_ada[layer], b_ada[layer].reshape(1, -1))[:b].reshape(b, N_MOD, d)

        dq, dk, dv, sq, sk, sv = _proj_call(
            x, mod, g_attn[layer].reshape(1, d), w_in[layer], _w_in_column_scale(w_in.shape[-1]),
            jnp.tile(q_norm_g[layer] * (qk_scale * LOG2E), 2).reshape(1, LANES),
            jnp.tile(k_norm_g[layer], 2).reshape(1, LANES), qaug_row,
            tm=proj_tile, tk=tile, slopes=slopes)

        lam_p = jnp.stack([lambda_q1[layer], lambda_k1[layer], lambda_q2[layer], lambda_k2[layer]])
        score_bound = (1.02 * HEAD_DIM * qk_scale) * jnp.max(jnp.abs(q_norm_g[layer])) * jnp.max(jnp.abs(k_norm_g[layer]))
        dead_dist = (2.0 * score_bound - F32_DEAD_LOG) / jnp.asarray(slopes, F32)
        n_back = jnp.clip(jnp.floor(dead_dist / tile) + 2.0, 1.0, s // tile).astype(jnp.int32)
        attn_args = (n_back, lam_p, diff_out_g[layer].reshape(1, LANES),
                     jnp.full((1, LANES), score_bound * LOG2E, F32), dq, dk, dv,
                     jnp.tile(sb_out_g[layer], 2).reshape(1, LANES), sq, sk, sv,
                     w1[layer].reshape(-1, w1.shape[-1]), w3[layer].reshape(-1, w3.shape[-1]),
                     w2[layer].reshape(-1, w2.shape[-1]))
        attn_kw = dict(t=tile, ts=sb_tile, slopes=slopes, lambda_init=lambda_init)
        a_out, b_out, w1_bf, w3_bf, w2_bf = lax.cond(score_bound <= DIFF_FIXED_REF_MAX,
                                                     lambda args: _attn_call(*args, fixed_ref=True, **attn_kw),
                                                     lambda args: _attn_call(*args, fixed_ref=False, **attn_kw),
                                                     attn_args)

        w_router = jnp.concatenate(
            [jnp.transpose(w_erouter[layer], (1, 0, 2)).reshape(d, N_EXPERTS), w_group[layer]], axis=1)
        w_router = jnp.pad(w_router, ((0, 0), (0, LANES - w_router.shape[1])))
        w_router_hi = w_router.astype(BF16)
        w_router = jnp.concatenate([w_router_hi, (w_router - w_router_hi.astype(F32)).astype(BF16)], axis=1)
        b_router = _pad_lanes(jnp.concatenate([b_expert[layer].reshape(-1), b_group[layer]]))
        x1, h2, comb = _outproj_call(a_out, b_out, w_out[layer].astype(BF16), x, mod,
                                     g_ffn[layer].reshape(1, d), w_router, b_router, tm=proj_tile)

        x = _moe_call(h2, comb, w1_bf.reshape(w1.shape[1:]), w3_bf.reshape(w3.shape[1:]), w2_bf.reshape(w2.shape[1:]),
                      x1, mod, tm=moe_tile, unit=moe_unit, max_units=moe_max_units)
    return x
```

```python
import functools
import math

import jax
import jax.numpy as jnp
import ml_dtypes
import numpy as np
from jax import lax
from jax.experimental import pallas as pl
from jax.experimental.pallas import tpu as pltpu

HEAD_DIM = 64
DIFF_HEADS = 4
SB_HEADS = 8
DIFF_VDIM = 2 * HEAD_DIM
N_GROUPS = 4
EXPERTS_PER_GROUP = 8
N_EXPERTS = N_GROUPS * EXPERTS_PER_GROUP
N_MOD = 6
EPS = 1e-6
LANES = 128
MXU_COLS = 256
NEG = -1e30
DIFF_BLOCKS_PER_ITER = 4
F32_DEAD_LOG = -104.0
VMEM_LIMIT = 56 * 1024 * 1024

F32 = jnp.float32
BF16 = jnp.bfloat16

LOG2E = math.log2(math.e)
LOG2E_PARTS = 3
DIFF_FIXED_REF_MAX = 32.0


def _bf16_parts(value, n):
    parts, rest = [], value
    for _ in range(n):
        part = float(np.float32(rest).astype(ml_dtypes.bfloat16))
        parts.append(part)
        rest -= part
    return parts


def _nt_dot(a, b):
    return lax.dot_general(a, b, (((1,), (1,)), ((), ())), preferred_element_type=F32)


def _mod_kernel(c_ref, w_ref, b_ref, o_ref):
    c = c_ref[...]
    sc = c * jax.nn.sigmoid(c)
    w = w_ref[...]
    sc_hi, w_hi = sc.astype(BF16), w.astype(BF16)
    sc_lo, w_lo = (sc - sc_hi.astype(F32)).astype(BF16), (w - w_hi.astype(F32)).astype(BF16)
    o_ref[...] = (jnp.dot(sc_hi, w_hi, preferred_element_type=F32)
                  + jnp.dot(sc_lo, w_hi, preferred_element_type=F32)
                  + jnp.dot(sc_hi, w_lo, preferred_element_type=F32)) + b_ref[...]


def _mod_call(c_pad, w_ada, b_ada):
    rows, d = c_pad.shape
    n = w_ada.shape[1]
    tn = 1536
    return pl.pallas_call(
        _mod_kernel,
        grid=(n // tn,),
        in_specs=[pl.BlockSpec((rows, d), lambda j: (0, 0)),
                  pl.BlockSpec((d, tn), lambda j: (0, j)),
                  pl.BlockSpec((1, tn), lambda j: (0, j))],
        out_specs=pl.BlockSpec((rows, tn), lambda j: (0, j)),
        out_shape=jax.ShapeDtypeStruct((rows, n), F32),
        compiler_params=pltpu.CompilerParams(dimension_semantics=("parallel",),
                                             vmem_limit_bytes=VMEM_LIMIT),
        name="adaln_mod",
    )(c_pad, w_ada, b_ada)


_PROJ_BLOCKS = 24


def _proj_kernel(x_ref, mod_ref, g_ref, w32_ref, wcol_ref, qg_ref, kg_ref, qaug_ref,
                 dq_ref, dk_ref, dv_ref, sq_ref, sk_ref, sv_ref, w_ref, *, tm, tk, slopes):
    @pl.when((pl.program_id(0) == 0) & (pl.program_id(1) == 0))
    def _():
        w_ref[...] = (w32_ref[...] * wcol_ref[...]).astype(BF16)

    x = x_ref[0]
    mod = mod_ref[0]
    shift, scale = mod[0:1, :], mod[1:2, :]
    ms = jnp.mean(x * x, axis=-1, keepdims=True)
    h = (x * lax.rsqrt(ms + EPS) * g_ref[...]) * (1.0 + scale) + shift
    hb = h.astype(BF16)

    lane = lax.broadcasted_iota(jnp.int32, (1, LANES), 1)
    aug = (lane >= HEAD_DIM) & (lane < HEAD_DIM + 2 * LOG2E_PARTS)
    lo_lane = (aug & ((lane & 1) == 0)).astype(F32)
    hi_lane = (aug & ((lane & 1) == 1)).astype(F32)
    row = lax.broadcasted_iota(jnp.int32, (tm, 1), 0) + pl.program_id(1) * tm
    koff = row & (tk - 1)
    koff_lo = (koff & 255).astype(F32)
    koff_hi = (koff - (koff & 255)).astype(F32)
    koff_lanes = lo_lane * koff_lo + hi_lane * koff_hi
    low = lane < HEAD_DIM

    for c in range(_PROJ_BLOCKS * LANES // MXU_COLS):
        pc = jnp.dot(hb, w_ref[:, c * MXU_COLS:(c + 1) * MXU_COLS], preferred_element_type=F32)
        for half in range(MXU_COLS // LANES):
            blk = c * (MXU_COLS // LANES) + half
            piece = pc[:, half * LANES:(half + 1) * LANES]
            if blk < 8:
                sq = piece * piece
                ss_low = jnp.sum(jnp.where(low, sq, 0.0), axis=-1, keepdims=True)
                ss_high = jnp.sum(sq, axis=-1, keepdims=True) - ss_low
                inv = jnp.where(low, lax.rsqrt(ss_low * (1.0 / HEAD_DIM) + EPS),
                                lax.rsqrt(ss_high * (1.0 / HEAD_DIM) + EPS))
                head = blk % DIFF_HEADS
                if blk < DIFF_HEADS:
                    scaled, extra, out = piece * inv * qg_ref[...], qaug_ref[...], dq_ref
                else:
                    scaled, extra, out = piece * inv * kg_ref[...], koff_lanes * slopes[head], dk_ref
                out[0, 2 * head] = (jnp.where(low, scaled, 0.0) + extra).astype(BF16)
                out[0, 2 * head + 1] = (jnp.where(low, pltpu.roll(scaled, HEAD_DIM, 1), 0.0) + extra).astype(BF16)
            elif blk < 12:
                dv_ref[0, blk - 8] = piece.astype(BF16)
            elif blk < 16:
                sq_ref[0, 2 * (blk - 12)] = jnp.where(low, piece, 0.0).astype(BF16)
                sq_ref[0, 2 * (blk - 12) + 1] = jnp.where(low, 0.0, piece).astype(BF16)
            elif blk < 20:
                sk_ref[0, blk - 16] = piece.astype(BF16)
            else:
                sv_ref[0, blk - 20] = piece.astype(BF16)


def _proj_call(x, mod, g_attn, w_in, w_col, qg_pad, kg_pad, qaug_row, *, tm, tk, slopes):
    b, s, d = x.shape
    n = w_in.shape[1]
    nt = s // tm

    def hm(nh):
        return (jax.ShapeDtypeStruct((b, nh, s, LANES), BF16),
                pl.BlockSpec((1, nh, tm, LANES), lambda bi, ti: (bi, 0, ti, 0)))

    outs = [hm(8), hm(8), hm(4), hm(8), hm(4), hm(4)]
    return pl.pallas_call(
        functools.partial(_proj_kernel, tm=tm, tk=tk, slopes=slopes),
        grid=(b, nt),
        in_specs=[pl.BlockSpec((1, tm, d), lambda bi, ti: (bi, ti, 0)),
                  pl.BlockSpec((1, N_MOD, d), lambda bi, ti: (bi, 0, 0)),
                  pl.BlockSpec((1, d), lambda bi, ti: (0, 0)),
                  pl.BlockSpec((d, n), lambda bi, ti: (0, 0)),
                  pl.BlockSpec((1, n), lambda bi, ti: (0, 0)),
                  pl.BlockSpec((1, LANES), lambda bi, ti: (0, 0)),
                  pl.BlockSpec((1, LANES), lambda bi, ti: (0, 0)),
                  pl.BlockSpec((1, LANES), lambda bi, ti: (0, 0))],
        out_specs=[o[1] for o in outs],
        out_shape=[o[0] for o in outs],
        scratch_shapes=[pltpu.VMEM((d, n), BF16)],
        compiler_params=pltpu.CompilerParams(dimension_semantics=("arbitrary", "arbitrary"),
                                             vmem_limit_bytes=VMEM_LIMIT),
        name="in_proj",
    )(x, mod, g_attn, w_in, w_col, qg_pad, kg_pad, qaug_row)


def _attn_kernel(nback_ref, lam_ref, gd_ref, bound_ref, dq_ref, dk_ref, dv_ref, gs_ref, sq_ref, sk_ref, sv_ref,
                 w1_ref, w3_ref, w2_ref, a_ref, b_ref, w1b_ref, w3b_ref, w2b_ref, acc0_ref, acc1_ref, sacc_ref,
                 *, t, ts, slopes, lambda_init, fixed_ref):
    hd = pl.program_id(1)
    i = pl.program_id(2)
    sub = t // ts
    chains = [(u, hh) for u in range(sub) for hh in range(2)]

    hd_col = jnp.full((t, 1), hd, jnp.int32)
    slope = jnp.full((t, 1), slopes[-1] * LOG2E, F32)
    for n in range(len(slopes) - 2, -1, -1):
        slope = jnp.where(hd_col == n, slopes[n] * LOG2E, slope)
    dq = (dq_ref[0, 0], dq_ref[0, 1])
    accs = (acc0_ref, acc1_ref)
    causal = lax.broadcasted_iota(jnp.int32, (t, t), 1) <= lax.broadcasted_iota(jnp.int32, (t, t), 0)
    ones_col = (lax.broadcasted_iota(jnp.int32, (t, LANES), 1) == 0).astype(BF16)
    row_pos = lax.broadcasted_iota(jnp.int32, (t, 1), 0) + i * t

    def d_block(j, carry, diagonal, kill=None):
        ks = pl.multiple_of(j * t, t)
        v_aug = jnp.concatenate([dv_ref[0, 0, pl.ds(ks, t), :], ones_col], axis=1)
        new = [None, None]
        for mp in range(2):
            s = _nt_dot(dq[mp], dk_ref[0, mp, pl.ds(ks, t), :])
            if diagonal:
                s = jnp.where(causal, s, NEG)
            if kill is not None:
                s = s - kill
            if fixed_ref:
                ref = slope * (row_pos - j * t).astype(F32) + bound_ref[:, 0:1]
                pv = jnp.dot(jnp.exp2(s - ref).astype(BF16), v_aug, preferred_element_type=F32)
                accs[mp][...] = pv if diagonal else accs[mp][...] + pv
            else:
                off = slope * jnp.full((t, 1), j * t, jnp.int32).astype(F32)
                m_new = jnp.max(s, axis=-1, keepdims=True) + off
                if not diagonal:
                    m_new = jnp.maximum(carry[mp], m_new)
                pv = jnp.dot(jnp.exp2(s - (m_new - off)).astype(BF16), v_aug, preferred_element_type=F32)
                accs[mp][...] = pv if diagonal else jnp.exp2(carry[mp] - m_new) * accs[mp][...] + pv
                new[mp] = m_new
        return () if fixed_ref else tuple(new)

    srow = lax.broadcasted_iota(jnp.int32, (ts, ts), 0)
    scol = lax.broadcasted_iota(jnp.int32, (ts, ts), 1)
    past = scol < srow
    later = (srow > scol).astype(BF16)

    def s_layer(layer, carry, masked):
        new = list(carry)
        for c, (u, hh) in enumerate(chains):
            jb = sub * i + u - layer
            c_in = carry[c] + jnp.where(jnp.full((ts, 1), jb, jnp.int32) >= 0, 0.0, NEG)
            ks = pl.multiple_of(jnp.maximum(jb, 0) * ts, ts)
            k = sk_ref[0, 0, pl.ds(ks, ts), :]
            v = sv_ref[0, 0, pl.ds(ks, ts), :]
            z = _nt_dot(sq_ref[0, hh, u * ts:(u + 1) * ts, :], k)
            neg_abs = -jnp.abs(z)
            log_rem = 0.5 * (neg_abs - z) - jnp.log2(1.0 + jnp.exp2(neg_abs))
            log_beta = log_rem + z
            if masked:
                log_rem = jnp.where(past, log_rem, 0.0)
            new[c] = c_in + jnp.sum(log_rem, axis=-1, keepdims=True)
            suffix = jnp.dot(log_rem.astype(BF16), later, preferred_element_type=F32)
            a = jnp.exp2(log_beta + suffix + c_in)
            if masked:
                a = jnp.where(past, a, 0.0)
            av = jnp.dot(a.astype(BF16), v, preferred_element_type=F32)
            sacc_ref[c] = av if masked else sacc_ref[c] + av
        return tuple(new)

    no_prev = jnp.where(jnp.full((t, 1), i, jnp.int32) >= 1, 0.0, -NEG)
    prev = jnp.maximum(i - 1, 0)
    d_carry, s_carry = (), (jnp.zeros((ts, 1), F32),) * len(chains)
    d_carry = d_block(i, d_carry, True)
    d_carry = d_block(prev, d_carry, False, kill=no_prev)
    s_carry = s_layer(0, s_carry, True)
    s_carry = s_layer(1, s_carry, False)

    first = jnp.maximum(i + 1 - nback_ref[hd], 0)
    n_rest = jnp.maximum(i - 1 - first, 0)

    def bunch(size, start, bb, carry):
        for n in range(size):
            carry = d_block(start + bb * size + n, carry, False)
        return carry

    start, left, size = first, n_rest, DIFF_BLOCKS_PER_ITER
    while size >= 1:
        trips = left // size
        d_carry = lax.fori_loop(0, trips, functools.partial(bunch, size, start), d_carry)
        start, left, size = start + trips * size, left - trips * size, size // 2

    def any_live(carry):
        top = functools.reduce(jnp.maximum, carry)
        return (jnp.max(top) > LOG2E * F32_DEAD_LOG).astype(jnp.int32)

    def live_cond(state):
        return jnp.logical_and(state[0] <= sub * i + sub - 1, state[1] > 0)

    def live_body(state):
        carry = s_layer(state[0], state[2:], False)
        return (state[0] + 1, any_live(carry)) + carry

    lax.while_loop(live_cond, live_body, (jnp.int32(2), any_live(s_carry)) + s_carry)

    lp = lam_ref[...]
    lam = (jnp.exp(jnp.sum(lp[0:1] * lp[1:2], axis=-1, keepdims=True))
           - jnp.exp(jnp.sum(lp[2:3] * lp[3:4], axis=-1, keepdims=True)) + lambda_init)
    o = (acc0_ref[:, 0:LANES] / acc0_ref[:, LANES:LANES + 1]
         - lam * (acc1_ref[:, 0:LANES] / acc1_ref[:, LANES:LANES + 1]))
    ms = jnp.mean(o * o, axis=-1, keepdims=True)
    a_ref[0] = ((o * lax.rsqrt(ms + EPS) * gd_ref[...]) * (1.0 - lambda_init)).astype(BF16)

    first_half = lax.broadcasted_iota(jnp.int32, (1, LANES), 1) < HEAD_DIM
    for u in range(sub):
        o = jnp.where(first_half, sacc_ref[2 * u], sacc_ref[2 * u + 1])
        sq = o * o
        s_first = jnp.sum(jnp.where(first_half, sq, 0.0), axis=-1, keepdims=True)
        s_all = jnp.sum(sq, axis=-1, keepdims=True)
        ms = jnp.where(first_half, s_first, s_all - s_first) * (1.0 / HEAD_DIM)
        b_ref[0, u * ts:(u + 1) * ts, :] = (o * lax.rsqrt(ms + EPS) * gs_ref[...]).astype(BF16)

    w1b_ref[...] = w1_ref[...].astype(BF16)
    w3b_ref[...] = w3_ref[...].astype(BF16)
    w2b_ref[...] = w2_ref[...].astype(BF16)


def _attn_call(n_back, lam_p, g_diff, bound_row, dq, dk, dv, g_pair, sq, sk, sv, w1, w3, w2,
               *, t, ts, slopes, lambda_init, fixed_ref):
    b, _, s, _ = dq.shape
    nq = s // t
    steps = b * DIFF_HEADS * nq
    const2 = lambda bi, hi, qi: (0, 0)
    tile4 = lambda bi, hi, qi: (bi, hi, qi, 0)
    full4 = lambda bi, hi, qi: (bi, hi, 0, 0)
    step2 = lambda bi, hi, qi: ((bi * DIFF_HEADS + hi) * nq + qi, 0)
    out_spec = pl.BlockSpec((1, t, LANES), lambda bi, hi, qi: (bi, qi, hi))
    out_shape = jax.ShapeDtypeStruct((b, s, DIFF_HEADS * LANES), BF16)
    w_specs = [pl.BlockSpec((w.shape[0] // steps, w.shape[1]), step2) for w in (w1, w3, w2)]
    w_shapes = [jax.ShapeDtypeStruct(w.shape, BF16) for w in (w1, w3, w2)]
    return pl.pallas_call(
        functools.partial(_attn_kernel, t=t, ts=ts, slopes=slopes, lambda_init=lambda_init, fixed_ref=fixed_ref),
        grid=(b, DIFF_HEADS, s // t),
        in_specs=[pl.BlockSpec(memory_space=pltpu.SMEM),
                  pl.BlockSpec((4, HEAD_DIM), const2),
                  pl.BlockSpec((1, LANES), const2),
                  pl.BlockSpec((1, LANES), const2),
                  pl.BlockSpec((1, 2, t, LANES), tile4),
                  pl.BlockSpec((1, 2, s, LANES), full4),
                  pl.BlockSpec((1, 1, s, LANES), full4),
                  pl.BlockSpec((1, LANES), const2),
                  pl.BlockSpec((1, 2, t, LANES), tile4),
                  pl.BlockSpec((1, 1, s, LANES), full4),
                  pl.BlockSpec((1, 1, s, LANES), full4)] + w_specs,
        out_specs=[out_spec, out_spec] + w_specs,
        out_shape=[out_shape, out_shape] + w_shapes,
        scratch_shapes=[pltpu.VMEM((t, 2 * LANES), F32), pltpu.VMEM((t, 2 * LANES), F32),
                        pltpu.VMEM((2 * (t // ts), ts, LANES), F32)],
        compiler_params=pltpu.CompilerParams(dimension_semantics=("parallel", "parallel", "parallel"),
                                             vmem_limit_bytes=VMEM_LIMIT),
        name="attn_fixed_ref" if fixed_ref else "attn_running_max",
    )(n_back, lam_p, g_diff, bound_row, dq, dk, dv, g_pair, sq, sk, sv, w1, w3, w2)


def _outproj_kernel(a_ref, b_ref, w_ref, x_ref, mod_ref, g_ref, wr_ref, br_ref,
                    x1_ref, h2_ref, comb_ref):
    half = a_ref.shape[-1]
    y = (jnp.dot(a_ref[0], w_ref[0:half, :], preferred_element_type=F32)
         + jnp.dot(b_ref[0], w_ref[half:, :], preferred_element_type=F32))
    mod = mod_ref[0]
    gate_a, shift, scale = mod[2:3, :], mod[3:4, :], mod[4:5, :]
    x1 = x_ref[0] + gate_a * y
    x1_ref[0] = x1
    ms = jnp.mean(x1 * x1, axis=-1, keepdims=True)
    h2 = (x1 * lax.rsqrt(ms + EPS) * g_ref[...]) * (1.0 + scale) + shift
    h2_hi = h2.astype(BF16)
    h2_ref[0] = h2_hi

    h2_lo = (h2 - h2_hi.astype(F32)).astype(BF16)
    both = jnp.dot(h2_hi, wr_ref[...], preferred_element_type=F32)
    logits = (both[:, 0:LANES] + both[:, LANES:2 * LANES]
              + jnp.dot(h2_lo, wr_ref[:, 0:LANES], preferred_element_type=F32)) + br_ref[...]
    lane = lax.broadcasted_iota(jnp.int32, logits.shape, 1).astype(F32)
    big = jnp.float32(LANES)

    def top(vals):
        mx = jnp.max(vals, axis=-1, keepdims=True)
        idx = jnp.min(jnp.where(vals == mx, lane, big), axis=-1, keepdims=True)
        return mx, idx

    is_group = (lane >= N_EXPERTS) & (lane < N_EXPERTS + N_GROUPS)
    g_logits = jnp.where(is_group, logits, NEG)
    g_max, g_idx = top(g_logits)
    gate_group = 1.0 / jnp.sum(jnp.where(is_group, jnp.exp(logits - g_max), 0.0), axis=-1, keepdims=True)
    g_sel = g_idx - N_EXPERTS
    in_group = (lane >= g_sel * EXPERTS_PER_GROUP) & (lane < (g_sel + 1.0) * EXPERTS_PER_GROUP)
    e_logits = jnp.where(in_group, logits, NEG)
    v1, i1 = top(e_logits)
    v2, i2 = top(jnp.where(lane == i1, NEG, e_logits))
    e2 = jnp.exp(v2 - v1)
    w_first = 1.0 / (1.0 + e2)
    w_second = e2 / (1.0 + e2)
    comb_ref[0] = (gate_group * (jnp.where(lane == i1, w_first, 0.0) + jnp.where(lane == i2, w_second, 0.0))
                   + jnp.where(lane == g_idx, 1.0, 0.0))


def _outproj_call(a_out, b_out, w_out, x, mod, g_ffn, w_router, b_router, *, tm):
    b, s, d = x.shape
    half = a_out.shape[-1]
    idx3 = lambda bi, ti: (bi, ti, 0)
    const2 = lambda bi, ti: (0, 0)
    return pl.pallas_call(
        _outproj_kernel,
        grid=(b, s // tm),
        in_specs=[pl.BlockSpec((1, tm, half), idx3),
                  pl.BlockSpec((1, tm, half), idx3),
                  pl.BlockSpec((2 * half, d), const2),
                  pl.BlockSpec((1, tm, d), idx3),
                  pl.BlockSpec((1, N_MOD, d), lambda bi, ti: (bi, 0, 0)),
                  pl.BlockSpec((1, d), const2),
                  pl.BlockSpec((d, 2 * LANES), const2),
                  pl.BlockSpec((1, LANES), const2)],
        out_specs=[pl.BlockSpec((1, tm, d), idx3),
                   pl.BlockSpec((1, tm, d), idx3),
                   pl.BlockSpec((1, tm, LANES), idx3)],
        out_shape=[jax.ShapeDtypeStruct((b, s, d), F32),
                   jax.ShapeDtypeStruct((b, s, d), BF16),
                   jax.ShapeDtypeStruct((b, s, LANES), F32)],
        compiler_params=pltpu.CompilerParams(dimension_semantics=("parallel", "parallel"),
                                             vmem_limit_bytes=VMEM_LIMIT),
        name="out_proj_router",
    )(a_out, b_out, w_out, x, mod, g_ffn, w_router, b_router)


def _moe_kernel(h_ref, comb_ref, earlier_ref, w1_ref, w3_ref, w2_ref, x1_ref, mod_ref, o_ref, *, unit, sizes):
    g = pl.program_id(2)
    tm = h_ref.shape[1]

    @pl.when(g == 0)
    def _():
        o_ref[0] = x1_ref[0]

    comb = comb_ref[0]
    lane = lax.broadcasted_iota(jnp.int32, (1, LANES), 1)
    member = jnp.sum(jnp.where(lane == N_EXPERTS + g, comb, 0.0), axis=-1, keepdims=True)
    sub = earlier_ref.shape[0]
    by_block = sum(jnp.where(lane == n, member[n * sub:(n + 1) * sub], 0.0) for n in range(tm // sub))
    by_block_t = by_block.T
    before_col = jnp.dot(earlier_ref[...], by_block.astype(BF16), preferred_element_type=F32)
    before_row = _nt_dot(by_block_t.astype(BF16), earlier_ref[...])
    totals = jnp.sum(by_block, axis=0, keepdims=True)
    offset = jnp.zeros((1, 1), F32)
    slot_cols, slot_rows = [], []
    for n in range(tm // sub):
        col_n = jnp.sum(jnp.where(lane == n, before_col, 0.0), axis=-1, keepdims=True) + offset
        slot_cols.append(jnp.where(member[n * sub:(n + 1) * sub] > 0.0, col_n, -1.0))
        slot_rows.append(jnp.where(by_block_t[n:n + 1, :] > 0.0, before_row[n:n + 1, :] + offset, -1.0))
        offset = offset + jnp.sum(jnp.where(lane == n, totals, 0.0), axis=-1, keepdims=True)
    slot_col = jnp.concatenate(slot_cols, axis=0)
    slot_row = jnp.concatenate(slot_rows, axis=1)
    count = jnp.sum(member)
    units = jnp.int32(0)
    for c in range(tm // unit):
        units += (count > float(c * unit)).astype(jnp.int32)

    comb_hi = comb.astype(BF16)
    comb_lo = (comb - comb_hi.astype(F32)).astype(BF16)
    gate_f = mod_ref[0][5:6, :]

    def chunk(first_unit, ch):
        base = (first_unit * unit).astype(F32)
        want_col = lax.broadcasted_iota(jnp.int32, (ch, 1), 0).astype(F32) + base
        want_row = lax.broadcasted_iota(jnp.int32, (1, ch), 1).astype(F32) + base
        pick = (slot_row == want_col).astype(BF16)
        place = (slot_col == want_row).astype(BF16)
        xc = jnp.dot(pick, h_ref[0], preferred_element_type=F32).astype(BF16)
        cw = (jnp.dot(pick, comb_hi, preferred_element_type=F32)
              + jnp.dot(pick, comb_lo, preferred_element_type=F32))
        y = jnp.zeros((ch, o_ref.shape[-1]), F32)
        for e in range(EXPERTS_PER_GROUP):
            weight = jnp.sum(jnp.where(lane == g * EXPERTS_PER_GROUP + e, cw, 0.0), axis=-1, keepdims=True)
            h1 = jnp.dot(xc, w1_ref[0, e], preferred_element_type=F32)
            h3 = jnp.dot(xc, w3_ref[0, e], preferred_element_type=F32)
            hg = (h1 * jax.nn.sigmoid(h1)) * h3 * weight
            y += jnp.dot(hg.astype(BF16), w2_ref[0, e], preferred_element_type=F32)
        o_ref[0] += gate_f * jnp.dot(place, y.astype(BF16), preferred_element_type=F32)

    def step(done):
        left = units - done
        choice = sum((left > size).astype(jnp.int32) for size in sizes[:-1])
        taken = jnp.int32(sizes[0])
        for n, size in enumerate(sizes):
            pl.when(choice == n)(functools.partial(chunk, done, size * unit))
            if n > 0:
                taken += (choice >= n).astype(jnp.int32) * (size - sizes[n - 1])
        return done + taken

    lax.while_loop(lambda done: done < units, step, jnp.int32(0))


def _moe_call(h2, comb, w1, w3, w2, x1, mod, *, tm, unit, sizes):
    b, s, d = x1.shape
    f = w1.shape[-1]
    sub = min(MXU_COLS, tm)
    tok3 = lambda bi, ti, gi: (bi, ti, 0)
    return pl.pallas_call(
        functools.partial(_moe_kernel, unit=unit, sizes=sizes),
        grid=(b, s // tm, N_GROUPS),
        in_specs=[pl.BlockSpec((1, tm, d), tok3),
                  pl.BlockSpec((1, tm, LANES), tok3),
                  pl.BlockSpec((sub, sub), lambda bi, ti, gi: (0, 0)),
                  pl.BlockSpec((1, EXPERTS_PER_GROUP, d, f), lambda bi, ti, gi: (gi, 0, 0, 0)),
                  pl.BlockSpec((1, EXPERTS_PER_GROUP, d, f), lambda bi, ti, gi: (gi, 0, 0, 0)),
                  pl.BlockSpec((1, EXPERTS_PER_GROUP, f, d), lambda bi, ti, gi: (gi, 0, 0, 0)),
                  pl.BlockSpec((1, tm, d), tok3),
                  pl.BlockSpec((1, N_MOD, d), lambda bi, ti, gi: (bi, 0, 0))],
        out_specs=pl.BlockSpec((1, tm, d), tok3),
        out_shape=jax.ShapeDtypeStruct((b, s, d), F32),
        compiler_params=pltpu.CompilerParams(dimension_semantics=("parallel", "parallel", "arbitrary"),
                                             vmem_limit_bytes=VMEM_LIMIT),
        name="moe_experts",
    )(h2, comb, jnp.tri(sub, k=-1, dtype=BF16), w1, w3, w2, x1, mod)


def _w_in_column_scale(n_cols):
    sq_start = 2 * DIFF_HEADS * 2 * HEAD_DIM + DIFF_HEADS * DIFF_VDIM
    col = np.ones((1, n_cols), np.float32)
    col[0, sq_start:sq_start + SB_HEADS * HEAD_DIM] = LOG2E / math.sqrt(HEAD_DIM)
    return jnp.asarray(col)


def _pad_lanes(v, width=LANES):
    v = v.reshape(1, -1)
    return jnp.pad(v, ((0, 0), (0, width - v.shape[1])))


def kernel(x, c, w_ada, b_ada, g_attn, w_in, q_norm_g, k_norm_g, lambda_q1, lambda_k1, lambda_q2, lambda_k2,
           diff_out_g, sb_out_g, w_out, g_ffn, w_group, b_group, w_erouter, b_expert, w1, w3, w2):
    b, s, d = x.shape
    depth = w_ada.shape[0]
    tile = min(512, s)
    proj_tile = tile
    moe_tile = min(1024, s)
    moe_unit = max(moe_tile // (4 * N_GROUPS), 16)
    moe_sizes = (3, 4, 5, 6)
    sb_tile = min(256, s)
    slopes = tuple(2.0 ** (-8.0 * (n + 1) / DIFF_HEADS) for n in range(DIFF_HEADS))
    qk_scale = 1.0 / math.sqrt(HEAD_DIM)
    c_pad = jnp.pad(c, ((0, 16 - b), (0, 0)))
    qaug = np.zeros((1, LANES), np.float32)
    qaug[0, HEAD_DIM:HEAD_DIM + 2 * LOG2E_PARTS] = np.repeat(_bf16_parts(LOG2E, LOG2E_PARTS), 2)
    qaug_row = jnp.asarray(qaug)

    for layer in range(depth):
        lambda_init = 0.8 - 0.6 * math.exp(-0.3 * layer)
        mod = _mod_call(c_pad, w_ada[layer], b_ada[layer].reshape(1, -1))[:b].reshape(b, N_MOD, d)

        dq, dk, dv, sq, sk, sv = _proj_call(
            x, mod, g_attn[layer].reshape(1, d), w_in[layer], _w_in_column_scale(w_in.shape[-1]),
            jnp.tile(q_norm_g[layer] * (qk_scale * LOG2E), 2).reshape(1, LANES),
            jnp.tile(k_norm_g[layer], 2).reshape(1, LANES), qaug_row,
            tm=proj_tile, tk=tile, slopes=slopes)

        lam_p = jnp.stack([lambda_q1[layer], lambda_k1[layer], lambda_q2[layer], lambda_k2[layer]])
        score_bound = (1.02 * HEAD_DIM * qk_scale) * jnp.max(jnp.abs(q_norm_g[layer])) * jnp.max(jnp.abs(k_norm_g[layer]))
        dead_dist = (2.0 * score_bound - F32_DEAD_LOG) / jnp.asarray(slopes, F32)
        n_back = jnp.clip(jnp.floor(dead_dist / tile) + 2.0, 1.0, s // tile).astype(jnp.int32)
        attn_args = (n_back, lam_p, diff_out_g[layer].reshape(1, LANES),
                     jnp.full((1, LANES), score_bound * LOG2E, F32), dq, dk, dv,
                     jnp.tile(sb_out_g[layer], 2).reshape(1, LANES), sq, sk, sv,
                     w1[layer].reshape(-1, w1.shape[-1]), w3[layer].reshape(-1, w3.shape[-1]),
                     w2[layer].reshape(-1, w2.shape[-1]))
        attn_kw = dict(t=tile, ts=sb_tile, slopes=slopes, lambda_init=lambda_init)
        a_out, b_out, w1_bf, w3_bf, w2_bf = lax.cond(score_bound <= DIFF_FIXED_REF_MAX,
                                                     lambda args: _attn_call(*args, fixed_ref=True, **attn_kw),
                                                     lambda args: _attn_call(*args, fixed_ref=False, **attn_kw),
                                                     attn_args)

        w_router = jnp.concatenate(
            [jnp.transpose(w_erouter[layer], (1, 0, 2)).reshape(d, N_EXPERTS), w_group[layer]], axis=1)
        w_router = jnp.pad(w_router, ((0, 0), (0, LANES - w_router.shape[1])))
        w_router_hi = w_router.astype(BF16)
        w_router = jnp.concatenate([w_router_hi, (w_router - w_router_hi.astype(F32)).astype(BF16)], axis=1)
        b_router = _pad_lanes(jnp.concatenate([b_expert[layer].reshape(-1), b_group[layer]]))
        x1, h2, comb = _outproj_call(a_out, b_out, w_out[layer].astype(BF16), x, mod,
                                     g_ffn[layer].reshape(1, d), w_router, b_router, tm=proj_tile)

        x = _moe_call(h2, comb, w1_bf.reshape(w1.shape[1:]), w3_bf.reshape(w3.shape[1:]), w2_bf.reshape(w2.shape[1:]),
                      x1, mod, tm=moe_tile, unit=moe_unit, sizes=moe_sizes)
    return x
```

```python
import functools
import math

import jax
import jax.numpy as jnp
import ml_dtypes
import numpy as np
from jax import lax
from jax.experimental import pallas as pl
from jax.experimental.pallas import tpu as pltpu

HEAD_DIM = 64
DIFF_HEADS = 4
SB_HEADS = 8
DIFF_VDIM = 2 * HEAD_DIM
N_GROUPS = 4
EXPERTS_PER_GROUP = 8
N_EXPERTS = N_GROUPS * EXPERTS_PER_GROUP
N_MOD = 6
EPS = 1e-6
LANES = 128
MXU_COLS = 256
BF16_SUBLANES = 16
BF16_EXACT_INTS = 256
NEG = -1e30
DIFF_BLOCKS_PER_ITER = 8
F32_DEAD_LOG = -104.0
VMEM_LIMIT = 56 * 1024 * 1024

F32 = jnp.float32
BF16 = jnp.bfloat16

LOG2E = math.log2(math.e)
LOG2E_PARTS = 3
DIFF_FIXED_REF_MAX = 32.0


def _bf16_parts(value, n):
    parts, rest = [], value
    for _ in range(n):
        part = float(np.float32(rest).astype(ml_dtypes.bfloat16))
        parts.append(part)
        rest -= part
    return parts


def _nt_dot(a, b):
    return lax.dot_general(a, b, (((1,), (1,)), ((), ())), preferred_element_type=F32)


def _mod_kernel(c_ref, w_ref, b_ref, o_ref):
    c = c_ref[...]
    sc = c * jax.nn.sigmoid(c)
    w = w_ref[...]
    sc_hi, w_hi = sc.astype(BF16), w.astype(BF16)
    sc_lo, w_lo = (sc - sc_hi.astype(F32)).astype(BF16), (w - w_hi.astype(F32)).astype(BF16)
    o_ref[...] = (jnp.dot(sc_hi, w_hi, preferred_element_type=F32)
                  + jnp.dot(sc_lo, w_hi, preferred_element_type=F32)
                  + jnp.dot(sc_hi, w_lo, preferred_element_type=F32)) + b_ref[...]


def _mod_call(c_pad, w_ada, b_ada):
    rows, d = c_pad.shape
    n = w_ada.shape[1]
    tn = n // 4
    return pl.pallas_call(
        _mod_kernel,
        grid=(n // tn,),
        in_specs=[pl.BlockSpec((rows, d), lambda j: (0, 0)),
                  pl.BlockSpec((d, tn), lambda j: (0, j)),
                  pl.BlockSpec((1, tn), lambda j: (0, j))],
        out_specs=pl.BlockSpec((rows, tn), lambda j: (0, j)),
        out_shape=jax.ShapeDtypeStruct((rows, n), F32),
        compiler_params=pltpu.CompilerParams(dimension_semantics=("parallel",),
                                             vmem_limit_bytes=VMEM_LIMIT),
        name="adaln_mod",
    )(c_pad, w_ada, b_ada)


_PROJ_BLOCKS = 24


def _proj_kernel(x_ref, mod_ref, g_ref, w32_ref, wcol_ref, qg_ref, kg_ref, qaug_ref,
                 dq_ref, dk_ref, dv_ref, sq_ref, sk_ref, sv_ref, w_ref, *, tm, tk, slopes):
    @pl.when((pl.program_id(0) == 0) & (pl.program_id(1) == 0))
    def _():
        w_ref[...] = (w32_ref[...] * wcol_ref[...]).astype(BF16)

    x = x_ref[0]
    mod = mod_ref[0]
    shift, scale = mod[0:1, :], mod[1:2, :]
    ms = jnp.mean(x * x, axis=-1, keepdims=True)
    h = (x * lax.rsqrt(ms + EPS) * g_ref[...]) * (1.0 + scale) + shift
    hb = h.astype(BF16)

    lane = lax.broadcasted_iota(jnp.int32, (1, LANES), 1)
    aug = (lane >= HEAD_DIM) & (lane < HEAD_DIM + 2 * LOG2E_PARTS)
    lo_lane = (aug & ((lane & 1) == 0)).astype(F32)
    hi_lane = (aug & ((lane & 1) == 1)).astype(F32)
    row = lax.broadcasted_iota(jnp.int32, (tm, 1), 0) + pl.program_id(1) * tm
    koff = row & (tk - 1)
    koff_lo = (koff & (BF16_EXACT_INTS - 1)).astype(F32)
    koff_hi = (koff - (koff & (BF16_EXACT_INTS - 1))).astype(F32)
    koff_lanes = lo_lane * koff_lo + hi_lane * koff_hi
    low = lane < HEAD_DIM

    for c in range(_PROJ_BLOCKS * LANES // MXU_COLS):
        pc = jnp.dot(hb, w_ref[:, c * MXU_COLS:(c + 1) * MXU_COLS], preferred_element_type=F32)
        for half in range(MXU_COLS // LANES):
            blk = c * (MXU_COLS // LANES) + half
            piece = pc[:, half * LANES:(half + 1) * LANES]
            if blk < 8:
                sq = piece * piece
                ss_low = jnp.sum(jnp.where(low, sq, 0.0), axis=-1, keepdims=True)
                ss_high = jnp.sum(sq, axis=-1, keepdims=True) - ss_low
                inv = jnp.where(low, lax.rsqrt(ss_low * (1.0 / HEAD_DIM) + EPS),
                                lax.rsqrt(ss_high * (1.0 / HEAD_DIM) + EPS))
                head = blk % DIFF_HEADS
                if blk < DIFF_HEADS:
                    scaled, extra, out = piece * inv * qg_ref[...], qaug_ref[...], dq_ref
                else:
                    scaled, extra, out = piece * inv * kg_ref[...], koff_lanes * slopes[head], dk_ref
                out[0, 2 * head] = (jnp.where(low, scaled, 0.0) + extra).astype(BF16)
                out[0, 2 * head + 1] = (jnp.where(low, pltpu.roll(scaled, HEAD_DIM, 1), 0.0) + extra).astype(BF16)
            elif blk < 12:
                dv_ref[0, blk - 8] = piece.astype(BF16)
            elif blk < 16:
                sq_ref[0, 2 * (blk - 12)] = jnp.where(low, piece, 0.0).astype(BF16)
                sq_ref[0, 2 * (blk - 12) + 1] = jnp.where(low, 0.0, piece).astype(BF16)
            elif blk < 20:
                sk_ref[0, blk - 16] = piece.astype(BF16)
            else:
                sv_ref[0, blk - 20] = piece.astype(BF16)


def _proj_call(x, mod, g_attn, w_in, w_col, qg_pad, kg_pad, qaug_row, *, tm, tk, slopes):
    b, s, d = x.shape
    n = w_in.shape[1]
    nt = s // tm

    def hm(nh):
        return (jax.ShapeDtypeStruct((b, nh, s, LANES), BF16),
                pl.BlockSpec((1, nh, tm, LANES), lambda bi, ti: (bi, 0, ti, 0)))

    outs = [hm(8), hm(8), hm(4), hm(8), hm(4), hm(4)]
    return pl.pallas_call(
        functools.partial(_proj_kernel, tm=tm, tk=tk, slopes=slopes),
        grid=(b, nt),
        in_specs=[pl.BlockSpec((1, tm, d), lambda bi, ti: (bi, ti, 0)),
                  pl.BlockSpec((1, N_MOD, d), lambda bi, ti: (bi, 0, 0)),
                  pl.BlockSpec((1, d), lambda bi, ti: (0, 0)),
                  pl.BlockSpec((d, n), lambda bi, ti: (0, 0)),
                  pl.BlockSpec((1, n), lambda bi, ti: (0, 0)),
                  pl.BlockSpec((1, LANES), lambda bi, ti: (0, 0)),
                  pl.BlockSpec((1, LANES), lambda bi, ti: (0, 0)),
                  pl.BlockSpec((1, LANES), lambda bi, ti: (0, 0))],
        out_specs=[o[1] for o in outs],
        out_shape=[o[0] for o in outs],
        scratch_shapes=[pltpu.VMEM((d, n), BF16)],
        compiler_params=pltpu.CompilerParams(dimension_semantics=("arbitrary", "arbitrary"),
                                             vmem_limit_bytes=VMEM_LIMIT),
        name="in_proj",
    )(x, mod, g_attn, w_in, w_col, qg_pad, kg_pad, qaug_row)


def _attn_kernel(nback_ref, lam_ref, gd_ref, bound_ref, dq_ref, dk_ref, dv_ref, gs_ref, sq_ref, sk_ref, sv_ref,
                 w1_ref, w3_ref, w2_ref, a_ref, b_ref, w1b_ref, w3b_ref, w2b_ref, acc0_ref, acc1_ref, sacc_ref,
                 *, t, ts, slopes, lambda_init, fixed_ref):
    hd = pl.program_id(1)
    i = pl.program_id(2)
    sub = t // ts
    chains = [(u, hh) for u in range(sub) for hh in range(2)]

    hd_col = jnp.full((t, 1), hd, jnp.int32)
    slope = jnp.full((t, 1), slopes[-1] * LOG2E, F32)
    for n in range(len(slopes) - 2, -1, -1):
        slope = jnp.where(hd_col == n, slopes[n] * LOG2E, slope)
    dq = (dq_ref[0, 0], dq_ref[0, 1])
    accs = (acc0_ref, acc1_ref)
    causal = lax.broadcasted_iota(jnp.int32, (t, t), 1) <= lax.broadcasted_iota(jnp.int32, (t, t), 0)
    ones_col = (lax.broadcasted_iota(jnp.int32, (t, LANES), 1) == 0).astype(BF16)
    row_pos = lax.broadcasted_iota(jnp.int32, (t, 1), 0) + i * t

    def d_block(j, carry, diagonal, kill=None):
        ks = pl.multiple_of(j * t, t)
        v_aug = jnp.concatenate([dv_ref[0, 0, pl.ds(ks, t), :], ones_col], axis=1)
        new = [None, None]
        for mp in range(2):
            s = _nt_dot(dq[mp], dk_ref[0, mp, pl.ds(ks, t), :])
            if diagonal:
                s = jnp.where(causal, s, NEG)
            if kill is not None:
                s = s - kill
            if fixed_ref:
                ref = slope * (row_pos - j * t).astype(F32) + bound_ref[:, 0:1]
                pv = jnp.dot(jnp.exp2(s - ref).astype(BF16), v_aug, preferred_element_type=F32)
                accs[mp][...] = pv if diagonal else accs[mp][...] + pv
            else:
                off = slope * jnp.full((t, 1), j * t, jnp.int32).astype(F32)
                m_new = jnp.max(s, axis=-1, keepdims=True) + off
                if not diagonal:
                    m_new = jnp.maximum(carry[mp], m_new)
                pv = jnp.dot(jnp.exp2(s - (m_new - off)).astype(BF16), v_aug, preferred_element_type=F32)
                accs[mp][...] = pv if diagonal else jnp.exp2(carry[mp] - m_new) * accs[mp][...] + pv
                new[mp] = m_new
        return () if fixed_ref else tuple(new)

    srow = lax.broadcasted_iota(jnp.int32, (ts, ts), 0)
    scol = lax.broadcasted_iota(jnp.int32, (ts, ts), 1)
    past = scol < srow
    later = (srow > scol).astype(BF16)

    def s_layer(layer, carry, masked):
        new = list(carry)
        for c, (u, hh) in enumerate(chains):
            jb = sub * i + u - layer
            c_in = carry[c] + jnp.where(jnp.full((ts, 1), jb, jnp.int32) >= 0, 0.0, NEG)
            ks = pl.multiple_of(jnp.maximum(jb, 0) * ts, ts)
            k = sk_ref[0, 0, pl.ds(ks, ts), :]
            v = sv_ref[0, 0, pl.ds(ks, ts), :]
            z = _nt_dot(sq_ref[0, hh, u * ts:(u + 1) * ts, :], k)
            neg_abs = -jnp.abs(z)
            log_rem = 0.5 * (neg_abs - z) - jnp.log2(1.0 + jnp.exp2(neg_abs))
            log_beta = log_rem + z
            if masked:
                log_rem = jnp.where(past, log_rem, 0.0)
            new[c] = c_in + jnp.sum(log_rem, axis=-1, keepdims=True)
            suffix = jnp.dot(log_rem.astype(BF16), later, preferred_element_type=F32)
            a = jnp.exp2(log_beta + suffix + c_in)
            if masked:
                a = jnp.where(past, a, 0.0)
            av = jnp.dot(a.astype(BF16), v, preferred_element_type=F32)
            sacc_ref[c] = av if masked else sacc_ref[c] + av
        return tuple(new)

    no_prev = jnp.where(jnp.full((t, 1), i, jnp.int32) >= 1, 0.0, -NEG)
    prev = jnp.maximum(i - 1, 0)
    d_carry, s_carry = (), (jnp.zeros((ts, 1), F32),) * len(chains)
    d_carry = d_block(i, d_carry, True)
    d_carry = d_block(prev, d_carry, False, kill=no_prev)
    s_carry = s_layer(0, s_carry, True)
    s_carry = s_layer(1, s_carry, False)

    first = jnp.maximum(i + 1 - nback_ref[hd], 0)
    n_rest = jnp.maximum(i - 1 - first, 0)

    def bunch(size, start, bb, carry):
        for n in range(size):
            carry = d_block(start + bb * size + n, carry, False)
        return carry

    start, left, size = first, n_rest, DIFF_BLOCKS_PER_ITER
    while size >= 1:
        trips = left // size
        d_carry = lax.fori_loop(0, trips, functools.partial(bunch, size, start), d_carry)
        start, left, size = start + trips * size, left - trips * size, size // 2

    def any_live(carry):
        top = functools.reduce(jnp.maximum, carry)
        return (jnp.max(top) > LOG2E * F32_DEAD_LOG).astype(jnp.int32)

    def live_cond(state):
        return jnp.logical_and(state[0] <= sub * i + sub - 1, state[1] > 0)

    def live_body(state):
        carry = s_layer(state[0], state[2:], False)
        return (state[0] + 1, any_live(carry)) + carry

    lax.while_loop(live_cond, live_body, (jnp.int32(2), any_live(s_carry)) + s_carry)

    lp = lam_ref[...]
    lam = (jnp.exp(jnp.sum(lp[0:1] * lp[1:2], axis=-1, keepdims=True))
           - jnp.exp(jnp.sum(lp[2:3] * lp[3:4], axis=-1, keepdims=True)) + lambda_init)
    o = (acc0_ref[:, 0:LANES] / acc0_ref[:, LANES:LANES + 1]
         - lam * (acc1_ref[:, 0:LANES] / acc1_ref[:, LANES:LANES + 1]))
    ms = jnp.mean(o * o, axis=-1, keepdims=True)
    a_ref[0] = ((o * lax.rsqrt(ms + EPS) * gd_ref[...]) * (1.0 - lambda_init)).astype(BF16)

    first_half = lax.broadcasted_iota(jnp.int32, (1, LANES), 1) < HEAD_DIM
    for u in range(sub):
        o = jnp.where(first_half, sacc_ref[2 * u], sacc_ref[2 * u + 1])
        sq = o * o
        s_first = jnp.sum(jnp.where(first_half, sq, 0.0), axis=-1, keepdims=True)
        s_all = jnp.sum(sq, axis=-1, keepdims=True)
        ms = jnp.where(first_half, s_first, s_all - s_first) * (1.0 / HEAD_DIM)
        b_ref[0, u * ts:(u + 1) * ts, :] = (o * lax.rsqrt(ms + EPS) * gs_ref[...]).astype(BF16)

    w1b_ref[...] = w1_ref[...].astype(BF16)
    w3b_ref[...] = w3_ref[...].astype(BF16)
    w2b_ref[...] = w2_ref[...].astype(BF16)


def _attn_call(n_back, lam_p, g_diff, bound_row, dq, dk, dv, g_pair, sq, sk, sv, w1, w3, w2,
               *, t, ts, slopes, lambda_init, fixed_ref):
    b, _, s, _ = dq.shape
    nq = s // t
    steps = b * DIFF_HEADS * nq
    const2 = lambda bi, hi, qi: (0, 0)
    tile4 = lambda bi, hi, qi: (bi, hi, qi, 0)
    full4 = lambda bi, hi, qi: (bi, hi, 0, 0)
    step2 = lambda bi, hi, qi: ((bi * DIFF_HEADS + hi) * nq + qi, 0)
    out_spec = pl.BlockSpec((1, t, LANES), lambda bi, hi, qi: (bi, qi, hi))
    out_shape = jax.ShapeDtypeStruct((b, s, DIFF_HEADS * LANES), BF16)
    w_specs = [pl.BlockSpec((w.shape[0] // steps, w.shape[1]), step2) for w in (w1, w3, w2)]
    w_shapes = [jax.ShapeDtypeStruct(w.shape, BF16) for w in (w1, w3, w2)]
    return pl.pallas_call(
        functools.partial(_attn_kernel, t=t, ts=ts, slopes=slopes, lambda_init=lambda_init, fixed_ref=fixed_ref),
        grid=(b, DIFF_HEADS, s // t),
        in_specs=[pl.BlockSpec(memory_space=pltpu.SMEM),
                  pl.BlockSpec((4, HEAD_DIM), const2),
                  pl.BlockSpec((1, LANES), const2),
                  pl.BlockSpec((1, LANES), const2),
                  pl.BlockSpec((1, 2, t, LANES), tile4),
                  pl.BlockSpec((1, 2, s, LANES), full4),
                  pl.BlockSpec((1, 1, s, LANES), full4),
                  pl.BlockSpec((1, LANES), const2),
                  pl.BlockSpec((1, 2, t, LANES), tile4),
                  pl.BlockSpec((1, 1, s, LANES), full4),
                  pl.BlockSpec((1, 1, s, LANES), full4)] + w_specs,
        out_specs=[out_spec, out_spec] + w_specs,
        out_shape=[out_shape, out_shape] + w_shapes,
        scratch_shapes=[pltpu.VMEM((t, 2 * LANES), F32), pltpu.VMEM((t, 2 * LANES), F32),
                        pltpu.VMEM((2 * (t // ts), ts, LANES), F32)],
        compiler_params=pltpu.CompilerParams(dimension_semantics=("parallel", "parallel", "parallel"),
                                             vmem_limit_bytes=VMEM_LIMIT),
        name="attn_fixed_ref" if fixed_ref else "attn_running_max",
    )(n_back, lam_p, g_diff, bound_row, dq, dk, dv, g_pair, sq, sk, sv, w1, w3, w2)


def _outproj_kernel(a_ref, b_ref, w_ref, x_ref, mod_ref, g_ref, wr_ref, br_ref,
                    x1_ref, h2_ref, comb_ref):
    half = a_ref.shape[-1]
    y = (jnp.dot(a_ref[0], w_ref[0:half, :], preferred_element_type=F32)
         + jnp.dot(b_ref[0], w_ref[half:, :], preferred_element_type=F32))
    mod = mod_ref[0]
    gate_a, shift, scale = mod[2:3, :], mod[3:4, :], mod[4:5, :]
    x1 = x_ref[0] + gate_a * y
    x1_ref[0] = x1
    ms = jnp.mean(x1 * x1, axis=-1, keepdims=True)
    h2 = (x1 * lax.rsqrt(ms + EPS) * g_ref[...]) * (1.0 + scale) + shift
    h2_hi = h2.astype(BF16)
    h2_ref[0] = h2_hi

    h2_lo = (h2 - h2_hi.astype(F32)).astype(BF16)
    both = jnp.dot(h2_hi, wr_ref[...], preferred_element_type=F32)
    logits = (both[:, 0:LANES] + both[:, LANES:2 * LANES]
              + jnp.dot(h2_lo, wr_ref[:, 0:LANES], preferred_element_type=F32)) + br_ref[...]
    lane = lax.broadcasted_iota(jnp.int32, logits.shape, 1).astype(F32)
    big = jnp.float32(LANES)

    def top(vals):
        mx = jnp.max(vals, axis=-1, keepdims=True)
        idx = jnp.min(jnp.where(vals == mx, lane, big), axis=-1, keepdims=True)
        return mx, idx

    is_group = (lane >= N_EXPERTS) & (lane < N_EXPERTS + N_GROUPS)
    g_logits = jnp.where(is_group, logits, NEG)
    g_max, g_idx = top(g_logits)
    gate_group = 1.0 / jnp.sum(jnp.where(is_group, jnp.exp(logits - g_max), 0.0), axis=-1, keepdims=True)
    g_sel = g_idx - N_EXPERTS
    in_group = (lane >= g_sel * EXPERTS_PER_GROUP) & (lane < (g_sel + 1.0) * EXPERTS_PER_GROUP)
    e_logits = jnp.where(in_group, logits, NEG)
    v1, i1 = top(e_logits)
    v2, i2 = top(jnp.where(lane == i1, NEG, e_logits))
    e2 = jnp.exp(v2 - v1)
    w_first = 1.0 / (1.0 + e2)
    w_second = e2 / (1.0 + e2)
    comb_ref[0] = (gate_group * (jnp.where(lane == i1, w_first, 0.0) + jnp.where(lane == i2, w_second, 0.0))
                   + jnp.where(lane == g_idx, 1.0, 0.0))


def _outproj_call(a_out, b_out, w_out, x, mod, g_ffn, w_router, b_router, *, tm):
    b, s, d = x.shape
    half = a_out.shape[-1]
    idx3 = lambda bi, ti: (bi, ti, 0)
    const2 = lambda bi, ti: (0, 0)
    return pl.pallas_call(
        _outproj_kernel,
        grid=(b, s // tm),
        in_specs=[pl.BlockSpec((1, tm, half), idx3),
                  pl.BlockSpec((1, tm, half), idx3),
                  pl.BlockSpec((2 * half, d), const2),
                  pl.BlockSpec((1, tm, d), idx3),
                  pl.BlockSpec((1, N_MOD, d), lambda bi, ti: (bi, 0, 0)),
                  pl.BlockSpec((1, d), const2),
                  pl.BlockSpec((d, 2 * LANES), const2),
                  pl.BlockSpec((1, LANES), const2)],
        out_specs=[pl.BlockSpec((1, tm, d), idx3),
                   pl.BlockSpec((1, tm, d), idx3),
                   pl.BlockSpec((1, tm, LANES), idx3)],
        out_shape=[jax.ShapeDtypeStruct((b, s, d), F32),
                   jax.ShapeDtypeStruct((b, s, d), BF16),
                   jax.ShapeDtypeStruct((b, s, LANES), F32)],
        compiler_params=pltpu.CompilerParams(dimension_semantics=("parallel", "parallel"),
                                             vmem_limit_bytes=VMEM_LIMIT),
        name="out_proj_router",
    )(a_out, b_out, w_out, x, mod, g_ffn, w_router, b_router)


def _moe_kernel(h_ref, comb_ref, earlier_ref, w1_ref, w3_ref, w2_ref, x1_ref, mod_ref, o_ref, *, unit, sizes):
    g = pl.program_id(2)
    tm = h_ref.shape[1]

    @pl.when(g == 0)
    def _():
        o_ref[0] = x1_ref[0]

    comb = comb_ref[0]
    lane = lax.broadcasted_iota(jnp.int32, (1, LANES), 1)
    member = jnp.sum(jnp.where(lane == N_EXPERTS + g, comb, 0.0), axis=-1, keepdims=True)
    sub = earlier_ref.shape[0]
    by_block = sum(jnp.where(lane == n, member[n * sub:(n + 1) * sub], 0.0) for n in range(tm // sub))
    by_block_t = by_block.T
    before_col = jnp.dot(earlier_ref[...], by_block.astype(BF16), preferred_element_type=F32)
    before_row = _nt_dot(by_block_t.astype(BF16), earlier_ref[...])
    totals = jnp.sum(by_block, axis=0, keepdims=True)
    offset = jnp.zeros((1, 1), F32)
    slot_cols, slot_rows = [], []
    for n in range(tm // sub):
        col_n = jnp.sum(jnp.where(lane == n, before_col, 0.0), axis=-1, keepdims=True) + offset
        slot_cols.append(jnp.where(member[n * sub:(n + 1) * sub] > 0.0, col_n, -1.0))
        slot_rows.append(jnp.where(by_block_t[n:n + 1, :] > 0.0, before_row[n:n + 1, :] + offset, -1.0))
        offset = offset + jnp.sum(jnp.where(lane == n, totals, 0.0), axis=-1, keepdims=True)
    slot_col = jnp.concatenate(slot_cols, axis=0)
    slot_row = jnp.concatenate(slot_rows, axis=1)
    count = jnp.sum(member)
    units = jnp.int32(0)
    for c in range(tm // unit):
        units += (count > float(c * unit)).astype(jnp.int32)

    comb_hi = comb.astype(BF16)
    comb_lo = (comb - comb_hi.astype(F32)).astype(BF16)
    gate_f = mod_ref[0][5:6, :]

    def chunk(first_unit, ch):
        base = (first_unit * unit).astype(F32)
        want_col = lax.broadcasted_iota(jnp.int32, (ch, 1), 0).astype(F32) + base
        want_row = lax.broadcasted_iota(jnp.int32, (1, ch), 1).astype(F32) + base
        pick = (slot_row == want_col).astype(BF16)
        place = (slot_col == want_row).astype(BF16)
        xc = jnp.dot(pick, h_ref[0], preferred_element_type=F32).astype(BF16)
        cw = (jnp.dot(pick, comb_hi, preferred_element_type=F32)
              + jnp.dot(pick, comb_lo, preferred_element_type=F32))
        y = jnp.zeros((ch, o_ref.shape[-1]), F32)
        for e in range(EXPERTS_PER_GROUP):
            weight = jnp.sum(jnp.where(lane == g * EXPERTS_PER_GROUP + e, cw, 0.0), axis=-1, keepdims=True)
            h1 = jnp.dot(xc, w1_ref[0, e], preferred_element_type=F32)
            h3 = jnp.dot(xc, w3_ref[0, e], preferred_element_type=F32)
            hg = (h1 * jax.nn.sigmoid(h1)) * h3 * weight
            y += jnp.dot(hg.astype(BF16), w2_ref[0, e], preferred_element_type=F32)
        o_ref[0] += gate_f * jnp.dot(place, y.astype(BF16), preferred_element_type=F32)

    def step(done):
        left = units - done
        choice = sum((left > size).astype(jnp.int32) for size in sizes[:-1])
        taken = jnp.int32(sizes[0])
        for n, size in enumerate(sizes):
            pl.when(choice == n)(functools.partial(chunk, done, size * unit))
            if n > 0:
                taken += (choice >= n).astype(jnp.int32) * (size - sizes[n - 1])
        return done + taken

    lax.while_loop(lambda done: done < units, step, jnp.int32(0))


def _moe_call(h2, comb, w1, w3, w2, x1, mod, *, tm, unit, sizes):
    b, s, d = x1.shape
    f = w1.shape[-1]
    sub = min(MXU_COLS, tm)
    tok3 = lambda bi, ti, gi: (bi, ti, 0)
    return pl.pallas_call(
        functools.partial(_moe_kernel, unit=unit, sizes=sizes),
        grid=(b, s // tm, N_GROUPS),
        in_specs=[pl.BlockSpec((1, tm, d), tok3),
                  pl.BlockSpec((1, tm, LANES), tok3),
                  pl.BlockSpec((sub, sub), lambda bi, ti, gi: (0, 0)),
                  pl.BlockSpec((1, EXPERTS_PER_GROUP, d, f), lambda bi, ti, gi: (gi, 0, 0, 0)),
                  pl.BlockSpec((1, EXPERTS_PER_GROUP, d, f), lambda bi, ti, gi: (gi, 0, 0, 0)),
                  pl.BlockSpec((1, EXPERTS_PER_GROUP, f, d), lambda bi, ti, gi: (gi, 0, 0, 0)),
                  pl.BlockSpec((1, tm, d), tok3),
                  pl.BlockSpec((1, N_MOD, d), lambda bi, ti, gi: (bi, 0, 0))],
        out_specs=pl.BlockSpec((1, tm, d), tok3),
        out_shape=jax.ShapeDtypeStruct((b, s, d), F32),
        compiler_params=pltpu.CompilerParams(dimension_semantics=("parallel", "parallel", "arbitrary"),
                                             vmem_limit_bytes=VMEM_LIMIT),
        name="moe_experts",
    )(h2, comb, jnp.tri(sub, k=-1, dtype=BF16), w1, w3, w2, x1, mod)


def _w_in_column_scale(n_cols):
    sq_start = 2 * DIFF_HEADS * 2 * HEAD_DIM + DIFF_HEADS * DIFF_VDIM
    col = np.ones((1, n_cols), np.float32)
    col[0, sq_start:sq_start + SB_HEADS * HEAD_DIM] = LOG2E / math.sqrt(HEAD_DIM)
    return jnp.asarray(col)


def _pad_lanes(v, width=LANES):
    v = v.reshape(1, -1)
    return jnp.pad(v, ((0, 0), (0, width - v.shape[1])))


def kernel(x, c, w_ada, b_ada, g_attn, w_in, q_norm_g, k_norm_g, lambda_q1, lambda_k1, lambda_q2, lambda_k2,
           diff_out_g, sb_out_g, w_out, g_ffn, w_group, b_group, w_erouter, b_expert, w1, w3, w2):
    b, s, d = x.shape
    depth = w_ada.shape[0]
    tile = min(512, s)
    proj_tile = tile
    moe_tile = min(1024, s)
    moe_unit = max(moe_tile // (4 * N_GROUPS), BF16_SUBLANES)
    moe_sizes = (3, 4, 5, 6)
    sb_tile = min(256, s)
    slopes = tuple(2.0 ** (-8.0 * (n + 1) / DIFF_HEADS) for n in range(DIFF_HEADS))
    qk_scale = 1.0 / math.sqrt(HEAD_DIM)
    c_pad = jnp.pad(c, ((0, BF16_SUBLANES - b), (0, 0)))
    qaug = np.zeros((1, LANES), np.float32)
    qaug[0, HEAD_DIM:HEAD_DIM + 2 * LOG2E_PARTS] = np.repeat(_bf16_parts(LOG2E, LOG2E_PARTS), 2)
    qaug_row = jnp.asarray(qaug)

    for layer in range(depth):
        lambda_init = 0.8 - 0.6 * math.exp(-0.3 * layer)
        mod = _mod_call(c_pad, w_ada[layer], b_ada[layer].reshape(1, -1))[:b].reshape(b, N_MOD, d)

        dq, dk, dv, sq, sk, sv = _proj_call(
            x, mod, g_attn[layer].reshape(1, d), w_in[layer], _w_in_column_scale(w_in.shape[-1]),
            jnp.tile(q_norm_g[layer] * (qk_scale * LOG2E), 2).reshape(1, LANES),
            jnp.tile(k_norm_g[layer], 2).reshape(1, LANES), qaug_row,
            tm=proj_tile, tk=tile, slopes=slopes)

        lam_p = jnp.stack([lambda_q1[layer], lambda_k1[layer], lambda_q2[layer], lambda_k2[layer]])
        score_bound = (1.02 * HEAD_DIM * qk_scale) * jnp.max(jnp.abs(q_norm_g[layer])) * jnp.max(jnp.abs(k_norm_g[layer]))
        dead_dist = (2.0 * score_bound - F32_DEAD_LOG) / jnp.asarray(slopes, F32)
        n_back = jnp.clip(jnp.floor(dead_dist / tile) + 2.0, 1.0, s // tile).astype(jnp.int32)
        attn_args = (n_back, lam_p, diff_out_g[layer].reshape(1, LANES),
                     jnp.full((1, LANES), score_bound * LOG2E, F32), dq, dk, dv,
                     jnp.tile(sb_out_g[layer], 2).reshape(1, LANES), sq, sk, sv,
                     w1[layer].reshape(-1, w1.shape[-1]), w3[layer].reshape(-1, w3.shape[-1]),
                     w2[layer].reshape(-1, w2.shape[-1]))
        attn_kw = dict(t=tile, ts=sb_tile, slopes=slopes, lambda_init=lambda_init)
        a_out, b_out, w1_bf, w3_bf, w2_bf = lax.cond(score_bound <= DIFF_FIXED_REF_MAX,
                                                     lambda args: _attn_call(*args, fixed_ref=True, **attn_kw),
                                                     lambda args: _attn_call(*args, fixed_ref=False, **attn_kw),
                                                     attn_args)

        w_router = jnp.concatenate(
            [jnp.transpose(w_erouter[layer], (1, 0, 2)).reshape(d, N_EXPERTS), w_group[layer]], axis=1)
        w_router = jnp.pad(w_router, ((0, 0), (0, LANES - w_router.shape[1])))
        w_router_hi = w_router.astype(BF16)
        w_router = jnp.concatenate([w_router_hi, (w_router - w_router_hi.astype(F32)).astype(BF16)], axis=1)
        b_router = _pad_lanes(jnp.concatenate([b_expert[layer].reshape(-1), b_group[layer]]))
        x1, h2, comb = _outproj_call(a_out, b_out, w_out[layer].astype(BF16), x, mod,
                                     g_ffn[layer].reshape(1, d), w_router, b_router, tm=proj_tile)

        x = _moe_call(h2, comb, w1_bf.reshape(w1.shape[1:]), w3_bf.reshape(w3.shape[1:]), w2_bf.reshape(w2.shape[1:]),
                      x1, mod, tm=moe_tile, unit=moe_unit, sizes=moe_sizes)
    return x
```

```python
import functools
import math

import jax
import jax.numpy as jnp
import ml_dtypes
import numpy as np
from jax import lax
from jax.experimental import pallas as pl
from jax.experimental.pallas import tpu as pltpu

HEAD_DIM = 64
DIFF_HEADS = 4
SB_HEADS = 8
DIFF_VDIM = 2 * HEAD_DIM
N_GROUPS = 4
EXPERTS_PER_GROUP = 8
N_EXPERTS = N_GROUPS * EXPERTS_PER_GROUP
N_MOD = 6
EPS = 1e-6
LANES = 128
MXU_COLS = 256
BF16_SUBLANES = 16
BF16_EXACT_INTS = 256
NEG = -1e30
DIFF_BLOCKS_PER_ITER = 8
F32_DEAD_LOG = -104.0
VMEM_LIMIT = 56 * 1024 * 1024

F32 = jnp.float32
BF16 = jnp.bfloat16

LOG2E = math.log2(math.e)
LOG2E_PARTS = 3
DIFF_FIXED_REF_MAX = 32.0


def _bf16_parts(value, n):
    parts, rest = [], value
    for _ in range(n):
        part = float(np.float32(rest).astype(ml_dtypes.bfloat16))
        parts.append(part)
        rest -= part
    return parts


def _nt_dot(a, b):
    return lax.dot_general(a, b, (((1,), (1,)), ((), ())), preferred_element_type=F32)


def _mod_kernel(c_ref, w_ref, b_ref, o_ref):
    c = c_ref[...]
    sc = c * jax.nn.sigmoid(c)
    w = w_ref[...]
    sc_hi, w_hi = sc.astype(BF16), w.astype(BF16)
    sc_lo, w_lo = (sc - sc_hi.astype(F32)).astype(BF16), (w - w_hi.astype(F32)).astype(BF16)
    o_ref[...] = (jnp.dot(sc_hi, w_hi, preferred_element_type=F32)
                  + jnp.dot(sc_lo, w_hi, preferred_element_type=F32)
                  + jnp.dot(sc_hi, w_lo, preferred_element_type=F32)) + b_ref[...]


def _mod_call(c_pad, w_ada, b_ada):
    rows, d = c_pad.shape
    n = w_ada.shape[1]
    tn = n // 4
    return pl.pallas_call(
        _mod_kernel,
        grid=(n // tn,),
        in_specs=[pl.BlockSpec((rows, d), lambda j: (0, 0)),
                  pl.BlockSpec((d, tn), lambda j: (0, j)),
                  pl.BlockSpec((1, tn), lambda j: (0, j))],
        out_specs=pl.BlockSpec((rows, tn), lambda j: (0, j)),
        out_shape=jax.ShapeDtypeStruct((rows, n), F32),
        compiler_params=pltpu.CompilerParams(dimension_semantics=("parallel",),
                                             vmem_limit_bytes=VMEM_LIMIT),
        name="adaln_mod",
    )(c_pad, w_ada, b_ada)


_PROJ_BLOCKS = 24


def _proj_kernel(x_ref, mod_ref, g_ref, w32_ref, wcol_ref, qg_ref, kg_ref, qaug_ref,
                 dq_ref, dk_ref, dv_ref, sq_ref, sk_ref, sv_ref, w_ref, *, tm, tk, slopes):
    @pl.when((pl.program_id(0) == 0) & (pl.program_id(1) == 0))
    def _():
        w_ref[...] = (w32_ref[...] * wcol_ref[...]).astype(BF16)

    x = x_ref[0]
    mod = mod_ref[0]
    shift, scale = mod[0:1, :], mod[1:2, :]
    ms = jnp.mean(x * x, axis=-1, keepdims=True)
    h = (x * lax.rsqrt(ms + EPS) * g_ref[...]) * (1.0 + scale) + shift
    hb = h.astype(BF16)

    lane = lax.broadcasted_iota(jnp.int32, (1, LANES), 1)
    aug = (lane >= HEAD_DIM) & (lane < HEAD_DIM + 2 * LOG2E_PARTS)
    lo_lane = (aug & ((lane & 1) == 0)).astype(F32)
    hi_lane = (aug & ((lane & 1) == 1)).astype(F32)
    row = lax.broadcasted_iota(jnp.int32, (tm, 1), 0) + pl.program_id(1) * tm
    koff = row & (tk - 1)
    koff_lo = (koff & (BF16_EXACT_INTS - 1)).astype(F32)
    koff_hi = (koff - (koff & (BF16_EXACT_INTS - 1))).astype(F32)
    koff_lanes = lo_lane * koff_lo + hi_lane * koff_hi
    low = lane < HEAD_DIM

    for c in range(_PROJ_BLOCKS * LANES // MXU_COLS):
        pc = jnp.dot(hb, w_ref[:, c * MXU_COLS:(c + 1) * MXU_COLS], preferred_element_type=F32)
        for half in range(MXU_COLS // LANES):
            blk = c * (MXU_COLS // LANES) + half
            piece = pc[:, half * LANES:(half + 1) * LANES]
            if blk < 8:
                sq = piece * piece
                ss_low = jnp.sum(jnp.where(low, sq, 0.0), axis=-1, keepdims=True)
                ss_high = jnp.sum(sq, axis=-1, keepdims=True) - ss_low
                inv = jnp.where(low, lax.rsqrt(ss_low * (1.0 / HEAD_DIM) + EPS),
                                lax.rsqrt(ss_high * (1.0 / HEAD_DIM) + EPS))
                head = blk % DIFF_HEADS
                if blk < DIFF_HEADS:
                    scaled, extra, out = piece * inv * qg_ref[...], qaug_ref[...], dq_ref
                else:
                    scaled, extra, out = piece * inv * kg_ref[...], koff_lanes * slopes[head], dk_ref
                out[0, 2 * head] = (jnp.where(low, scaled, 0.0) + extra).astype(BF16)
                out[0, 2 * head + 1] = (jnp.where(low, pltpu.roll(scaled, HEAD_DIM, 1), 0.0) + extra).astype(BF16)
            elif blk < 12:
                dv_ref[0, blk - 8] = piece.astype(BF16)
            elif blk < 16:
                sq_ref[0, 2 * (blk - 12)] = jnp.where(low, piece, 0.0).astype(BF16)
                sq_ref[0, 2 * (blk - 12) + 1] = jnp.where(low, 0.0, piece).astype(BF16)
            elif blk < 20:
                sk_ref[0, blk - 16] = piece.astype(BF16)
            else:
                sv_ref[0, blk - 20] = piece.astype(BF16)


def _proj_call(x, mod, g_attn, w_in, w_col, qg_pad, kg_pad, qaug_row, *, tm, tk, slopes):
    b, s, d = x.shape
    n = w_in.shape[1]
    nt = s // tm

    def hm(nh):
        return (jax.ShapeDtypeStruct((b, nh, s, LANES), BF16),
                pl.BlockSpec((1, nh, tm, LANES), lambda bi, ti: (bi, 0, ti, 0)))

    outs = [hm(8), hm(8), hm(4), hm(8), hm(4), hm(4)]
    return pl.pallas_call(
        functools.partial(_proj_kernel, tm=tm, tk=tk, slopes=slopes),
        grid=(b, nt),
        in_specs=[pl.BlockSpec((1, tm, d), lambda bi, ti: (bi, ti, 0)),
                  pl.BlockSpec((1, N_MOD, d), lambda bi, ti: (bi, 0, 0)),
                  pl.BlockSpec((1, d), lambda bi, ti: (0, 0)),
                  pl.BlockSpec((d, n), lambda bi, ti: (0, 0)),
                  pl.BlockSpec((1, n), lambda bi, ti: (0, 0)),
                  pl.BlockSpec((1, LANES), lambda bi, ti: (0, 0)),
                  pl.BlockSpec((1, LANES), lambda bi, ti: (0, 0)),
                  pl.BlockSpec((1, LANES), lambda bi, ti: (0, 0))],
        out_specs=[o[1] for o in outs],
        out_shape=[o[0] for o in outs],
        scratch_shapes=[pltpu.VMEM((d, n), BF16)],
        compiler_params=pltpu.CompilerParams(dimension_semantics=("arbitrary", "arbitrary"),
                                             vmem_limit_bytes=VMEM_LIMIT),
        name="in_proj",
    )(x, mod, g_attn, w_in, w_col, qg_pad, kg_pad, qaug_row)


def _attn_kernel(nback_ref, lam_ref, gd_ref, bound_ref, dq_ref, dk_ref, dv_ref, gs_ref, sq_ref, sk_ref, sv_ref,
                 w1_ref, w3_ref, w2_ref, a_ref, b_ref, w1b_ref, w3b_ref, w2b_ref, acc0_ref, acc1_ref, sacc_ref,
                 *, t, ts, slopes, lambda_init, fixed_ref):
    hd = pl.program_id(1)
    i = pl.program_id(2)
    sub = t // ts
    chains = [(u, hh) for u in range(sub) for hh in range(2)]

    hd_col = jnp.full((t, 1), hd, jnp.int32)
    slope = jnp.full((t, 1), slopes[-1] * LOG2E, F32)
    for n in range(len(slopes) - 2, -1, -1):
        slope = jnp.where(hd_col == n, slopes[n] * LOG2E, slope)
    dq = (dq_ref[0, 0], dq_ref[0, 1])
    accs = (acc0_ref, acc1_ref)
    causal = lax.broadcasted_iota(jnp.int32, (t, t), 1) <= lax.broadcasted_iota(jnp.int32, (t, t), 0)
    ones_col = (lax.broadcasted_iota(jnp.int32, (t, LANES), 1) == 0).astype(BF16)
    row_pos = lax.broadcasted_iota(jnp.int32, (t, 1), 0) + i * t

    def d_block(j, carry, diagonal, kill=None):
        ks = pl.multiple_of(j * t, t)
        v_aug = jnp.concatenate([dv_ref[0, 0, pl.ds(ks, t), :], ones_col], axis=1)
        new = [None, None]
        for mp in range(2):
            s = _nt_dot(dq[mp], dk_ref[0, mp, pl.ds(ks, t), :])
            if diagonal:
                s = jnp.where(causal, s, NEG)
            if kill is not None:
                s = s - kill
            if fixed_ref:
                ref = slope * (row_pos - j * t).astype(F32) + bound_ref[:, 0:1]
                pv = jnp.dot(jnp.exp2(s - ref).astype(BF16), v_aug, preferred_element_type=F32)
                accs[mp][...] = pv if diagonal else accs[mp][...] + pv
            else:
                off = slope * jnp.full((t, 1), j * t, jnp.int32).astype(F32)
                m_new = jnp.max(s, axis=-1, keepdims=True) + off
                if not diagonal:
                    m_new = jnp.maximum(carry[mp], m_new)
                pv = jnp.dot(jnp.exp2(s - (m_new - off)).astype(BF16), v_aug, preferred_element_type=F32)
                accs[mp][...] = pv if diagonal else jnp.exp2(carry[mp] - m_new) * accs[mp][...] + pv
                new[mp] = m_new
        return () if fixed_ref else tuple(new)

    srow = lax.broadcasted_iota(jnp.int32, (ts, ts), 0)
    scol = lax.broadcasted_iota(jnp.int32, (ts, ts), 1)
    past = scol < srow
    later = (srow > scol).astype(BF16)

    def s_layer(layer, carry, masked):
        new = list(carry)
        for c, (u, hh) in enumerate(chains):
            jb = sub * i + u - layer
            c_in = carry[c] + jnp.where(jnp.full((ts, 1), jb, jnp.int32) >= 0, 0.0, NEG)
            ks = pl.multiple_of(jnp.maximum(jb, 0) * ts, ts)
            k = sk_ref[0, 0, pl.ds(ks, ts), :]
            v = sv_ref[0, 0, pl.ds(ks, ts), :]
            z = _nt_dot(sq_ref[0, hh, u * ts:(u + 1) * ts, :], k)
            neg_z = -z
            log_rem = jnp.minimum(neg_z, 0.0) - jnp.log2(1.0 + jnp.exp2(jnp.minimum(z, neg_z)))
            log_beta = log_rem + z
            if masked:
                log_rem = jnp.where(past, log_rem, 0.0)
            new[c] = c_in + jnp.sum(log_rem, axis=-1, keepdims=True)
            suffix = jnp.dot(log_rem.astype(BF16), later, preferred_element_type=F32)
            a = jnp.exp2(log_beta + suffix + c_in)
            if masked:
                a = jnp.where(past, a, 0.0)
            av = jnp.dot(a.astype(BF16), v, preferred_element_type=F32)
            sacc_ref[c] = av if masked else sacc_ref[c] + av
        return tuple(new)

    no_prev = jnp.where(jnp.full((t, 1), i, jnp.int32) >= 1, 0.0, -NEG)
    prev = jnp.maximum(i - 1, 0)
    d_carry, s_carry = (), (jnp.zeros((ts, 1), F32),) * len(chains)
    d_carry = d_block(i, d_carry, True)
    d_carry = d_block(prev, d_carry, False, kill=no_prev)
    s_carry = s_layer(0, s_carry, True)
    s_carry = s_layer(1, s_carry, False)

    first = jnp.maximum(i + 1 - nback_ref[hd], 0)
    n_rest = jnp.maximum(i - 1 - first, 0)

    def bunch(size, start, bb, carry):
        for n in range(size):
            carry = d_block(start + bb * size + n, carry, False)
        return carry

    start, left, size = first, n_rest, DIFF_BLOCKS_PER_ITER
    while size >= 1:
        trips = left // size
        d_carry = lax.fori_loop(0, trips, functools.partial(bunch, size, start), d_carry)
        start, left, size = start + trips * size, left - trips * size, size // 2

    def any_live(carry):
        top = functools.reduce(jnp.maximum, carry)
        return (jnp.max(top) > LOG2E * F32_DEAD_LOG).astype(jnp.int32)

    def live_cond(state):
        return jnp.logical_and(state[0] <= sub * i + sub - 1, state[1] > 0)

    def live_body(state):
        carry = s_layer(state[0], state[2:], False)
        return (state[0] + 1, any_live(carry)) + carry

    lax.while_loop(live_cond, live_body, (jnp.int32(2), any_live(s_carry)) + s_carry)

    lp = lam_ref[...]
    lam = (jnp.exp(jnp.sum(lp[0:1] * lp[1:2], axis=-1, keepdims=True))
           - jnp.exp(jnp.sum(lp[2:3] * lp[3:4], axis=-1, keepdims=True)) + lambda_init)
    o = (acc0_ref[:, 0:LANES] / acc0_ref[:, LANES:LANES + 1]
         - lam * (acc1_ref[:, 0:LANES] / acc1_ref[:, LANES:LANES + 1]))
    ms = jnp.mean(o * o, axis=-1, keepdims=True)
    a_ref[0] = ((o * lax.rsqrt(ms + EPS) * gd_ref[...]) * (1.0 - lambda_init)).astype(BF16)

    first_half = lax.broadcasted_iota(jnp.int32, (1, LANES), 1) < HEAD_DIM
    for u in range(sub):
        o = jnp.where(first_half, sacc_ref[2 * u], sacc_ref[2 * u + 1])
        sq = o * o
        s_first = jnp.sum(jnp.where(first_half, sq, 0.0), axis=-1, keepdims=True)
        s_all = jnp.sum(sq, axis=-1, keepdims=True)
        ms = jnp.where(first_half, s_first, s_all - s_first) * (1.0 / HEAD_DIM)
        b_ref[0, u * ts:(u + 1) * ts, :] = (o * lax.rsqrt(ms + EPS) * gs_ref[...]).astype(BF16)

    w1b_ref[...] = w1_ref[...].astype(BF16)
    w3b_ref[...] = w3_ref[...].astype(BF16)
    w2b_ref[...] = w2_ref[...].astype(BF16)


def _attn_call(n_back, lam_p, g_diff, bound_row, dq, dk, dv, g_pair, sq, sk, sv, w1, w3, w2,
               *, t, ts, slopes, lambda_init, fixed_ref):
    b, _, s, _ = dq.shape
    nq = s // t
    steps = b * DIFF_HEADS * nq
    const2 = lambda bi, hi, qi: (0, 0)
    tile4 = lambda bi, hi, qi: (bi, hi, qi, 0)
    full4 = lambda bi, hi, qi: (bi, hi, 0, 0)
    step2 = lambda bi, hi, qi: ((bi * DIFF_HEADS + hi) * nq + qi, 0)
    out_spec = pl.BlockSpec((1, t, LANES), lambda bi, hi, qi: (bi, qi, hi))
    out_shape = jax.ShapeDtypeStruct((b, s, DIFF_HEADS * LANES), BF16)
    w_specs = [pl.BlockSpec((w.shape[0] // steps, w.shape[1]), step2) for w in (w1, w3, w2)]
    w_shapes = [jax.ShapeDtypeStruct(w.shape, BF16) for w in (w1, w3, w2)]
    return pl.pallas_call(
        functools.partial(_attn_kernel, t=t, ts=ts, slopes=slopes, lambda_init=lambda_init, fixed_ref=fixed_ref),
        grid=(b, DIFF_HEADS, s // t),
        in_specs=[pl.BlockSpec(memory_space=pltpu.SMEM),
                  pl.BlockSpec((4, HEAD_DIM), const2),
                  pl.BlockSpec((1, LANES), const2),
                  pl.BlockSpec((1, LANES), const2),
                  pl.BlockSpec((1, 2, t, LANES), tile4),
                  pl.BlockSpec((1, 2, s, LANES), full4),
                  pl.BlockSpec((1, 1, s, LANES), full4),
                  pl.BlockSpec((1, LANES), const2),
                  pl.BlockSpec((1, 2, t, LANES), tile4),
                  pl.BlockSpec((1, 1, s, LANES), full4),
                  pl.BlockSpec((1, 1, s, LANES), full4)] + w_specs,
        out_specs=[out_spec, out_spec] + w_specs,
        out_shape=[out_shape, out_shape] + w_shapes,
        scratch_shapes=[pltpu.VMEM((t, 2 * LANES), F32), pltpu.VMEM((t, 2 * LANES), F32),
                        pltpu.VMEM((2 * (t // ts), ts, LANES), F32)],
        compiler_params=pltpu.CompilerParams(dimension_semantics=("parallel", "parallel", "parallel"),
                                             vmem_limit_bytes=VMEM_LIMIT),
        name="attn_fixed_ref" if fixed_ref else "attn_running_max",
    )(n_back, lam_p, g_diff, bound_row, dq, dk, dv, g_pair, sq, sk, sv, w1, w3, w2)


def _outproj_kernel(a_ref, b_ref, w_ref, x_ref, mod_ref, g_ref, wr_ref, br_ref,
                    x1_ref, h2_ref, comb_ref):
    half = a_ref.shape[-1]
    y = (jnp.dot(a_ref[0], w_ref[0:half, :], preferred_element_type=F32)
         + jnp.dot(b_ref[0], w_ref[half:, :], preferred_element_type=F32))
    mod = mod_ref[0]
    gate_a, shift, scale = mod[2:3, :], mod[3:4, :], mod[4:5, :]
    x1 = x_ref[0] + gate_a * y
    x1_ref[0] = x1
    ms = jnp.mean(x1 * x1, axis=-1, keepdims=True)
    h2 = (x1 * lax.rsqrt(ms + EPS) * g_ref[...]) * (1.0 + scale) + shift
    h2_hi = h2.astype(BF16)
    h2_ref[0] = h2_hi

    h2_lo = (h2 - h2_hi.astype(F32)).astype(BF16)
    both = jnp.dot(h2_hi, wr_ref[...], preferred_element_type=F32)
    logits = (both[:, 0:LANES] + both[:, LANES:2 * LANES]
              + jnp.dot(h2_lo, wr_ref[:, 0:LANES], preferred_element_type=F32)) + br_ref[...]
    lane = lax.broadcasted_iota(jnp.int32, logits.shape, 1).astype(F32)
    big = jnp.float32(LANES)

    def top(vals):
        mx = jnp.max(vals, axis=-1, keepdims=True)
        idx = jnp.min(jnp.where(vals == mx, lane, big), axis=-1, keepdims=True)
        return mx, idx

    is_group = (lane >= N_EXPERTS) & (lane < N_EXPERTS + N_GROUPS)
    g_logits = jnp.where(is_group, logits, NEG)
    g_max, g_idx = top(g_logits)
    gate_group = 1.0 / jnp.sum(jnp.where(is_group, jnp.exp(logits - g_max), 0.0), axis=-1, keepdims=True)
    g_sel = g_idx - N_EXPERTS
    in_group = (lane >= g_sel * EXPERTS_PER_GROUP) & (lane < (g_sel + 1.0) * EXPERTS_PER_GROUP)
    e_logits = jnp.where(in_group, logits, NEG)
    v1, i1 = top(e_logits)
    v2, i2 = top(jnp.where(lane == i1, NEG, e_logits))
    e2 = jnp.exp(v2 - v1)
    w_first = 1.0 / (1.0 + e2)
    w_second = e2 / (1.0 + e2)
    comb_ref[0] = (gate_group * (jnp.where(lane == i1, w_first, 0.0) + jnp.where(lane == i2, w_second, 0.0))
                   + jnp.where(lane == g_idx, 1.0, 0.0))


def _outproj_call(a_out, b_out, w_out, x, mod, g_ffn, w_router, b_router, *, tm):
    b, s, d = x.shape
    half = a_out.shape[-1]
    idx3 = lambda bi, ti: (bi, ti, 0)
    const2 = lambda bi, ti: (0, 0)
    return pl.pallas_call(
        _outproj_kernel,
        grid=(b, s // tm),
        in_specs=[pl.BlockSpec((1, tm, half), idx3),
                  pl.BlockSpec((1, tm, half), idx3),
                  pl.BlockSpec((2 * half, d), const2),
                  pl.BlockSpec((1, tm, d), idx3),
                  pl.BlockSpec((1, N_MOD, d), lambda bi, ti: (bi, 0, 0)),
                  pl.BlockSpec((1, d), const2),
                  pl.BlockSpec((d, 2 * LANES), const2),
                  pl.BlockSpec((1, LANES), const2)],
        out_specs=[pl.BlockSpec((1, tm, d), idx3),
                   pl.BlockSpec((1, tm, d), idx3),
                   pl.BlockSpec((1, tm, LANES), idx3)],
        out_shape=[jax.ShapeDtypeStruct((b, s, d), F32),
                   jax.ShapeDtypeStruct((b, s, d), BF16),
                   jax.ShapeDtypeStruct((b, s, LANES), F32)],
        compiler_params=pltpu.CompilerParams(dimension_semantics=("parallel", "parallel"),
                                             vmem_limit_bytes=VMEM_LIMIT),
        name="out_proj_router",
    )(a_out, b_out, w_out, x, mod, g_ffn, w_router, b_router)


def _moe_kernel(h_ref, comb_ref, earlier_ref, w1_ref, w3_ref, w2_ref, x1_ref, mod_ref, o_ref, *, unit, sizes):
    g = pl.program_id(2)
    tm = h_ref.shape[1]

    @pl.when(g == 0)
    def _():
        o_ref[0] = x1_ref[0]

    comb = comb_ref[0]
    lane = lax.broadcasted_iota(jnp.int32, (1, LANES), 1)
    member = jnp.sum(jnp.where(lane == N_EXPERTS + g, comb, 0.0), axis=-1, keepdims=True)
    sub = earlier_ref.shape[0]
    by_block = sum(jnp.where(lane == n, member[n * sub:(n + 1) * sub], 0.0) for n in range(tm // sub))
    by_block_t = by_block.T
    before_col = jnp.dot(earlier_ref[...], by_block.astype(BF16), preferred_element_type=F32)
    before_row = _nt_dot(by_block_t.astype(BF16), earlier_ref[...])
    totals = jnp.sum(by_block, axis=0, keepdims=True)
    offset = jnp.zeros((1, 1), F32)
    slot_cols, slot_rows = [], []
    for n in range(tm // sub):
        col_n = jnp.sum(jnp.where(lane == n, before_col, 0.0), axis=-1, keepdims=True) + offset
        slot_cols.append(jnp.where(member[n * sub:(n + 1) * sub] > 0.0, col_n, -1.0))
        slot_rows.append(jnp.where(by_block_t[n:n + 1, :] > 0.0, before_row[n:n + 1, :] + offset, -1.0))
        offset = offset + jnp.sum(jnp.where(lane == n, totals, 0.0), axis=-1, keepdims=True)
    slot_col = jnp.concatenate(slot_cols, axis=0)
    slot_row = jnp.concatenate(slot_rows, axis=1)
    count = jnp.sum(member)
    units = jnp.int32(0)
    for c in range(tm // unit):
        units += (count > float(c * unit)).astype(jnp.int32)

    comb_hi = comb.astype(BF16)
    comb_lo = (comb - comb_hi.astype(F32)).astype(BF16)
    gate_f = mod_ref[0][5:6, :]

    def chunk(first_unit, ch):
        base = (first_unit * unit).astype(F32)
        want_col = lax.broadcasted_iota(jnp.int32, (ch, 1), 0).astype(F32) + base
        want_row = lax.broadcasted_iota(jnp.int32, (1, ch), 1).astype(F32) + base
        pick = (slot_row == want_col).astype(BF16)
        place = (slot_col == want_row).astype(BF16)
        xc = jnp.dot(pick, h_ref[0], preferred_element_type=F32).astype(BF16)
        cw = (jnp.dot(pick, comb_hi, preferred_element_type=F32)
              + jnp.dot(pick, comb_lo, preferred_element_type=F32))
        y = jnp.zeros((ch, o_ref.shape[-1]), F32)
        for e in range(EXPERTS_PER_GROUP):
            weight = jnp.sum(jnp.where(lane == g * EXPERTS_PER_GROUP + e, cw, 0.0), axis=-1, keepdims=True)
            h1 = jnp.dot(xc, w1_ref[0, e], preferred_element_type=F32)
            h3 = jnp.dot(xc, w3_ref[0, e], preferred_element_type=F32)
            hg = (h1 * jax.nn.sigmoid(h1)) * h3 * weight
            y += jnp.dot(hg.astype(BF16), w2_ref[0, e], preferred_element_type=F32)
        o_ref[0] += gate_f * jnp.dot(place, y.astype(BF16), preferred_element_type=F32)

    def step(done):
        left = units - done
        choice = sum((left > size).astype(jnp.int32) for size in sizes[:-1])
        taken = jnp.int32(sizes[0])
        for n, size in enumerate(sizes):
            pl.when(choice == n)(functools.partial(chunk, done, size * unit))
            if n > 0:
                taken += (choice >= n).astype(jnp.int32) * (size - sizes[n - 1])
        return done + taken

    lax.while_loop(lambda done: done < units, step, jnp.int32(0))


def _moe_call(h2, comb, w1, w3, w2, x1, mod, *, tm, unit, sizes):
    b, s, d = x1.shape
    f = w1.shape[-1]
    sub = min(MXU_COLS, tm)
    tok3 = lambda bi, ti, gi: (bi, ti, 0)
    return pl.pallas_call(
        functools.partial(_moe_kernel, unit=unit, sizes=sizes),
        grid=(b, s // tm, N_GROUPS),
        in_specs=[pl.BlockSpec((1, tm, d), tok3),
                  pl.BlockSpec((1, tm, LANES), tok3),
                  pl.BlockSpec((sub, sub), lambda bi, ti, gi: (0, 0)),
                  pl.BlockSpec((1, EXPERTS_PER_GROUP, d, f), lambda bi, ti, gi: (gi, 0, 0, 0)),
                  pl.BlockSpec((1, EXPERTS_PER_GROUP, d, f), lambda bi, ti, gi: (gi, 0, 0, 0)),
                  pl.BlockSpec((1, EXPERTS_PER_GROUP, f, d), lambda bi, ti, gi: (gi, 0, 0, 0)),
                  pl.BlockSpec((1, tm, d), tok3),
                  pl.BlockSpec((1, N_MOD, d), lambda bi, ti, gi: (bi, 0, 0))],
        out_specs=pl.BlockSpec((1, tm, d), tok3),
        out_shape=jax.ShapeDtypeStruct((b, s, d), F32),
        compiler_params=pltpu.CompilerParams(dimension_semantics=("parallel", "parallel", "arbitrary"),
                                             vmem_limit_bytes=VMEM_LIMIT),
        name="moe_experts",
    )(h2, comb, jnp.tri(sub, k=-1, dtype=BF16), w1, w3, w2, x1, mod)


def _w_in_column_scale(n_cols):
    sq_start = 2 * DIFF_HEADS * 2 * HEAD_DIM + DIFF_HEADS * DIFF_VDIM
    col = np.ones((1, n_cols), np.float32)
    col[0, sq_start:sq_start + SB_HEADS * HEAD_DIM] = LOG2E / math.sqrt(HEAD_DIM)
    return jnp.asarray(col)


def _pad_lanes(v, width=LANES):
    v = v.reshape(1, -1)
    return jnp.pad(v, ((0, 0), (0, width - v.shape[1])))


def kernel(x, c, w_ada, b_ada, g_attn, w_in, q_norm_g, k_norm_g, lambda_q1, lambda_k1, lambda_q2, lambda_k2,
           diff_out_g, sb_out_g, w_out, g_ffn, w_group, b_group, w_erouter, b_expert, w1, w3, w2):
    b, s, d = x.shape
    depth = w_ada.shape[0]
    tile = min(512, s)
    proj_tile = tile
    moe_tile = min(1024, s)
    moe_unit = max(moe_tile // (4 * N_GROUPS), BF16_SUBLANES)
    moe_sizes = (3, 4, 5, 6)
    sb_tile = min(256, s)
    slopes = tuple(2.0 ** (-8.0 * (n + 1) / DIFF_HEADS) for n in range(DIFF_HEADS))
    qk_scale = 1.0 / math.sqrt(HEAD_DIM)
    c_pad = jnp.pad(c, ((0, BF16_SUBLANES - b), (0, 0)))
    qaug = np.zeros((1, LANES), np.float32)
    qaug[0, HEAD_DIM:HEAD_DIM + 2 * LOG2E_PARTS] = np.repeat(_bf16_parts(LOG2E, LOG2E_PARTS), 2)
    qaug_row = jnp.asarray(qaug)

    for layer in range(depth):
        lambda_init = 0.8 - 0.6 * math.exp(-0.3 * layer)
        mod = _mod_call(c_pad, w_ada[layer], b_ada[layer].reshape(1, -1))[:b].reshape(b, N_MOD, d)

        dq, dk, dv, sq, sk, sv = _proj_call(
            x, mod, g_attn[layer].reshape(1, d), w_in[layer], _w_in_column_scale(w_in.shape[-1]),
            jnp.tile(q_norm_g[layer] * (qk_scale * LOG2E), 2).reshape(1, LANES),
            jnp.tile(k_norm_g[layer], 2).reshape(1, LANES), qaug_row,
            tm=proj_tile, tk=tile, slopes=slopes)

        lam_p = jnp.stack([lambda_q1[layer], lambda_k1[layer], lambda_q2[layer], lambda_k2[layer]])
        score_bound = (1.02 * HEAD_DIM * qk_scale) * jnp.max(jnp.abs(q_norm_g[layer])) * jnp.max(jnp.abs(k_norm_g[layer]))
        dead_dist = (2.0 * score_bound - F32_DEAD_LOG) / jnp.asarray(slopes, F32)
        n_back = jnp.clip(jnp.floor(dead_dist / tile) + 2.0, 1.0, s // tile).astype(jnp.int32)
        attn_args = (n_back, lam_p, diff_out_g[layer].reshape(1, LANES),
                     jnp.full((1, LANES), score_bound * LOG2E, F32), dq, dk, dv,
                     jnp.tile(sb_out_g[layer], 2).reshape(1, LANES), sq, sk, sv,
                     w1[layer].reshape(-1, w1.shape[-1]), w3[layer].reshape(-1, w3.shape[-1]),
                     w2[layer].reshape(-1, w2.shape[-1]))
        attn_kw = dict(t=tile, ts=sb_tile, slopes=slopes, lambda_init=lambda_init)
        a_out, b_out, w1_bf, w3_bf, w2_bf = lax.cond(score_bound <= DIFF_FIXED_REF_MAX,
                                                     lambda args: _attn_call(*args, fixed_ref=True, **attn_kw),
                                                     lambda args: _attn_call(*args, fixed_ref=False, **attn_kw),
                                                     attn_args)

        w_router = jnp.concatenate(
            [jnp.transpose(w_erouter[layer], (1, 0, 2)).reshape(d, N_EXPERTS), w_group[layer]], axis=1)
        w_router = jnp.pad(w_router, ((0, 0), (0, LANES - w_router.shape[1])))
        w_router_hi = w_router.astype(BF16)
        w_router = jnp.concatenate([w_router_hi, (w_router - w_router_hi.astype(F32)).astype(BF16)], axis=1)
        b_router = _pad_lanes(jnp.concatenate([b_expert[layer].reshape(-1), b_group[layer]]))
        x1, h2, comb = _outproj_call(a_out, b_out, w_out[layer].astype(BF16), x, mod,
                                     g_ffn[layer].reshape(1, d), w_router, b_router, tm=proj_tile)

        x = _moe_call(h2, comb, w1_bf.reshape(w1.shape[1:]), w3_bf.reshape(w3.shape[1:]), w2_bf.reshape(w2.shape[1:]),
                      x1, mod, tm=moe_tile, unit=moe_unit, sizes=moe_sizes)
    return x
```

```python
import functools
import math

import jax
import jax.numpy as jnp
import ml_dtypes
import numpy as np
from jax import lax
from jax.experimental import pallas as pl
from jax.experimental.pallas import tpu as pltpu

HEAD_DIM = 64
DIFF_HEADS = 4
SB_HEADS = 8
DIFF_VDIM = 2 * HEAD_DIM
N_GROUPS = 4
EXPERTS_PER_GROUP = 8
N_EXPERTS = N_GROUPS * EXPERTS_PER_GROUP
N_MOD = 6
EPS = 1e-6
LANES = 128
MXU_COLS = 256
BF16_SUBLANES = 16
BF16_EXACT_INTS = 256
NEG = -1e30
DIFF_BLOCKS_PER_ITER = 8
F32_DEAD_LOG = -104.0
VMEM_LIMIT = 56 * 1024 * 1024

F32 = jnp.float32
BF16 = jnp.bfloat16

LOG2E = math.log2(math.e)
LOG2E_PARTS = 3
DIFF_FIXED_REF_MAX = 32.0


def _bf16_parts(value, n):
    parts, rest = [], value
    for _ in range(n):
        part = float(np.float32(rest).astype(ml_dtypes.bfloat16))
        parts.append(part)
        rest -= part
    return parts


def _nt_dot(a, b):
    return lax.dot_general(a, b, (((1,), (1,)), ((), ())), preferred_element_type=F32)


def _mod_kernel(c_ref, w_ref, b_ref, o_ref):
    c = c_ref[...]
    sc = c * jax.nn.sigmoid(c)
    w = w_ref[...]
    sc_hi, w_hi = sc.astype(BF16), w.astype(BF16)
    sc_lo, w_lo = (sc - sc_hi.astype(F32)).astype(BF16), (w - w_hi.astype(F32)).astype(BF16)
    o_ref[...] = (jnp.dot(sc_hi, w_hi, preferred_element_type=F32)
                  + jnp.dot(sc_lo, w_hi, preferred_element_type=F32)
                  + jnp.dot(sc_hi, w_lo, preferred_element_type=F32)) + b_ref[...]


def _mod_call(c_pad, w_ada, b_ada):
    rows, d = c_pad.shape
    n = w_ada.shape[1]
    tn = n // 4
    return pl.pallas_call(
        _mod_kernel,
        grid=(n // tn,),
        in_specs=[pl.BlockSpec((rows, d), lambda j: (0, 0)),
                  pl.BlockSpec((d, tn), lambda j: (0, j)),
                  pl.BlockSpec((1, tn), lambda j: (0, j))],
        out_specs=pl.BlockSpec((rows, tn), lambda j: (0, j)),
        out_shape=jax.ShapeDtypeStruct((rows, n), F32),
        compiler_params=pltpu.CompilerParams(dimension_semantics=("parallel",),
                                             vmem_limit_bytes=VMEM_LIMIT),
        name="adaln_mod",
    )(c_pad, w_ada, b_ada)


_PROJ_BLOCKS = 24


def _proj_kernel(x_ref, mod_ref, g_ref, w32_ref, wcol_ref, qg_ref, kg_ref, qaug_ref,
                 dq_ref, dk_ref, dv_ref, sq_ref, sk_ref, sv_ref, w_ref, *, tm, tk, slopes):
    @pl.when((pl.program_id(0) == 0) & (pl.program_id(1) == 0))
    def _():
        w_ref[...] = (w32_ref[...] * wcol_ref[...]).astype(BF16)

    x = x_ref[0]
    mod = mod_ref[0]
    shift, scale = mod[0:1, :], mod[1:2, :]
    ms = jnp.mean(x * x, axis=-1, keepdims=True)
    h = (x * lax.rsqrt(ms + EPS) * g_ref[...]) * (1.0 + scale) + shift
    hb = h.astype(BF16)

    lane = lax.broadcasted_iota(jnp.int32, (1, LANES), 1)
    aug = (lane >= HEAD_DIM) & (lane < HEAD_DIM + 2 * LOG2E_PARTS)
    lo_lane = (aug & ((lane & 1) == 0)).astype(F32)
    hi_lane = (aug & ((lane & 1) == 1)).astype(F32)
    row = lax.broadcasted_iota(jnp.int32, (tm, 1), 0) + pl.program_id(1) * tm
    koff = row & (tk - 1)
    koff_lo = (koff & (BF16_EXACT_INTS - 1)).astype(F32)
    koff_hi = (koff - (koff & (BF16_EXACT_INTS - 1))).astype(F32)
    koff_lanes = lo_lane * koff_lo + hi_lane * koff_hi
    low = lane < HEAD_DIM

    for c in range(_PROJ_BLOCKS * LANES // MXU_COLS):
        pc = jnp.dot(hb, w_ref[:, c * MXU_COLS:(c + 1) * MXU_COLS], preferred_element_type=F32)
        for half in range(MXU_COLS // LANES):
            blk = c * (MXU_COLS // LANES) + half
            piece = pc[:, half * LANES:(half + 1) * LANES]
            if blk < 8:
                sq = piece * piece
                ss_low = jnp.sum(jnp.where(low, sq, 0.0), axis=-1, keepdims=True)
                ss_high = jnp.sum(sq, axis=-1, keepdims=True) - ss_low
                inv = jnp.where(low, lax.rsqrt(ss_low * (1.0 / HEAD_DIM) + EPS),
                                lax.rsqrt(ss_high * (1.0 / HEAD_DIM) + EPS))
                head = blk % DIFF_HEADS
                if blk < DIFF_HEADS:
                    scaled, extra, out = piece * inv * qg_ref[...], qaug_ref[...], dq_ref
                else:
                    scaled, extra, out = piece * inv * kg_ref[...], koff_lanes * slopes[head], dk_ref
                out[0, 2 * head] = (jnp.where(low, scaled, 0.0) + extra).astype(BF16)
                out[0, 2 * head + 1] = (jnp.where(low, pltpu.roll(scaled, HEAD_DIM, 1), 0.0) + extra).astype(BF16)
            elif blk < 12:
                dv_ref[0, blk - 8] = piece.astype(BF16)
            elif blk < 16:
                sq_ref[0, 2 * (blk - 12)] = jnp.where(low, piece, 0.0).astype(BF16)
                sq_ref[0, 2 * (blk - 12) + 1] = jnp.where(low, 0.0, piece).astype(BF16)
            elif blk < 20:
                sk_ref[0, blk - 16] = piece.astype(BF16)
            else:
                sv_ref[0, blk - 20] = piece.astype(BF16)


def _proj_call(x, mod, g_attn, w_in, w_col, qg_pad, kg_pad, qaug_row, *, tm, tk, slopes):
    b, s, d = x.shape
    n = w_in.shape[1]
    nt = s // tm

    def hm(nh):
        return (jax.ShapeDtypeStruct((b, nh, s, LANES), BF16),
                pl.BlockSpec((1, nh, tm, LANES), lambda bi, ti: (bi, 0, ti, 0)))

    outs = [hm(8), hm(8), hm(4), hm(8), hm(4), hm(4)]
    return pl.pallas_call(
        functools.partial(_proj_kernel, tm=tm, tk=tk, slopes=slopes),
        grid=(b, nt),
        in_specs=[pl.BlockSpec((1, tm, d), lambda bi, ti: (bi, ti, 0)),
                  pl.BlockSpec((1, N_MOD, d), lambda bi, ti: (bi, 0, 0)),
                  pl.BlockSpec((1, d), lambda bi, ti: (0, 0)),
                  pl.BlockSpec((d, n), lambda bi, ti: (0, 0)),
                  pl.BlockSpec((1, n), lambda bi, ti: (0, 0)),
                  pl.BlockSpec((1, LANES), lambda bi, ti: (0, 0)),
                  pl.BlockSpec((1, LANES), lambda bi, ti: (0, 0)),
                  pl.BlockSpec((1, LANES), lambda bi, ti: (0, 0))],
        out_specs=[o[1] for o in outs],
        out_shape=[o[0] for o in outs],
        scratch_shapes=[pltpu.VMEM((d, n), BF16)],
        compiler_params=pltpu.CompilerParams(dimension_semantics=("arbitrary", "arbitrary"),
                                             vmem_limit_bytes=VMEM_LIMIT),
        name="in_proj",
    )(x, mod, g_attn, w_in, w_col, qg_pad, kg_pad, qaug_row)


def _attn_kernel(nback_ref, lam_ref, gd_ref, bound_ref, dq_ref, dk_ref, dv_ref, gs_ref, sq_ref, sk_ref, sv_ref,
                 w1_ref, w3_ref, w2_ref, a_ref, b_ref, w1b_ref, w3b_ref, w2b_ref, acc0_ref, acc1_ref, sacc_ref,
                 *, t, ts, slopes, lambda_init, fixed_ref):
    hd = pl.program_id(1)
    i = pl.program_id(2)
    sub = t // ts
    chains = [(u, hh) for u in range(sub) for hh in range(2)]

    hd_col = jnp.full((t, 1), hd, jnp.int32)
    slope = jnp.full((t, 1), slopes[-1] * LOG2E, F32)
    for n in range(len(slopes) - 2, -1, -1):
        slope = jnp.where(hd_col == n, slopes[n] * LOG2E, slope)
    dq = (dq_ref[0, 0], dq_ref[0, 1])
    accs = (acc0_ref, acc1_ref)
    causal = lax.broadcasted_iota(jnp.int32, (t, t), 1) <= lax.broadcasted_iota(jnp.int32, (t, t), 0)
    ones_col = (lax.broadcasted_iota(jnp.int32, (t, LANES), 1) == 0).astype(BF16)
    row_pos = lax.broadcasted_iota(jnp.int32, (t, 1), 0) + i * t

    def d_block(j, carry, diagonal, kill=None):
        ks = pl.multiple_of(j * t, t)
        v_aug = jnp.concatenate([dv_ref[0, 0, pl.ds(ks, t), :], ones_col], axis=1)
        new = [None, None]
        for mp in range(2):
            s = _nt_dot(dq[mp], dk_ref[0, mp, pl.ds(ks, t), :])
            if diagonal:
                s = jnp.where(causal, s, NEG)
            if kill is not None:
                s = s - kill
            if fixed_ref:
                ref = slope * (row_pos - j * t).astype(F32) + bound_ref[:, 0:1]
                pv = jnp.dot(jnp.exp2(s - ref).astype(BF16), v_aug, preferred_element_type=F32)
                accs[mp][...] = pv if diagonal else accs[mp][...] + pv
            else:
                off = slope * jnp.full((t, 1), j * t, jnp.int32).astype(F32)
                m_new = jnp.max(s, axis=-1, keepdims=True) + off
                if not diagonal:
                    m_new = jnp.maximum(carry[mp], m_new)
                pv = jnp.dot(jnp.exp2(s - (m_new - off)).astype(BF16), v_aug, preferred_element_type=F32)
                accs[mp][...] = pv if diagonal else jnp.exp2(carry[mp] - m_new) * accs[mp][...] + pv
                new[mp] = m_new
        return () if fixed_ref else tuple(new)

    srow = lax.broadcasted_iota(jnp.int32, (ts, ts), 0)
    scol = lax.broadcasted_iota(jnp.int32, (ts, ts), 1)
    past = scol < srow
    later = (srow > scol).astype(BF16)

    def s_layer(layer, carry, masked):
        new = list(carry)
        for c, (u, hh) in enumerate(chains):
            jb = sub * i + u - layer
            c_in = carry[c] + jnp.where(jnp.full((ts, 1), jb, jnp.int32) >= 0, 0.0, NEG)
            ks = pl.multiple_of(jnp.maximum(jb, 0) * ts, ts)
            k = sk_ref[0, 0, pl.ds(ks, ts), :]
            v = sv_ref[0, 0, pl.ds(ks, ts), :]
            z = _nt_dot(sq_ref[0, hh, u * ts:(u + 1) * ts, :], k)
            neg_z = -z
            log_rem = jnp.minimum(neg_z, 0.0) - jnp.log2(1.0 + jnp.exp2(jnp.minimum(z, neg_z)))
            log_beta = log_rem + z
            if masked:
                log_rem = jnp.where(past, log_rem, 0.0)
            new[c] = c_in + jnp.sum(log_rem, axis=-1, keepdims=True)
            suffix = jnp.dot(log_rem.astype(BF16), later, preferred_element_type=F32)
            a = jnp.exp2(log_beta + suffix + c_in)
            if masked:
                a = jnp.where(past, a, 0.0)
            av = jnp.dot(a.astype(BF16), v, preferred_element_type=F32)
            sacc_ref[c] = av if masked else sacc_ref[c] + av
        return tuple(new)

    no_prev = jnp.where(jnp.full((t, 1), i, jnp.int32) >= 1, 0.0, -NEG)
    prev = jnp.maximum(i - 1, 0)
    d_carry, s_carry = (), (jnp.zeros((ts, 1), F32),) * len(chains)
    d_carry = d_block(i, d_carry, True)
    d_carry = d_block(prev, d_carry, False, kill=no_prev)
    s_carry = s_layer(0, s_carry, True)
    s_carry = s_layer(1, s_carry, False)

    first = jnp.maximum(i + 1 - nback_ref[hd], 0)
    n_rest = jnp.maximum(i - 1 - first, 0)

    def bunch(size, start, bb, carry):
        for n in range(size):
            carry = d_block(start + bb * size + n, carry, False)
        return carry

    start, left, size = first, n_rest, DIFF_BLOCKS_PER_ITER
    while size >= 1:
        trips = left // size
        d_carry = lax.fori_loop(0, trips, functools.partial(bunch, size, start), d_carry)
        start, left, size = start + trips * size, left - trips * size, size // 2

    def any_live(carry):
        top = functools.reduce(jnp.maximum, carry)
        return (jnp.max(top) > LOG2E * F32_DEAD_LOG).astype(jnp.int32)

    def live_cond(state):
        return jnp.logical_and(state[0] <= sub * i + sub - 1, state[1] > 0)

    def live_body(state):
        carry = s_layer(state[0], state[2:], False)
        return (state[0] + 1, any_live(carry)) + carry

    lax.while_loop(live_cond, live_body, (jnp.int32(2), any_live(s_carry)) + s_carry)

    lp = lam_ref[...]
    lam = (jnp.exp(jnp.sum(lp[0:1] * lp[1:2], axis=-1, keepdims=True))
           - jnp.exp(jnp.sum(lp[2:3] * lp[3:4], axis=-1, keepdims=True)) + lambda_init)
    o = (acc0_ref[:, 0:LANES] / acc0_ref[:, LANES:LANES + 1]
         - lam * (acc1_ref[:, 0:LANES] / acc1_ref[:, LANES:LANES + 1]))
    ms = jnp.mean(o * o, axis=-1, keepdims=True)
    a_ref[0] = ((o * lax.rsqrt(ms + EPS) * gd_ref[...]) * (1.0 - lambda_init)).astype(BF16)

    first_half = lax.broadcasted_iota(jnp.int32, (1, LANES), 1) < HEAD_DIM
    for u in range(sub):
        o = jnp.where(first_half, sacc_ref[2 * u], sacc_ref[2 * u + 1])
        sq = o * o
        s_first = jnp.sum(jnp.where(first_half, sq, 0.0), axis=-1, keepdims=True)
        s_all = jnp.sum(sq, axis=-1, keepdims=True)
        ms = jnp.where(first_half, s_first, s_all - s_first) * (1.0 / HEAD_DIM)
        b_ref[0, u * ts:(u + 1) * ts, :] = (o * lax.rsqrt(ms + EPS) * gs_ref[...]).astype(BF16)

    w1b_ref[...] = w1_ref[...].astype(BF16)
    w3b_ref[...] = w3_ref[...].astype(BF16)
    w2b_ref[...] = w2_ref[...].astype(BF16)


def _attn_call(n_back, lam_p, g_diff, bound_row, dq, dk, dv, g_pair, sq, sk, sv, w1, w3, w2,
               *, t, ts, slopes, lambda_init, fixed_ref):
    b, _, s, _ = dq.shape
    nq = s // t
    steps = b * DIFF_HEADS * nq
    const2 = lambda bi, hi, qi: (0, 0)
    tile4 = lambda bi, hi, qi: (bi, hi, qi, 0)
    full4 = lambda bi, hi, qi: (bi, hi, 0, 0)
    step2 = lambda bi, hi, qi: ((bi * DIFF_HEADS + hi) * nq + qi, 0)
    out_spec = pl.BlockSpec((1, t, LANES), lambda bi, hi, qi: (bi, qi, hi))
    out_shape = jax.ShapeDtypeStruct((b, s, DIFF_HEADS * LANES), BF16)
    w_specs = [pl.BlockSpec((w.shape[0] // steps, w.shape[1]), step2) for w in (w1, w3, w2)]
    w_shapes = [jax.ShapeDtypeStruct(w.shape, BF16) for w in (w1, w3, w2)]
    return pl.pallas_call(
        functools.partial(_attn_kernel, t=t, ts=ts, slopes=slopes, lambda_init=lambda_init, fixed_ref=fixed_ref),
        grid=(b, DIFF_HEADS, s // t),
        in_specs=[pl.BlockSpec(memory_space=pltpu.SMEM),
                  pl.BlockSpec((4, HEAD_DIM), const2),
                  pl.BlockSpec((1, LANES), const2),
                  pl.BlockSpec((1, LANES), const2),
                  pl.BlockSpec((1, 2, t, LANES), tile4),
                  pl.BlockSpec((1, 2, s, LANES), full4),
                  pl.BlockSpec((1, 1, s, LANES), full4),
                  pl.BlockSpec((1, LANES), const2),
                  pl.BlockSpec((1, 2, t, LANES), tile4),
                  pl.BlockSpec((1, 1, s, LANES), full4),
                  pl.BlockSpec((1, 1, s, LANES), full4)] + w_specs,
        out_specs=[out_spec, out_spec] + w_specs,
        out_shape=[out_shape, out_shape] + w_shapes,
        scratch_shapes=[pltpu.VMEM((t, 2 * LANES), F32), pltpu.VMEM((t, 2 * LANES), F32),
                        pltpu.VMEM((2 * (t // ts), ts, LANES), F32)],
        compiler_params=pltpu.CompilerParams(dimension_semantics=("parallel", "parallel", "parallel"),
                                             vmem_limit_bytes=VMEM_LIMIT),
        name="attn_fixed_ref" if fixed_ref else "attn_running_max",
    )(n_back, lam_p, g_diff, bound_row, dq, dk, dv, g_pair, sq, sk, sv, w1, w3, w2)


def _outproj_kernel(a_ref, b_ref, w_ref, x_ref, mod_ref, g_ref, wr_ref, br_ref,
                    x1_ref, h2_ref, comb_ref):
    half = a_ref.shape[-1]
    y = (jnp.dot(a_ref[0], w_ref[0:half, :], preferred_element_type=F32)
         + jnp.dot(b_ref[0], w_ref[half:, :], preferred_element_type=F32))
    mod = mod_ref[0]
    gate_a, shift, scale = mod[2:3, :], mod[3:4, :], mod[4:5, :]
    x1 = x_ref[0] + gate_a * y
    x1_ref[0] = x1
    ms = jnp.mean(x1 * x1, axis=-1, keepdims=True)
    h2 = (x1 * lax.rsqrt(ms + EPS) * g_ref[...]) * (1.0 + scale) + shift
    h2_hi = h2.astype(BF16)
    h2_ref[0] = h2_hi

    h2_lo = (h2 - h2_hi.astype(F32)).astype(BF16)
    both = jnp.dot(h2_hi, wr_ref[...], preferred_element_type=F32)
    logits = (both[:, 0:LANES] + both[:, LANES:2 * LANES]
              + jnp.dot(h2_lo, wr_ref[:, 0:LANES], preferred_element_type=F32)) + br_ref[...]
    lane = lax.broadcasted_iota(jnp.int32, logits.shape, 1).astype(F32)
    big = jnp.float32(LANES)

    def top(vals):
        mx = jnp.max(vals, axis=-1, keepdims=True)
        idx = jnp.min(jnp.where(vals == mx, lane, big), axis=-1, keepdims=True)
        return mx, idx

    is_group = (lane >= N_EXPERTS) & (lane < N_EXPERTS + N_GROUPS)
    g_logits = jnp.where(is_group, logits, NEG)
    g_max, g_idx = top(g_logits)
    gate_group = 1.0 / jnp.sum(jnp.where(is_group, jnp.exp(logits - g_max), 0.0), axis=-1, keepdims=True)
    g_sel = g_idx - N_EXPERTS
    in_group = (lane >= g_sel * EXPERTS_PER_GROUP) & (lane < (g_sel + 1.0) * EXPERTS_PER_GROUP)
    e_logits = jnp.where(in_group, logits, NEG)
    v1, i1 = top(e_logits)
    v2, i2 = top(jnp.where(lane == i1, NEG, e_logits))
    e2 = jnp.exp(v2 - v1)
    w_first = 1.0 / (1.0 + e2)
    w_second = e2 / (1.0 + e2)
    comb_ref[0] = (gate_group * (jnp.where(lane == i1, w_first, 0.0) + jnp.where(lane == i2, w_second, 0.0))
                   + jnp.where(lane == g_idx, 1.0, 0.0))


def _outproj_call(a_out, b_out, w_out, x, mod, g_ffn, w_router, b_router, *, tm):
    b, s, d = x.shape
    half = a_out.shape[-1]
    idx3 = lambda bi, ti: (bi, ti, 0)
    const2 = lambda bi, ti: (0, 0)
    return pl.pallas_call(
        _outproj_kernel,
        grid=(b, s // tm),
        in_specs=[pl.BlockSpec((1, tm, half), idx3),
                  pl.BlockSpec((1, tm, half), idx3),
                  pl.BlockSpec((2 * half, d), const2),
                  pl.BlockSpec((1, tm, d), idx3),
                  pl.BlockSpec((1, N_MOD, d), lambda bi, ti: (bi, 0, 0)),
                  pl.BlockSpec((1, d), const2),
                  pl.BlockSpec((d, 2 * LANES), const2),
                  pl.BlockSpec((1, LANES), const2)],
        out_specs=[pl.BlockSpec((1, tm, d), idx3),
                   pl.BlockSpec((1, tm, d), idx3),
                   pl.BlockSpec((1, tm, LANES), idx3)],
        out_shape=[jax.ShapeDtypeStruct((b, s, d), F32),
                   jax.ShapeDtypeStruct((b, s, d), BF16),
                   jax.ShapeDtypeStruct((b, s, LANES), F32)],
        compiler_params=pltpu.CompilerParams(dimension_semantics=("parallel", "parallel"),
                                             vmem_limit_bytes=VMEM_LIMIT),
        name="out_proj_router",
    )(a_out, b_out, w_out, x, mod, g_ffn, w_router, b_router)


def _moe_kernel(h_ref, comb_ref, earlier_ref, w1_ref, w3_ref, w2_ref, x1_ref, mod_ref, o_ref, *, unit, sizes):
    g = pl.program_id(2)
    tm = h_ref.shape[1]

    @pl.when(g == 0)
    def _():
        o_ref[0] = x1_ref[0]

    comb = comb_ref[0]
    lane = lax.broadcasted_iota(jnp.int32, (1, LANES), 1)
    member = jnp.sum(jnp.where(lane == N_EXPERTS + g, comb, 0.0), axis=-1, keepdims=True)
    sub = earlier_ref.shape[0]
    by_block = sum(jnp.where(lane == n, member[n * sub:(n + 1) * sub], 0.0) for n in range(tm // sub))
    by_block_t = by_block.T
    before_col = jnp.dot(earlier_ref[...], by_block.astype(BF16), preferred_element_type=F32)
    before_row = _nt_dot(by_block_t.astype(BF16), earlier_ref[...])
    totals = jnp.sum(by_block, axis=0, keepdims=True)
    offset = jnp.zeros((1, 1), F32)
    slot_cols, slot_rows = [], []
    for n in range(tm // sub):
        col_n = jnp.sum(jnp.where(lane == n, before_col, 0.0), axis=-1, keepdims=True) + offset
        slot_cols.append(jnp.where(member[n * sub:(n + 1) * sub] > 0.0, col_n, -1.0))
        slot_rows.append(jnp.where(by_block_t[n:n + 1, :] > 0.0, before_row[n:n + 1, :] + offset, -1.0))
        offset = offset + jnp.sum(jnp.where(lane == n, totals, 0.0), axis=-1, keepdims=True)
    slot_col = jnp.concatenate(slot_cols, axis=0)
    slot_row = jnp.concatenate(slot_rows, axis=1)
    count = jnp.sum(member)
    units = jnp.int32(0)
    for c in range(tm // unit):
        units += (count > float(c * unit)).astype(jnp.int32)

    comb_hi = comb.astype(BF16)
    comb_lo = (comb - comb_hi.astype(F32)).astype(BF16)
    gate_f = mod_ref[0][5:6, :]

    def chunk(first_unit, ch):
        base = (first_unit * unit).astype(F32)
        want_col = lax.broadcasted_iota(jnp.int32, (ch, 1), 0).astype(F32) + base
        want_row = lax.broadcasted_iota(jnp.int32, (1, ch), 1).astype(F32) + base
        pick = (slot_row == want_col).astype(BF16)
        place = (slot_col == want_row).astype(BF16)
        xc = jnp.dot(pick, h_ref[0], preferred_element_type=F32).astype(BF16)
        cw = (jnp.dot(pick, comb_hi, preferred_element_type=F32)
              + jnp.dot(pick, comb_lo, preferred_element_type=F32))
        y = jnp.zeros((ch, o_ref.shape[-1]), F32)
        for e in range(EXPERTS_PER_GROUP):
            weight = jnp.sum(jnp.where(lane == g * EXPERTS_PER_GROUP + e, cw, 0.0), axis=-1, keepdims=True)
            h1 = jnp.dot(xc, w1_ref[0, e], preferred_element_type=F32)
            h3 = jnp.dot(xc, w3_ref[0, e], preferred_element_type=F32)
            hg = (h1 * jax.nn.sigmoid(h1)) * h3 * weight
            y += jnp.dot(hg.astype(BF16), w2_ref[0, e], preferred_element_type=F32)
        o_ref[0] += gate_f * jnp.dot(place, y.astype(BF16), preferred_element_type=F32)

    def step(done):
        left = units - done
        choice = sum((left > size).astype(jnp.int32) for size in sizes[:-1])
        taken = jnp.int32(sizes[0])
        for n, size in enumerate(sizes):
            pl.when(choice == n)(functools.partial(chunk, done, size * unit))
            if n > 0:
                taken += (choice >= n).astype(jnp.int32) * (size - sizes[n - 1])
        return done + taken

    lax.while_loop(lambda done: done < units, step, jnp.int32(0))


def _moe_call(h2, comb, w1, w3, w2, x1, mod, *, tm, unit, sizes):
    b, s, d = x1.shape
    f = w1.shape[-1]
    sub = min(MXU_COLS, tm)
    tok3 = lambda bi, ti, gi: (bi, ti, 0)
    return pl.pallas_call(
        functools.partial(_moe_kernel, unit=unit, sizes=sizes),
        grid=(b, s // tm, N_GROUPS),
        in_specs=[pl.BlockSpec((1, tm, d), tok3),
                  pl.BlockSpec((1, tm, LANES), tok3),
                  pl.BlockSpec((sub, sub), lambda bi, ti, gi: (0, 0)),
                  pl.BlockSpec((1, EXPERTS_PER_GROUP, d, f), lambda bi, ti, gi: (gi, 0, 0, 0)),
                  pl.BlockSpec((1, EXPERTS_PER_GROUP, d, f), lambda bi, ti, gi: (gi, 0, 0, 0)),
                  pl.BlockSpec((1, EXPERTS_PER_GROUP, f, d), lambda bi, ti, gi: (gi, 0, 0, 0)),
                  pl.BlockSpec((1, tm, d), tok3),
                  pl.BlockSpec((1, N_MOD, d), lambda bi, ti, gi: (bi, 0, 0))],
        out_specs=pl.BlockSpec((1, tm, d), tok3),
        out_shape=jax.ShapeDtypeStruct((b, s, d), F32),
        compiler_params=pltpu.CompilerParams(dimension_semantics=("parallel", "parallel", "arbitrary"),
                                             vmem_limit_bytes=VMEM_LIMIT),
        name="moe_experts",
    )(h2, comb, jnp.tri(sub, k=-1, dtype=BF16), w1, w3, w2, x1, mod)


def _w_in_column_scale(n_cols):
    sq_start = 2 * DIFF_HEADS * 2 * HEAD_DIM + DIFF_HEADS * DIFF_VDIM
    col = np.ones((1, n_cols), np.float32)
    col[0, sq_start:sq_start + SB_HEADS * HEAD_DIM] = LOG2E / math.sqrt(HEAD_DIM)
    return jnp.asarray(col)


def _pad_lanes(v, width=LANES):
    v = v.reshape(1, -1)
    return jnp.pad(v, ((0, 0), (0, width - v.shape[1])))


def kernel(x, c, w_ada, b_ada, g_attn, w_in, q_norm_g, k_norm_g, lambda_q1, lambda_k1, lambda_q2, lambda_k2,
           diff_out_g, sb_out_g, w_out, g_ffn, w_group, b_group, w_erouter, b_expert, w1, w3, w2):
    b, s, d = x.shape
    depth = w_ada.shape[0]
    tile = min(512, s)
    proj_tile = tile
    out_tile = min(1024, s)
    moe_tile = min(1024, s)
    moe_unit = max(moe_tile // (4 * N_GROUPS), BF16_SUBLANES)
    moe_sizes = (3, 4, 5, 6)
    sb_tile = min(256, s)
    slopes = tuple(2.0 ** (-8.0 * (n + 1) / DIFF_HEADS) for n in range(DIFF_HEADS))
    qk_scale = 1.0 / math.sqrt(HEAD_DIM)
    c_pad = jnp.pad(c, ((0, BF16_SUBLANES - b), (0, 0)))
    qaug = np.zeros((1, LANES), np.float32)
    qaug[0, HEAD_DIM:HEAD_DIM + 2 * LOG2E_PARTS] = np.repeat(_bf16_parts(LOG2E, LOG2E_PARTS), 2)
    qaug_row = jnp.asarray(qaug)

    for layer in range(depth):
        lambda_init = 0.8 - 0.6 * math.exp(-0.3 * layer)
        mod = _mod_call(c_pad, w_ada[layer], b_ada[layer].reshape(1, -1))[:b].reshape(b, N_MOD, d)

        dq, dk, dv, sq, sk, sv = _proj_call(
            x, mod, g_attn[layer].reshape(1, d), w_in[layer], _w_in_column_scale(w_in.shape[-1]),
            jnp.tile(q_norm_g[layer] * (qk_scale * LOG2E), 2).reshape(1, LANES),
            jnp.tile(k_norm_g[layer], 2).reshape(1, LANES), qaug_row,
            tm=proj_tile, tk=tile, slopes=slopes)

        lam_p = jnp.stack([lambda_q1[layer], lambda_k1[layer], lambda_q2[layer], lambda_k2[layer]])
        score_bound = (1.02 * HEAD_DIM * qk_scale) * jnp.max(jnp.abs(q_norm_g[layer])) * jnp.max(jnp.abs(k_norm_g[layer]))
        dead_dist = (2.0 * score_bound - F32_DEAD_LOG) / jnp.asarray(slopes, F32)
        n_back = jnp.clip(jnp.floor(dead_dist / tile) + 2.0, 1.0, s // tile).astype(jnp.int32)
        attn_args = (n_back, lam_p, diff_out_g[layer].reshape(1, LANES),
                     jnp.full((1, LANES), score_bound * LOG2E, F32), dq, dk, dv,
                     jnp.tile(sb_out_g[layer], 2).reshape(1, LANES), sq, sk, sv,
                     w1[layer].reshape(-1, w1.shape[-1]), w3[layer].reshape(-1, w3.shape[-1]),
                     w2[layer].reshape(-1, w2.shape[-1]))
        attn_kw = dict(t=tile, ts=sb_tile, slopes=slopes, lambda_init=lambda_init)
        a_out, b_out, w1_bf, w3_bf, w2_bf = lax.cond(score_bound <= DIFF_FIXED_REF_MAX,
                                                     lambda args: _attn_call(*args, fixed_ref=True, **attn_kw),
                                                     lambda args: _attn_call(*args, fixed_ref=False, **attn_kw),
                                                     attn_args)

        w_router = jnp.concatenate(
            [jnp.transpose(w_erouter[layer], (1, 0, 2)).reshape(d, N_EXPERTS), w_group[layer]], axis=1)
        w_router = jnp.pad(w_router, ((0, 0), (0, LANES - w_router.shape[1])))
        w_router_hi = w_router.astype(BF16)
        w_router = jnp.concatenate([w_router_hi, (w_router - w_router_hi.astype(F32)).astype(BF16)], axis=1)
        b_router = _pad_lanes(jnp.concatenate([b_expert[layer].reshape(-1), b_group[layer]]))
        x1, h2, comb = _outproj_call(a_out, b_out, w_out[layer].astype(BF16), x, mod,
                                     g_ffn[layer].reshape(1, d), w_router, b_router, tm=out_tile)

        x = _moe_call(h2, comb, w1_bf.reshape(w1.shape[1:]), w3_bf.reshape(w3.shape[1:]), w2_bf.reshape(w2.shape[1:]),
                      x1, mod, tm=moe_tile, unit=moe_unit, sizes=moe_sizes)
    return x
```

```python
import functools
import math

import jax
import jax.numpy as jnp
import ml_dtypes
import numpy as np
from jax import lax
from jax.experimental import pallas as pl
from jax.experimental.pallas import tpu as pltpu

HEAD_DIM = 64
DIFF_HEADS = 4
SB_HEADS = 8
DIFF_VDIM = 2 * HEAD_DIM
N_GROUPS = 4
EXPERTS_PER_GROUP = 8
N_EXPERTS = N_GROUPS * EXPERTS_PER_GROUP
N_MOD = 6
EPS = 1e-6
LANES = 128
MXU_COLS = 256
BF16_SUBLANES = 16
BF16_EXACT_INTS = 256
NEG = -1e30
DIFF_BLOCKS_PER_ITER = 8
F32_DEAD_LOG = -104.0
VMEM_LIMIT = 56 * 1024 * 1024

F32 = jnp.float32
BF16 = jnp.bfloat16

LOG2E = math.log2(math.e)
LOG2E_PARTS = 3
DIFF_FIXED_REF_MAX = 32.0


def _bf16_parts(value, n):
    parts, rest = [], value
    for _ in range(n):
        part = float(np.float32(rest).astype(ml_dtypes.bfloat16))
        parts.append(part)
        rest -= part
    return parts


def _nt_dot(a, b):
    return lax.dot_general(a, b, (((1,), (1,)), ((), ())), preferred_element_type=F32)


def _mod_kernel(c_ref, w_ref, b_ref, o_ref):
    c = c_ref[...]
    sc = c * jax.nn.sigmoid(c)
    w = w_ref[...]
    sc_hi, w_hi = sc.astype(BF16), w.astype(BF16)
    sc_lo, w_lo = (sc - sc_hi.astype(F32)).astype(BF16), (w - w_hi.astype(F32)).astype(BF16)
    o_ref[...] = (jnp.dot(sc_hi, w_hi, preferred_element_type=F32)
                  + jnp.dot(sc_lo, w_hi, preferred_element_type=F32)
                  + jnp.dot(sc_hi, w_lo, preferred_element_type=F32)) + b_ref[...]


def _mod_call(c_pad, w_ada, b_ada):
    rows, d = c_pad.shape
    n = w_ada.shape[1]
    tn = n // 4
    return pl.pallas_call(
        _mod_kernel,
        grid=(n // tn,),
        in_specs=[pl.BlockSpec((rows, d), lambda j: (0, 0)),
                  pl.BlockSpec((d, tn), lambda j: (0, j)),
                  pl.BlockSpec((1, tn), lambda j: (0, j))],
        out_specs=pl.BlockSpec((rows, tn), lambda j: (0, j)),
        out_shape=jax.ShapeDtypeStruct((rows, n), F32),
        compiler_params=pltpu.CompilerParams(dimension_semantics=("parallel",),
                                             vmem_limit_bytes=VMEM_LIMIT),
        name="adaln_mod",
    )(c_pad, w_ada, b_ada)


_PROJ_BLOCKS = 24


def _proj_kernel(x_ref, mod_ref, g_ref, w32_ref, wcol_ref, qg_ref, kg_ref, qaug_ref,
                 dq_ref, dk_ref, dv_ref, sq_ref, sk_ref, sv_ref, w_ref, *, tm, tk, slopes):
    @pl.when((pl.program_id(0) == 0) & (pl.program_id(1) == 0))
    def _():
        w_ref[...] = (w32_ref[...] * wcol_ref[...]).astype(BF16)

    x = x_ref[0]
    mod = mod_ref[0]
    shift, scale = mod[0:1, :], mod[1:2, :]
    ms = jnp.mean(x * x, axis=-1, keepdims=True)
    h = (x * lax.rsqrt(ms + EPS) * g_ref[...]) * (1.0 + scale) + shift
    hb = h.astype(BF16)

    lane = lax.broadcasted_iota(jnp.int32, (1, LANES), 1)
    aug = (lane >= HEAD_DIM) & (lane < HEAD_DIM + 2 * LOG2E_PARTS)
    lo_lane = (aug & ((lane & 1) == 0)).astype(F32)
    hi_lane = (aug & ((lane & 1) == 1)).astype(F32)
    row = lax.broadcasted_iota(jnp.int32, (tm, 1), 0) + pl.program_id(1) * tm
    koff = row & (tk - 1)
    koff_lo = (koff & (BF16_EXACT_INTS - 1)).astype(F32)
    koff_hi = (koff - (koff & (BF16_EXACT_INTS - 1))).astype(F32)
    koff_lanes = lo_lane * koff_lo + hi_lane * koff_hi
    low = lane < HEAD_DIM

    for c in range(_PROJ_BLOCKS * LANES // MXU_COLS):
        pc = jnp.dot(hb, w_ref[:, c * MXU_COLS:(c + 1) * MXU_COLS], preferred_element_type=F32)
        for half in range(MXU_COLS // LANES):
            blk = c * (MXU_COLS // LANES) + half
            piece = pc[:, half * LANES:(half + 1) * LANES]
            if blk < 8:
                sq = piece * piece
                ss_low = jnp.sum(jnp.where(low, sq, 0.0), axis=-1, keepdims=True)
                ss_high = jnp.sum(sq, axis=-1, keepdims=True) - ss_low
                inv = jnp.where(low, lax.rsqrt(ss_low * (1.0 / HEAD_DIM) + EPS),
                                lax.rsqrt(ss_high * (1.0 / HEAD_DIM) + EPS))
                head = blk % DIFF_HEADS
                if blk < DIFF_HEADS:
                    scaled, extra, out = piece * inv * qg_ref[...], qaug_ref[...], dq_ref
                else:
                    scaled, extra, out = piece * inv * kg_ref[...], koff_lanes * slopes[head], dk_ref
                out[0, 2 * head] = (jnp.where(low, scaled, 0.0) + extra).astype(BF16)
                out[0, 2 * head + 1] = (jnp.where(low, pltpu.roll(scaled, HEAD_DIM, 1), 0.0) + extra).astype(BF16)
            elif blk < 12:
                dv_ref[0, blk - 8] = piece.astype(BF16)
            elif blk < 16:
                sq_ref[0, 2 * (blk - 12)] = jnp.where(low, piece, 0.0).astype(BF16)
                sq_ref[0, 2 * (blk - 12) + 1] = jnp.where(low, 0.0, piece).astype(BF16)
            elif blk < 20:
                sk_ref[0, blk - 16] = piece.astype(BF16)
            else:
                sv_ref[0, blk - 20] = piece.astype(BF16)


def _proj_call(x, mod, g_attn, w_in, w_col, qg_pad, kg_pad, qaug_row, *, tm, tk, slopes):
    b, s, d = x.shape
    n = w_in.shape[1]
    nt = s // tm

    def hm(nh):
        return (jax.ShapeDtypeStruct((b, nh, s, LANES), BF16),
                pl.BlockSpec((1, nh, tm, LANES), lambda bi, ti: (bi, 0, ti, 0)))

    outs = [hm(8), hm(8), hm(4), hm(8), hm(4), hm(4)]
    return pl.pallas_call(
        functools.partial(_proj_kernel, tm=tm, tk=tk, slopes=slopes),
        grid=(b, nt),
        in_specs=[pl.BlockSpec((1, tm, d), lambda bi, ti: (bi, ti, 0)),
                  pl.BlockSpec((1, N_MOD, d), lambda bi, ti: (bi, 0, 0)),
                  pl.BlockSpec((1, d), lambda bi, ti: (0, 0)),
                  pl.BlockSpec((d, n), lambda bi, ti: (0, 0)),
                  pl.BlockSpec((1, n), lambda bi, ti: (0, 0)),
                  pl.BlockSpec((1, LANES), lambda bi, ti: (0, 0)),
                  pl.BlockSpec((1, LANES), lambda bi, ti: (0, 0)),
                  pl.BlockSpec((1, LANES), lambda bi, ti: (0, 0))],
        out_specs=[o[1] for o in outs],
        out_shape=[o[0] for o in outs],
        scratch_shapes=[pltpu.VMEM((d, n), BF16)],
        compiler_params=pltpu.CompilerParams(dimension_semantics=("arbitrary", "arbitrary"),
                                             vmem_limit_bytes=VMEM_LIMIT),
        name="in_proj",
    )(x, mod, g_attn, w_in, w_col, qg_pad, kg_pad, qaug_row)


def _attn_kernel(nback_ref, lam_ref, gd_ref, bound_ref, dq_ref, dk_ref, dv_ref, gs_ref, sq_ref, sk_ref, sv_ref,
                 w1_ref, w3_ref, w2_ref, a_ref, b_ref, w1b_ref, w3b_ref, w2b_ref, acc0_ref, acc1_ref, sacc_ref,
                 *, t, ts, slopes, lambda_init, fixed_ref):
    hd = pl.program_id(1)
    i = pl.program_id(2)
    sub = t // ts
    chains = [(u, hh) for u in range(sub) for hh in range(2)]

    hd_col = jnp.full((t, 1), hd, jnp.int32)
    slope = jnp.full((t, 1), slopes[-1] * LOG2E, F32)
    for n in range(len(slopes) - 2, -1, -1):
        slope = jnp.where(hd_col == n, slopes[n] * LOG2E, slope)
    dq = (dq_ref[0, 0], dq_ref[0, 1])
    accs = (acc0_ref, acc1_ref)
    causal = lax.broadcasted_iota(jnp.int32, (t, t), 1) <= lax.broadcasted_iota(jnp.int32, (t, t), 0)
    ones_col = (lax.broadcasted_iota(jnp.int32, (t, LANES), 1) == 0).astype(BF16)
    row_pos = lax.broadcasted_iota(jnp.int32, (t, 1), 0) + i * t

    def d_block(j, carry, diagonal, kill=None):
        ks = pl.multiple_of(j * t, t)
        v_aug = jnp.concatenate([dv_ref[0, 0, pl.ds(ks, t), :], ones_col], axis=1)
        new = [None, None]
        for mp in range(2):
            s = _nt_dot(dq[mp], dk_ref[0, mp, pl.ds(ks, t), :])
            if diagonal:
                s = jnp.where(causal, s, NEG)
            if kill is not None:
                s = s - kill
            if fixed_ref:
                ref = slope * (row_pos - j * t).astype(F32) + bound_ref[:, 0:1]
                pv = jnp.dot(jnp.exp2(s - ref).astype(BF16), v_aug, preferred_element_type=F32)
                accs[mp][...] = pv if diagonal else accs[mp][...] + pv
            else:
                off = slope * jnp.full((t, 1), j * t, jnp.int32).astype(F32)
                m_new = jnp.max(s, axis=-1, keepdims=True) + off
                if not diagonal:
                    m_new = jnp.maximum(carry[mp], m_new)
                pv = jnp.dot(jnp.exp2(s - (m_new - off)).astype(BF16), v_aug, preferred_element_type=F32)
                accs[mp][...] = pv if diagonal else jnp.exp2(carry[mp] - m_new) * accs[mp][...] + pv
                new[mp] = m_new
        return () if fixed_ref else tuple(new)

    srow = lax.broadcasted_iota(jnp.int32, (ts, ts), 0)
    scol = lax.broadcasted_iota(jnp.int32, (ts, ts), 1)
    past = scol < srow
    later = (srow > scol).astype(BF16)

    def s_layer(layer, carry, masked):
        new = list(carry)
        for c, (u, hh) in enumerate(chains):
            jb = sub * i + u - layer
            c_in = carry[c] + jnp.where(jnp.full((ts, 1), jb, jnp.int32) >= 0, 0.0, NEG)
            ks = pl.multiple_of(jnp.maximum(jb, 0) * ts, ts)
            k = sk_ref[0, 0, pl.ds(ks, ts), :]
            v = sv_ref[0, 0, pl.ds(ks, ts), :]
            z = _nt_dot(sq_ref[0, hh, u * ts:(u + 1) * ts, :], k)
            neg_z = -z
            log_rem = jnp.minimum(neg_z, 0.0) - jnp.log2(1.0 + jnp.exp2(jnp.minimum(z, neg_z)))
            log_beta = log_rem + z
            if masked:
                log_rem = jnp.where(past, log_rem, 0.0)
            new[c] = c_in + jnp.sum(log_rem, axis=-1, keepdims=True)
            suffix = jnp.dot(log_rem.astype(BF16), later, preferred_element_type=F32)
            a = jnp.exp2(log_beta + suffix + c_in)
            if masked:
                a = jnp.where(past, a, 0.0)
            av = jnp.dot(a.astype(BF16), v, preferred_element_type=F32)
            sacc_ref[c] = av if masked else sacc_ref[c] + av
        return tuple(new)

    no_prev = jnp.where(jnp.full((t, 1), i, jnp.int32) >= 1, 0.0, -NEG)
    prev = jnp.maximum(i - 1, 0)
    d_carry, s_carry = (), (jnp.zeros((ts, 1), F32),) * len(chains)
    d_carry = d_block(i, d_carry, True)
    d_carry = d_block(prev, d_carry, False, kill=no_prev)
    s_carry = s_layer(0, s_carry, True)
    s_carry = s_layer(1, s_carry, False)

    first = jnp.maximum(i + 1 - nback_ref[hd], 0)
    n_rest = jnp.maximum(i - 1 - first, 0)

    def bunch(size, start, bb, carry):
        for n in range(size):
            carry = d_block(start + bb * size + n, carry, False)
        return carry

    start, left, size = first, n_rest, DIFF_BLOCKS_PER_ITER
    while size >= 1:
        trips = left // size
        d_carry = lax.fori_loop(0, trips, functools.partial(bunch, size, start), d_carry)
        start, left, size = start + trips * size, left - trips * size, size // 2

    def any_live(carry):
        top = functools.reduce(jnp.maximum, carry)
        return (jnp.max(top) > LOG2E * F32_DEAD_LOG).astype(jnp.int32)

    def live_cond(state):
        return jnp.logical_and(state[0] <= sub * i + sub - 1, state[1] > 0)

    def live_body(state):
        carry = s_layer(state[0], state[2:], False)
        return (state[0] + 1, any_live(carry)) + carry

    lax.while_loop(live_cond, live_body, (jnp.int32(2), any_live(s_carry)) + s_carry)

    lp = lam_ref[...]
    lam = (jnp.exp(jnp.sum(lp[0:1] * lp[1:2], axis=-1, keepdims=True))
           - jnp.exp(jnp.sum(lp[2:3] * lp[3:4], axis=-1, keepdims=True)) + lambda_init)
    o = (acc0_ref[:, 0:LANES] / acc0_ref[:, LANES:LANES + 1]
         - lam * (acc1_ref[:, 0:LANES] / acc1_ref[:, LANES:LANES + 1]))
    ms = jnp.mean(o * o, axis=-1, keepdims=True)
    a_ref[0] = ((o * lax.rsqrt(ms + EPS) * gd_ref[...]) * (1.0 - lambda_init)).astype(BF16)

    first_half = lax.broadcasted_iota(jnp.int32, (1, LANES), 1) < HEAD_DIM
    for u in range(sub):
        o = jnp.where(first_half, sacc_ref[2 * u], sacc_ref[2 * u + 1])
        sq = o * o
        s_first = jnp.sum(jnp.where(first_half, sq, 0.0), axis=-1, keepdims=True)
        s_all = jnp.sum(sq, axis=-1, keepdims=True)
        ms = jnp.where(first_half, s_first, s_all - s_first) * (1.0 / HEAD_DIM)
        b_ref[0, u * ts:(u + 1) * ts, :] = (o * lax.rsqrt(ms + EPS) * gs_ref[...]).astype(BF16)

    w1b_ref[...] = w1_ref[...].astype(BF16)
    w3b_ref[...] = w3_ref[...].astype(BF16)
    w2b_ref[...] = w2_ref[...].astype(BF16)


def _attn_dispatch_kernel(plan_ref, *refs, **static):
    use_fixed = plan_ref[DIFF_HEADS]
    pl.when(use_fixed == 1)(lambda: _attn_kernel(plan_ref, *refs, fixed_ref=True, **static))
    pl.when(use_fixed == 0)(lambda: _attn_kernel(plan_ref, *refs, fixed_ref=False, **static))


def _attn_call(n_back, lam_p, g_diff, bound_row, dq, dk, dv, g_pair, sq, sk, sv, w1, w3, w2,
               *, t, ts, slopes, lambda_init):
    b, _, s, _ = dq.shape
    nq = s // t
    steps = b * DIFF_HEADS * nq
    const2 = lambda bi, hi, qi: (0, 0)
    tile4 = lambda bi, hi, qi: (bi, hi, qi, 0)
    full4 = lambda bi, hi, qi: (bi, hi, 0, 0)
    step2 = lambda bi, hi, qi: ((bi * DIFF_HEADS + hi) * nq + qi, 0)
    out_spec = pl.BlockSpec((1, t, LANES), lambda bi, hi, qi: (bi, qi, hi))
    out_shape = jax.ShapeDtypeStruct((b, s, DIFF_HEADS * LANES), BF16)
    w_specs = [pl.BlockSpec((w.shape[0] // steps, w.shape[1]), step2) for w in (w1, w3, w2)]
    w_shapes = [jax.ShapeDtypeStruct(w.shape, BF16) for w in (w1, w3, w2)]
    return pl.pallas_call(
        functools.partial(_attn_dispatch_kernel, t=t, ts=ts, slopes=slopes, lambda_init=lambda_init),
        grid=(b, DIFF_HEADS, s // t),
        in_specs=[pl.BlockSpec(memory_space=pltpu.SMEM),
                  pl.BlockSpec((4, HEAD_DIM), const2),
                  pl.BlockSpec((1, LANES), const2),
                  pl.BlockSpec((1, LANES), const2),
                  pl.BlockSpec((1, 2, t, LANES), tile4),
                  pl.BlockSpec((1, 2, s, LANES), full4),
                  pl.BlockSpec((1, 1, s, LANES), full4),
                  pl.BlockSpec((1, LANES), const2),
                  pl.BlockSpec((1, 2, t, LANES), tile4),
                  pl.BlockSpec((1, 1, s, LANES), full4),
                  pl.BlockSpec((1, 1, s, LANES), full4)] + w_specs,
        out_specs=[out_spec, out_spec] + w_specs,
        out_shape=[out_shape, out_shape] + w_shapes,
        scratch_shapes=[pltpu.VMEM((t, 2 * LANES), F32), pltpu.VMEM((t, 2 * LANES), F32),
                        pltpu.VMEM((2 * (t // ts), ts, LANES), F32)],
        compiler_params=pltpu.CompilerParams(dimension_semantics=("parallel", "parallel", "parallel"),
                                             vmem_limit_bytes=VMEM_LIMIT),
        name="attn",
    )(n_back, lam_p, g_diff, bound_row, dq, dk, dv, g_pair, sq, sk, sv, w1, w3, w2)


def _outproj_kernel(a_ref, b_ref, w_ref, x_ref, mod_ref, g_ref, wr_ref, br_ref,
                    x1_ref, h2_ref, comb_ref):
    half = a_ref.shape[-1]
    y = (jnp.dot(a_ref[0], w_ref[0:half, :], preferred_element_type=F32)
         + jnp.dot(b_ref[0], w_ref[half:, :], preferred_element_type=F32))
    mod = mod_ref[0]
    gate_a, shift, scale = mod[2:3, :], mod[3:4, :], mod[4:5, :]
    x1 = x_ref[0] + gate_a * y
    x1_ref[0] = x1
    ms = jnp.mean(x1 * x1, axis=-1, keepdims=True)
    h2 = (x1 * lax.rsqrt(ms + EPS) * g_ref[...]) * (1.0 + scale) + shift
    h2_hi = h2.astype(BF16)
    h2_ref[0] = h2_hi

    h2_lo = (h2 - h2_hi.astype(F32)).astype(BF16)
    both = jnp.dot(h2_hi, wr_ref[...], preferred_element_type=F32)
    logits = (both[:, 0:LANES] + both[:, LANES:2 * LANES]
              + jnp.dot(h2_lo, wr_ref[:, 0:LANES], preferred_element_type=F32)) + br_ref[...]
    lane = lax.broadcasted_iota(jnp.int32, logits.shape, 1).astype(F32)
    big = jnp.float32(LANES)

    def top(vals):
        mx = jnp.max(vals, axis=-1, keepdims=True)
        idx = jnp.min(jnp.where(vals == mx, lane, big), axis=-1, keepdims=True)
        return mx, idx

    is_group = (lane >= N_EXPERTS) & (lane < N_EXPERTS + N_GROUPS)
    g_logits = jnp.where(is_group, logits, NEG)
    g_max, g_idx = top(g_logits)
    gate_group = 1.0 / jnp.sum(jnp.where(is_group, jnp.exp(logits - g_max), 0.0), axis=-1, keepdims=True)
    g_sel = g_idx - N_EXPERTS
    in_group = (lane >= g_sel * EXPERTS_PER_GROUP) & (lane < (g_sel + 1.0) * EXPERTS_PER_GROUP)
    e_logits = jnp.where(in_group, logits, NEG)
    v1, i1 = top(e_logits)
    v2, i2 = top(jnp.where(lane == i1, NEG, e_logits))
    e2 = jnp.exp(v2 - v1)
    w_first = 1.0 / (1.0 + e2)
    w_second = e2 / (1.0 + e2)
    comb_ref[0] = (gate_group * (jnp.where(lane == i1, w_first, 0.0) + jnp.where(lane == i2, w_second, 0.0))
                   + jnp.where(lane == g_idx, 1.0, 0.0))


def _outproj_call(a_out, b_out, w_out, x, mod, g_ffn, w_router, b_router, *, tm):
    b, s, d = x.shape
    half = a_out.shape[-1]
    idx3 = lambda bi, ti: (bi, ti, 0)
    const2 = lambda bi, ti: (0, 0)
    return pl.pallas_call(
        _outproj_kernel,
        grid=(b, s // tm),
        in_specs=[pl.BlockSpec((1, tm, half), idx3),
                  pl.BlockSpec((1, tm, half), idx3),
                  pl.BlockSpec((2 * half, d), const2),
                  pl.BlockSpec((1, tm, d), idx3),
                  pl.BlockSpec((1, N_MOD, d), lambda bi, ti: (bi, 0, 0)),
                  pl.BlockSpec((1, d), const2),
                  pl.BlockSpec((d, 2 * LANES), const2),
                  pl.BlockSpec((1, LANES), const2)],
        out_specs=[pl.BlockSpec((1, tm, d), idx3),
                   pl.BlockSpec((1, tm, d), idx3),
                   pl.BlockSpec((1, tm, LANES), idx3)],
        out_shape=[jax.ShapeDtypeStruct((b, s, d), F32),
                   jax.ShapeDtypeStruct((b, s, d), BF16),
                   jax.ShapeDtypeStruct((b, s, LANES), F32)],
        compiler_params=pltpu.CompilerParams(dimension_semantics=("parallel", "parallel"),
                                             vmem_limit_bytes=VMEM_LIMIT),
        name="out_proj_router",
    )(a_out, b_out, w_out, x, mod, g_ffn, w_router, b_router)


def _moe_kernel(h_ref, comb_ref, earlier_ref, w1_ref, w3_ref, w2_ref, x1_ref, mod_ref, o_ref, *, unit, sizes):
    g = pl.program_id(2)
    tm = h_ref.shape[1]

    @pl.when(g == 0)
    def _():
        o_ref[0] = x1_ref[0]

    comb = comb_ref[0]
    lane = lax.broadcasted_iota(jnp.int32, (1, LANES), 1)
    member = jnp.sum(jnp.where(lane == N_EXPERTS + g, comb, 0.0), axis=-1, keepdims=True)
    sub = earlier_ref.shape[0]
    by_block = sum(jnp.where(lane == n, member[n * sub:(n + 1) * sub], 0.0) for n in range(tm // sub))
    by_block_t = by_block.T
    before_col = jnp.dot(earlier_ref[...], by_block.astype(BF16), preferred_element_type=F32)
    before_row = _nt_dot(by_block_t.astype(BF16), earlier_ref[...])
    totals = jnp.sum(by_block, axis=0, keepdims=True)
    offset = jnp.zeros((1, 1), F32)
    slot_cols, slot_rows = [], []
    for n in range(tm // sub):
        col_n = jnp.sum(jnp.where(lane == n, before_col, 0.0), axis=-1, keepdims=True) + offset
        slot_cols.append(jnp.where(member[n * sub:(n + 1) * sub] > 0.0, col_n, -1.0))
        slot_rows.append(jnp.where(by_block_t[n:n + 1, :] > 0.0, before_row[n:n + 1, :] + offset, -1.0))
        offset = offset + jnp.sum(jnp.where(lane == n, totals, 0.0), axis=-1, keepdims=True)
    slot_col = jnp.concatenate(slot_cols, axis=0)
    slot_row = jnp.concatenate(slot_rows, axis=1)
    count = jnp.sum(member)
    units = jnp.int32(0)
    for c in range(tm // unit):
        units += (count > float(c * unit)).astype(jnp.int32)

    comb_hi = comb.astype(BF16)
    comb_lo = (comb - comb_hi.astype(F32)).astype(BF16)
    gate_f = mod_ref[0][5:6, :]

    def chunk(first_unit, ch):
        base = (first_unit * unit).astype(F32)
        want_col = lax.broadcasted_iota(jnp.int32, (ch, 1), 0).astype(F32) + base
        want_row = lax.broadcasted_iota(jnp.int32, (1, ch), 1).astype(F32) + base
        pick = (slot_row == want_col).astype(BF16)
        place = (slot_col == want_row).astype(BF16)
        xc = jnp.dot(pick, h_ref[0], preferred_element_type=F32).astype(BF16)
        cw = (jnp.dot(pick, comb_hi, preferred_element_type=F32)
              + jnp.dot(pick, comb_lo, preferred_element_type=F32))
        y = jnp.zeros((ch, o_ref.shape[-1]), F32)
        for e in range(EXPERTS_PER_GROUP):
            weight = jnp.sum(jnp.where(lane == g * EXPERTS_PER_GROUP + e, cw, 0.0), axis=-1, keepdims=True)
            h1 = jnp.dot(xc, w1_ref[0, e], preferred_element_type=F32)
            h3 = jnp.dot(xc, w3_ref[0, e], preferred_element_type=F32)
            hg = (h1 * jax.nn.sigmoid(h1)) * h3 * weight
            y += jnp.dot(hg.astype(BF16), w2_ref[0, e], preferred_element_type=F32)
        o_ref[0] += gate_f * jnp.dot(place, y.astype(BF16), preferred_element_type=F32)

    def step(done):
        left = units - done
        choice = sum((left > size).astype(jnp.int32) for size in sizes[:-1])
        taken = jnp.int32(sizes[0])
        for n, size in enumerate(sizes):
            pl.when(choice == n)(functools.partial(chunk, done, size * unit))
            if n > 0:
                taken += (choice >= n).astype(jnp.int32) * (size - sizes[n - 1])
        return done + taken

    lax.while_loop(lambda done: done < units, step, jnp.int32(0))


def _moe_call(h2, comb, w1, w3, w2, x1, mod, *, tm, unit, sizes):
    b, s, d = x1.shape
    f = w1.shape[-1]
    sub = min(MXU_COLS, tm)
    tok3 = lambda bi, ti, gi: (bi, ti, 0)
    return pl.pallas_call(
        functools.partial(_moe_kernel, unit=unit, sizes=sizes),
        grid=(b, s // tm, N_GROUPS),
        in_specs=[pl.BlockSpec((1, tm, d), tok3),
                  pl.BlockSpec((1, tm, LANES), tok3),
                  pl.BlockSpec((sub, sub), lambda bi, ti, gi: (0, 0)),
                  pl.BlockSpec((1, EXPERTS_PER_GROUP, d, f), lambda bi, ti, gi: (gi, 0, 0, 0)),
                  pl.BlockSpec((1, EXPERTS_PER_GROUP, d, f), lambda bi, ti, gi: (gi, 0, 0, 0)),
                  pl.BlockSpec((1, EXPERTS_PER_GROUP, f, d), lambda bi, ti, gi: (gi, 0, 0, 0)),
                  pl.BlockSpec((1, tm, d), tok3),
                  pl.BlockSpec((1, N_MOD, d), lambda bi, ti, gi: (bi, 0, 0))],
        out_specs=pl.BlockSpec((1, tm, d), tok3),
        out_shape=jax.ShapeDtypeStruct((b, s, d), F32),
        compiler_params=pltpu.CompilerParams(dimension_semantics=("parallel", "parallel", "arbitrary"),
                                             vmem_limit_bytes=VMEM_LIMIT),
        name="moe_experts",
    )(h2, comb, jnp.tri(sub, k=-1, dtype=BF16), w1, w3, w2, x1, mod)


def _w_in_column_scale(n_cols):
    sq_start = 2 * DIFF_HEADS * 2 * HEAD_DIM + DIFF_HEADS * DIFF_VDIM
    col = np.ones((1, n_cols), np.float32)
    col[0, sq_start:sq_start + SB_HEADS * HEAD_DIM] = LOG2E / math.sqrt(HEAD_DIM)
    return jnp.asarray(col)


def _pad_lanes(v, width=LANES):
    v = v.reshape(1, -1)
    return jnp.pad(v, ((0, 0), (0, width - v.shape[1])))


def kernel(x, c, w_ada, b_ada, g_attn, w_in, q_norm_g, k_norm_g, lambda_q1, lambda_k1, lambda_q2, lambda_k2,
           diff_out_g, sb_out_g, w_out, g_ffn, w_group, b_group, w_erouter, b_expert, w1, w3, w2):
    b, s, d = x.shape
    depth = w_ada.shape[0]
    tile = min(512, s)
    proj_tile = tile
    out_tile = min(1024, s)
    moe_tile = min(1024, s)
    moe_unit = max(moe_tile // (4 * N_GROUPS), BF16_SUBLANES)
    moe_sizes = (3, 4, 5, 6)
    sb_tile = min(256, s)
    slopes = tuple(2.0 ** (-8.0 * (n + 1) / DIFF_HEADS) for n in range(DIFF_HEADS))
    qk_scale = 1.0 / math.sqrt(HEAD_DIM)
    c_pad = jnp.pad(c, ((0, BF16_SUBLANES - b), (0, 0)))
    qaug = np.zeros((1, LANES), np.float32)
    qaug[0, HEAD_DIM:HEAD_DIM + 2 * LOG2E_PARTS] = np.repeat(_bf16_parts(LOG2E, LOG2E_PARTS), 2)
    qaug_row = jnp.asarray(qaug)

    for layer in range(depth):
        lambda_init = 0.8 - 0.6 * math.exp(-0.3 * layer)
        mod = _mod_call(c_pad, w_ada[layer], b_ada[layer].reshape(1, -1))[:b].reshape(b, N_MOD, d)

        dq, dk, dv, sq, sk, sv = _proj_call(
            x, mod, g_attn[layer].reshape(1, d), w_in[layer], _w_in_column_scale(w_in.shape[-1]),
            jnp.tile(q_norm_g[layer] * (qk_scale * LOG2E), 2).reshape(1, LANES),
            jnp.tile(k_norm_g[layer], 2).reshape(1, LANES), qaug_row,
            tm=proj_tile, tk=tile, slopes=slopes)

        lam_p = jnp.stack([lambda_q1[layer], lambda_k1[layer], lambda_q2[layer], lambda_k2[layer]])
        score_bound = (1.02 * HEAD_DIM * qk_scale) * jnp.max(jnp.abs(q_norm_g[layer])) * jnp.max(jnp.abs(k_norm_g[layer]))
        dead_dist = (2.0 * score_bound - F32_DEAD_LOG) / jnp.asarray(slopes, F32)
        n_back = jnp.clip(jnp.floor(dead_dist / tile) + 2.0, 1.0, s // tile).astype(jnp.int32)
        plan = jnp.concatenate([n_back, (score_bound <= DIFF_FIXED_REF_MAX).astype(jnp.int32).reshape(1)])
        a_out, b_out, w1_bf, w3_bf, w2_bf = _attn_call(
            plan, lam_p, diff_out_g[layer].reshape(1, LANES),
            jnp.full((1, LANES), score_bound * LOG2E, F32), dq, dk, dv,
            jnp.tile(sb_out_g[layer], 2).reshape(1, LANES), sq, sk, sv,
            w1[layer].reshape(-1, w1.shape[-1]), w3[layer].reshape(-1, w3.shape[-1]),
            w2[layer].reshape(-1, w2.shape[-1]),
            t=tile, ts=sb_tile, slopes=slopes, lambda_init=lambda_init)

        w_router = jnp.concatenate(
            [jnp.transpose(w_erouter[layer], (1, 0, 2)).reshape(d, N_EXPERTS), w_group[layer]], axis=1)
        w_router = jnp.pad(w_router, ((0, 0), (0, LANES - w_router.shape[1])))
        w_router_hi = w_router.astype(BF16)
        w_router = jnp.concatenate([w_router_hi, (w_router - w_router_hi.astype(F32)).astype(BF16)], axis=1)
        b_router = _pad_lanes(jnp.concatenate([b_expert[layer].reshape(-1), b_group[layer]]))
        x1, h2, comb = _outproj_call(a_out, b_out, w_out[layer].astype(BF16), x, mod,
                                     g_ffn[layer].reshape(1, d), w_router, b_router, tm=out_tile)

        x = _moe_call(h2, comb, w1_bf.reshape(w1.shape[1:]), w3_bf.reshape(w3.shape[1:]), w2_bf.reshape(w2.shape[1:]),
                      x1, mod, tm=moe_tile, unit=moe_unit, sizes=moe_sizes)
    return x
```

```python
import functools
import math

import jax
import jax.numpy as jnp
import ml_dtypes
import numpy as np
from jax import lax
from jax.experimental import pallas as pl
from jax.experimental.pallas import tpu as pltpu

HEAD_DIM = 64
DIFF_HEADS = 4
SB_HEADS = 8
DIFF_VDIM = 2 * HEAD_DIM
N_GROUPS = 4
EXPERTS_PER_GROUP = 8
N_EXPERTS = N_GROUPS * EXPERTS_PER_GROUP
N_MOD = 6
EPS = 1e-6
LANES = 128
MXU_COLS = 256
BF16_SUBLANES = 16
BF16_EXACT_INTS = 256
NEG = -1e30
DIFF_BLOCKS_PER_ITER = 8
F32_DEAD_LOG = -104.0
VMEM_LIMIT = 56 * 1024 * 1024

F32 = jnp.float32
BF16 = jnp.bfloat16

LOG2E = math.log2(math.e)
LOG2E_PARTS = 3
DIFF_FIXED_REF_MAX = 32.0


def _bf16_parts(value, n):
    parts, rest = [], value
    for _ in range(n):
        part = float(np.float32(rest).astype(ml_dtypes.bfloat16))
        parts.append(part)
        rest -= part
    return parts


def _nt_dot(a, b):
    return lax.dot_general(a, b, (((1,), (1,)), ((), ())), preferred_element_type=F32)


def _mod_kernel(c_ref, w_ref, b_ref, o_ref):
    c = c_ref[...]
    sc = c * jax.nn.sigmoid(c)
    w = w_ref[...]
    sc_hi, w_hi = sc.astype(BF16), w.astype(BF16)
    sc_lo, w_lo = (sc - sc_hi.astype(F32)).astype(BF16), (w - w_hi.astype(F32)).astype(BF16)
    o_ref[...] = (jnp.dot(sc_hi, w_hi, preferred_element_type=F32)
                  + jnp.dot(sc_lo, w_hi, preferred_element_type=F32)
                  + jnp.dot(sc_hi, w_lo, preferred_element_type=F32)) + b_ref[...]


def _mod_call(c_pad, w_ada, b_ada):
    rows, d = c_pad.shape
    n = w_ada.shape[1]
    tn = n // 4
    return pl.pallas_call(
        _mod_kernel,
        grid=(n // tn,),
        in_specs=[pl.BlockSpec((rows, d), lambda j: (0, 0)),
                  pl.BlockSpec((d, tn), lambda j: (0, j)),
                  pl.BlockSpec((1, tn), lambda j: (0, j))],
        out_specs=pl.BlockSpec((rows, tn), lambda j: (0, j)),
        out_shape=jax.ShapeDtypeStruct((rows, n), F32),
        compiler_params=pltpu.CompilerParams(dimension_semantics=("parallel",),
                                             vmem_limit_bytes=VMEM_LIMIT),
        name="adaln_mod",
    )(c_pad, w_ada, b_ada)


_PROJ_BLOCKS = 24


def _proj_kernel(x_ref, mod_ref, g_ref, w32_ref, wcol_ref, qg_ref, kg_ref, qaug_ref,
                 dq_ref, dk_ref, dv_ref, sq_ref, sk_ref, sv_ref, w_ref, *, tm, tk, slopes):
    @pl.when((pl.program_id(0) == 0) & (pl.program_id(1) == 0))
    def _():
        w_ref[...] = (w32_ref[...] * wcol_ref[...]).astype(BF16)

    x = x_ref[0]
    mod = mod_ref[0]
    shift, scale = mod[0:1, :], mod[1:2, :]
    ms = jnp.mean(x * x, axis=-1, keepdims=True)
    h = (x * lax.rsqrt(ms + EPS) * g_ref[...]) * (1.0 + scale) + shift
    hb = h.astype(BF16)

    lane = lax.broadcasted_iota(jnp.int32, (1, LANES), 1)
    aug = (lane >= HEAD_DIM) & (lane < HEAD_DIM + 2 * LOG2E_PARTS)
    lo_lane = (aug & ((lane & 1) == 0)).astype(F32)
    hi_lane = (aug & ((lane & 1) == 1)).astype(F32)
    row = lax.broadcasted_iota(jnp.int32, (tm, 1), 0) + pl.program_id(1) * tm
    koff = row & (tk - 1)
    koff_lo = (koff & (BF16_EXACT_INTS - 1)).astype(F32)
    koff_hi = (koff - (koff & (BF16_EXACT_INTS - 1))).astype(F32)
    koff_lanes = lo_lane * koff_lo + hi_lane * koff_hi
    low = lane < HEAD_DIM

    for c in range(_PROJ_BLOCKS * LANES // MXU_COLS):
        pc = jnp.dot(hb, w_ref[:, c * MXU_COLS:(c + 1) * MXU_COLS], preferred_element_type=F32)
        for half in range(MXU_COLS // LANES):
            blk = c * (MXU_COLS // LANES) + half
            piece = pc[:, half * LANES:(half + 1) * LANES]
            if blk < 8:
                sq = piece * piece
                ss_low = jnp.sum(jnp.where(low, sq, 0.0), axis=-1, keepdims=True)
                ss_high = jnp.sum(sq, axis=-1, keepdims=True) - ss_low
                inv = jnp.where(low, lax.rsqrt(ss_low * (1.0 / HEAD_DIM) + EPS),
                                lax.rsqrt(ss_high * (1.0 / HEAD_DIM) + EPS))
                head = blk % DIFF_HEADS
                if blk < DIFF_HEADS:
                    scaled, extra, out = piece * inv * qg_ref[...], qaug_ref[...], dq_ref
                else:
                    scaled, extra, out = piece * inv * kg_ref[...], koff_lanes * slopes[head], dk_ref
                out[0, 2 * head] = (jnp.where(low, scaled, 0.0) + extra).astype(BF16)
                out[0, 2 * head + 1] = (jnp.where(low, pltpu.roll(scaled, HEAD_DIM, 1), 0.0) + extra).astype(BF16)
            elif blk < 12:
                dv_ref[0, blk - 8] = piece.astype(BF16)
            elif blk < 16:
                sq_ref[0, 2 * (blk - 12)] = jnp.where(low, piece, 0.0).astype(BF16)
                sq_ref[0, 2 * (blk - 12) + 1] = jnp.where(low, 0.0, piece).astype(BF16)
            elif blk < 20:
                sk_ref[0, blk - 16] = piece.astype(BF16)
            else:
                sv_ref[0, blk - 20] = piece.astype(BF16)


def _proj_call(x, mod, g_attn, w_in, w_col, qg_pad, kg_pad, qaug_row, *, tm, tk, slopes):
    b, s, d = x.shape
    n = w_in.shape[1]
    nt = s // tm

    def hm(nh):
        return (jax.ShapeDtypeStruct((b, nh, s, LANES), BF16),
                pl.BlockSpec((1, nh, tm, LANES), lambda bi, ti: (bi, 0, ti, 0)))

    outs = [hm(8), hm(8), hm(4), hm(8), hm(4), hm(4)]
    return pl.pallas_call(
        functools.partial(_proj_kernel, tm=tm, tk=tk, slopes=slopes),
        grid=(b, nt),
        in_specs=[pl.BlockSpec((1, tm, d), lambda bi, ti: (bi, ti, 0)),
                  pl.BlockSpec((1, N_MOD, d), lambda bi, ti: (bi, 0, 0)),
                  pl.BlockSpec((1, d), lambda bi, ti: (0, 0)),
                  pl.BlockSpec((d, n), lambda bi, ti: (0, 0)),
                  pl.BlockSpec((1, n), lambda bi, ti: (0, 0)),
                  pl.BlockSpec((1, LANES), lambda bi, ti: (0, 0)),
                  pl.BlockSpec((1, LANES), lambda bi, ti: (0, 0)),
                  pl.BlockSpec((1, LANES), lambda bi, ti: (0, 0))],
        out_specs=[o[1] for o in outs],
        out_shape=[o[0] for o in outs],
        scratch_shapes=[pltpu.VMEM((d, n), BF16)],
        compiler_params=pltpu.CompilerParams(dimension_semantics=("arbitrary", "arbitrary"),
                                             vmem_limit_bytes=VMEM_LIMIT),
        name="in_proj",
    )(x, mod, g_attn, w_in, w_col, qg_pad, kg_pad, qaug_row)


def _attn_kernel(nback_ref, lam_ref, gd_ref, bound_ref, dq_ref, dk_ref, dv_ref, gs_ref, sq_ref, sk_ref, sv_ref,
                 w1_ref, w3_ref, w2_ref, a_ref, b_ref, w1b_ref, w3b_ref, w2b_ref, acc0_ref, acc1_ref, sacc_ref,
                 *, t, ts, slopes, lambda_init, fixed_ref):
    hd = pl.program_id(1)
    i = pl.program_id(2)
    sub = t // ts
    chains = [(u, hh) for u in range(sub) for hh in range(2)]

    hd_col = jnp.full((t, 1), hd, jnp.int32)
    slope = jnp.full((t, 1), slopes[-1] * LOG2E, F32)
    for n in range(len(slopes) - 2, -1, -1):
        slope = jnp.where(hd_col == n, slopes[n] * LOG2E, slope)
    dq = (dq_ref[0, 0], dq_ref[0, 1])
    accs = (acc0_ref, acc1_ref)
    causal = lax.broadcasted_iota(jnp.int32, (t, t), 1) <= lax.broadcasted_iota(jnp.int32, (t, t), 0)
    ones_col = (lax.broadcasted_iota(jnp.int32, (t, LANES), 1) == 0).astype(BF16)
    row_pos = lax.broadcasted_iota(jnp.int32, (t, 1), 0) + i * t

    def d_block(j, carry, diagonal, kill=None):
        ks = pl.multiple_of(j * t, t)
        v_aug = jnp.concatenate([dv_ref[0, 0, pl.ds(ks, t), :], ones_col], axis=1)
        new = [None, None]
        for mp in range(2):
            s = _nt_dot(dq[mp], dk_ref[0, mp, pl.ds(ks, t), :])
            if diagonal:
                s = jnp.where(causal, s, NEG)
            if kill is not None:
                s = s - kill
            if fixed_ref:
                ref = slope * (row_pos - j * t).astype(F32) + bound_ref[:, 0:1]
                pv = jnp.dot(jnp.exp2(s - ref).astype(BF16), v_aug, preferred_element_type=F32)
                accs[mp][...] = pv if diagonal else accs[mp][...] + pv
            else:
                off = slope * jnp.full((t, 1), j * t, jnp.int32).astype(F32)
                m_new = jnp.max(s, axis=-1, keepdims=True) + off
                if not diagonal:
                    m_new = jnp.maximum(carry[mp], m_new)
                pv = jnp.dot(jnp.exp2(s - (m_new - off)).astype(BF16), v_aug, preferred_element_type=F32)
                accs[mp][...] = pv if diagonal else jnp.exp2(carry[mp] - m_new) * accs[mp][...] + pv
                new[mp] = m_new
        return () if fixed_ref else tuple(new)

    srow = lax.broadcasted_iota(jnp.int32, (ts, ts), 0)
    scol = lax.broadcasted_iota(jnp.int32, (ts, ts), 1)
    past = scol < srow
    later = (srow > scol).astype(BF16)

    def s_layer(layer, carry, masked):
        new = list(carry)
        for c, (u, hh) in enumerate(chains):
            jb = sub * i + u - layer
            c_in = carry[c] + jnp.where(jnp.full((ts, 1), jb, jnp.int32) >= 0, 0.0, NEG)
            ks = pl.multiple_of(jnp.maximum(jb, 0) * ts, ts)
            k = sk_ref[0, 0, pl.ds(ks, ts), :]
            v = sv_ref[0, 0, pl.ds(ks, ts), :]
            z = _nt_dot(sq_ref[0, hh, u * ts:(u + 1) * ts, :], k)
            neg_z = -z
            log_rem = jnp.minimum(neg_z, 0.0) - jnp.log2(1.0 + jnp.exp2(jnp.minimum(z, neg_z)))
            log_beta = log_rem + z
            if masked:
                log_rem = jnp.where(past, log_rem, 0.0)
            new[c] = c_in + jnp.sum(log_rem, axis=-1, keepdims=True)
            suffix = jnp.dot(log_rem.astype(BF16), later, preferred_element_type=F32)
            a = jnp.exp2(log_beta + suffix + c_in)
            if masked:
                a = jnp.where(past, a, 0.0)
            av = jnp.dot(a.astype(BF16), v, preferred_element_type=F32)
            sacc_ref[c] = av if masked else sacc_ref[c] + av
        return tuple(new)

    no_prev = jnp.where(jnp.full((t, 1), i, jnp.int32) >= 1, 0.0, -NEG)
    prev = jnp.maximum(i - 1, 0)
    d_carry, s_carry = (), (jnp.zeros((ts, 1), F32),) * len(chains)
    d_carry = d_block(i, d_carry, True)
    d_carry = d_block(prev, d_carry, False, kill=no_prev)
    s_carry = s_layer(0, s_carry, True)
    s_carry = s_layer(1, s_carry, False)

    first = jnp.maximum(i + 1 - nback_ref[hd], 0)
    n_rest = jnp.maximum(i - 1 - first, 0)

    def bunch(size, start, bb, carry):
        for n in range(size):
            carry = d_block(start + bb * size + n, carry, False)
        return carry

    start, left, size = first, n_rest, DIFF_BLOCKS_PER_ITER
    while size >= 1:
        trips = left // size
        d_carry = lax.fori_loop(0, trips, functools.partial(bunch, size, start), d_carry)
        start, left, size = start + trips * size, left - trips * size, size // 2

    def any_live(carry):
        top = functools.reduce(jnp.maximum, carry)
        return (jnp.max(top) > LOG2E * F32_DEAD_LOG).astype(jnp.int32)

    def live_cond(state):
        return jnp.logical_and(state[0] <= sub * i + sub - 1, state[1] > 0)

    def live_body(state):
        carry = s_layer(state[0], state[2:], False)
        return (state[0] + 1, any_live(carry)) + carry

    lax.while_loop(live_cond, live_body, (jnp.int32(2), any_live(s_carry)) + s_carry)

    lp = lam_ref[...]
    lam = (jnp.exp(jnp.sum(lp[0:1] * lp[1:2], axis=-1, keepdims=True))
           - jnp.exp(jnp.sum(lp[2:3] * lp[3:4], axis=-1, keepdims=True)) + lambda_init)
    o = (acc0_ref[:, 0:LANES] / acc0_ref[:, LANES:LANES + 1]
         - lam * (acc1_ref[:, 0:LANES] / acc1_ref[:, LANES:LANES + 1]))
    ms = jnp.mean(o * o, axis=-1, keepdims=True)
    a_ref[0] = ((o * lax.rsqrt(ms + EPS) * gd_ref[...]) * (1.0 - lambda_init)).astype(BF16)

    first_half = lax.broadcasted_iota(jnp.int32, (1, LANES), 1) < HEAD_DIM
    for u in range(sub):
        o = jnp.where(first_half, sacc_ref[2 * u], sacc_ref[2 * u + 1])
        sq = o * o
        s_first = jnp.sum(jnp.where(first_half, sq, 0.0), axis=-1, keepdims=True)
        s_all = jnp.sum(sq, axis=-1, keepdims=True)
        ms = jnp.where(first_half, s_first, s_all - s_first) * (1.0 / HEAD_DIM)
        b_ref[0, u * ts:(u + 1) * ts, :] = (o * lax.rsqrt(ms + EPS) * gs_ref[...]).astype(BF16)

    w1b_ref[...] = w1_ref[...].astype(BF16)
    w3b_ref[...] = w3_ref[...].astype(BF16)
    w2b_ref[...] = w2_ref[...].astype(BF16)


def _attn_dispatch_kernel(plan_ref, *refs, **static):
    use_fixed = plan_ref[DIFF_HEADS]
    pl.when(use_fixed == 1)(lambda: _attn_kernel(plan_ref, *refs, fixed_ref=True, **static))
    pl.when(use_fixed == 0)(lambda: _attn_kernel(plan_ref, *refs, fixed_ref=False, **static))


def _attn_call(n_back, lam_p, g_diff, bound_row, dq, dk, dv, g_pair, sq, sk, sv, w1, w3, w2,
               *, t, ts, slopes, lambda_init):
    b, _, s, _ = dq.shape
    nq = s // t
    steps = b * DIFF_HEADS * nq
    const2 = lambda bi, hi, qi: (0, 0)
    tile4 = lambda bi, hi, qi: (bi, hi, qi, 0)
    full4 = lambda bi, hi, qi: (bi, hi, 0, 0)
    step2 = lambda bi, hi, qi: ((bi * DIFF_HEADS + hi) * nq + qi, 0)
    out_spec = pl.BlockSpec((1, t, LANES), lambda bi, hi, qi: (bi, qi, hi))
    out_shape = jax.ShapeDtypeStruct((b, s, DIFF_HEADS * LANES), BF16)
    w_specs = [pl.BlockSpec((w.shape[0] // steps, w.shape[1]), step2) for w in (w1, w3, w2)]
    w_shapes = [jax.ShapeDtypeStruct(w.shape, BF16) for w in (w1, w3, w2)]
    return pl.pallas_call(
        functools.partial(_attn_dispatch_kernel, t=t, ts=ts, slopes=slopes, lambda_init=lambda_init),
        grid=(b, DIFF_HEADS, s // t),
        in_specs=[pl.BlockSpec(memory_space=pltpu.SMEM),
                  pl.BlockSpec((4, HEAD_DIM), const2),
                  pl.BlockSpec((1, LANES), const2),
                  pl.BlockSpec((1, LANES), const2),
                  pl.BlockSpec((1, 2, t, LANES), tile4),
                  pl.BlockSpec((1, 2, s, LANES), full4),
                  pl.BlockSpec((1, 1, s, LANES), full4),
                  pl.BlockSpec((1, LANES), const2),
                  pl.BlockSpec((1, 2, t, LANES), tile4),
                  pl.BlockSpec((1, 1, s, LANES), full4),
                  pl.BlockSpec((1, 1, s, LANES), full4)] + w_specs,
        out_specs=[out_spec, out_spec] + w_specs,
        out_shape=[out_shape, out_shape] + w_shapes,
        scratch_shapes=[pltpu.VMEM((t, 2 * LANES), F32), pltpu.VMEM((t, 2 * LANES), F32),
                        pltpu.VMEM((2 * (t // ts), ts, LANES), F32)],
        compiler_params=pltpu.CompilerParams(dimension_semantics=("parallel", "parallel", "parallel"),
                                             vmem_limit_bytes=VMEM_LIMIT),
        name="attn",
    )(n_back, lam_p, g_diff, bound_row, dq, dk, dv, g_pair, sq, sk, sv, w1, w3, w2)


def _outproj_kernel(a_ref, b_ref, w_ref, x_ref, mod_ref, g_ref, wr_ref, br_ref,
                    x1_ref, h2_ref, comb_ref):
    half = a_ref.shape[-1]
    y = (jnp.dot(a_ref[0], w_ref[0:half, :], preferred_element_type=F32)
         + jnp.dot(b_ref[0], w_ref[half:, :], preferred_element_type=F32))
    mod = mod_ref[0]
    gate_a, shift, scale = mod[2:3, :], mod[3:4, :], mod[4:5, :]
    x1 = x_ref[0] + gate_a * y
    x1_ref[0] = x1
    ms = jnp.mean(x1 * x1, axis=-1, keepdims=True)
    h2 = (x1 * lax.rsqrt(ms + EPS) * g_ref[...]) * (1.0 + scale) + shift
    h2_hi = h2.astype(BF16)
    h2_ref[0] = h2_hi

    h2_lo = (h2 - h2_hi.astype(F32)).astype(BF16)
    both = jnp.dot(h2_hi, wr_ref[...], preferred_element_type=F32)
    logits = (both[:, 0:LANES] + both[:, LANES:2 * LANES]
              + jnp.dot(h2_lo, wr_ref[:, 0:LANES], preferred_element_type=F32)) + br_ref[...]
    lane = lax.broadcasted_iota(jnp.int32, logits.shape, 1).astype(F32)
    big = jnp.float32(LANES)

    def top(vals):
        mx = jnp.max(vals, axis=-1, keepdims=True)
        idx = jnp.min(jnp.where(vals == mx, lane, big), axis=-1, keepdims=True)
        return mx, idx

    is_group = (lane >= N_EXPERTS) & (lane < N_EXPERTS + N_GROUPS)
    g_logits = jnp.where(is_group, logits, NEG)
    g_max, g_idx = top(g_logits)
    gate_group = 1.0 / jnp.sum(jnp.where(is_group, jnp.exp(logits - g_max), 0.0), axis=-1, keepdims=True)
    g_sel = g_idx - N_EXPERTS
    in_group = (lane >= g_sel * EXPERTS_PER_GROUP) & (lane < (g_sel + 1.0) * EXPERTS_PER_GROUP)
    e_logits = jnp.where(in_group, logits, NEG)
    v1, i1 = top(e_logits)
    v2, i2 = top(jnp.where(lane == i1, NEG, e_logits))
    e2 = jnp.exp(v2 - v1)
    w_first = 1.0 / (1.0 + e2)
    w_second = e2 / (1.0 + e2)
    comb_ref[0] = (gate_group * (jnp.where(lane == i1, w_first, 0.0) + jnp.where(lane == i2, w_second, 0.0))
                   + jnp.where(lane == g_idx, 1.0, 0.0))


def _outproj_call(a_out, b_out, w_out, x, mod, g_ffn, w_router, b_router, *, tm):
    b, s, d = x.shape
    half = a_out.shape[-1]
    idx3 = lambda bi, ti: (bi, ti, 0)
    const2 = lambda bi, ti: (0, 0)
    return pl.pallas_call(
        _outproj_kernel,
        grid=(b, s // tm),
        in_specs=[pl.BlockSpec((1, tm, half), idx3),
                  pl.BlockSpec((1, tm, half), idx3),
                  pl.BlockSpec((2 * half, d), const2),
                  pl.BlockSpec((1, tm, d), idx3),
                  pl.BlockSpec((1, N_MOD, d), lambda bi, ti: (bi, 0, 0)),
                  pl.BlockSpec((1, d), const2),
                  pl.BlockSpec((d, 2 * LANES), const2),
                  pl.BlockSpec((1, LANES), const2)],
        out_specs=[pl.BlockSpec((1, tm, d), idx3),
                   pl.BlockSpec((1, tm, d), idx3),
                   pl.BlockSpec((1, tm, LANES), idx3)],
        out_shape=[jax.ShapeDtypeStruct((b, s, d), F32),
                   jax.ShapeDtypeStruct((b, s, d), BF16),
                   jax.ShapeDtypeStruct((b, s, LANES), F32)],
        compiler_params=pltpu.CompilerParams(dimension_semantics=("parallel", "parallel"),
                                             vmem_limit_bytes=VMEM_LIMIT),
        name="out_proj_router",
    )(a_out, b_out, w_out, x, mod, g_ffn, w_router, b_router)


def _moe_kernel(h_ref, comb_ref, earlier_ref, w1_ref, w3_ref, w2_ref, x1_ref, mod_ref, o_ref, *, unit, sizes):
    g = pl.program_id(2)
    tm = h_ref.shape[1]

    @pl.when(g == 0)
    def _():
        o_ref[0] = x1_ref[0]

    comb = comb_ref[0]
    lane = lax.broadcasted_iota(jnp.int32, (1, LANES), 1)
    member = jnp.sum(jnp.where(lane == N_EXPERTS + g, comb, 0.0), axis=-1, keepdims=True)
    sub = earlier_ref.shape[0]
    by_block = sum(jnp.where(lane == n, member[n * sub:(n + 1) * sub], 0.0) for n in range(tm // sub))
    by_block_t = by_block.T
    before_col = jnp.dot(earlier_ref[...], by_block.astype(BF16), preferred_element_type=F32)
    before_row = _nt_dot(by_block_t.astype(BF16), earlier_ref[...])
    totals = jnp.sum(by_block, axis=0, keepdims=True)
    offset = jnp.zeros((1, 1), F32)
    slot_cols, slot_rows = [], []
    for n in range(tm // sub):
        col_n = jnp.sum(jnp.where(lane == n, before_col, 0.0), axis=-1, keepdims=True) + offset
        slot_cols.append(jnp.where(member[n * sub:(n + 1) * sub] > 0.0, col_n, -1.0))
        slot_rows.append(jnp.where(by_block_t[n:n + 1, :] > 0.0, before_row[n:n + 1, :] + offset, -1.0))
        offset = offset + jnp.sum(jnp.where(lane == n, totals, 0.0), axis=-1, keepdims=True)
    slot_col = jnp.concatenate(slot_cols, axis=0)
    slot_row = jnp.concatenate(slot_rows, axis=1)
    count = jnp.sum(member)
    units = jnp.int32(0)
    for c in range(tm // unit):
        units += (count > float(c * unit)).astype(jnp.int32)

    comb_hi = comb.astype(BF16)
    comb_lo = (comb - comb_hi.astype(F32)).astype(BF16)
    gate_f = mod_ref[0][5:6, :]

    def chunk(first_unit, ch):
        base = (first_unit * unit).astype(F32)
        want_col = lax.broadcasted_iota(jnp.int32, (ch, 1), 0).astype(F32) + base
        want_row = lax.broadcasted_iota(jnp.int32, (1, ch), 1).astype(F32) + base
        pick = (slot_row == want_col).astype(BF16)
        place = (slot_col == want_row).astype(BF16)
        xc = jnp.dot(pick, h_ref[0], preferred_element_type=F32).astype(BF16)
        cw = (jnp.dot(pick, comb_hi, preferred_element_type=F32)
              + jnp.dot(pick, comb_lo, preferred_element_type=F32))
        gated = []
        for e in range(EXPERTS_PER_GROUP):
            weight = jnp.sum(jnp.where(lane == g * EXPERTS_PER_GROUP + e, cw, 0.0), axis=-1, keepdims=True)
            h1 = jnp.dot(xc, w1_ref[0, e], preferred_element_type=F32)
            h3 = jnp.dot(xc, w3_ref[0, e], preferred_element_type=F32)
            gated.append(((h1 * jax.nn.sigmoid(h1)) * h3 * weight).astype(BF16))
        w2_all = w2_ref[0].reshape(EXPERTS_PER_GROUP * w2_ref.shape[2], w2_ref.shape[3])
        y = jnp.dot(jnp.concatenate(gated, axis=1), w2_all, preferred_element_type=F32)
        o_ref[0] += gate_f * jnp.dot(place, y.astype(BF16), preferred_element_type=F32)

    def step(done):
        left = units - done
        choice = sum((left > size).astype(jnp.int32) for size in sizes[:-1])
        taken = jnp.int32(sizes[0])
        for n, size in enumerate(sizes):
            pl.when(choice == n)(functools.partial(chunk, done, size * unit))
            if n > 0:
                taken += (choice >= n).astype(jnp.int32) * (size - sizes[n - 1])
        return done + taken

    lax.while_loop(lambda done: done < units, step, jnp.int32(0))


def _moe_call(h2, comb, w1, w3, w2, x1, mod, *, tm, unit, sizes):
    b, s, d = x1.shape
    f = w1.shape[-1]
    sub = min(MXU_COLS, tm)
    tok3 = lambda bi, ti, gi: (bi, ti, 0)
    return pl.pallas_call(
        functools.partial(_moe_kernel, unit=unit, sizes=sizes),
        grid=(b, s // tm, N_GROUPS),
        in_specs=[pl.BlockSpec((1, tm, d), tok3),
                  pl.BlockSpec((1, tm, LANES), tok3),
                  pl.BlockSpec((sub, sub), lambda bi, ti, gi: (0, 0)),
                  pl.BlockSpec((1, EXPERTS_PER_GROUP, d, f), lambda bi, ti, gi: (gi, 0, 0, 0)),
                  pl.BlockSpec((1, EXPERTS_PER_GROUP, d, f), lambda bi, ti, gi: (gi, 0, 0, 0)),
                  pl.BlockSpec((1, EXPERTS_PER_GROUP, f, d), lambda bi, ti, gi: (gi, 0, 0, 0)),
                  pl.BlockSpec((1, tm, d), tok3),
                  pl.BlockSpec((1, N_MOD, d), lambda bi, ti, gi: (bi, 0, 0))],
        out_specs=pl.BlockSpec((1, tm, d), tok3),
        out_shape=jax.ShapeDtypeStruct((b, s, d), F32),
        compiler_params=pltpu.CompilerParams(dimension_semantics=("parallel", "parallel", "arbitrary"),
                                             vmem_limit_bytes=VMEM_LIMIT),
        name="moe_experts",
    )(h2, comb, jnp.tri(sub, k=-1, dtype=BF16), w1, w3, w2, x1, mod)


def _w_in_column_scale(n_cols):
    sq_start = 2 * DIFF_HEADS * 2 * HEAD_DIM + DIFF_HEADS * DIFF_VDIM
    col = np.ones((1, n_cols), np.float32)
    col[0, sq_start:sq_start + SB_HEADS * HEAD_DIM] = LOG2E / math.sqrt(HEAD_DIM)
    return jnp.asarray(col)


def _pad_lanes(v, width=LANES):
    v = v.reshape(1, -1)
    return jnp.pad(v, ((0, 0), (0, width - v.shape[1])))


def kernel(x, c, w_ada, b_ada, g_attn, w_in, q_norm_g, k_norm_g, lambda_q1, lambda_k1, lambda_q2, lambda_k2,
           diff_out_g, sb_out_g, w_out, g_ffn, w_group, b_group, w_erouter, b_expert, w1, w3, w2):
    b, s, d = x.shape
    depth = w_ada.shape[0]
    tile = min(512, s)
    proj_tile = tile
    out_tile = min(1024, s)
    moe_tile = min(1024, s)
    moe_unit = max(moe_tile // (4 * N_GROUPS), BF16_SUBLANES)
    moe_sizes = (3, 4, 5, 6)
    sb_tile = min(256, s)
    slopes = tuple(2.0 ** (-8.0 * (n + 1) / DIFF_HEADS) for n in range(DIFF_HEADS))
    qk_scale = 1.0 / math.sqrt(HEAD_DIM)
    c_pad = jnp.pad(c, ((0, BF16_SUBLANES - b), (0, 0)))
    qaug = np.zeros((1, LANES), np.float32)
    qaug[0, HEAD_DIM:HEAD_DIM + 2 * LOG2E_PARTS] = np.repeat(_bf16_parts(LOG2E, LOG2E_PARTS), 2)
    qaug_row = jnp.asarray(qaug)

    for layer in range(depth):
        lambda_init = 0.8 - 0.6 * math.exp(-0.3 * layer)
        mod = _mod_call(c_pad, w_ada[layer], b_ada[layer].reshape(1, -1))[:b].reshape(b, N_MOD, d)

        dq, dk, dv, sq, sk, sv = _proj_call(
            x, mod, g_attn[layer].reshape(1, d), w_in[layer], _w_in_column_scale(w_in.shape[-1]),
            jnp.tile(q_norm_g[layer] * (qk_scale * LOG2E), 2).reshape(1, LANES),
            jnp.tile(k_norm_g[layer], 2).reshape(1, LANES), qaug_row,
            tm=proj_tile, tk=tile, slopes=slopes)

        lam_p = jnp.stack([lambda_q1[layer], lambda_k1[layer], lambda_q2[layer], lambda_k2[layer]])
        score_bound = (1.02 * HEAD_DIM * qk_scale) * jnp.max(jnp.abs(q_norm_g[layer])) * jnp.max(jnp.abs(k_norm_g[layer]))
        dead_dist = (2.0 * score_bound - F32_DEAD_LOG) / jnp.asarray(slopes, F32)
        n_back = jnp.clip(jnp.floor(dead_dist / tile) + 2.0, 1.0, s // tile).astype(jnp.int32)
        plan = jnp.concatenate([n_back, (score_bound <= DIFF_FIXED_REF_MAX).astype(jnp.int32).reshape(1)])
        a_out, b_out, w1_bf, w3_bf, w2_bf = _attn_call(
            plan, lam_p, diff_out_g[layer].reshape(1, LANES),
            jnp.full((1, LANES), score_bound * LOG2E, F32), dq, dk, dv,
            jnp.tile(sb_out_g[layer], 2).reshape(1, LANES), sq, sk, sv,
            w1[layer].reshape(-1, w1.shape[-1]), w3[layer].reshape(-1, w3.shape[-1]),
            w2[layer].reshape(-1, w2.shape[-1]),
            t=tile, ts=sb_tile, slopes=slopes, lambda_init=lambda_init)

        w_router = jnp.concatenate(
            [jnp.transpose(w_erouter[layer], (1, 0, 2)).reshape(d, N_EXPERTS), w_group[layer]], axis=1)
        w_router = jnp.pad(w_router, ((0, 0), (0, LANES - w_router.shape[1])))
        w_router_hi = w_router.astype(BF16)
        w_router = jnp.concatenate([w_router_hi, (w_router - w_router_hi.astype(F32)).astype(BF16)], axis=1)
        b_router = _pad_lanes(jnp.concatenate([b_expert[layer].reshape(-1), b_group[layer]]))
        x1, h2, comb = _outproj_call(a_out, b_out, w_out[layer].astype(BF16), x, mod,
                                     g_ffn[layer].reshape(1, d), w_router, b_router, tm=out_tile)

        x = _moe_call(h2, comb, w1_bf.reshape(w1.shape[1:]), w3_bf.reshape(w3.shape[1:]), w2_bf.reshape(w2.shape[1:]),
                      x1, mod, tm=moe_tile, unit=moe_unit, sizes=moe_sizes)
    return x
```
